```python
import math
import jax, jax.numpy as jnp
from jax import lax
import numpy as np

D_MODEL = 1024
BATCH = 32
SEQ = 256
DEPTH = 4
DEC_BATCH = 2
DEC_SEQ = 2048
PAST_LEN = 512

GRID_W = 64
HEAD_DIM = 64
N_Q_HEADS = 8
N_KV_HEADS = 2
Q_PER_KV = N_Q_HEADS // N_KV_HEADS
Q_W = N_Q_HEADS * HEAD_DIM
KV_W = N_KV_HEADS * HEAD_DIM
CONV_CH = D_MODEL // 2
CONV_K = 31
CONV_PAD = CONV_K // 2
MIX_WIDTH = Q_W + CONV_CH
IN_COLS = Q_W + 2 * KV_W + 2 * CONV_CH
AXIS_PAIRS = HEAD_DIM // 4
ROPE_THETA = 10000.0
Q_BLOCK = 128
N_EXPERTS = 16
N_GROUPS = 4
EXPERTS_PER_GROUP = N_EXPERTS // N_GROUPS
TOP_K = 2
D_FF_EXPERT = 512
EPS = 1e-6

kernel_name = 'hybrid_diffusion_attn_conformer_moe_step'


def rms_norm(x, g):
    xf = x.astype(jnp.float32)
    y = xf * lax.rsqrt(jnp.mean(xf * xf, axis=-1, keepdims=True) + EPS)
    return (y * g.astype(jnp.float32)).astype(x.dtype)


def layer_norm(x, g, b):
    xf = x.astype(jnp.float32)
    mu = jnp.mean(xf, axis=-1, keepdims=True)
    var = jnp.mean(jnp.square(xf - mu), axis=-1, keepdims=True)
    y = (xf - mu) * lax.rsqrt(var + EPS)
    return (y * g.astype(jnp.float32) + b.astype(jnp.float32)).astype(x.dtype)


def modulation(cvec, w_mod_l, b_mod_l):
    m = jax.nn.silu(cvec) @ w_mod_l + b_mod_l
    return m.reshape(cvec.shape[0], 6, D_MODEL)


def axial_rope_tables(n):
    rows = n // GRID_W
    r_idx, c_idx = jnp.meshgrid(jnp.arange(rows), jnp.arange(GRID_W), indexing='ij')
    r_idx = r_idx.reshape(-1).astype(jnp.float32)
    c_idx = c_idx.reshape(-1).astype(jnp.float32)
    freqs = ROPE_THETA ** (-jnp.arange(AXIS_PAIRS, dtype=jnp.float32) / AXIS_PAIRS)
    ang = jnp.stack([r_idx[:, None] * freqs, c_idx[:, None] * freqs], axis=1)
    return jnp.cos(ang), jnp.sin(ang)


def apply_rope(x, cos, sin):
    B, n, H, _ = x.shape
    xr = x.astype(jnp.float32).reshape(B, n, H, 2, 2, AXIS_PAIRS)
    x1, x2 = xr[..., 0, :], xr[..., 1, :]
    cb, sb = cos[None, :, None], sin[None, :, None]
    out = jnp.stack([x1 * cb - x2 * sb, x2 * cb + x1 * sb], axis=-2)
    return out.reshape(B, n, H, HEAD_DIM).astype(x.dtype)


def block_attention(q, k, v):
    B, Sq, _, _ = q.shape
    nb = Sq // Q_BLOCK
    scale = 1.0 / math.sqrt(HEAD_DIM)
    qb = q.reshape(B, nb, Q_BLOCK, N_KV_HEADS, Q_PER_KV, HEAD_DIM).swapaxes(0, 1)

    def one_block(qblk):
        s = jnp.einsum('bqhgd,bkhd->bhgqk', qblk, k, preferred_element_type=jnp.float32) * scale
        p = jax.nn.softmax(s, axis=-1).astype(v.dtype)
        return jnp.einsum('bhgqk,bkhd->bqhgd', p, v)

    o = lax.map(one_block, qb)
    return o.swapaxes(0, 1).reshape(B, Sq, Q_W)


def conv_module(u, dw_w, dw_b, ln_g, ln_b):
    a, b = jnp.split(u, 2, axis=-1)
    z = a * jax.nn.sigmoid(b)
    z = lax.conv_general_dilated(z, dw_w[:, None, :].astype(z.dtype), window_strides=(1,),
                                 padding=[(CONV_PAD, CONV_PAD)],
                                 dimension_numbers=('NWC', 'WIO', 'NWC'),
                                 feature_group_count=CONV_CH) + dw_b
    return jax.nn.silu(layer_norm(z, ln_g, ln_b))


def grouped_moe(h, w_router, b_router, w_gate, w_up, w_down):
    B, S, D = h.shape
    t = h.reshape(-1, D)
    T = t.shape[0]
    s = jax.nn.sigmoid((t @ w_router).astype(jnp.float32))
    sb = s + b_router.astype(jnp.float32)
    grp = sb.reshape(T, N_GROUPS, EXPERTS_PER_GROUP)
    gscore = lax.top_k(grp, 2)[0].sum(-1)
    gmask = jax.nn.one_hot(jnp.argmax(gscore, axis=-1), N_GROUPS, dtype=jnp.bool_)[:, :, None]
    masked = jnp.where(gmask, grp, -jnp.inf).reshape(T, N_EXPERTS)
    _, idx = lax.top_k(masked, TOP_K)
    w = jnp.take_along_axis(s, idx, axis=-1)
    w = w / jnp.sum(w, axis=-1, keepdims=True)
    gates = jnp.sum(jax.nn.one_hot(idx, N_EXPERTS, dtype=jnp.float32) * w[..., None], axis=1)
    hid = jax.nn.silu(jnp.einsum('td,edf->tef', t, w_gate)) * jnp.einsum('td,edf->tef', t, w_up)
    hid = hid * gates.astype(hid.dtype)[:, :, None]
    out = jnp.einsum('tef,efd->td', hid, w_down)
    return out.reshape(B, S, D)


def trunk_layer(x, mod, n1, n2, w_in, qg, kg, dw_w, dw_b, ln_g, ln_b, w_out,
                w_router, b_router, wg, wu, wd, rope=None, ctx_kv=None):
    sh1, sc1, g1, sh2, sc2, g2 = [mod[:, i][:, None, :] for i in range(6)]
    B, S, _ = x.shape
    h = rms_norm(x, n1) * (1.0 + sc1) + sh1
    proj = h @ w_in
    q, k, v, u = jnp.split(proj, [Q_W, Q_W + KV_W, Q_W + 2 * KV_W], axis=-1)
    q = rms_norm(q.reshape(B, S, N_Q_HEADS, HEAD_DIM), qg)
    k = rms_norm(k.reshape(B, S, N_KV_HEADS, HEAD_DIM), kg)
    v = v.reshape(B, S, N_KV_HEADS, HEAD_DIM)
    if rope is not None:
        q = apply_rope(q, rope[0], rope[1])
        k = apply_rope(k, rope[0], rope[1])
    if ctx_kv is not None:
        k_all = jnp.concatenate([ctx_kv[0].astype(k.dtype), k], axis=1)
        v_all = jnp.concatenate([ctx_kv[1].astype(v.dtype), v], axis=1)
    else:
        k_all, v_all = k, v
    attn = block_attention(q, k_all, v_all)
    conv = conv_module(u, dw_w, dw_b, ln_g, ln_b)
    x = x + g1 * (jnp.concatenate([attn, conv], axis=-1) @ w_out)
    h2 = rms_norm(x, n2) * (1.0 + sc2) + sh2
    x = x + g2 * grouped_moe(h2, w_router, b_router, wg, wu, wd)
    return x, k, v


def setup_inputs(seed: int = 0) -> dict:
    key = jax.random.key(seed)
    ks = jax.random.split(key, 24)
    f32 = jnp.float32
    nrm = lambda k, shape, s: jax.random.normal(k, shape, f32) * s
    return {
        'x_prompt': nrm(ks[0], (BATCH, SEQ, D_MODEL), 1.0),
        'x_sample': nrm(ks[1], (DEC_BATCH, DEC_SEQ, D_MODEL), 1.0),
        'cache_k': nrm(ks[2], (DEC_BATCH, DEPTH, PAST_LEN, N_KV_HEADS, HEAD_DIM), 1.0),
        'cache_v': nrm(ks[3], (DEC_BATCH, DEPTH, PAST_LEN, N_KV_HEADS, HEAD_DIM), 1.0),
        'c': nrm(ks[4], (DEC_BATCH, D_MODEL), 1.0),
        'c_ctx': nrm(ks[5], (D_MODEL,), 1.0),
        'w_mod': nrm(ks[6], (DEPTH, D_MODEL, 6 * D_MODEL), 0.5 * D_MODEL ** -0.5),
        'b_mod': nrm(ks[7], (DEPTH, 6 * D_MODEL), 0.02),
        'norm1_g': 1.0 + nrm(ks[8], (DEPTH, D_MODEL), 0.02),
        'norm2_g': 1.0 + nrm(ks[9], (DEPTH, D_MODEL), 0.02),
        'w_in': nrm(ks[10], (DEPTH, D_MODEL, IN_COLS), D_MODEL ** -0.5),
        'q_norm_g': 1.0 + nrm(ks[11], (DEPTH, HEAD_DIM), 0.02),
        'k_norm_g': 1.0 + nrm(ks[12], (DEPTH, HEAD_DIM), 0.02),
        'conv_dw_w': nrm(ks[13], (DEPTH, CONV_K, CONV_CH), CONV_K ** -0.5),
        'conv_dw_b': nrm(ks[14], (DEPTH, CONV_CH), 0.02),
        'conv_ln_g': 1.0 + nrm(ks[15], (DEPTH, CONV_CH), 0.02),
        'conv_ln_b': nrm(ks[16], (DEPTH, CONV_CH), 0.02),
        'w_out': nrm(ks[17], (DEPTH, MIX_WIDTH, D_MODEL), MIX_WIDTH ** -0.5),
        'w_router': nrm(ks[18], (D_MODEL, N_EXPERTS), D_MODEL ** -0.5),
        'b_router': nrm(ks[19], (N_EXPERTS,), 0.01),
        'w_gate': nrm(ks[20], (DEPTH, N_EXPERTS, D_MODEL, D_FF_EXPERT), D_MODEL ** -0.5),
        'w_up': nrm(ks[21], (DEPTH, N_EXPERTS, D_MODEL, D_FF_EXPERT), D_MODEL ** -0.5),
        'w_down': nrm(ks[22], (DEPTH, N_EXPERTS, D_FF_EXPERT, D_MODEL), D_FF_EXPERT ** -0.5),
    }


def reference(x_prompt, x_sample, cache_k, cache_v, c, c_ctx, w_mod, b_mod, norm1_g, norm2_g,
              w_in, q_norm_g, k_norm_g, conv_dw_w, conv_dw_b, conv_ln_g, conv_ln_b, w_out,
              w_router, b_router, w_gate, w_up, w_down):
    x = x_prompt
    ks_list, vs_list = [], []
    for l in range(DEPTH):
        mod = modulation(c_ctx[None, :], w_mod[l], b_mod[l])
        x, k, v = trunk_layer(x, mod, norm1_g[l], norm2_g[l], w_in[l], q_norm_g[l], k_norm_g[l],
                              conv_dw_w[l], conv_dw_b[l], conv_ln_g[l], conv_ln_b[l], w_out[l],
                              w_router, b_router, w_gate[l], w_up[l], w_down[l])
        ks_list.append(k)
        vs_list.append(v)
    y_prompt = x
    new_cache_k = jnp.stack(ks_list, axis=1)
    new_cache_v = jnp.stack(vs_list, axis=1)

    rope = axial_rope_tables(x_sample.shape[1])
    z = x_sample
    for l in range(DEPTH):
        mod = modulation(c, w_mod[l], b_mod[l])
        z, _, _ = trunk_layer(z, mod, norm1_g[l], norm2_g[l], w_in[l], q_norm_g[l], k_norm_g[l],
                              conv_dw_w[l], conv_dw_b[l], conv_ln_g[l], conv_ln_b[l], w_out[l],
                              w_router, b_router, w_gate[l], w_up[l], w_down[l],
                              rope=rope, ctx_kv=(cache_k[:, l], cache_v[:, l]))
    y_sample = z
    return (y_prompt, y_sample, new_cache_k, new_cache_v)
```

```python
import functools
import math
from typing import NamedTuple

import jax
import jax.numpy as jnp
from jax import lax
from jax.experimental import pallas as pl
from jax.experimental.pallas import tpu as pltpu

F32 = jnp.float32
BF16 = jnp.bfloat16

D_MODEL = 1024
HEAD_DIM = 64
N_Q_HEADS = 8
N_KV_HEADS = 2
Q_PER_KV = N_Q_HEADS // N_KV_HEADS
Q_W = N_Q_HEADS * HEAD_DIM
KV_W = N_KV_HEADS * HEAD_DIM
QK_W = Q_W + KV_W
CONV_CH = D_MODEL // 2
CONV_K = 31
CONV_PAD = CONV_K // 2
IN_COLS = Q_W + 2 * KV_W + 2 * CONV_CH
GRID_W = 64
AXIS_PAIRS = HEAD_DIM // 4
ROPE_THETA = 10000.0
N_EXPERTS = 16
N_GROUPS = 4
EXPERTS_PER_GROUP = N_EXPERTS // N_GROUPS
D_FF = 512
EPS = 1e-6

LANES = 128
BF16_ROWS = 16
TM = 512
LT = 640
MOE_TM = 512
CONV_ROWS = 256
HALO = 16
GATE_W = LANES
XS_W = D_MODEL + GATE_W
SEG_SIZES = (512, 256, 128, 64, 32, 16)
Q_SCALE = (1.0 / math.sqrt(HEAD_DIM)) * math.log2(math.e)
VMEM_LIMIT = 48 * 1024 * 1024


class Cfg(NamedTuple):
    ctx_batch: int
    ctx_seq: int
    smp_batch: int
    smp_seq: int
    past: int
    depth: int

    @property
    def t_ctx(self):
        return self.ctx_batch * self.ctx_seq

    @property
    def t_all(self):
        return self.t_ctx + self.smp_batch * self.smp_seq

    @property
    def n_tiles(self):
        return self.t_all // TM

    @property
    def n_ctx_tiles(self):
        return self.t_ctx // TM

    @property
    def smp_tiles(self):
        return self.smp_seq // TM

    @property
    def moe_tiles(self):
        rows = self.t_all + self.n_tiles * N_GROUPS * (BF16_ROWS - 1) + N_GROUPS * (MOE_TM - 1)
        return -(-rows // MOE_TM)


def _mod_row(cfg, i):
    return jnp.where(i < cfg.n_ctx_tiles, 0, 1 + (i - cfg.n_ctx_tiles) // cfg.smp_tiles)


def _cparams(sem, vmem=VMEM_LIMIT):
    return pltpu.CompilerParams(dimension_semantics=sem, vmem_limit_bytes=vmem)


def _dot(a, b):
    return jnp.dot(a, b, preferred_element_type=F32)


def _dot_nt(a, b):
    return lax.dot_general(a, b, (((1,), (1,)), ((), ())), preferred_element_type=F32)


def _sigmoid(x):
    return 1.0 / (1.0 + jnp.exp(-x))


def _mod_kernel(c_ref, w_ref, b_ref, o_ref):
    c = c_ref[...]
    a = (c * _sigmoid(c)).astype(BF16)
    o_ref[0] = _dot(a, w_ref[0].astype(BF16)) + b_ref[0]


def _modulation(cvec, w_mod, b_mod):
    depth = w_mod.shape[0]
    out = pl.pallas_call(
        _mod_kernel,
        grid=(depth, 6),
        in_specs=[
            pl.BlockSpec((8, D_MODEL), lambda l, j: (0, 0)),
            pl.BlockSpec((1, D_MODEL, D_MODEL), lambda l, j: (l, 0, j)),
            pl.BlockSpec((1, 1, D_MODEL), lambda l, j: (l, 0, j)),
        ],
        out_specs=pl.BlockSpec((1, 8, D_MODEL), lambda l, j: (l, 0, j)),
        out_shape=jax.ShapeDtypeStruct((depth, 8, 6 * D_MODEL), F32),
        compiler_params=_cparams(("arbitrary", "arbitrary")),
        name="modulation",
    )(cvec, w_mod, b_mod.reshape(depth, 1, 6 * D_MODEL))
    return out.reshape(depth, 8, 6, D_MODEL)


def _inproj_kernel(x_ref, mod_ref, n1_ref, w_ref, gain_ref, bdq_ref, bdk_ref, cos_ref, sin_ref,
                   q_ref, k_ref, v_ref, z_ref, *, n_ctx_tiles):
    i = pl.program_id(0)
    x = x_ref[...]
    mod = mod_ref[0, 0]
    sh1, sc1 = mod[0:1], mod[1:2]
    ms = jnp.mean(x * x, axis=-1, keepdims=True)
    h = (x * lax.rsqrt(ms + EPS) * n1_ref[0]) * (1.0 + sc1) + sh1
    proj = _dot(h.astype(BF16), w_ref[0])

    qk = proj[:, :QK_W]
    sq = (qk * qk).astype(BF16)
    half = Q_W // 2
    msq = jnp.concatenate([_dot(sq[:, :half], bdq_ref[...]),
                           _dot(sq[:, half:Q_W], bdq_ref[...]),
                           _dot(sq[:, Q_W:], bdk_ref[...])], axis=1)
    qkn = qk * lax.rsqrt(msq + EPS) * gain_ref[0]

    v_ref[...] = proj[:, QK_W:QK_W + KV_W]
    a = proj[:, QK_W + KV_W:QK_W + KV_W + CONV_CH]
    b = proj[:, QK_W + KV_W + CONV_CH:]
    z_ref[...] = a * _sigmoid(b)

    @pl.when(i < n_ctx_tiles)
    def _():
        q_ref[...] = (qkn[:, :Q_W] * Q_SCALE).astype(BF16)
        k_ref[...] = qkn[:, Q_W:]

    @pl.when(i >= n_ctx_tiles)
    def _():
        cos = cos_ref[...]
        sin = sin_ref[...]
        first = (lax.broadcasted_iota(jnp.int32, (TM, LANES), 1) % 32) < 16
        for c in range(QK_W // LANES):
            blk = qkn[:, c * LANES:(c + 1) * LANES]
            partner = jnp.where(first, pltpu.roll(blk, LANES - 16, 1), pltpu.roll(blk, 16, 1))
            rot = blk * cos + partner * sin
            if c < Q_W // LANES:
                q_ref[:, c * LANES:(c + 1) * LANES] = (rot * Q_SCALE).astype(BF16)
            else:
                k_ref[...] = rot


def _inproj(cfg, l, x, mod, n1, w_in, gain, bdq, bdk, cos, sin):
    t = cfg.t_all
    nct, st = cfg.n_ctx_tiles, cfg.smp_tiles
    rope_map = lambda i: (jnp.maximum(i - nct, 0) % st, 0)
    return pl.pallas_call(
        functools.partial(_inproj_kernel, n_ctx_tiles=nct),
        grid=(cfg.n_tiles,),
        in_specs=[
            pl.BlockSpec((TM, D_MODEL), lambda i: (i, 0)),
            pl.BlockSpec((1, 1, 6, D_MODEL), lambda i: (l, _mod_row(cfg, i), 0, 0)),
            pl.BlockSpec((1, 1, D_MODEL), lambda i: (l, 0, 0)),
            pl.BlockSpec((1, D_MODEL, IN_COLS), lambda i: (l, 0, 0)),
            pl.BlockSpec((1, 1, QK_W), lambda i: (l, 0, 0)),
            pl.BlockSpec((Q_W // 2, Q_W // 2), lambda i: (0, 0)),
            pl.BlockSpec((KV_W, KV_W), lambda i: (0, 0)),
            pl.BlockSpec((TM, LANES), rope_map),
            pl.BlockSpec((TM, LANES), rope_map),
        ],
        out_specs=[
            pl.BlockSpec((TM, Q_W), lambda i: (i, 0)),
            pl.BlockSpec((TM, KV_W), lambda i: (i, 0)),
            pl.BlockSpec((TM, KV_W), lambda i: (i, 0)),
            pl.BlockSpec((TM, CONV_CH), lambda i: (i, 0)),
        ],
        out_shape=[
            jax.ShapeDtypeStruct((t, Q_W), BF16),
            jax.ShapeDtypeStruct((t, KV_W), F32),
            jax.ShapeDtypeStruct((t, KV_W), F32),
            jax.ShapeDtypeStruct((t, CONV_CH), F32),
        ],
        compiler_params=_cparams(("arbitrary",)),
        name="inproj",
    )(x, mod, n1, w_in, gain, bdq, bdk, cos, sin)


def _attn_kernel(*refs, rq, n_past, first_axis):
    if n_past:
        q_ref, k_ref, v_ref, ck_ref, cv_ref, o_ref, krep, vrep = refs
    else:
        q_ref, k_ref, v_ref, o_ref, krep, vrep = refs
    head_w = Q_PER_KV * HEAD_DIM

    @pl.when(pl.program_id(first_axis) == 0)
    def _():
        src = lax.broadcasted_iota(jnp.int32, (KV_W, head_w), 0)
        dst = lax.broadcasted_iota(jnp.int32, (KV_W, head_w), 1) % HEAD_DIM
        for h in range(N_KV_HEADS):
            rep = jnp.where(src == dst + h * HEAD_DIM, 1.0, 0.0).astype(BF16)
            n_new = k_ref.shape[0]
            if n_past:
                krep[h, :n_past, :] = _dot(ck_ref[0, 0].astype(BF16), rep).astype(BF16)
                vrep[h, :n_past, :] = _dot(cv_ref[0, 0].astype(BF16), rep).astype(BF16)
            krep[h, n_past:n_past + n_new, :] = _dot(k_ref[...].astype(BF16), rep).astype(BF16)
            vrep[h, n_past:n_past + n_new, :] = _dot(v_ref[...].astype(BF16), rep).astype(BF16)

    lane_head = lax.broadcasted_iota(jnp.int32, (rq, head_w), 1) // HEAD_DIM
    for h in range(N_KV_HEADS):
        qh = q_ref[:, h * head_w:(h + 1) * head_w]
        kh = krep[h]
        vh = vrep[h]
        acc = jnp.zeros((rq, head_w), F32)
        for g in range(Q_PER_KV):
            mask = lane_head == g
            qm = jnp.where(mask, qh, jnp.zeros_like(qh))
            s = _dot_nt(qm, kh)
            m = jnp.max(s, axis=-1, keepdims=True)
            p = jnp.exp2(s - m)
            denom = jnp.sum(p, axis=-1, keepdims=True)
            o = _dot(p.astype(BF16), vh)
            acc = jnp.where(mask, o * (1.0 / denom), acc)
        o_ref[:, h * head_w:(h + 1) * head_w] = acc.astype(BF16)


def _attention(cfg, l, q, k, v, cache_k, cache_v):
    head_w = Q_PER_KV * HEAD_DIM
    sc = cfg.ctx_seq
    a_ctx = pl.pallas_call(
        functools.partial(_attn_kernel, rq=sc, n_past=0, first_axis=1),
        grid=(cfg.ctx_batch, 1),
        in_specs=[
            pl.BlockSpec((sc, Q_W), lambda b, j: (b, 0)),
            pl.BlockSpec((sc, KV_W), lambda b, j: (b, 0)),
            pl.BlockSpec((sc, KV_W), lambda b, j: (b, 0)),
        ],
        out_specs=pl.BlockSpec((sc, Q_W), lambda b, j: (b, 0)),
        out_shape=jax.ShapeDtypeStruct((cfg.t_ctx, Q_W), BF16),
        scratch_shapes=[pltpu.VMEM((N_KV_HEADS, sc, head_w), BF16),
                        pltpu.VMEM((N_KV_HEADS, sc, head_w), BF16)],
        compiler_params=_cparams(("arbitrary", "arbitrary")),
        name="attn_ctx",
    )(q, k, v)
    rq = 256
    ss = cfg.smp_seq
    sk = cfg.past + ss
    kv_base = cfg.t_ctx // ss
    q_base = cfg.t_ctx // rq
    nqb = ss // rq
    a_smp = pl.pallas_call(
        functools.partial(_attn_kernel, rq=rq, n_past=cfg.past, first_axis=1),
        grid=(cfg.smp_batch, nqb),
        in_specs=[
            pl.BlockSpec((rq, Q_W), lambda b, j: (q_base + b * nqb + j, 0)),
            pl.BlockSpec((ss, KV_W), lambda b, j: (kv_base + b, 0)),
            pl.BlockSpec((ss, KV_W), lambda b, j: (kv_base + b, 0)),
            pl.BlockSpec((1, 1, cfg.past, KV_W), lambda b, j: (b, l, 0, 0)),
            pl.BlockSpec((1, 1, cfg.past, KV_W), lambda b, j: (b, l, 0, 0)),
        ],
        out_specs=pl.BlockSpec((rq, Q_W), lambda b, j: (b * nqb + j, 0)),
        out_shape=jax.ShapeDtypeStruct((cfg.smp_batch * ss, Q_W), BF16),
        scratch_shapes=[pltpu.VMEM((N_KV_HEADS, sk, head_w), BF16),
                        pltpu.VMEM((N_KV_HEADS, sk, head_w), BF16)],
        compiler_params=_cparams(("arbitrary", "arbitrary")),
        name="attn_smp",
    )(q, k, v, cache_k, cache_v)
    return a_ctx, a_smp


def _conv_kernel(z_ref, zp_ref, zn_ref, w_ref, b_ref, g_ref, beta_ref, o_ref, pad, *, n_ctx_blocks, seq_blocks):
    i = pl.program_id(0)
    j = jnp.maximum(i - n_ctx_blocks, 0) % seq_blocks
    latent = i >= n_ctx_blocks
    has_prev = jnp.logical_and(latent, j > 0)
    has_next = jnp.logical_and(latent, j < seq_blocks - 1)
    pad[0:HALO, :] = jnp.where(has_prev, zp_ref[...], 0.0)
    pad[HALO:HALO + CONV_ROWS, :] = z_ref[...]
    pad[HALO + CONV_ROWS:, :] = jnp.where(has_next, zn_ref[...], 0.0)
    w = w_ref[0]
    acc = jnp.zeros((CONV_ROWS, CONV_CH), F32)
    for kk in range(CONV_K):
        off = HALO - CONV_PAD + kk
        acc = acc + pad[off:off + CONV_ROWS, :] * w[kk:kk + 1, :]
    y = acc + b_ref[0]
    mu = jnp.mean(y, axis=-1, keepdims=True)
    yc = y - mu
    var = jnp.mean(yc * yc, axis=-1, keepdims=True)
    yn = yc * lax.rsqrt(var + EPS) * g_ref[0] + beta_ref[0]
    o_ref[...] = (yn * _sigmoid(yn)).astype(BF16)


def _conv(cfg, l, z, dw_w, dw_b, ln_g, ln_b):
    t = cfg.t_all
    nb = t // CONV_ROWS
    hb = CONV_ROWS // HALO
    last = t // HALO - 1
    vec = lambda: pl.BlockSpec((1, 1, CONV_CH), lambda i: (l, 0, 0))
    return pl.pallas_call(
        functools.partial(_conv_kernel, n_ctx_blocks=cfg.t_ctx // CONV_ROWS, seq_blocks=cfg.smp_seq // CONV_ROWS),
        grid=(nb,),
        in_specs=[
            pl.BlockSpec((CONV_ROWS, CONV_CH), lambda i: (i, 0)),
            pl.BlockSpec((HALO, CONV_CH), lambda i: (jnp.maximum(i * hb - 1, 0), 0)),
            pl.BlockSpec((HALO, CONV_CH), lambda i: (jnp.minimum(i * hb + hb, last), 0)),
            pl.BlockSpec((1, 32, CONV_CH), lambda i: (l, 0, 0)),
            vec(), vec(), vec(),
        ],
        out_specs=pl.BlockSpec((CONV_ROWS, CONV_CH), lambda i: (i, 0)),
        out_shape=jax.ShapeDtypeStruct((t, CONV_CH), BF16),
        scratch_shapes=[pltpu.VMEM((CONV_ROWS + 2 * HALO, CONV_CH), F32)],
        compiler_params=_cparams(("arbitrary",)),
        name="conv",
    )(z, z, z, dw_w, dw_b, ln_g, ln_b)


def _beats(ai, aj, i_first):
    return jnp.where((ai >= aj) if i_first else (ai > aj), 1.0, 0.0)


def _split_bf16(x):
    hi = x.astype(BF16)
    lo = (x - hi.astype(F32)).astype(BF16)
    return hi, lo


def _outproj_kernel(ac_ref, as_ref, c_ref, x_ref, mod_ref, n2_ref, wo_ref, wr_ref, br_ref,
                    xm_ref, hl_ref, lp_ref, cnt_ref, *, n_ctx_tiles):
    attn = jnp.where(pl.program_id(0) < n_ctx_tiles, ac_ref[...], as_ref[...])
    ac = jnp.concatenate([attn, c_ref[...]], axis=1)
    y = _dot(ac, wo_ref[0])
    mod = mod_ref[0, 0]
    g1, sh2, sc2 = mod[2:3], mod[3:4], mod[4:5]
    xm = x_ref[...] + g1 * y
    xm_ref[...] = xm
    ms = jnp.mean(xm * xm, axis=-1, keepdims=True)
    h2 = (xm * lax.rsqrt(ms + EPS) * n2_ref[0]) * (1.0 + sc2) + sh2
    h_hi, h_lo = _split_bf16(h2)

    w_hi, w_lo = _split_bf16(wr_ref[...])
    logits = _dot(h_hi, w_hi) + (_dot(h_hi, w_lo) + _dot(h_lo, w_hi))
    lt = logits.T
    s_all = _sigmoid(lt[0:N_EXPERTS, :])
    sb_all = s_all + br_ref[...]
    s = [s_all[e:e + 1, :] for e in range(N_EXPERTS)]
    sb = [sb_all[e:e + 1, :] for e in range(N_EXPERTS)]

    sel = []
    gscore = []
    for g in range(N_GROUPS):
        base = g * EXPERTS_PER_GROUP
        gs = jnp.zeros((1, TM), F32)
        for j in range(EXPERTS_PER_GROUP):
            rank = jnp.zeros((1, TM), F32)
            for i in range(EXPERTS_PER_GROUP):
                if i != j:
                    rank = rank + _beats(sb[base + i], sb[base + j], i < j)
            sj = rank < 1.5
            sel.append(sj)
            gs = gs + jnp.where(sj, sb[base + j], 0.0)
        gscore.append(gs)
    chosen = []
    for g in range(N_GROUPS):
        rank = jnp.zeros((1, TM), F32)
        for i in range(N_GROUPS):
            if i != g:
                rank = rank + _beats(gscore[i], gscore[g], i < g)
        chosen.append(rank < 0.5)

    gates = []
    for j in range(EXPERTS_PER_GROUP):
        gj = jnp.zeros((1, TM), F32)
        for g in range(N_GROUPS):
            e = g * EXPERTS_PER_GROUP + j
            gj = gj + jnp.where(chosen[g], jnp.where(sel[e], s[e], 0.0), 0.0)
        gates.append(gj)
    denom = gates[0] + gates[1] + gates[2] + gates[3]
    inv = 1.0 / denom
    gates = [gj * inv for gj in gates]

    cgf = [jnp.where(c, 1.0, 0.0) for c in chosen]
    onehot_g = jnp.concatenate(cgf + [jnp.zeros((8 - N_GROUPS, TM), F32)], axis=0).astype(BF16)
    before = (lax.broadcasted_iota(jnp.int32, (TM, TM), 0) < lax.broadcasted_iota(jnp.int32, (TM, TM), 1))
    prior = _dot(onehot_g, jnp.where(before, 1.0, 0.0).astype(BF16))
    lp = jnp.zeros((1, TM), F32)
    start = jnp.zeros((1, 1), F32)
    counts = []
    for g in range(N_GROUPS):
        n_g = jnp.sum(cgf[g], axis=-1, keepdims=True)
        counts.append(n_g)
        lp = lp + cgf[g] * (start + prior[g:g + 1, :])
        start = start + jnp.floor((n_g + (BF16_ROWS - 1)) * (1.0 / BF16_ROWS)) * BF16_ROWS
    lp_i = lp.astype(jnp.int32)
    perm = jnp.where(lax.broadcasted_iota(jnp.int32, (LT, TM), 0) == lp_i, 1.0, 0.0).astype(BF16)

    hl_ref[0, :, :D_MODEL] = _dot(perm, h_hi).astype(BF16)
    g_hi_lo = []
    for part in range(2):
        for gj in gates:
            g_hi_lo.append(_split_bf16(gj)[part].astype(F32))
    gate_rows = jnp.concatenate(g_hi_lo + [jnp.zeros((GATE_W - 8, TM), F32)], axis=0).astype(BF16)
    hl_ref[0, :, D_MODEL:] = _dot_nt(perm, gate_rows).astype(BF16)

    lp_hi = jnp.floor(lp * (1.0 / 32.0))
    lp_parts = jnp.concatenate([lp_hi, lp - 32.0 * lp_hi, jnp.zeros((LANES - 2, TM), F32)], axis=0).astype(BF16)
    eye = jnp.where(lax.broadcasted_iota(jnp.int32, (TM, TM), 0) == lax.broadcasted_iota(jnp.int32, (TM, TM), 1),
                    1.0, 0.0).astype(BF16)
    lp_ref[...] = _dot_nt(eye, lp_parts)

    cnt = jnp.concatenate([jnp.broadcast_to(c, (1, LANES)) for c in counts]
                          + [jnp.zeros((8 - N_GROUPS, LANES), F32)], axis=0)
    cnt_ref[0] = cnt.astype(jnp.int32)


def _outproj(cfg, l, a_ctx, a_smp, c, x, mod, n2, w_out, w_router, b_router):
    t, nt, nct = cfg.t_all, cfg.n_tiles, cfg.n_ctx_tiles
    return pl.pallas_call(
        functools.partial(_outproj_kernel, n_ctx_tiles=nct),
        grid=(nt,),
        in_specs=[
            pl.BlockSpec((TM, Q_W), lambda i: (jnp.minimum(i, nct - 1), 0)),
            pl.BlockSpec((TM, Q_W), lambda i: (jnp.maximum(i - nct, 0), 0)),
            pl.BlockSpec((TM, CONV_CH), lambda i: (i, 0)),
            pl.BlockSpec((TM, D_MODEL), lambda i: (i, 0)),
            pl.BlockSpec((1, 1, 6, D_MODEL), lambda i: (l, _mod_row(cfg, i), 0, 0)),
            pl.BlockSpec((1, 1, D_MODEL), lambda i: (l, 0, 0)),
            pl.BlockSpec((1, D_MODEL, D_MODEL), lambda i: (l, 0, 0)),
            pl.BlockSpec((D_MODEL, LANES), lambda i: (0, 0)),
            pl.BlockSpec((N_EXPERTS, 1), lambda i: (0, 0)),
        ],
        out_specs=[
            pl.BlockSpec((TM, D_MODEL), lambda i: (i, 0)),
            pl.BlockSpec((1, LT, XS_W), lambda i: (i, 0, 0)),
            pl.BlockSpec((TM, LANES), lambda i: (i, 0)),
            pl.BlockSpec((1, 8, LANES), lambda i: (i, 0, 0)),
        ],
        out_shape=[
            jax.ShapeDtypeStruct((t, D_MODEL), F32),
            jax.ShapeDtypeStruct((nt, LT, XS_W), BF16),
            jax.ShapeDtypeStruct((t, LANES), F32),
            jax.ShapeDtypeStruct((nt, 8, LANES), jnp.int32),
        ],
        compiler_params=_cparams(("arbitrary",)),
        name="outproj",
    )(a_ctx, a_smp, c, x, mod, n2, w_out, w_router, b_router)


def _shuffle_kernel(so_ref, do_ref, ln_ref, src_ref, init_ref, dst_ref, sem, *, n_seg):
    del init_ref

    def pieces(seg, fn):
        so, do, ln = so_ref[seg], do_ref[seg], ln_ref[seg]
        for size in SEG_SIZES:
            off = ln & ~(2 * size - 1)

            @pl.when((ln & size) != 0)
            def _():
                fn(pltpu.make_async_copy(
                    src_ref.at[pl.ds(pl.multiple_of(so + off, BF16_ROWS), size)],
                    dst_ref.at[pl.ds(pl.multiple_of(do + off, BF16_ROWS), size)],
                    sem))

    def start(seg, carry):
        pieces(seg, lambda cp: cp.start())
        return carry

    def wait(seg, carry):
        pieces(seg, lambda cp: cp.wait())
        return carry

    lax.fori_loop(0, n_seg, start, 0)
    lax.fori_loop(0, n_seg, wait, 0)


def _shuffle(src, dst_init, src_off, dst_off, lens, name):
    n_seg = lens.shape[0]
    return pl.pallas_call(
        functools.partial(_shuffle_kernel, n_seg=n_seg),
        grid_spec=pltpu.PrefetchScalarGridSpec(
            num_scalar_prefetch=3,
            grid=(1,),
            in_specs=[pl.BlockSpec(memory_space=pl.ANY), pl.BlockSpec(memory_space=pl.ANY)],
            out_specs=pl.BlockSpec(memory_space=pl.ANY),
            scratch_shapes=[pltpu.SemaphoreType.DMA(())],
        ),
        out_shape=jax.ShapeDtypeStruct(dst_init.shape, dst_init.dtype),
        input_output_aliases={4: 0},
        compiler_params=_cparams(("arbitrary",)),
        name=name,
    )(src_off, dst_off, lens, src, dst_init)


def _moe_kernel(tg_ref, nv_ref, x_ref, wg_ref, wu_ref, wd_ref, y_ref):
    del tg_ref

    @pl.when(pl.program_id(0) >= nv_ref[0])
    def _():
        y_ref[...] = jnp.zeros_like(y_ref)

    @pl.when(pl.program_id(0) < nv_ref[0])
    def _():
        x = x_ref[:, :D_MODEL]
        gcols = x_ref[:, D_MODEL:].astype(F32)
        hid = []
        for j in range(EXPERTS_PER_GROUP):
            gate = gcols[:, j:j + 1] + gcols[:, EXPERTS_PER_GROUP + j:EXPERTS_PER_GROUP + j + 1]
            hg = _dot(x, wg_ref[0, j])
            hu = _dot(x, wu_ref[0, j])
            hid.append(((hg * _sigmoid(hg)) * hu * gate).astype(BF16))
        hcat = jnp.concatenate(hid, axis=1)
        wd = wd_ref[0].reshape(EXPERTS_PER_GROUP * D_FF, D_MODEL)
        y_ref[...] = _dot(hcat, wd).astype(BF16)


def _moe(cfg, l, xs, tile_group, n_valid, w_gate, w_up, w_down):
    nmt = cfg.moe_tiles

    def row_map(i, tg, nv):
        return (jnp.minimum(i, nv[0] - 1), 0)

    def w_map(i, tg, nv):
        return (l * N_GROUPS + tg[jnp.minimum(i, nv[0] - 1)], 0, 0, 0)

    return pl.pallas_call(
        _moe_kernel,
        grid_spec=pltpu.PrefetchScalarGridSpec(
            num_scalar_prefetch=2,
            grid=(nmt,),
            in_specs=[
                pl.BlockSpec((MOE_TM, XS_W), row_map),
                pl.BlockSpec((1, EXPERTS_PER_GROUP, D_MODEL, D_FF), w_map),
                pl.BlockSpec((1, EXPERTS_PER_GROUP, D_MODEL, D_FF), w_map),
                pl.BlockSpec((1, EXPERTS_PER_GROUP, D_FF, D_MODEL), w_map),
            ],
            out_specs=pl.BlockSpec((MOE_TM, D_MODEL), lambda i, tg, nv: (i, 0)),
        ),
        out_shape=jax.ShapeDtypeStruct((nmt * MOE_TM, D_MODEL), BF16),
        compiler_params=_cparams(("arbitrary",)),
        name="moe",
    )(tile_group, n_valid, xs, w_gate, w_up, w_down)


def _combine_kernel(xm_ref, yl_ref, lp_ref, mod_ref, o_ref):
    lp = lp_ref[...]
    pos = (lp[:, 0:1] * 32.0 + lp[:, 1:2]).astype(jnp.int32)
    perm_t = jnp.where(lax.broadcasted_iota(jnp.int32, (TM, LT), 1) == pos, 1.0, 0.0).astype(BF16)
    y = _dot(perm_t, yl_ref[0])
    g2 = mod_ref[0, 0][5:6]
    o_ref[...] = xm_ref[...] + g2 * y


def _combine(cfg, l, xm, yl, lp, mod):
    return pl.pallas_call(
        _combine_kernel,
        grid=(cfg.n_tiles,),
        in_specs=[
            pl.BlockSpec((TM, D_MODEL), lambda i: (i, 0)),
            pl.BlockSpec((1, LT, D_MODEL), lambda i: (i, 0, 0)),
            pl.BlockSpec((TM, LANES), lambda i: (i, 0)),
            pl.BlockSpec((1, 1, 6, D_MODEL), lambda i: (l, _mod_row(cfg, i), 0, 0)),
        ],
        out_specs=pl.BlockSpec((TM, D_MODEL), lambda i: (i, 0)),
        out_shape=jax.ShapeDtypeStruct((cfg.t_all, D_MODEL), F32),
        compiler_params=_cparams(("arbitrary",)),
        name="combine",
    )(xm, yl, lp, mod)


def _plan(cfg, counts):
    nt = cfg.n_tiles
    m = (counts + (BF16_ROWS - 1)) // BF16_ROWS * BF16_ROWS
    local_off = jnp.cumsum(m, axis=1) - m
    within_group = jnp.cumsum(m, axis=0) - m
    group_rows = jnp.sum(m, axis=0)
    group_tiles = (group_rows + (MOE_TM - 1)) // MOE_TM
    tiles_end = jnp.cumsum(group_tiles)
    group_start = (tiles_end - group_tiles) * MOE_TM
    sorted_off = (group_start[None, :] + within_group).reshape(-1).astype(jnp.int32)
    tiled_off = (jnp.arange(nt, dtype=jnp.int32)[:, None] * LT + local_off).reshape(-1).astype(jnp.int32)
    lens = m.reshape(-1).astype(jnp.int32)
    n_valid = tiles_end[-1:].astype(jnp.int32)
    tile_ids = jnp.arange(cfg.moe_tiles, dtype=jnp.int32)
    tile_group = jnp.minimum(jnp.sum(tile_ids[:, None] >= tiles_end[None, :], axis=1), N_GROUPS - 1).astype(jnp.int32)
    return tiled_off, sorted_off, lens, tile_group, n_valid


def _rope_tables(n):
    pos = jnp.arange(n)
    rc = jnp.stack([(pos // GRID_W).astype(F32), (pos % GRID_W).astype(F32)], axis=1)
    freqs = ROPE_THETA ** (-jnp.arange(AXIS_PAIRS, dtype=F32) / AXIS_PAIRS)
    ang = rc[:, :, None] * freqs[None, None, :]
    cos = jnp.cos(ang)[:, :, None, :]
    sin = jnp.sin(ang)[:, :, None, :] * jnp.array([-1.0, 1.0], F32)[None, None, :, None]
    cos = jnp.broadcast_to(cos, (n, 2, 2, AXIS_PAIRS)).reshape(n, HEAD_DIM)
    sin = jnp.broadcast_to(sin, (n, 2, 2, AXIS_PAIRS)).reshape(n, HEAD_DIM)
    reps = LANES // HEAD_DIM
    return jnp.tile(cos, (1, reps)), jnp.tile(sin, (1, reps))


def _block_diag_mean(width):
    r = jnp.arange(width)
    return jnp.where((r[:, None] // HEAD_DIM) == (r[None, :] // HEAD_DIM), 1.0 / HEAD_DIM, 0.0).astype(BF16)


def _forward(cfg, x_prompt, x_sample, cache_k, cache_v, c, c_ctx, w_mod, b_mod, norm1_g, norm2_g,
             w_in, q_norm_g, k_norm_g, conv_dw_w, conv_dw_b, conv_ln_g, conv_ln_b, w_out,
             w_router, b_router, w_gate, w_up, w_down):
    depth = cfg.depth
    assert cfg.ctx_seq == CONV_ROWS and cfg.t_ctx % TM == 0 and cfg.smp_seq % TM == 0
    assert cfg.t_ctx % cfg.smp_seq == 0 and cfg.smp_batch + 1 <= 8

    x = jnp.concatenate([x_prompt.reshape(cfg.t_ctx, D_MODEL), x_sample.reshape(-1, D_MODEL)], axis=0)
    cvec = jnp.zeros((8, D_MODEL), F32).at[0].set(c_ctx).at[1:1 + cfg.smp_batch].set(c)
    mod = _modulation(cvec, w_mod, b_mod)

    n1 = norm1_g.reshape(depth, 1, D_MODEL)
    n2 = norm2_g.reshape(depth, 1, D_MODEL)
    gain = jnp.concatenate([jnp.tile(q_norm_g, (1, N_Q_HEADS)), jnp.tile(k_norm_g, (1, N_KV_HEADS))],
                           axis=1).reshape(depth, 1, QK_W)
    w_in_b = w_in.astype(BF16)
    w_out_b = w_out.astype(BF16)
    wg_b = w_gate.astype(BF16).reshape(depth * N_GROUPS, EXPERTS_PER_GROUP, D_MODEL, D_FF)
    wu_b = w_up.astype(BF16).reshape(depth * N_GROUPS, EXPERTS_PER_GROUP, D_MODEL, D_FF)
    wd_b = w_down.astype(BF16).reshape(depth * N_GROUPS, EXPERTS_PER_GROUP, D_FF, D_MODEL)
    dw_w = jnp.pad(conv_dw_w, ((0, 0), (0, 32 - CONV_K), (0, 0)))
    dw_b = conv_dw_b.reshape(depth, 1, CONV_CH)
    ln_g = conv_ln_g.reshape(depth, 1, CONV_CH)
    ln_b = conv_ln_b.reshape(depth, 1, CONV_CH)
    wr = jnp.pad(w_router, ((0, 0), (0, LANES - N_EXPERTS)))
    br = b_router.reshape(N_EXPERTS, 1)
    cos, sin = _rope_tables(cfg.smp_seq)
    bdq = _block_diag_mean(Q_W // 2)
    bdk = _block_diag_mean(KV_W)
    ck = cache_k.reshape(cfg.smp_batch, depth, cfg.past, KV_W)
    cv = cache_v.reshape(cfg.smp_batch, depth, cfg.past, KV_W)
    xs_zero = jnp.zeros((cfg.moe_tiles * MOE_TM, XS_W), BF16)
    yl_zero = jnp.zeros((cfg.n_tiles * LT, D_MODEL), BF16)

    ks, vs = [], []
    for l in range(depth):
        q, k, v, z = _inproj(cfg, l, x, mod, n1, w_in_b, gain, bdq, bdk, cos, sin)
        ks.append(k[:cfg.t_ctx])
        vs.append(v[:cfg.t_ctx])
        a_ctx, a_smp = _attention(cfg, l, q, k, v, ck, cv)
        cv_out = _conv(cfg, l, z, dw_w, dw_b, ln_g, ln_b)
        xm, hl, lp, cnt = _outproj(cfg, l, a_ctx, a_smp, cv_out, x, mod, n2, w_out_b, wr, br)
        tiled_off, sorted_off, lens, tile_group, n_valid = _plan(cfg, cnt[:, :N_GROUPS, 0])
        xs = _shuffle(hl.reshape(cfg.n_tiles * LT, XS_W), xs_zero, tiled_off, sorted_off, lens, "dispatch")
        ys = _moe(cfg, l, xs, tile_group, n_valid, wg_b, wu_b, wd_b)
        yl = _shuffle(ys, yl_zero, sorted_off, tiled_off, lens, "gather_back")
        x = _combine(cfg, l, xm, yl.reshape(cfg.n_tiles, LT, D_MODEL), lp, mod)

    y_prompt = x[:cfg.t_ctx].reshape(cfg.ctx_batch, cfg.ctx_seq, D_MODEL)
    y_sample = x[cfg.t_ctx:].reshape(cfg.smp_batch, cfg.smp_seq, D_MODEL)
    new_k = jnp.stack(ks, axis=0).reshape(depth, cfg.ctx_batch, cfg.ctx_seq, N_KV_HEADS, HEAD_DIM)
    new_v = jnp.stack(vs, axis=0).reshape(depth, cfg.ctx_batch, cfg.ctx_seq, N_KV_HEADS, HEAD_DIM)
    return y_prompt, y_sample, jnp.swapaxes(new_k, 0, 1), jnp.swapaxes(new_v, 0, 1)


def kernel(x_prompt, x_sample, cache_k, cache_v, c, c_ctx, w_mod, b_mod, norm1_g, norm2_g, w_in, q_norm_g,
           k_norm_g, conv_dw_w, conv_dw_b, conv_ln_g, conv_ln_b, w_out, w_router, b_router, w_gate, w_up, w_down):
    cfg = Cfg(ctx_batch=x_prompt.shape[0], ctx_seq=x_prompt.shape[1], smp_batch=x_sample.shape[0],
              smp_seq=x_sample.shape[1], past=cache_k.shape[2], depth=w_mod.shape[0])
    return _forward(cfg, x_prompt, x_sample, cache_k, cache_v, c, c_ctx, w_mod, b_mod, norm1_g, norm2_g,
                    w_in, q_norm_g, k_norm_g, conv_dw_w, conv_dw_b, conv_ln_g, conv_ln_b, w_out,
                    w_router, b_router, w_gate, w_up, w_down)
```

```python
import functools
import math
from typing import NamedTuple

import jax
import jax.numpy as jnp
from jax import lax
from jax.experimental import pallas as pl
from jax.experimental.pallas import tpu as pltpu

F32 = jnp.float32
BF16 = jnp.bfloat16

D_MODEL = 1024
HEAD_DIM = 64
N_Q_HEADS = 8
N_KV_HEADS = 2
Q_PER_KV = N_Q_HEADS // N_KV_HEADS
Q_W = N_Q_HEADS * HEAD_DIM
KV_W = N_KV_HEADS * HEAD_DIM
QK_W = Q_W + KV_W
CONV_CH = D_MODEL // 2
CONV_K = 31
CONV_PAD = CONV_K // 2
IN_COLS = Q_W + 2 * KV_W + 2 * CONV_CH
GRID_W = 64
AXIS_PAIRS = HEAD_DIM // 4
ROPE_THETA = 10000.0
N_EXPERTS = 16
N_GROUPS = 4
EXPERTS_PER_GROUP = N_EXPERTS // N_GROUPS
D_FF = 512
EPS = 1e-6

LANES = 128
BF16_ROWS = 16
TM = 512
LT = 640
MOE_TM = 512
CONV_ROWS = 256
HALO = 16
GATE_W = LANES
XS_W = D_MODEL + GATE_W
SEG_SIZES = (512, 256, 128, 64, 32, 16)
Q_SCALE = (1.0 / math.sqrt(HEAD_DIM)) * math.log2(math.e)
VMEM_LIMIT = 48 * 1024 * 1024


class Cfg(NamedTuple):
    ctx_batch: int
    ctx_seq: int
    smp_batch: int
    smp_seq: int
    past: int
    depth: int

    @property
    def t_ctx(self):
        return self.ctx_batch * self.ctx_seq

    @property
    def t_all(self):
        return self.t_ctx + self.smp_batch * self.smp_seq

    @property
    def n_tiles(self):
        return self.t_all // TM

    @property
    def n_ctx_tiles(self):
        return self.t_ctx // TM

    @property
    def smp_tiles(self):
        return self.smp_seq // TM

    @property
    def moe_tiles(self):
        rows = self.t_all + self.n_tiles * N_GROUPS * (BF16_ROWS - 1) + N_GROUPS * (MOE_TM - 1)
        return -(-rows // MOE_TM)


def _mod_row(cfg, i):
    return jnp.where(i < cfg.n_ctx_tiles, 0, 1 + (i - cfg.n_ctx_tiles) // cfg.smp_tiles)


def _cparams(sem, vmem=VMEM_LIMIT):
    return pltpu.CompilerParams(dimension_semantics=sem, vmem_limit_bytes=vmem)


def _dot(a, b):
    return jnp.dot(a, b, preferred_element_type=F32)


def _dot_nt(a, b):
    return lax.dot_general(a, b, (((1,), (1,)), ((), ())), preferred_element_type=F32)


def _sigmoid(x):
    return 1.0 / (1.0 + jnp.exp(-x))


def _mod_kernel(c_ref, w_ref, b_ref, o_ref):
    c = c_ref[...]
    a = (c * _sigmoid(c)).astype(BF16)
    o_ref[0] = _dot(a, w_ref[0].astype(BF16)) + b_ref[0]


def _modulation(cvec, w_mod, b_mod):
    depth = w_mod.shape[0]
    out = pl.pallas_call(
        _mod_kernel,
        grid=(depth, 6),
        in_specs=[
            pl.BlockSpec((8, D_MODEL), lambda l, j: (0, 0)),
            pl.BlockSpec((1, D_MODEL, D_MODEL), lambda l, j: (l, 0, j)),
            pl.BlockSpec((1, 1, D_MODEL), lambda l, j: (l, 0, j)),
        ],
        out_specs=pl.BlockSpec((1, 8, D_MODEL), lambda l, j: (l, 0, j)),
        out_shape=jax.ShapeDtypeStruct((depth, 8, 6 * D_MODEL), F32),
        compiler_params=_cparams(("arbitrary", "arbitrary")),
        name="modulation",
    )(cvec, w_mod, b_mod.reshape(depth, 1, 6 * D_MODEL))
    return out.reshape(depth, 8, 6, D_MODEL)


def _inproj_kernel(x_ref, mod_ref, n1_ref, w_ref, gain_ref, bdq_ref, bdk_ref, cos_ref, sin_ref,
                   q_ref, k_ref, v_ref, z_ref, *, n_ctx_tiles):
    i = pl.program_id(0)
    x = x_ref[...]
    mod = mod_ref[0, 0]
    sh1, sc1 = mod[0:1], mod[1:2]
    ms = jnp.mean(x * x, axis=-1, keepdims=True)
    h = (x * lax.rsqrt(ms + EPS) * n1_ref[0]) * (1.0 + sc1) + sh1
    proj = _dot(h.astype(BF16), w_ref[0])

    qk = proj[:, :QK_W]
    sq = (qk * qk).astype(BF16)
    half = Q_W // 2
    msq = jnp.concatenate([_dot(sq[:, :half], bdq_ref[...]),
                           _dot(sq[:, half:Q_W], bdq_ref[...]),
                           _dot(sq[:, Q_W:], bdk_ref[...])], axis=1)
    qkn = qk * lax.rsqrt(msq + EPS) * gain_ref[0]

    v_ref[...] = proj[:, QK_W:QK_W + KV_W]
    a = proj[:, QK_W + KV_W:QK_W + KV_W + CONV_CH]
    b = proj[:, QK_W + KV_W + CONV_CH:]
    z_ref[...] = a * _sigmoid(b)

    @pl.when(i < n_ctx_tiles)
    def _():
        q_ref[...] = (qkn[:, :Q_W] * Q_SCALE).astype(BF16)
        k_ref[...] = qkn[:, Q_W:]

    @pl.when(i >= n_ctx_tiles)
    def _():
        cos = cos_ref[...]
        sin = sin_ref[...]
        first = (lax.broadcasted_iota(jnp.int32, (TM, LANES), 1) % 32) < 16
        for c in range(QK_W // LANES):
            blk = qkn[:, c * LANES:(c + 1) * LANES]
            partner = jnp.where(first, pltpu.roll(blk, LANES - 16, 1), pltpu.roll(blk, 16, 1))
            rot = blk * cos + partner * sin
            if c < Q_W // LANES:
                q_ref[:, c * LANES:(c + 1) * LANES] = (rot * Q_SCALE).astype(BF16)
            else:
                k_ref[...] = rot


def _inproj(cfg, l, x, mod, n1, w_in, gain, bdq, bdk, cos, sin):
    t = cfg.t_all
    nct, st = cfg.n_ctx_tiles, cfg.smp_tiles
    rope_map = lambda i: (jnp.maximum(i - nct, 0) % st, 0)
    return pl.pallas_call(
        functools.partial(_inproj_kernel, n_ctx_tiles=nct),
        grid=(cfg.n_tiles,),
        in_specs=[
            pl.BlockSpec((TM, D_MODEL), lambda i: (i, 0)),
            pl.BlockSpec((1, 1, 6, D_MODEL), lambda i: (l, _mod_row(cfg, i), 0, 0)),
            pl.BlockSpec((1, 1, D_MODEL), lambda i: (l, 0, 0)),
            pl.BlockSpec((1, D_MODEL, IN_COLS), lambda i: (l, 0, 0)),
            pl.BlockSpec((1, 1, QK_W), lambda i: (l, 0, 0)),
            pl.BlockSpec((Q_W // 2, Q_W // 2), lambda i: (0, 0)),
            pl.BlockSpec((KV_W, KV_W), lambda i: (0, 0)),
            pl.BlockSpec((TM, LANES), rope_map),
            pl.BlockSpec((TM, LANES), rope_map),
        ],
        out_specs=[
            pl.BlockSpec((TM, Q_W), lambda i: (i, 0)),
            pl.BlockSpec((TM, KV_W), lambda i: (i, 0)),
            pl.BlockSpec((TM, KV_W), lambda i: (i, 0)),
            pl.BlockSpec((TM, CONV_CH), lambda i: (i, 0)),
        ],
        out_shape=[
            jax.ShapeDtypeStruct((t, Q_W), BF16),
            jax.ShapeDtypeStruct((t, KV_W), F32),
            jax.ShapeDtypeStruct((t, KV_W), F32),
            jax.ShapeDtypeStruct((t, CONV_CH), F32),
        ],
        compiler_params=_cparams(("arbitrary",)),
        name="inproj",
    )(x, mod, n1, w_in, gain, bdq, bdk, cos, sin)


def _attn_kernel(*refs, rq, n_past, first_axis):
    if n_past:
        q_ref, k_ref, v_ref, ck_ref, cv_ref, o_ref, krep, vrep = refs
    else:
        q_ref, k_ref, v_ref, o_ref, krep, vrep = refs
    head_w = Q_PER_KV * HEAD_DIM

    @pl.when(pl.program_id(first_axis) == 0)
    def _():
        src = lax.broadcasted_iota(jnp.int32, (KV_W, head_w), 0)
        dst = lax.broadcasted_iota(jnp.int32, (KV_W, head_w), 1) % HEAD_DIM
        for h in range(N_KV_HEADS):
            rep = jnp.where(src == dst + h * HEAD_DIM, 1.0, 0.0).astype(BF16)
            n_new = k_ref.shape[0]
            if n_past:
                krep[h, :n_past, :] = _dot(ck_ref[0, 0].astype(BF16), rep).astype(BF16)
                vrep[h, :n_past, :] = _dot(cv_ref[0, 0].astype(BF16), rep).astype(BF16)
            krep[h, n_past:n_past + n_new, :] = _dot(k_ref[...].astype(BF16), rep).astype(BF16)
            vrep[h, n_past:n_past + n_new, :] = _dot(v_ref[...].astype(BF16), rep).astype(BF16)

    lane_head = lax.broadcasted_iota(jnp.int32, (rq, head_w), 1) // HEAD_DIM
    for h in range(N_KV_HEADS):
        qh = q_ref[:, h * head_w:(h + 1) * head_w]
        kh = krep[h]
        vh = vrep[h]
        acc = jnp.zeros((rq, head_w), F32)
        for g in range(Q_PER_KV):
            mask = lane_head == g
            qm = jnp.where(mask, qh, jnp.zeros_like(qh))
            s = _dot_nt(qm, kh)
            m = jnp.max(s, axis=-1, keepdims=True)
            p = jnp.exp2(s - m)
            denom = jnp.sum(p, axis=-1, keepdims=True)
            o = _dot(p.astype(BF16), vh)
            acc = jnp.where(mask, o * (1.0 / denom), acc)
        o_ref[:, h * head_w:(h + 1) * head_w] = acc.astype(BF16)


def _attention(cfg, l, q, k, v, cache_k, cache_v):
    head_w = Q_PER_KV * HEAD_DIM
    sc = cfg.ctx_seq
    a_ctx = pl.pallas_call(
        functools.partial(_attn_kernel, rq=sc, n_past=0, first_axis=1),
        grid=(cfg.ctx_batch, 1),
        in_specs=[
            pl.BlockSpec((sc, Q_W), lambda b, j: (b, 0)),
            pl.BlockSpec((sc, KV_W), lambda b, j: (b, 0)),
            pl.BlockSpec((sc, KV_W), lambda b, j: (b, 0)),
        ],
        out_specs=pl.BlockSpec((sc, Q_W), lambda b, j: (b, 0)),
        out_shape=jax.ShapeDtypeStruct((cfg.t_ctx, Q_W), BF16),
        scratch_shapes=[pltpu.VMEM((N_KV_HEADS, sc, head_w), BF16),
                        pltpu.VMEM((N_KV_HEADS, sc, head_w), BF16)],
        compiler_params=_cparams(("arbitrary", "arbitrary")),
        name="attn_ctx",
    )(q, k, v)
    rq = 256
    ss = cfg.smp_seq
    sk = cfg.past + ss
    kv_base = cfg.t_ctx // ss
    q_base = cfg.t_ctx // rq
    nqb = ss // rq
    a_smp = pl.pallas_call(
        functools.partial(_attn_kernel, rq=rq, n_past=cfg.past, first_axis=1),
        grid=(cfg.smp_batch, nqb),
        in_specs=[
            pl.BlockSpec((rq, Q_W), lambda b, j: (q_base + b * nqb + j, 0)),
            pl.BlockSpec((ss, KV_W), lambda b, j: (kv_base + b, 0)),
            pl.BlockSpec((ss, KV_W), lambda b, j: (kv_base + b, 0)),
            pl.BlockSpec((1, 1, cfg.past, KV_W), lambda b, j: (b, l, 0, 0)),
            pl.BlockSpec((1, 1, cfg.past, KV_W), lambda b, j: (b, l, 0, 0)),
        ],
        out_specs=pl.BlockSpec((rq, Q_W), lambda b, j: (b * nqb + j, 0)),
        out_shape=jax.ShapeDtypeStruct((cfg.smp_batch * ss, Q_W), BF16),
        scratch_shapes=[pltpu.VMEM((N_KV_HEADS, sk, head_w), BF16),
                        pltpu.VMEM((N_KV_HEADS, sk, head_w), BF16)],
        compiler_params=_cparams(("arbitrary", "arbitrary")),
        name="attn_smp",
    )(q, k, v, cache_k, cache_v)
    return a_ctx, a_smp


def _conv_kernel(z_ref, zp_ref, zn_ref, w_ref, b_ref, g_ref, beta_ref, o_ref, pad, *, n_ctx_blocks, seq_blocks):
    i = pl.program_id(0)
    j = jnp.maximum(i - n_ctx_blocks, 0) % seq_blocks
    latent = i >= n_ctx_blocks
    has_prev = jnp.logical_and(latent, j > 0)
    has_next = jnp.logical_and(latent, j < seq_blocks - 1)
    pad[0:HALO, :] = jnp.where(has_prev, zp_ref[...], 0.0)
    pad[HALO:HALO + CONV_ROWS, :] = z_ref[...]
    pad[HALO + CONV_ROWS:, :] = jnp.where(has_next, zn_ref[...], 0.0)
    w = w_ref[0]
    acc = jnp.zeros((CONV_ROWS, CONV_CH), F32)
    for kk in range(CONV_K):
        off = HALO - CONV_PAD + kk
        acc = acc + pad[off:off + CONV_ROWS, :] * w[kk:kk + 1, :]
    y = acc + b_ref[0]
    mu = jnp.mean(y, axis=-1, keepdims=True)
    yc = y - mu
    var = jnp.mean(yc * yc, axis=-1, keepdims=True)
    yn = yc * lax.rsqrt(var + EPS) * g_ref[0] + beta_ref[0]
    o_ref[...] = (yn * _sigmoid(yn)).astype(BF16)


def _conv(cfg, l, z, dw_w, dw_b, ln_g, ln_b):
    t = cfg.t_all
    nb = t // CONV_ROWS
    hb = CONV_ROWS // HALO
    last = t // HALO - 1
    vec = lambda: pl.BlockSpec((1, 1, CONV_CH), lambda i: (l, 0, 0))
    return pl.pallas_call(
        functools.partial(_conv_kernel, n_ctx_blocks=cfg.t_ctx // CONV_ROWS, seq_blocks=cfg.smp_seq // CONV_ROWS),
        grid=(nb,),
        in_specs=[
            pl.BlockSpec((CONV_ROWS, CONV_CH), lambda i: (i, 0)),
            pl.BlockSpec((HALO, CONV_CH), lambda i: (jnp.maximum(i * hb - 1, 0), 0)),
            pl.BlockSpec((HALO, CONV_CH), lambda i: (jnp.minimum(i * hb + hb, last), 0)),
            pl.BlockSpec((1, 32, CONV_CH), lambda i: (l, 0, 0)),
            vec(), vec(), vec(),
        ],
        out_specs=pl.BlockSpec((CONV_ROWS, CONV_CH), lambda i: (i, 0)),
        out_shape=jax.ShapeDtypeStruct((t, CONV_CH), BF16),
        scratch_shapes=[pltpu.VMEM((CONV_ROWS + 2 * HALO, CONV_CH), F32)],
        compiler_params=_cparams(("arbitrary",)),
        name="conv",
    )(z, z, z, dw_w, dw_b, ln_g, ln_b)


def _beats(ai, aj, i_first):
    return jnp.where((ai >= aj) if i_first else (ai > aj), 1.0, 0.0)


def _split_bf16(x):
    hi = x.astype(BF16)
    lo = (x - hi.astype(F32)).astype(BF16)
    return hi, lo


def _outproj_kernel(ac_ref, as_ref, c_ref, x_ref, mod_ref, n2_ref, wo_ref, wr_ref, br_ref,
                    xm_ref, hl_ref, lp_ref, cnt_ref, *, n_ctx_tiles):
    attn = jnp.where(pl.program_id(0) < n_ctx_tiles, ac_ref[...], as_ref[...])
    ac = jnp.concatenate([attn, c_ref[...]], axis=1)
    y = _dot(ac, wo_ref[0])
    mod = mod_ref[0, 0]
    g1, sh2, sc2 = mod[2:3], mod[3:4], mod[4:5]
    xm = x_ref[...] + g1 * y
    xm_ref[...] = xm
    ms = jnp.mean(xm * xm, axis=-1, keepdims=True)
    h2 = (xm * lax.rsqrt(ms + EPS) * n2_ref[0]) * (1.0 + sc2) + sh2
    h_hi, h_lo = _split_bf16(h2)

    w_hi, w_lo = _split_bf16(wr_ref[...])
    logits = _dot(h_hi, w_hi) + (_dot(h_hi, w_lo) + _dot(h_lo, w_hi))
    lt = logits.T
    s_all = _sigmoid(lt[0:N_EXPERTS, :])
    sb_all = s_all + br_ref[...]
    s = [s_all[e:e + 1, :] for e in range(N_EXPERTS)]
    sb = [sb_all[e:e + 1, :] for e in range(N_EXPERTS)]

    sel = []
    gscore = []
    for g in range(N_GROUPS):
        base = g * EXPERTS_PER_GROUP
        gs = jnp.zeros((1, TM), F32)
        for j in range(EXPERTS_PER_GROUP):
            rank = jnp.zeros((1, TM), F32)
            for i in range(EXPERTS_PER_GROUP):
                if i != j:
                    rank = rank + _beats(sb[base + i], sb[base + j], i < j)
            sj = rank < 1.5
            sel.append(sj)
            gs = gs + jnp.where(sj, sb[base + j], 0.0)
        gscore.append(gs)
    chosen = []
    for g in range(N_GROUPS):
        rank = jnp.zeros((1, TM), F32)
        for i in range(N_GROUPS):
            if i != g:
                rank = rank + _beats(gscore[i], gscore[g], i < g)
        chosen.append(rank < 0.5)

    gates = []
    for j in range(EXPERTS_PER_GROUP):
        gj = jnp.zeros((1, TM), F32)
        for g in range(N_GROUPS):
            e = g * EXPERTS_PER_GROUP + j
            gj = gj + jnp.where(chosen[g], jnp.where(sel[e], s[e], 0.0), 0.0)
        gates.append(gj)
    denom = gates[0] + gates[1] + gates[2] + gates[3]
    inv = 1.0 / denom
    gates = [gj * inv for gj in gates]

    cgf = [jnp.where(c, 1.0, 0.0) for c in chosen]
    onehot_g = jnp.concatenate(cgf + [jnp.zeros((8 - N_GROUPS, TM), F32)], axis=0).astype(BF16)
    before = (lax.broadcasted_iota(jnp.int32, (TM, TM), 0) < lax.broadcasted_iota(jnp.int32, (TM, TM), 1))
    prior = _dot(onehot_g, jnp.where(before, 1.0, 0.0).astype(BF16))
    lp = jnp.zeros((1, TM), F32)
    start = jnp.zeros((1, 1), F32)
    counts = []
    for g in range(N_GROUPS):
        n_g = jnp.sum(cgf[g], axis=-1, keepdims=True)
        counts.append(n_g)
        lp = lp + cgf[g] * (start + prior[g:g + 1, :])
        start = start + jnp.floor((n_g + (BF16_ROWS - 1)) * (1.0 / BF16_ROWS)) * BF16_ROWS
    lp_i = lp.astype(jnp.int32)
    perm = jnp.where(lax.broadcasted_iota(jnp.int32, (LT, TM), 0) == lp_i, 1.0, 0.0).astype(BF16)

    hl_ref[0, :, :D_MODEL] = _dot(perm, h_hi).astype(BF16)
    g_hi_lo = []
    for part in range(2):
        for gj in gates:
            g_hi_lo.append(_split_bf16(gj)[part].astype(F32))
    gate_rows = jnp.concatenate(g_hi_lo + [jnp.zeros((GATE_W - 8, TM), F32)], axis=0).astype(BF16)
    hl_ref[0, :, D_MODEL:] = _dot_nt(perm, gate_rows).astype(BF16)

    lp_hi = jnp.floor(lp * (1.0 / 32.0))
    lp_parts = jnp.concatenate([lp_hi, lp - 32.0 * lp_hi, jnp.zeros((LANES - 2, TM), F32)], axis=0).astype(BF16)
    eye = jnp.where(lax.broadcasted_iota(jnp.int32, (TM, TM), 0) == lax.broadcasted_iota(jnp.int32, (TM, TM), 1),
                    1.0, 0.0).astype(BF16)
    lp_ref[...] = _dot_nt(eye, lp_parts)

    cnt = jnp.concatenate([jnp.broadcast_to(c, (1, LANES)) for c in counts]
                          + [jnp.zeros((8 - N_GROUPS, LANES), F32)], axis=0)
    cnt_ref[0] = cnt.astype(jnp.int32)


def _outproj(cfg, l, a_ctx, a_smp, c, x, mod, n2, w_out, w_router, b_router):
    t, nt, nct = cfg.t_all, cfg.n_tiles, cfg.n_ctx_tiles
    return pl.pallas_call(
        functools.partial(_outproj_kernel, n_ctx_tiles=nct),
        grid=(nt,),
        in_specs=[
            pl.BlockSpec((TM, Q_W), lambda i: (jnp.minimum(i, nct - 1), 0)),
            pl.BlockSpec((TM, Q_W), lambda i: (jnp.maximum(i - nct, 0), 0)),
            pl.BlockSpec((TM, CONV_CH), lambda i: (i, 0)),
            pl.BlockSpec((TM, D_MODEL), lambda i: (i, 0)),
            pl.BlockSpec((1, 1, 6, D_MODEL), lambda i: (l, _mod_row(cfg, i), 0, 0)),
            pl.BlockSpec((1, 1, D_MODEL), lambda i: (l, 0, 0)),
            pl.BlockSpec((1, D_MODEL, D_MODEL), lambda i: (l, 0, 0)),
            pl.BlockSpec((D_MODEL, LANES), lambda i: (0, 0)),
            pl.BlockSpec((N_EXPERTS, 1), lambda i: (0, 0)),
        ],
        out_specs=[
            pl.BlockSpec((TM, D_MODEL), lambda i: (i, 0)),
            pl.BlockSpec((1, LT, XS_W), lambda i: (i, 0, 0)),
            pl.BlockSpec((TM, LANES), lambda i: (i, 0)),
            pl.BlockSpec((1, 8, LANES), lambda i: (i, 0, 0)),
        ],
        out_shape=[
            jax.ShapeDtypeStruct((t, D_MODEL), F32),
            jax.ShapeDtypeStruct((nt, LT, XS_W), BF16),
            jax.ShapeDtypeStruct((t, LANES), F32),
            jax.ShapeDtypeStruct((nt, 8, LANES), jnp.int32),
        ],
        compiler_params=_cparams(("arbitrary",)),
        name="outproj",
    )(a_ctx, a_smp, c, x, mod, n2, w_out, w_router, b_router)


def _segment_pieces(n, src_row, dst_row, src_ref, dst_ref, sem, fn):
    for size in SEG_SIZES:
        off = n & ~(2 * size - 1)

        @pl.when((n & size) != 0)
        def _():
            fn(pltpu.make_async_copy(
                src_ref.at[pl.ds(pl.multiple_of(src_row + off, BF16_ROWS), size)],
                dst_ref.at[pl.ds(pl.multiple_of(dst_row + off, BF16_ROWS), size)],
                sem))


def _start(cp):
    cp.start()


def _wait(cp):
    cp.wait()


def _moe_kernel(tg_ref, nv_ref, ssrc_ref, sdst_ref, slen_ref, tlo_ref, thi_ref,
                hl_ref, wg_ref, wu_ref, wd_ref, y_ref, xbuf, sem):
    del tg_ref
    i = pl.program_id(0)
    nv = nv_ref[0]

    def tile_copies(tile, slot, fn):
        base = tile * MOE_TM

        def body(s, carry):
            d = sdst_ref[s]
            lo = jnp.maximum(d, base)
            hi = jnp.minimum(d + slen_ref[s], base + MOE_TM)
            n = jnp.maximum(hi - lo, 0)
            _segment_pieces(n, ssrc_ref[s] + (lo - d), lo - base, hl_ref, xbuf.at[slot], sem.at[slot], fn)
            return carry

        lax.fori_loop(tlo_ref[tile], thi_ref[tile], body, 0)

    @pl.when(i == 0)
    def _():
        xbuf[...] = jnp.zeros_like(xbuf)
        tile_copies(0, 0, _start)

    @pl.when(i + 1 < nv)
    def _():
        tile_copies(i + 1, (i + 1) % 2, _start)

    @pl.when(i >= nv)
    def _():
        y_ref[...] = jnp.zeros_like(y_ref)

    @pl.when(i < nv)
    def _():
        slot = i % 2
        tile_copies(i, slot, _wait)
        xg = xbuf[slot]
        x = xg[:, :D_MODEL]
        gcols = xg[:, D_MODEL:].astype(F32)
        hid = []
        for j in range(EXPERTS_PER_GROUP):
            gate = gcols[:, j:j + 1] + gcols[:, EXPERTS_PER_GROUP + j:EXPERTS_PER_GROUP + j + 1]
            hg = _dot(x, wg_ref[0, j])
            hu = _dot(x, wu_ref[0, j])
            hid.append(((hg * _sigmoid(hg)) * hu * gate).astype(BF16))
        hcat = jnp.concatenate(hid, axis=1)
        wd = wd_ref[0].reshape(EXPERTS_PER_GROUP * D_FF, D_MODEL)
        y_ref[...] = _dot(hcat, wd).astype(BF16)


def _moe(cfg, l, hl, plan, w_gate, w_up, w_down):
    nmt = cfg.moe_tiles

    def w_map(i, tg, nv, *_):
        return (l * N_GROUPS + tg[jnp.minimum(i, nv[0] - 1)], 0, 0, 0)

    return pl.pallas_call(
        _moe_kernel,
        grid_spec=pltpu.PrefetchScalarGridSpec(
            num_scalar_prefetch=7,
            grid=(nmt,),
            in_specs=[
                pl.BlockSpec(memory_space=pl.ANY),
                pl.BlockSpec((1, EXPERTS_PER_GROUP, D_MODEL, D_FF), w_map),
                pl.BlockSpec((1, EXPERTS_PER_GROUP, D_MODEL, D_FF), w_map),
                pl.BlockSpec((1, EXPERTS_PER_GROUP, D_FF, D_MODEL), w_map),
            ],
            out_specs=pl.BlockSpec((MOE_TM, D_MODEL), lambda i, *_: (i, 0)),
            scratch_shapes=[pltpu.VMEM((2, MOE_TM, XS_W), BF16), pltpu.SemaphoreType.DMA((2,))],
        ),
        out_shape=jax.ShapeDtypeStruct((nmt * MOE_TM, D_MODEL), BF16),
        compiler_params=_cparams(("arbitrary",)),
        name="moe",
    )(plan.tile_group, plan.n_valid, plan.seg_tiled, plan.seg_sorted, plan.seg_len, plan.tile_lo, plan.tile_hi,
      hl, w_gate, w_up, w_down)


def _combine_kernel(stiled_ref, ssorted_ref, slen_ref, xm_ref, ys_ref, lp_ref, mod_ref, o_ref, ybuf, sem,
                    *, n_tiles):
    t = pl.program_id(0)

    def tile_copies(tile, slot, fn):
        for g in range(N_GROUPS):
            s = g * n_tiles + tile
            _segment_pieces(slen_ref[s], ssorted_ref[s], stiled_ref[s] - tile * LT,
                            ys_ref, ybuf.at[slot], sem.at[slot], fn)

    @pl.when(t == 0)
    def _():
        ybuf[...] = jnp.zeros_like(ybuf)
        tile_copies(0, 0, _start)

    @pl.when(t + 1 < n_tiles)
    def _():
        tile_copies(t + 1, (t + 1) % 2, _start)

    slot = t % 2
    tile_copies(t, slot, _wait)
    lp = lp_ref[...]
    pos = (lp[:, 0:1] * 32.0 + lp[:, 1:2]).astype(jnp.int32)
    perm_t = jnp.where(lax.broadcasted_iota(jnp.int32, (TM, LT), 1) == pos, 1.0, 0.0).astype(BF16)
    y = _dot(perm_t, ybuf[slot])
    g2 = mod_ref[0, 0][5:6]
    o_ref[...] = xm_ref[...] + g2 * y


def _combine(cfg, l, xm, ys, lp, mod, plan):
    nt = cfg.n_tiles
    return pl.pallas_call(
        functools.partial(_combine_kernel, n_tiles=nt),
        grid_spec=pltpu.PrefetchScalarGridSpec(
            num_scalar_prefetch=3,
            grid=(nt,),
            in_specs=[
                pl.BlockSpec((TM, D_MODEL), lambda i, *_: (i, 0)),
                pl.BlockSpec(memory_space=pl.ANY),
                pl.BlockSpec((TM, LANES), lambda i, *_: (i, 0)),
                pl.BlockSpec((1, 1, 6, D_MODEL), lambda i, *_: (l, _mod_row(cfg, i), 0, 0)),
            ],
            out_specs=pl.BlockSpec((TM, D_MODEL), lambda i, *_: (i, 0)),
            scratch_shapes=[pltpu.VMEM((2, LT, D_MODEL), BF16), pltpu.SemaphoreType.DMA((2,))],
        ),
        out_shape=jax.ShapeDtypeStruct((cfg.t_all, D_MODEL), F32),
        compiler_params=_cparams(("arbitrary",)),
        name="combine",
    )(plan.seg_tiled, plan.seg_sorted, plan.seg_len, xm, ys, lp, mod)


class Plan(NamedTuple):
    seg_tiled: jax.Array
    seg_sorted: jax.Array
    seg_len: jax.Array
    tile_group: jax.Array
    n_valid: jax.Array
    tile_lo: jax.Array
    tile_hi: jax.Array


def _plan(cfg, counts):
    nt = cfg.n_tiles
    m = (counts + (BF16_ROWS - 1)) // BF16_ROWS * BF16_ROWS
    local_off = jnp.cumsum(m, axis=1) - m
    within_group = jnp.cumsum(m, axis=0) - m
    group_rows = jnp.sum(m, axis=0)
    group_tiles = (group_rows + (MOE_TM - 1)) // MOE_TM
    tiles_end = jnp.cumsum(group_tiles)
    group_start = (tiles_end - group_tiles) * MOE_TM
    seg_sorted = (group_start[None, :] + within_group).T.reshape(-1).astype(jnp.int32)
    seg_tiled = (jnp.arange(nt, dtype=jnp.int32)[:, None] * LT + local_off).T.reshape(-1).astype(jnp.int32)
    seg_len = m.T.reshape(-1).astype(jnp.int32)
    n_valid = tiles_end[-1:].astype(jnp.int32)
    tile_ids = jnp.arange(cfg.moe_tiles, dtype=jnp.int32)
    tile_group = jnp.minimum(jnp.sum(tile_ids[:, None] >= tiles_end[None, :], axis=1), N_GROUPS - 1).astype(jnp.int32)
    tile_base = tile_ids * MOE_TM
    tile_lo = jnp.searchsorted(seg_sorted + seg_len, tile_base, side="right").astype(jnp.int32)
    tile_hi = jnp.searchsorted(seg_sorted, tile_base + MOE_TM, side="left").astype(jnp.int32)
    return Plan(seg_tiled, seg_sorted, seg_len, tile_group, n_valid, tile_lo, tile_hi)


def _rope_tables(n):
    pos = jnp.arange(n)
    rc = jnp.stack([(pos // GRID_W).astype(F32), (pos % GRID_W).astype(F32)], axis=1)
    freqs = ROPE_THETA ** (-jnp.arange(AXIS_PAIRS, dtype=F32) / AXIS_PAIRS)
    ang = rc[:, :, None] * freqs[None, None, :]
    cos = jnp.cos(ang)[:, :, None, :]
    sin = jnp.sin(ang)[:, :, None, :] * jnp.array([-1.0, 1.0], F32)[None, None, :, None]
    cos = jnp.broadcast_to(cos, (n, 2, 2, AXIS_PAIRS)).reshape(n, HEAD_DIM)
    sin = jnp.broadcast_to(sin, (n, 2, 2, AXIS_PAIRS)).reshape(n, HEAD_DIM)
    reps = LANES // HEAD_DIM
    return jnp.tile(cos, (1, reps)), jnp.tile(sin, (1, reps))


def _block_diag_mean(width):
    r = jnp.arange(width)
    return jnp.where((r[:, None] // HEAD_DIM) == (r[None, :] // HEAD_DIM), 1.0 / HEAD_DIM, 0.0).astype(BF16)


def _forward(cfg, x_prompt, x_sample, cache_k, cache_v, c, c_ctx, w_mod, b_mod, norm1_g, norm2_g,
             w_in, q_norm_g, k_norm_g, conv_dw_w, conv_dw_b, conv_ln_g, conv_ln_b, w_out,
             w_router, b_router, w_gate, w_up, w_down):
    depth = cfg.depth
    assert cfg.ctx_seq == CONV_ROWS and cfg.t_ctx % TM == 0 and cfg.smp_seq % TM == 0
    assert cfg.t_ctx % cfg.smp_seq == 0 and cfg.smp_batch + 1 <= 8

    x = jnp.concatenate([x_prompt.reshape(cfg.t_ctx, D_MODEL), x_sample.reshape(-1, D_MODEL)], axis=0)
    cvec = jnp.zeros((8, D_MODEL), F32).at[0].set(c_ctx).at[1:1 + cfg.smp_batch].set(c)
    mod = _modulation(cvec, w_mod, b_mod)

    n1 = norm1_g.reshape(depth, 1, D_MODEL)
    n2 = norm2_g.reshape(depth, 1, D_MODEL)
    gain = jnp.concatenate([jnp.tile(q_norm_g, (1, N_Q_HEADS)), jnp.tile(k_norm_g, (1, N_KV_HEADS))],
                           axis=1).reshape(depth, 1, QK_W)
    w_in_b = w_in.astype(BF16)
    w_out_b = w_out.astype(BF16)
    wg_b = w_gate.astype(BF16).reshape(depth * N_GROUPS, EXPERTS_PER_GROUP, D_MODEL, D_FF)
    wu_b = w_up.astype(BF16).reshape(depth * N_GROUPS, EXPERTS_PER_GROUP, D_MODEL, D_FF)
    wd_b = w_down.astype(BF16).reshape(depth * N_GROUPS, EXPERTS_PER_GROUP, D_FF, D_MODEL)
    dw_w = jnp.pad(conv_dw_w, ((0, 0), (0, 32 - CONV_K), (0, 0)))
    dw_b = conv_dw_b.reshape(depth, 1, CONV_CH)
    ln_g = conv_ln_g.reshape(depth, 1, CONV_CH)
    ln_b = conv_ln_b.reshape(depth, 1, CONV_CH)
    wr = jnp.pad(w_router, ((0, 0), (0, LANES - N_EXPERTS)))
    br = b_router.reshape(N_EXPERTS, 1)
    cos, sin = _rope_tables(cfg.smp_seq)
    bdq = _block_diag_mean(Q_W // 2)
    bdk = _block_diag_mean(KV_W)
    ck = cache_k.reshape(cfg.smp_batch, depth, cfg.past, KV_W)
    cv = cache_v.reshape(cfg.smp_batch, depth, cfg.past, KV_W)

    ks, vs = [], []
    for l in range(depth):
        q, k, v, z = _inproj(cfg, l, x, mod, n1, w_in_b, gain, bdq, bdk, cos, sin)
        ks.append(k[:cfg.t_ctx])
        vs.append(v[:cfg.t_ctx])
        a_ctx, a_smp = _attention(cfg, l, q, k, v, ck, cv)
        cv_out = _conv(cfg, l, z, dw_w, dw_b, ln_g, ln_b)
        xm, hl, lp, cnt = _outproj(cfg, l, a_ctx, a_smp, cv_out, x, mod, n2, w_out_b, wr, br)
        plan = _plan(cfg, cnt[:, :N_GROUPS, 0])
        ys = _moe(cfg, l, hl.reshape(cfg.n_tiles * LT, XS_W), plan, wg_b, wu_b, wd_b)
        x = _combine(cfg, l, xm, ys, lp, mod, plan)

    y_prompt = x[:cfg.t_ctx].reshape(cfg.ctx_batch, cfg.ctx_seq, D_MODEL)
    y_sample = x[cfg.t_ctx:].reshape(cfg.smp_batch, cfg.smp_seq, D_MODEL)
    new_k = jnp.stack(ks, axis=0).reshape(depth, cfg.ctx_batch, cfg.ctx_seq, N_KV_HEADS, HEAD_DIM)
    new_v = jnp.stack(vs, axis=0).reshape(depth, cfg.ctx_batch, cfg.ctx_seq, N_KV_HEADS, HEAD_DIM)
    return y_prompt, y_sample, jnp.swapaxes(new_k, 0, 1), jnp.swapaxes(new_v, 0, 1)


def kernel(x_prompt, x_sample, cache_k, cache_v, c, c_ctx, w_mod, b_mod, norm1_g, norm2_g, w_in, q_norm_g,
           k_norm_g, conv_dw_w, conv_dw_b, conv_ln_g, conv_ln_b, w_out, w_router, b_router, w_gate, w_up, w_down):
    cfg = Cfg(ctx_batch=x_prompt.shape[0], ctx_seq=x_prompt.shape[1], smp_batch=x_sample.shape[0],
              smp_seq=x_sample.shape[1], past=cache_k.shape[2], depth=w_mod.shape[0])
    return _forward(cfg, x_prompt, x_sample, cache_k, cache_v, c, c_ctx, w_mod, b_mod, norm1_g, norm2_g,
                    w_in, q_norm_g, k_norm_g, conv_dw_w, conv_dw_b, conv_ln_g, conv_ln_b, w_out,
                    w_router, b_router, w_gate, w_up, w_down)
```

```python
import functools
import math
from typing import NamedTuple

import jax
import jax.numpy as jnp
from jax import lax
from jax.experimental import pallas as pl
from jax.experimental.pallas import tpu as pltpu

F32 = jnp.float32
BF16 = jnp.bfloat16

D_MODEL = 1024
HEAD_DIM = 64
N_Q_HEADS = 8
N_KV_HEADS = 2
Q_PER_KV = N_Q_HEADS // N_KV_HEADS
Q_W = N_Q_HEADS * HEAD_DIM
KV_W = N_KV_HEADS * HEAD_DIM
QK_W = Q_W + KV_W
CONV_CH = D_MODEL // 2
CONV_K = 31
CONV_PAD = CONV_K // 2
IN_COLS = Q_W + 2 * KV_W + 2 * CONV_CH
GRID_W = 64
AXIS_PAIRS = HEAD_DIM // 4
ROPE_THETA = 10000.0
N_EXPERTS = 16
N_GROUPS = 4
EXPERTS_PER_GROUP = N_EXPERTS // N_GROUPS
D_FF = 512
EPS = 1e-6

LANES = 128
BF16_ROWS = 16
TM = 512
LT = 640
MOE_TM = 512
CONV_ROWS = 256
HALO = 16
GATE_W = LANES
XS_W = D_MODEL + GATE_W
SEG_SIZES = (512, 256, 128, 64, 32, 16)
Q_SCALE = (1.0 / math.sqrt(HEAD_DIM)) * math.log2(math.e)
VMEM_LIMIT = 48 * 1024 * 1024


class Cfg(NamedTuple):
    ctx_batch: int
    ctx_seq: int
    smp_batch: int
    smp_seq: int
    past: int
    depth: int

    @property
    def t_ctx(self):
        return self.ctx_batch * self.ctx_seq

    @property
    def t_all(self):
        return self.t_ctx + self.smp_batch * self.smp_seq

    @property
    def n_tiles(self):
        return self.t_all // TM

    @property
    def n_ctx_tiles(self):
        return self.t_ctx // TM

    @property
    def smp_tiles(self):
        return self.smp_seq // TM

    @property
    def moe_tiles(self):
        rows = self.t_all + self.n_tiles * N_GROUPS * (BF16_ROWS - 1) + N_GROUPS * (MOE_TM - 1)
        return -(-rows // MOE_TM)


def _mod_row(cfg, i):
    return jnp.where(i < cfg.n_ctx_tiles, 0, 1 + (i - cfg.n_ctx_tiles) // cfg.smp_tiles)


def _pair_specs(cfg, width):
    nct = cfg.n_ctx_tiles
    return [pl.BlockSpec((TM, width), lambda i, *_: (jnp.minimum(i, nct - 1), 0)),
            pl.BlockSpec((TM, width), lambda i, *_: (jnp.maximum(i - nct, 0), 0))]


def _pair_shapes(cfg, width, dtype):
    return [jax.ShapeDtypeStruct((cfg.t_ctx, width), dtype),
            jax.ShapeDtypeStruct((cfg.t_all - cfg.t_ctx, width), dtype)]


def _load_pair(n_ctx_tiles, c_ref, s_ref):
    return jnp.where(pl.program_id(0) < n_ctx_tiles, c_ref[...], s_ref[...])


def _store_pair(n_ctx_tiles, c_ref, s_ref, val):
    @pl.when(pl.program_id(0) < n_ctx_tiles)
    def _():
        c_ref[...] = val

    @pl.when(pl.program_id(0) >= n_ctx_tiles)
    def _():
        s_ref[...] = val


def _cparams(sem, vmem=VMEM_LIMIT):
    return pltpu.CompilerParams(dimension_semantics=sem, vmem_limit_bytes=vmem)


def _dot(a, b):
    return jnp.dot(a, b, preferred_element_type=F32)


def _dot_nt(a, b):
    return lax.dot_general(a, b, (((1,), (1,)), ((), ())), preferred_element_type=F32)


def _sigmoid(x):
    return 1.0 / (1.0 + jnp.exp(-x))


def _mod_kernel(c_ref, w_ref, b_ref, o_ref):
    c = c_ref[...]
    a = (c * _sigmoid(c)).astype(BF16)
    o_ref[0] = _dot(a, w_ref[0].astype(BF16)) + b_ref[0]


def _modulation(cvec, w_mod, b_mod):
    depth = w_mod.shape[0]
    out = pl.pallas_call(
        _mod_kernel,
        grid=(depth, 6),
        in_specs=[
            pl.BlockSpec((8, D_MODEL), lambda l, j: (0, 0)),
            pl.BlockSpec((1, D_MODEL, D_MODEL), lambda l, j: (l, 0, j)),
            pl.BlockSpec((1, 1, D_MODEL), lambda l, j: (l, 0, j)),
        ],
        out_specs=pl.BlockSpec((1, 8, D_MODEL), lambda l, j: (l, 0, j)),
        out_shape=jax.ShapeDtypeStruct((depth, 8, 6 * D_MODEL), F32),
        compiler_params=_cparams(("arbitrary", "arbitrary")),
        name="modulation",
    )(cvec, w_mod, b_mod.reshape(depth, 1, 6 * D_MODEL))
    return out.reshape(depth, 8, 6, D_MODEL)


def _inproj_kernel(xc_ref, xs_ref, mod_ref, n1_ref, w_ref, gain_ref, bdq_ref, bdk_ref, cos_ref, sin_ref,
                   kc_in, vc_in, q_ref, k_ref, v_ref, z_ref, kc_ref, vc_ref, w_bf, *, n_ctx_tiles, ctx_seq):
    del kc_in, vc_in
    i = pl.program_id(0)

    @pl.when(i == 0)
    def _():
        w_bf[...] = w_ref[0].astype(BF16)

    x = _load_pair(n_ctx_tiles, xc_ref, xs_ref)
    mod = mod_ref[0, 0]
    sh1, sc1 = mod[0:1], mod[1:2]
    ms = jnp.mean(x * x, axis=-1, keepdims=True)
    h = (x * lax.rsqrt(ms + EPS) * n1_ref[0]) * (1.0 + sc1) + sh1
    proj = _dot(h.astype(BF16), w_bf[...])

    qk = proj[:, :QK_W]
    sq = (qk * qk).astype(BF16)
    half = Q_W // 2
    msq = jnp.concatenate([_dot(sq[:, :half], bdq_ref[...]),
                           _dot(sq[:, half:Q_W], bdq_ref[...]),
                           _dot(sq[:, Q_W:], bdk_ref[...])], axis=1)
    qkn = qk * lax.rsqrt(msq + EPS) * gain_ref[0]

    v = proj[:, QK_W:QK_W + KV_W]
    v_ref[...] = v
    a = proj[:, QK_W + KV_W:QK_W + KV_W + CONV_CH]
    b = proj[:, QK_W + KV_W + CONV_CH:]
    z_ref[...] = a * _sigmoid(b)

    @pl.when(i < n_ctx_tiles)
    def _():
        q_ref[...] = (qkn[:, :Q_W] * Q_SCALE).astype(BF16)
        k = qkn[:, Q_W:]
        k_ref[...] = k
        for s in range(TM // ctx_seq):
            kc_ref[s, 0] = k[s * ctx_seq:(s + 1) * ctx_seq]
            vc_ref[s, 0] = v[s * ctx_seq:(s + 1) * ctx_seq]

    @pl.when(i >= n_ctx_tiles)
    def _():
        cos = cos_ref[...]
        sin = sin_ref[...]
        first = (lax.broadcasted_iota(jnp.int32, (TM, LANES), 1) % 32) < 16
        for c in range(QK_W // LANES):
            blk = qkn[:, c * LANES:(c + 1) * LANES]
            partner = jnp.where(first, pltpu.roll(blk, LANES - 16, 1), pltpu.roll(blk, 16, 1))
            rot = blk * cos + partner * sin
            if c < Q_W // LANES:
                q_ref[:, c * LANES:(c + 1) * LANES] = (rot * Q_SCALE).astype(BF16)
            else:
                k_ref[...] = rot


def _inproj(cfg, l, x_pair, mod, n1, w_in, gain, bdq, bdk, cos, sin, kc, vc):
    t = cfg.t_all
    nct, st = cfg.n_ctx_tiles, cfg.smp_tiles
    rope_map = lambda i: (jnp.maximum(i - nct, 0) % st, 0)
    seqs = TM // cfg.ctx_seq
    cache_spec = lambda: pl.BlockSpec((seqs, 1, cfg.ctx_seq, KV_W), lambda i: (jnp.minimum(i, nct - 1), l, 0, 0))
    return pl.pallas_call(
        functools.partial(_inproj_kernel, n_ctx_tiles=nct, ctx_seq=cfg.ctx_seq),
        grid=(cfg.n_tiles,),
        in_specs=_pair_specs(cfg, D_MODEL) + [
            pl.BlockSpec((1, 1, 6, D_MODEL), lambda i: (l, _mod_row(cfg, i), 0, 0)),
            pl.BlockSpec((1, 1, D_MODEL), lambda i: (l, 0, 0)),
            pl.BlockSpec((1, D_MODEL, IN_COLS), lambda i: (l, 0, 0)),
            pl.BlockSpec((1, 1, QK_W), lambda i: (l, 0, 0)),
            pl.BlockSpec((Q_W // 2, Q_W // 2), lambda i: (0, 0)),
            pl.BlockSpec((KV_W, KV_W), lambda i: (0, 0)),
            pl.BlockSpec((TM, LANES), rope_map),
            pl.BlockSpec((TM, LANES), rope_map),
            pl.BlockSpec(memory_space=pl.ANY),
            pl.BlockSpec(memory_space=pl.ANY),
        ],
        out_specs=[
            pl.BlockSpec((TM, Q_W), lambda i: (i, 0)),
            pl.BlockSpec((TM, KV_W), lambda i: (i, 0)),
            pl.BlockSpec((TM, KV_W), lambda i: (i, 0)),
            pl.BlockSpec((TM, CONV_CH), lambda i: (i, 0)),
            cache_spec(), cache_spec(),
        ],
        out_shape=[
            jax.ShapeDtypeStruct((t, Q_W), BF16),
            jax.ShapeDtypeStruct((t, KV_W), F32),
            jax.ShapeDtypeStruct((t, KV_W), F32),
            jax.ShapeDtypeStruct((t, CONV_CH), F32),
            jax.ShapeDtypeStruct(kc.shape, F32),
            jax.ShapeDtypeStruct(vc.shape, F32),
        ],
        scratch_shapes=[pltpu.VMEM((D_MODEL, IN_COLS), BF16)],
        input_output_aliases={10: 4, 11: 5},
        compiler_params=_cparams(("arbitrary",)),
        name="inproj",
    )(*x_pair, mod, n1, w_in, gain, bdq, bdk, cos, sin, kc, vc)


def _attn_kernel(*refs, rq, n_past, first_axis):
    if n_past:
        q_ref, k_ref, v_ref, ck_ref, cv_ref, o_ref, krep, vrep = refs
    else:
        q_ref, k_ref, v_ref, o_ref, krep, vrep = refs
    head_w = Q_PER_KV * HEAD_DIM

    @pl.when(pl.program_id(first_axis) == 0)
    def _():
        src = lax.broadcasted_iota(jnp.int32, (KV_W, head_w), 0)
        dst = lax.broadcasted_iota(jnp.int32, (KV_W, head_w), 1) % HEAD_DIM
        n_new = k_ref.shape[0]
        for h in range(N_KV_HEADS):
            rep = jnp.where(src == dst + h * HEAD_DIM, 1.0, 0.0).astype(BF16)
            if n_past:
                krep[h, :n_past, :] = _dot(ck_ref[0, 0].astype(BF16), rep).astype(BF16)
                vrep[h, :n_past, :] = _dot(cv_ref[0, 0].astype(BF16), rep).astype(BF16)
            krep[h, n_past:n_past + n_new, :] = _dot(k_ref[...].astype(BF16), rep).astype(BF16)
            vrep[h, n_past:n_past + n_new, :] = _dot(v_ref[...].astype(BF16), rep).astype(BF16)

    lane_head = lax.broadcasted_iota(jnp.int32, (rq, head_w), 1) // HEAD_DIM
    for h in range(N_KV_HEADS):
        qh = q_ref[:, h * head_w:(h + 1) * head_w]
        kh = krep[h]
        vh = vrep[h]
        acc = jnp.zeros((rq, head_w), F32)
        for g in range(Q_PER_KV):
            mask = lane_head == g
            qm = jnp.where(mask, qh, jnp.zeros_like(qh))
            s = _dot_nt(qm, kh)
            m = jnp.max(s, axis=-1, keepdims=True)
            p = jnp.exp2(s - m)
            denom = jnp.sum(p, axis=-1, keepdims=True)
            o = _dot(p.astype(BF16), vh)
            acc = jnp.where(mask, o * (1.0 / denom), acc)
        o_ref[:, h * head_w:(h + 1) * head_w] = acc.astype(BF16)


def _attention(cfg, l, q, k, v, cache_k, cache_v):
    head_w = Q_PER_KV * HEAD_DIM
    sc = cfg.ctx_seq
    a_ctx = pl.pallas_call(
        functools.partial(_attn_kernel, rq=sc, n_past=0, first_axis=1),
        grid=(cfg.ctx_batch, 1),
        in_specs=[
            pl.BlockSpec((sc, Q_W), lambda b, j: (b, 0)),
            pl.BlockSpec((sc, KV_W), lambda b, j: (b, 0)),
            pl.BlockSpec((sc, KV_W), lambda b, j: (b, 0)),
        ],
        out_specs=pl.BlockSpec((sc, Q_W), lambda b, j: (b, 0)),
        out_shape=jax.ShapeDtypeStruct((cfg.t_ctx, Q_W), BF16),
        scratch_shapes=[pltpu.VMEM((N_KV_HEADS, sc, head_w), BF16),
                        pltpu.VMEM((N_KV_HEADS, sc, head_w), BF16)],
        compiler_params=_cparams(("arbitrary", "arbitrary")),
        name="attn_ctx",
    )(q, k, v)
    rq = 256
    ss = cfg.smp_seq
    sk = cfg.past + ss
    kv_base = cfg.t_ctx // ss
    q_base = cfg.t_ctx // rq
    nqb = ss // rq
    a_smp = pl.pallas_call(
        functools.partial(_attn_kernel, rq=rq, n_past=cfg.past, first_axis=1),
        grid=(cfg.smp_batch, nqb),
        in_specs=[
            pl.BlockSpec((rq, Q_W), lambda b, j: (q_base + b * nqb + j, 0)),
            pl.BlockSpec((ss, KV_W), lambda b, j: (kv_base + b, 0)),
            pl.BlockSpec((ss, KV_W), lambda b, j: (kv_base + b, 0)),
            pl.BlockSpec((1, 1, cfg.past, KV_W), lambda b, j: (b, l, 0, 0)),
            pl.BlockSpec((1, 1, cfg.past, KV_W), lambda b, j: (b, l, 0, 0)),
        ],
        out_specs=pl.BlockSpec((rq, Q_W), lambda b, j: (b * nqb + j, 0)),
        out_shape=jax.ShapeDtypeStruct((cfg.smp_batch * ss, Q_W), BF16),
        scratch_shapes=[pltpu.VMEM((N_KV_HEADS, sk, head_w), BF16),
                        pltpu.VMEM((N_KV_HEADS, sk, head_w), BF16)],
        compiler_params=_cparams(("arbitrary", "arbitrary")),
        name="attn_smp",
    )(q, k, v, cache_k, cache_v)
    return a_ctx, a_smp


def _conv_kernel(z_ref, zp_ref, zn_ref, w_ref, b_ref, g_ref, beta_ref, o_ref, pad, *, n_ctx_blocks, seq_blocks):
    i = pl.program_id(0)
    j = jnp.maximum(i - n_ctx_blocks, 0) % seq_blocks
    latent = i >= n_ctx_blocks
    has_prev = jnp.logical_and(latent, j > 0)
    has_next = jnp.logical_and(latent, j < seq_blocks - 1)
    pad[0:HALO, :] = jnp.where(has_prev, zp_ref[...], 0.0)
    pad[HALO:HALO + CONV_ROWS, :] = z_ref[...]
    pad[HALO + CONV_ROWS:, :] = jnp.where(has_next, zn_ref[...], 0.0)
    w = w_ref[0]
    n_inner = CONV_ROWS + 8
    blocks = []
    for c in range(CONV_CH // LANES):
        cols = slice(c * LANES, (c + 1) * LANES)
        acc = None
        for o in range(8):
            inner = None
            for q in range(4):
                j = 8 * q + o
                if j == 0:
                    continue
                term = pad[8 * q:8 * q + n_inner, cols] * w[j:j + 1, cols]
                inner = term if inner is None else inner + term
            shifted = inner[o:o + CONV_ROWS]
            acc = shifted if acc is None else acc + shifted
        blocks.append(acc)
    y = jnp.concatenate(blocks, axis=1) + b_ref[0]
    mu = jnp.mean(y, axis=-1, keepdims=True)
    yc = y - mu
    var = jnp.mean(yc * yc, axis=-1, keepdims=True)
    yn = yc * lax.rsqrt(var + EPS) * g_ref[0] + beta_ref[0]
    o_ref[...] = (yn * _sigmoid(yn)).astype(BF16)


def _conv(cfg, l, z, dw_w, dw_b, ln_g, ln_b):
    t = cfg.t_all
    nb = t // CONV_ROWS
    hb = CONV_ROWS // HALO
    last = t // HALO - 1
    vec = lambda: pl.BlockSpec((1, 1, CONV_CH), lambda i: (l, 0, 0))
    return pl.pallas_call(
        functools.partial(_conv_kernel, n_ctx_blocks=cfg.t_ctx // CONV_ROWS, seq_blocks=cfg.smp_seq // CONV_ROWS),
        grid=(nb,),
        in_specs=[
            pl.BlockSpec((CONV_ROWS, CONV_CH), lambda i: (i, 0)),
            pl.BlockSpec((HALO, CONV_CH), lambda i: (jnp.maximum(i * hb - 1, 0), 0)),
            pl.BlockSpec((HALO, CONV_CH), lambda i: (jnp.minimum(i * hb + hb, last), 0)),
            pl.BlockSpec((1, 32, CONV_CH), lambda i: (l, 0, 0)),
            vec(), vec(), vec(),
        ],
        out_specs=pl.BlockSpec((CONV_ROWS, CONV_CH), lambda i: (i, 0)),
        out_shape=jax.ShapeDtypeStruct((t, CONV_CH), BF16),
        scratch_shapes=[pltpu.VMEM((CONV_ROWS + 2 * HALO, CONV_CH), F32)],
        compiler_params=_cparams(("arbitrary",)),
        name="conv",
    )(z, z, z, dw_w, dw_b, ln_g, ln_b)


def _beats(ai, aj, i_first):
    return jnp.where((ai >= aj) if i_first else (ai > aj), 1.0, 0.0)


def _split_bf16(x):
    hi = x.astype(BF16)
    lo = (x - hi.astype(F32)).astype(BF16)
    return hi, lo


def _outproj_kernel(ac_ref, as_ref, c_ref, xc_ref, xs_ref, mod_ref, n2_ref, wo_ref, wr_ref, br_ref,
                    xmc_ref, xms_ref, hl_ref, lp_ref, cnt_ref, wo_bf, *, n_ctx_tiles):
    @pl.when(pl.program_id(0) == 0)
    def _():
        wo_bf[...] = wo_ref[0].astype(BF16)

    ac = jnp.concatenate([_load_pair(n_ctx_tiles, ac_ref, as_ref), c_ref[...]], axis=1)
    y = _dot(ac, wo_bf[...])
    mod = mod_ref[0, 0]
    g1, sh2, sc2 = mod[2:3], mod[3:4], mod[4:5]
    xm = _load_pair(n_ctx_tiles, xc_ref, xs_ref) + g1 * y
    _store_pair(n_ctx_tiles, xmc_ref, xms_ref, xm)
    ms = jnp.mean(xm * xm, axis=-1, keepdims=True)
    h2 = (xm * lax.rsqrt(ms + EPS) * n2_ref[0]) * (1.0 + sc2) + sh2
    h_hi, h_lo = _split_bf16(h2)

    w_hi, w_lo = _split_bf16(wr_ref[...])
    logits = _dot(h_hi, w_hi) + (_dot(h_hi, w_lo) + _dot(h_lo, w_hi))
    lt = logits.T
    s_all = _sigmoid(lt[0:N_EXPERTS, :])
    sb_all = s_all + br_ref[...]
    s = [s_all[e:e + 1, :] for e in range(N_EXPERTS)]
    sb = [sb_all[e:e + 1, :] for e in range(N_EXPERTS)]

    sel = []
    gscore = []
    for g in range(N_GROUPS):
        base = g * EXPERTS_PER_GROUP
        gs = jnp.zeros((1, TM), F32)
        for j in range(EXPERTS_PER_GROUP):
            rank = jnp.zeros((1, TM), F32)
            for i in range(EXPERTS_PER_GROUP):
                if i != j:
                    rank = rank + _beats(sb[base + i], sb[base + j], i < j)
            sj = rank < 1.5
            sel.append(sj)
            gs = gs + jnp.where(sj, sb[base + j], 0.0)
        gscore.append(gs)
    chosen = []
    for g in range(N_GROUPS):
        rank = jnp.zeros((1, TM), F32)
        for i in range(N_GROUPS):
            if i != g:
                rank = rank + _beats(gscore[i], gscore[g], i < g)
        chosen.append(rank < 0.5)

    gates = []
    for j in range(EXPERTS_PER_GROUP):
        gj = jnp.zeros((1, TM), F32)
        for g in range(N_GROUPS):
            e = g * EXPERTS_PER_GROUP + j
            gj = gj + jnp.where(chosen[g], jnp.where(sel[e], s[e], 0.0), 0.0)
        gates.append(gj)
    denom = gates[0] + gates[1] + gates[2] + gates[3]
    inv = 1.0 / denom
    gates = [gj * inv for gj in gates]

    cgf = [jnp.where(c, 1.0, 0.0) for c in chosen]
    onehot_g = jnp.concatenate(cgf + [jnp.zeros((8 - N_GROUPS, TM), F32)], axis=0).astype(BF16)
    before = (lax.broadcasted_iota(jnp.int32, (TM, TM), 0) < lax.broadcasted_iota(jnp.int32, (TM, TM), 1))
    prior = _dot(onehot_g, jnp.where(before, 1.0, 0.0).astype(BF16))
    lp = jnp.zeros((1, TM), F32)
    start = jnp.zeros((1, 1), F32)
    counts = []
    for g in range(N_GROUPS):
        n_g = jnp.sum(cgf[g], axis=-1, keepdims=True)
        counts.append(n_g)
        lp = lp + cgf[g] * (start + prior[g:g + 1, :])
        start = start + jnp.floor((n_g + (BF16_ROWS - 1)) * (1.0 / BF16_ROWS)) * BF16_ROWS
    lp_i = lp.astype(jnp.int32)
    perm = jnp.where(lax.broadcasted_iota(jnp.int32, (LT, TM), 0) == lp_i, 1.0, 0.0).astype(BF16)

    hl_ref[0, :, :D_MODEL] = _dot(perm, h_hi).astype(BF16)
    g_hi_lo = []
    for part in range(2):
        for gj in gates:
            g_hi_lo.append(_split_bf16(gj)[part].astype(F32))
    gate_rows = jnp.concatenate(g_hi_lo + [jnp.zeros((GATE_W - 8, TM), F32)], axis=0).astype(BF16)
    hl_ref[0, :, D_MODEL:] = _dot_nt(perm, gate_rows).astype(BF16)

    lp_hi = jnp.floor(lp * (1.0 / 32.0))
    lp_parts = jnp.concatenate([lp_hi, lp - 32.0 * lp_hi, jnp.zeros((LANES - 2, TM), F32)], axis=0).astype(BF16)
    eye = jnp.where(lax.broadcasted_iota(jnp.int32, (TM, TM), 0) == lax.broadcasted_iota(jnp.int32, (TM, TM), 1),
                    1.0, 0.0).astype(BF16)
    lp_ref[...] = _dot_nt(eye, lp_parts)

    cnt = jnp.concatenate([jnp.broadcast_to(c, (1, LANES)) for c in counts]
                          + [jnp.zeros((8 - N_GROUPS, LANES), F32)], axis=0)
    cnt_ref[0] = cnt.astype(jnp.int32)


def _outproj(cfg, l, a_pair, c, x_pair, mod, n2, w_out, w_router, b_router):
    t, nt, nct = cfg.t_all, cfg.n_tiles, cfg.n_ctx_tiles
    return pl.pallas_call(
        functools.partial(_outproj_kernel, n_ctx_tiles=nct),
        grid=(nt,),
        in_specs=_pair_specs(cfg, Q_W) + [
            pl.BlockSpec((TM, CONV_CH), lambda i: (i, 0)),
        ] + _pair_specs(cfg, D_MODEL) + [
            pl.BlockSpec((1, 1, 6, D_MODEL), lambda i: (l, _mod_row(cfg, i), 0, 0)),
            pl.BlockSpec((1, 1, D_MODEL), lambda i: (l, 0, 0)),
            pl.BlockSpec((1, D_MODEL, D_MODEL), lambda i: (l, 0, 0)),
            pl.BlockSpec((D_MODEL, LANES), lambda i: (0, 0)),
            pl.BlockSpec((N_EXPERTS, 1), lambda i: (0, 0)),
        ],
        out_specs=_pair_specs(cfg, D_MODEL) + [
            pl.BlockSpec((1, LT, XS_W), lambda i: (i, 0, 0)),
            pl.BlockSpec((TM, LANES), lambda i: (i, 0)),
            pl.BlockSpec((1, 8, LANES), lambda i: (i, 0, 0)),
        ],
        out_shape=_pair_shapes(cfg, D_MODEL, F32) + [
            jax.ShapeDtypeStruct((nt, LT, XS_W), BF16),
            jax.ShapeDtypeStruct((t, LANES), F32),
            jax.ShapeDtypeStruct((nt, 8, LANES), jnp.int32),
        ],
        scratch_shapes=[pltpu.VMEM((D_MODEL, D_MODEL), BF16)],
        compiler_params=_cparams(("arbitrary",)),
        name="outproj",
    )(*a_pair, c, *x_pair, mod, n2, w_out, w_router, b_router)


def _segment_pieces(n, src_row, dst_row, src_ref, dst_ref, sem, fn):
    for size in SEG_SIZES:
        off = n & ~(2 * size - 1)

        @pl.when((n & size) != 0)
        def _():
            fn(pltpu.make_async_copy(
                src_ref.at[pl.ds(pl.multiple_of(src_row + off, BF16_ROWS), size)],
                dst_ref.at[pl.ds(pl.multiple_of(dst_row + off, BF16_ROWS), size)],
                sem))


def _start(cp):
    cp.start()


def _wait(cp):
    cp.wait()


def _moe_kernel(tg_ref, nv_ref, ssrc_ref, sdst_ref, slen_ref, tlo_ref, thi_ref,
                hl_ref, wg_ref, wu_ref, wd_ref, y_ref, xbuf, sem):
    del tg_ref
    i = pl.program_id(0)
    nv = nv_ref[0]

    def tile_copies(tile, slot, fn):
        base = tile * MOE_TM

        def body(s, carry):
            d = sdst_ref[s]
            lo = jnp.maximum(d, base)
            hi = jnp.minimum(d + slen_ref[s], base + MOE_TM)
            n = jnp.maximum(hi - lo, 0)
            _segment_pieces(n, ssrc_ref[s] + (lo - d), lo - base, hl_ref, xbuf.at[slot], sem.at[slot], fn)
            return carry

        lax.fori_loop(tlo_ref[tile], thi_ref[tile], body, 0)

    @pl.when(i == 0)
    def _():
        xbuf[...] = jnp.zeros_like(xbuf)
        tile_copies(0, 0, _start)

    @pl.when(i + 1 < nv)
    def _():
        tile_copies(i + 1, (i + 1) % 2, _start)

    @pl.when(i >= nv)
    def _():
        y_ref[...] = jnp.zeros_like(y_ref)

    @pl.when(i < nv)
    def _():
        slot = i % 2
        tile_copies(i, slot, _wait)
        xg = xbuf[slot]
        x = xg[:, :D_MODEL]
        gcols = xg[:, D_MODEL:].astype(F32)
        hid = []
        for j in range(EXPERTS_PER_GROUP):
            gate = gcols[:, j:j + 1] + gcols[:, EXPERTS_PER_GROUP + j:EXPERTS_PER_GROUP + j + 1]
            hg = _dot(x, wg_ref[0, j])
            hu = _dot(x, wu_ref[0, j])
            hid.append(((hg * _sigmoid(hg)) * hu * gate).astype(BF16))
        hcat = jnp.concatenate(hid, axis=1)
        wd = wd_ref[0].reshape(EXPERTS_PER_GROUP * D_FF, D_MODEL)
        y_ref[...] = _dot(hcat, wd).astype(BF16)


def _moe(cfg, l, hl, plan, w_gate, w_up, w_down):
    nmt = cfg.moe_tiles

    def w_map(i, tg, nv, *_):
        return (l * N_GROUPS + tg[jnp.minimum(i, nv[0] - 1)], 0, 0, 0)

    return pl.pallas_call(
        _moe_kernel,
        grid_spec=pltpu.PrefetchScalarGridSpec(
            num_scalar_prefetch=7,
            grid=(nmt,),
            in_specs=[
                pl.BlockSpec(memory_space=pl.ANY),
                pl.BlockSpec((1, EXPERTS_PER_GROUP, D_MODEL, D_FF), w_map),
                pl.BlockSpec((1, EXPERTS_PER_GROUP, D_MODEL, D_FF), w_map),
                pl.BlockSpec((1, EXPERTS_PER_GROUP, D_FF, D_MODEL), w_map),
            ],
            out_specs=pl.BlockSpec((MOE_TM, D_MODEL), lambda i, *_: (i, 0)),
            scratch_shapes=[pltpu.VMEM((2, MOE_TM, XS_W), BF16), pltpu.SemaphoreType.DMA((2,))],
        ),
        out_shape=jax.ShapeDtypeStruct((nmt * MOE_TM, D_MODEL), BF16),
        compiler_params=_cparams(("arbitrary",)),
        name="moe",
    )(plan.tile_group, plan.n_valid, plan.seg_tiled, plan.seg_sorted, plan.seg_len, plan.tile_lo, plan.tile_hi,
      hl, w_gate, w_up, w_down)


def _combine_kernel(stiled_ref, ssorted_ref, slen_ref, xmc_ref, xms_ref, ys_ref, lp_ref, mod_ref,
                    oc_ref, os_ref, ybuf, sem, *, n_tiles, n_ctx_tiles):
    t = pl.program_id(0)

    def tile_copies(tile, slot, fn):
        for g in range(N_GROUPS):
            s = g * n_tiles + tile
            _segment_pieces(slen_ref[s], ssorted_ref[s], stiled_ref[s] - tile * LT,
                            ys_ref, ybuf.at[slot], sem.at[slot], fn)

    @pl.when(t == 0)
    def _():
        ybuf[...] = jnp.zeros_like(ybuf)
        tile_copies(0, 0, _start)

    @pl.when(t + 1 < n_tiles)
    def _():
        tile_copies(t + 1, (t + 1) % 2, _start)

    slot = t % 2
    tile_copies(t, slot, _wait)
    lp = lp_ref[...]
    pos = (lp[:, 0:1] * 32.0 + lp[:, 1:2]).astype(jnp.int32)
    perm_t = jnp.where(lax.broadcasted_iota(jnp.int32, (TM, LT), 1) == pos, 1.0, 0.0).astype(BF16)
    y = _dot(perm_t, ybuf[slot])
    g2 = mod_ref[0, 0][5:6]
    _store_pair(n_ctx_tiles, oc_ref, os_ref, _load_pair(n_ctx_tiles, xmc_ref, xms_ref) + g2 * y)


def _combine(cfg, l, xm_pair, ys, lp, mod, plan):
    nt = cfg.n_tiles
    return pl.pallas_call(
        functools.partial(_combine_kernel, n_tiles=nt, n_ctx_tiles=cfg.n_ctx_tiles),
        grid_spec=pltpu.PrefetchScalarGridSpec(
            num_scalar_prefetch=3,
            grid=(nt,),
            in_specs=_pair_specs(cfg, D_MODEL) + [
                pl.BlockSpec(memory_space=pl.ANY),
                pl.BlockSpec((TM, LANES), lambda i, *_: (i, 0)),
                pl.BlockSpec((1, 1, 6, D_MODEL), lambda i, *_: (l, _mod_row(cfg, i), 0, 0)),
            ],
            out_specs=_pair_specs(cfg, D_MODEL),
            scratch_shapes=[pltpu.VMEM((2, LT, D_MODEL), BF16), pltpu.SemaphoreType.DMA((2,))],
        ),
        out_shape=_pair_shapes(cfg, D_MODEL, F32),
        compiler_params=_cparams(("arbitrary",)),
        name="combine",
    )(plan.seg_tiled, plan.seg_sorted, plan.seg_len, *xm_pair, ys, lp, mod)


class Plan(NamedTuple):
    seg_tiled: jax.Array
    seg_sorted: jax.Array
    seg_len: jax.Array
    tile_group: jax.Array
    n_valid: jax.Array
    tile_lo: jax.Array
    tile_hi: jax.Array


def _plan(cfg, counts):
    nt = cfg.n_tiles
    m = (counts + (BF16_ROWS - 1)) // BF16_ROWS * BF16_ROWS
    local_off = jnp.cumsum(m, axis=1) - m
    within_group = jnp.cumsum(m, axis=0) - m
    group_rows = jnp.sum(m, axis=0)
    group_tiles = (group_rows + (MOE_TM - 1)) // MOE_TM
    tiles_end = jnp.cumsum(group_tiles)
    group_start = (tiles_end - group_tiles) * MOE_TM
    seg_sorted = (group_start[None, :] + within_group).T.reshape(-1).astype(jnp.int32)
    seg_tiled = (jnp.arange(nt, dtype=jnp.int32)[:, None] * LT + local_off).T.reshape(-1).astype(jnp.int32)
    seg_len = m.T.reshape(-1).astype(jnp.int32)
    n_valid = tiles_end[-1:].astype(jnp.int32)
    tile_ids = jnp.arange(cfg.moe_tiles, dtype=jnp.int32)
    tile_group = jnp.minimum(jnp.sum(tile_ids[:, None] >= tiles_end[None, :], axis=1), N_GROUPS - 1).astype(jnp.int32)
    tile_base = tile_ids * MOE_TM
    seg_end = seg_sorted + seg_len
    tile_lo = jnp.sum(seg_end[None, :] <= tile_base[:, None], axis=1).astype(jnp.int32)
    tile_hi = jnp.sum(seg_sorted[None, :] < tile_base[:, None] + MOE_TM, axis=1).astype(jnp.int32)
    return Plan(seg_tiled, seg_sorted, seg_len, tile_group, n_valid, tile_lo, tile_hi)


def _rope_tables(n):
    pos = jnp.arange(n)
    rc = jnp.stack([(pos // GRID_W).astype(F32), (pos % GRID_W).astype(F32)], axis=1)
    freqs = ROPE_THETA ** (-jnp.arange(AXIS_PAIRS, dtype=F32) / AXIS_PAIRS)
    ang = rc[:, :, None] * freqs[None, None, :]
    cos = jnp.cos(ang)[:, :, None, :]
    sin = jnp.sin(ang)[:, :, None, :] * jnp.array([-1.0, 1.0], F32)[None, None, :, None]
    cos = jnp.broadcast_to(cos, (n, 2, 2, AXIS_PAIRS)).reshape(n, HEAD_DIM)
    sin = jnp.broadcast_to(sin, (n, 2, 2, AXIS_PAIRS)).reshape(n, HEAD_DIM)
    reps = LANES // HEAD_DIM
    return jnp.tile(cos, (1, reps)), jnp.tile(sin, (1, reps))


def _block_diag_mean(width):
    r = jnp.arange(width)
    return jnp.where((r[:, None] // HEAD_DIM) == (r[None, :] // HEAD_DIM), 1.0 / HEAD_DIM, 0.0).astype(BF16)


def _forward(cfg, x_prompt, x_sample, cache_k, cache_v, c, c_ctx, w_mod, b_mod, norm1_g, norm2_g,
             w_in, q_norm_g, k_norm_g, conv_dw_w, conv_dw_b, conv_ln_g, conv_ln_b, w_out,
             w_router, b_router, w_gate, w_up, w_down):
    depth = cfg.depth
    assert cfg.ctx_seq == CONV_ROWS and cfg.t_ctx % TM == 0 and cfg.smp_seq % TM == 0
    assert cfg.t_ctx % cfg.smp_seq == 0 and cfg.smp_batch + 1 <= 8

    x = (x_prompt.reshape(cfg.t_ctx, D_MODEL), x_sample.reshape(-1, D_MODEL))
    cvec = jnp.zeros((8, D_MODEL), F32).at[0].set(c_ctx).at[1:1 + cfg.smp_batch].set(c)
    mod = _modulation(cvec, w_mod, b_mod)

    n1 = norm1_g.reshape(depth, 1, D_MODEL)
    n2 = norm2_g.reshape(depth, 1, D_MODEL)
    gain = jnp.concatenate([jnp.tile(q_norm_g, (1, N_Q_HEADS)), jnp.tile(k_norm_g, (1, N_KV_HEADS))],
                           axis=1).reshape(depth, 1, QK_W)
    wg_b = w_gate.astype(BF16).reshape(depth * N_GROUPS, EXPERTS_PER_GROUP, D_MODEL, D_FF)
    wu_b = w_up.astype(BF16).reshape(depth * N_GROUPS, EXPERTS_PER_GROUP, D_MODEL, D_FF)
    wd_b = w_down.astype(BF16).reshape(depth * N_GROUPS, EXPERTS_PER_GROUP, D_FF, D_MODEL)
    dw_w = jnp.pad(conv_dw_w, ((0, 0), (HALO - CONV_PAD, 32 - CONV_K - (HALO - CONV_PAD)), (0, 0)))
    dw_b = conv_dw_b.reshape(depth, 1, CONV_CH)
    ln_g = conv_ln_g.reshape(depth, 1, CONV_CH)
    ln_b = conv_ln_b.reshape(depth, 1, CONV_CH)
    wr = jnp.pad(w_router, ((0, 0), (0, LANES - N_EXPERTS)))
    br = b_router.reshape(N_EXPERTS, 1)
    cos, sin = _rope_tables(cfg.smp_seq)
    bdq = _block_diag_mean(Q_W // 2)
    bdk = _block_diag_mean(KV_W)
    ck = cache_k.reshape(cfg.smp_batch, depth, cfg.past, KV_W)
    cv = cache_v.reshape(cfg.smp_batch, depth, cfg.past, KV_W)

    new_k = jnp.zeros((cfg.ctx_batch, depth, cfg.ctx_seq, KV_W), F32)
    new_v = jnp.zeros((cfg.ctx_batch, depth, cfg.ctx_seq, KV_W), F32)
    for l in range(depth):
        q, k, v, z, new_k, new_v = _inproj(cfg, l, x, mod, n1, w_in, gain, bdq, bdk, cos, sin, new_k, new_v)
        a_pair = _attention(cfg, l, q, k, v, ck, cv)
        cv_out = _conv(cfg, l, z, dw_w, dw_b, ln_g, ln_b)
        xmc, xms, hl, lp, cnt = _outproj(cfg, l, a_pair, cv_out, x, mod, n2, w_out, wr, br)
        plan = _plan(cfg, cnt[:, :N_GROUPS, 0])
        ys = _moe(cfg, l, hl.reshape(cfg.n_tiles * LT, XS_W), plan, wg_b, wu_b, wd_b)
        x = _combine(cfg, l, (xmc, xms), ys, lp, mod, plan)

    y_prompt = x[0].reshape(cfg.ctx_batch, cfg.ctx_seq, D_MODEL)
    y_sample = x[1].reshape(cfg.smp_batch, cfg.smp_seq, D_MODEL)
    cache_shape = (cfg.ctx_batch, depth, cfg.ctx_seq, N_KV_HEADS, HEAD_DIM)
    return y_prompt, y_sample, new_k.reshape(cache_shape), new_v.reshape(cache_shape)


def kernel(x_prompt, x_sample, cache_k, cache_v, c, c_ctx, w_mod, b_mod, norm1_g, norm2_g, w_in, q_norm_g,
           k_norm_g, conv_dw_w, conv_dw_b, conv_ln_g, conv_ln_b, w_out, w_router, b_router, w_gate, w_up, w_down):
    cfg = Cfg(ctx_batch=x_prompt.shape[0], ctx_seq=x_prompt.shape[1], smp_batch=x_sample.shape[0],
              smp_seq=x_sample.shape[1], past=cache_k.shape[2], depth=w_mod.shape[0])
    return _forward(cfg, x_prompt, x_sample, cache_k, cache_v, c, c_ctx, w_mod, b_mod, norm1_g, norm2_g,
                    w_in, q_norm_g, k_norm_g, conv_dw_w, conv_dw_b, conv_ln_g, conv_ln_b, w_out,
                    w_router, b_router, w_gate, w_up, w_down)
```

```python
import functools
import math
from typing import NamedTuple

import jax
import jax.numpy as jnp
from jax import lax
from jax.experimental import pallas as pl
from jax.experimental.pallas import tpu as pltpu

F32 = jnp.float32
BF16 = jnp.bfloat16

D_MODEL = 1024
HEAD_DIM = 64
N_Q_HEADS = 8
N_KV_HEADS = 2
Q_PER_KV = N_Q_HEADS // N_KV_HEADS
Q_W = N_Q_HEADS * HEAD_DIM
KV_W = N_KV_HEADS * HEAD_DIM
QK_W = Q_W + KV_W
CONV_CH = D_MODEL // 2
CONV_K = 31
CONV_PAD = CONV_K // 2
IN_COLS = Q_W + 2 * KV_W + 2 * CONV_CH
GRID_W = 64
AXIS_PAIRS = HEAD_DIM // 4
ROPE_THETA = 10000.0
N_EXPERTS = 16
N_GROUPS = 4
EXPERTS_PER_GROUP = N_EXPERTS // N_GROUPS
D_FF = 512
EPS = 1e-6

LANES = 128
BF16_ROWS = 16
TM = 512
LT = 640
MOE_TM = 512
CONV_ROWS = 256
HALO = 16
CONV_INNER = CONV_ROWS + 16
PAD_ROWS = 24 + CONV_INNER + 8
GATE_W = LANES
XS_W = D_MODEL + GATE_W
SEG_SIZES = (512, 256, 128, 64, 32, 16)
Q_SCALE = (1.0 / math.sqrt(HEAD_DIM)) * math.log2(math.e)
VMEM_LIMIT = 48 * 1024 * 1024


class Cfg(NamedTuple):
    ctx_batch: int
    ctx_seq: int
    smp_batch: int
    smp_seq: int
    past: int
    depth: int

    @property
    def t_ctx(self):
        return self.ctx_batch * self.ctx_seq

    @property
    def t_all(self):
        return self.t_ctx + self.smp_batch * self.smp_seq

    @property
    def n_tiles(self):
        return self.t_all // TM

    @property
    def n_ctx_tiles(self):
        return self.t_ctx // TM

    @property
    def smp_tiles(self):
        return self.smp_seq // TM

    @property
    def moe_tiles(self):
        rows = self.t_all + self.n_tiles * N_GROUPS * (BF16_ROWS - 1) + N_GROUPS * (MOE_TM - 1)
        return -(-rows // MOE_TM)


def _mod_row(cfg, i):
    return jnp.where(i < cfg.n_ctx_tiles, 0, 1 + (i - cfg.n_ctx_tiles) // cfg.smp_tiles)


def _pair_specs(cfg, width, tile_of=lambda i: i):
    nct = cfg.n_ctx_tiles
    return [pl.BlockSpec((TM, width), lambda i, *_: (jnp.minimum(tile_of(i), nct - 1), 0)),
            pl.BlockSpec((TM, width), lambda i, *_: (jnp.maximum(tile_of(i) - nct, 0), 0))]


def _pair_shapes(cfg, width, dtype):
    return [jax.ShapeDtypeStruct((cfg.t_ctx, width), dtype),
            jax.ShapeDtypeStruct((cfg.t_all - cfg.t_ctx, width), dtype)]


def _load_pair(is_ctx, c_ref, s_ref):
    return jnp.where(is_ctx, c_ref[...], s_ref[...])


def _store_pair(is_ctx, c_ref, s_ref, val):
    @pl.when(is_ctx)
    def _():
        c_ref[...] = val

    @pl.when(jnp.logical_not(is_ctx))
    def _():
        s_ref[...] = val


def _cparams(sem, vmem=VMEM_LIMIT):
    return pltpu.CompilerParams(dimension_semantics=sem, vmem_limit_bytes=vmem)


def _dot(a, b):
    return jnp.dot(a, b, preferred_element_type=F32)


def _dot_nt(a, b):
    return lax.dot_general(a, b, (((1,), (1,)), ((), ())), preferred_element_type=F32)


def _sigmoid(x):
    return 1.0 / (1.0 + jnp.exp(-x))


def _mod_kernel(c_ref, w_ref, b_ref, o_ref):
    c = c_ref[...]
    a = (c * _sigmoid(c)).astype(BF16)
    o_ref[0] = _dot(a, w_ref[0].astype(BF16)) + b_ref[0]


def _modulation(cvec, w_mod, b_mod):
    depth = w_mod.shape[0]
    out = pl.pallas_call(
        _mod_kernel,
        grid=(depth, 6),
        in_specs=[
            pl.BlockSpec((8, D_MODEL), lambda l, j: (0, 0)),
            pl.BlockSpec((1, D_MODEL, D_MODEL), lambda l, j: (l, 0, j)),
            pl.BlockSpec((1, 1, D_MODEL), lambda l, j: (l, 0, j)),
        ],
        out_specs=pl.BlockSpec((1, 8, D_MODEL), lambda l, j: (l, 0, j)),
        out_shape=jax.ShapeDtypeStruct((depth, 8, 6 * D_MODEL), F32),
        compiler_params=_cparams(("arbitrary", "arbitrary")),
        name="modulation",
    )(cvec, w_mod, b_mod.reshape(depth, 1, 6 * D_MODEL))
    return out.reshape(depth, 8, 6, D_MODEL)


def _inproj_kernel(xc_ref, xs_ref, mod_ref, n1_ref, w_ref, gain_ref, bdq_ref, bdk_ref, cos_ref, sin_ref,
                   kc_in, vc_in, q_ref, k_ref, v_ref, z_ref, kc_ref, vc_ref, w_bf, *, n_ctx_tiles, ctx_seq):
    del kc_in, vc_in
    i = pl.program_id(0)

    @pl.when(i == 0)
    def _():
        w_bf[...] = w_ref[0].astype(BF16)

    x = _load_pair(i < n_ctx_tiles, xc_ref, xs_ref)
    mod = mod_ref[0, 0]
    sh1, sc1 = mod[0:1], mod[1:2]
    ms = jnp.mean(x * x, axis=-1, keepdims=True)
    h = (x * lax.rsqrt(ms + EPS) * n1_ref[0]) * (1.0 + sc1) + sh1
    proj = _dot(h.astype(BF16), w_bf[...])

    qk = proj[:, :QK_W]
    sq = (qk * qk).astype(BF16)
    half = Q_W // 2
    msq = jnp.concatenate([_dot(sq[:, :half], bdq_ref[...]),
                           _dot(sq[:, half:Q_W], bdq_ref[...]),
                           _dot(sq[:, Q_W:], bdk_ref[...])], axis=1)
    qkn = qk * lax.rsqrt(msq + EPS) * gain_ref[0]

    v = proj[:, QK_W:QK_W + KV_W]
    v_ref[...] = v
    a = proj[:, QK_W + KV_W:QK_W + KV_W + CONV_CH]
    b = proj[:, QK_W + KV_W + CONV_CH:]
    z_ref[...] = a * _sigmoid(b)

    @pl.when(i < n_ctx_tiles)
    def _():
        q_ref[...] = (qkn[:, :Q_W] * Q_SCALE).astype(BF16)
        k = qkn[:, Q_W:]
        k_ref[...] = k
        for s in range(TM // ctx_seq):
            kc_ref[s, 0] = k[s * ctx_seq:(s + 1) * ctx_seq]
            vc_ref[s, 0] = v[s * ctx_seq:(s + 1) * ctx_seq]

    @pl.when(i >= n_ctx_tiles)
    def _():
        cos = cos_ref[...]
        sin = sin_ref[...]
        first = (lax.broadcasted_iota(jnp.int32, (TM, LANES), 1) % 32) < 16
        for c in range(QK_W // LANES):
            blk = qkn[:, c * LANES:(c + 1) * LANES]
            partner = jnp.where(first, pltpu.roll(blk, LANES - 16, 1), pltpu.roll(blk, 16, 1))
            rot = blk * cos + partner * sin
            if c < Q_W // LANES:
                q_ref[:, c * LANES:(c + 1) * LANES] = (rot * Q_SCALE).astype(BF16)
            else:
                k_ref[...] = rot


def _inproj(cfg, l, x_pair, mod, n1, w_in, gain, bdq, bdk, cos, sin, kc, vc):
    t = cfg.t_all
    nct, st = cfg.n_ctx_tiles, cfg.smp_tiles
    rope_map = lambda i: (jnp.maximum(i - nct, 0) % st, 0)
    seqs = TM // cfg.ctx_seq
    cache_spec = lambda: pl.BlockSpec((seqs, 1, cfg.ctx_seq, KV_W), lambda i: (jnp.minimum(i, nct - 1), l, 0, 0))
    return pl.pallas_call(
        functools.partial(_inproj_kernel, n_ctx_tiles=nct, ctx_seq=cfg.ctx_seq),
        grid=(cfg.n_tiles,),
        in_specs=_pair_specs(cfg, D_MODEL) + [
            pl.BlockSpec((1, 1, 6, D_MODEL), lambda i: (l, _mod_row(cfg, i), 0, 0)),
            pl.BlockSpec((1, 1, D_MODEL), lambda i: (l, 0, 0)),
            pl.BlockSpec((1, D_MODEL, IN_COLS), lambda i: (l, 0, 0)),
            pl.BlockSpec((1, 1, QK_W), lambda i: (l, 0, 0)),
            pl.BlockSpec((Q_W // 2, Q_W // 2), lambda i: (0, 0)),
            pl.BlockSpec((KV_W, KV_W), lambda i: (0, 0)),
            pl.BlockSpec((TM, LANES), rope_map),
            pl.BlockSpec((TM, LANES), rope_map),
            pl.BlockSpec(memory_space=pl.ANY),
            pl.BlockSpec(memory_space=pl.ANY),
        ],
        out_specs=[
            pl.BlockSpec((TM, Q_W), lambda i: (i, 0)),
            pl.BlockSpec((TM, KV_W), lambda i: (i, 0)),
            pl.BlockSpec((TM, KV_W), lambda i: (i, 0)),
            pl.BlockSpec((TM, CONV_CH), lambda i: (i, 0)),
            cache_spec(), cache_spec(),
        ],
        out_shape=[
            jax.ShapeDtypeStruct((t, Q_W), BF16),
            jax.ShapeDtypeStruct((t, KV_W), F32),
            jax.ShapeDtypeStruct((t, KV_W), F32),
            jax.ShapeDtypeStruct((t, CONV_CH), F32),
            jax.ShapeDtypeStruct(kc.shape, F32),
            jax.ShapeDtypeStruct(vc.shape, F32),
        ],
        scratch_shapes=[pltpu.VMEM((D_MODEL, IN_COLS), BF16)],
        input_output_aliases={10: 4, 11: 5},
        compiler_params=_cparams(("arbitrary",)),
        name="inproj",
    )(*x_pair, mod, n1, w_in, gain, bdq, bdk, cos, sin, kc, vc)


def _mixer_kernel(*refs, rq, n_past, seq_blocks):
    if n_past:
        q_ref, k_ref, v_ref, ck_ref, cv_ref = refs[:5]
        refs = refs[5:]
    else:
        q_ref, k_ref, v_ref = refs[:3]
        refs = refs[3:]
    z_ref, zp_ref, zn_ref, taps_ref, b_ref, g_ref, beta_ref, o_ref, co_ref, krep, vrep, pad = refs
    head_w = Q_PER_KV * HEAD_DIM
    j = pl.program_id(1)

    @pl.when(j == 0)
    def _():
        src = lax.broadcasted_iota(jnp.int32, (KV_W, head_w), 0)
        dst = lax.broadcasted_iota(jnp.int32, (KV_W, head_w), 1) % HEAD_DIM
        n_new = k_ref.shape[0]
        for h in range(N_KV_HEADS):
            rep = jnp.where(src == dst + h * HEAD_DIM, 1.0, 0.0).astype(BF16)
            if n_past:
                krep[h, :n_past, :] = _dot(ck_ref[0, 0].astype(BF16), rep).astype(BF16)
                vrep[h, :n_past, :] = _dot(cv_ref[0, 0].astype(BF16), rep).astype(BF16)
            krep[h, n_past:n_past + n_new, :] = _dot(k_ref[...].astype(BF16), rep).astype(BF16)
            vrep[h, n_past:n_past + n_new, :] = _dot(v_ref[...].astype(BF16), rep).astype(BF16)

    end = HALO + CONV_ROWS
    if seq_blocks > 1:
        pad[0:HALO, :] = jnp.where(j > 0, zp_ref[...], 0.0)
        pad[end:end + HALO, :] = jnp.where(j < seq_blocks - 1, zn_ref[...], 0.0)
    else:
        pad[0:HALO, :] = jnp.zeros((HALO, CONV_CH), F32)
        pad[end:end + HALO, :] = jnp.zeros((HALO, CONV_CH), F32)
    pad[end + HALO:, :] = jnp.zeros((PAD_ROWS - end - HALO, CONV_CH), F32)
    pad[HALO:end, :] = z_ref[...]

    lane_head = lax.broadcasted_iota(jnp.int32, (rq, head_w), 1) // HEAD_DIM
    for h in range(N_KV_HEADS):
        qh = q_ref[:, h * head_w:(h + 1) * head_w]
        kh = krep[h]
        vh = vrep[h]
        acc = jnp.zeros((rq, head_w), F32)
        for g in range(Q_PER_KV):
            mask = lane_head == g
            qm = jnp.where(mask, qh, jnp.zeros_like(qh))
            s = _dot_nt(qm, kh)
            m = jnp.max(s, axis=-1, keepdims=True)
            p = jnp.exp2(s - m)
            denom = jnp.sum(p, axis=-1, keepdims=True)
            o = _dot(p.astype(BF16), vh)
            acc = jnp.where(mask, o * (1.0 / denom), acc)
        o_ref[:, h * head_w:(h + 1) * head_w] = acc.astype(BF16)
    conv_cols = [_conv_cols(pad, taps_ref, c) for c in range(CONV_CH // LANES)]
    co_ref[...] = _conv_finish(conv_cols, b_ref, g_ref, beta_ref)


def _mixer(cfg, l, q, k, v, cache_k, cache_v, z, taps, dw_b, ln_g, ln_b):
    head_w = Q_PER_KV * HEAD_DIM
    rq = CONV_ROWS
    hb = CONV_ROWS // HALO
    last = cfg.t_all // HALO - 1
    vec = lambda: pl.BlockSpec((1, 1, CONV_CH), lambda b, j: (l, 0, 0))

    def conv_specs(row_block):
        return [
            pl.BlockSpec((CONV_ROWS, CONV_CH), lambda b, j: (row_block(b, j), 0)),
            pl.BlockSpec((HALO, CONV_CH), lambda b, j: (jnp.maximum(row_block(b, j) * hb - 1, 0), 0)),
            pl.BlockSpec((HALO, CONV_CH), lambda b, j: (jnp.minimum(row_block(b, j) * hb + hb, last), 0)),
            pl.BlockSpec((1, CONV_CH // LANES, 4, 4 * LANES, 2 * LANES), lambda b, j: (l, 0, 0, 0, 0)),
            vec(), vec(), vec(),
        ]

    def scratch(sk):
        return [pltpu.VMEM((N_KV_HEADS, sk, head_w), BF16), pltpu.VMEM((N_KV_HEADS, sk, head_w), BF16),
                pltpu.VMEM((PAD_ROWS, CONV_CH), F32)]

    sc = cfg.ctx_seq
    a_ctx, c_ctx = pl.pallas_call(
        functools.partial(_mixer_kernel, rq=rq, n_past=0, seq_blocks=1),
        grid=(cfg.ctx_batch, 1),
        in_specs=[
            pl.BlockSpec((sc, Q_W), lambda b, j: (b, 0)),
            pl.BlockSpec((sc, KV_W), lambda b, j: (b, 0)),
            pl.BlockSpec((sc, KV_W), lambda b, j: (b, 0)),
        ] + conv_specs(lambda b, j: b),
        out_specs=[pl.BlockSpec((sc, Q_W), lambda b, j: (b, 0)),
                   pl.BlockSpec((sc, CONV_CH), lambda b, j: (b, 0))],
        out_shape=[jax.ShapeDtypeStruct((cfg.t_ctx, Q_W), BF16),
                   jax.ShapeDtypeStruct((cfg.t_ctx, CONV_CH), BF16)],
        scratch_shapes=scratch(sc),
        compiler_params=_cparams(("arbitrary", "arbitrary")),
        name="mixer_ctx",
    )(q, k, v, z, z, z, taps, dw_b, ln_g, ln_b)
    ss = cfg.smp_seq
    sk = cfg.past + ss
    kv_base = cfg.t_ctx // ss
    q_base = cfg.t_ctx // rq
    nqb = ss // rq
    t_smp = cfg.smp_batch * ss
    a_smp, c_smp = pl.pallas_call(
        functools.partial(_mixer_kernel, rq=rq, n_past=cfg.past, seq_blocks=nqb),
        grid=(cfg.smp_batch, nqb),
        in_specs=[
            pl.BlockSpec((rq, Q_W), lambda b, j: (q_base + b * nqb + j, 0)),
            pl.BlockSpec((ss, KV_W), lambda b, j: (kv_base + b, 0)),
            pl.BlockSpec((ss, KV_W), lambda b, j: (kv_base + b, 0)),
            pl.BlockSpec((1, 1, cfg.past, KV_W), lambda b, j: (b, l, 0, 0)),
            pl.BlockSpec((1, 1, cfg.past, KV_W), lambda b, j: (b, l, 0, 0)),
        ] + conv_specs(lambda b, j: q_base + b * nqb + j),
        out_specs=[pl.BlockSpec((rq, Q_W), lambda b, j: (b * nqb + j, 0)),
                   pl.BlockSpec((rq, CONV_CH), lambda b, j: (b * nqb + j, 0))],
        out_shape=[jax.ShapeDtypeStruct((t_smp, Q_W), BF16),
                   jax.ShapeDtypeStruct((t_smp, CONV_CH), BF16)],
        scratch_shapes=scratch(sk),
        compiler_params=_cparams(("arbitrary", "arbitrary")),
        name="mixer_smp",
    )(q, k, v, cache_k, cache_v, z, z, z, taps, dw_b, ln_g, ln_b)
    return (a_ctx, a_smp), (c_ctx, c_smp)


def _conv_cols(pad, taps_ref, c):
    cols = slice(c * LANES, (c + 1) * LANES)
    lhs = jnp.concatenate([pad[8 * q:8 * q + CONV_INNER, cols].astype(BF16) for q in range(4)], axis=1)
    acc = None
    for p in range(4):
        pair = _dot(lhs, taps_ref[0, c, p])
        for half in range(2):
            o = 2 * p + half
            shifted = pair[o:o + CONV_ROWS, half * LANES:(half + 1) * LANES]
            acc = shifted if acc is None else acc + shifted
    return acc


def _conv_tap_matrices(conv_dw_w):
    depth = conv_dw_w.shape[0]
    w = jnp.pad(conv_dw_w, ((0, 0), (HALO - CONV_PAD, 32 - CONV_K - (HALO - CONV_PAD)), (0, 0)))
    w = w.reshape(depth, 4, 4, 2, CONV_CH // LANES, LANES)
    m = jnp.einsum("dqphcl,lm->dcpqlhm", w, jnp.eye(LANES, dtype=F32))
    return m.reshape(depth, CONV_CH // LANES, 4, 4 * LANES, 2 * LANES).astype(BF16)


def _conv_finish(blocks, b_ref, g_ref, beta_ref):
    y = jnp.concatenate(blocks, axis=1) + b_ref[0]
    mu = jnp.mean(y, axis=-1, keepdims=True)
    yc = y - mu
    var = jnp.mean(yc * yc, axis=-1, keepdims=True)
    yn = yc * lax.rsqrt(var + EPS) * g_ref[0] + beta_ref[0]
    return (yn * _sigmoid(yn)).astype(BF16)


def _split_bf16(x):
    hi = x.astype(BF16)
    lo = (x - hi.astype(F32)).astype(BF16)
    return hi, lo


def _outproj_kernel(ac_ref, as_ref, cc_ref, cs_ref, xc_ref, xs_ref, mod_ref, n2_ref, wo_ref, wr_ref, br_ref,
                    xmc_ref, xms_ref, hl_ref, lp_ref, cnt_ref, wo_bf, h_prev, lt_prev, *, n_tiles, n_ctx_tiles):
    step = pl.program_id(0)

    @pl.when(step == 0)
    def _():
        wo_bf[...] = wo_ref[0].astype(BF16)
        h_prev[...] = jnp.zeros_like(h_prev)
        lt_prev[...] = jnp.zeros_like(lt_prev)

    is_ctx = jnp.minimum(step, n_tiles - 1) < n_ctx_tiles
    ac = jnp.concatenate([_load_pair(is_ctx, ac_ref, as_ref), _load_pair(is_ctx, cc_ref, cs_ref)], axis=1)
    y = _dot(ac, wo_bf[...])
    mod = mod_ref[0, 0]
    g1, sh2, sc2 = mod[2:3], mod[3:4], mod[4:5]
    xm = _load_pair(is_ctx, xc_ref, xs_ref) + g1 * y
    ms = jnp.mean(xm * xm, axis=-1, keepdims=True)
    h2 = (xm * lax.rsqrt(ms + EPS) * n2_ref[0]) * (1.0 + sc2) + sh2
    h_new, h_lo = _split_bf16(h2)
    w_hi, w_lo = _split_bf16(wr_ref[...])
    logits = _dot(h_new, w_hi) + (_dot(h_new, w_lo) + _dot(h_lo, w_hi))
    lt_new = logits.T[0:N_EXPERTS, :]

    h_hi = h_prev[...]
    s_all = _sigmoid(lt_prev[...])
    sb_all = s_all + br_ref[...]

    rows8 = 2 * EXPERTS_PER_GROUP
    row = lax.broadcasted_iota(jnp.int32, (rows8, TM), 0)
    member = row % EXPERTS_PER_GROUP
    second = row >= EXPERTS_PER_GROUP

    def partner(x, k):
        return jnp.where(member + k < EXPERTS_PER_GROUP,
                         pltpu.roll(x, rows8 - k, 0), pltpu.roll(x, EXPERTS_PER_GROUP - k, 0))

    def other_group(x):
        return pltpu.roll(x, EXPERTS_PER_GROUP, 0)

    def group_sum(x):
        return x + partner(x, 1) + (partner(x, 2) + partner(x, 3))

    def ahead(other, mine, other_first):
        return jnp.where((other >= mine) if other_first else (other > mine), 1.0, 0.0)

    sel, gscore = [], []
    for hv in range(2):
        sb8 = sb_all[hv * rows8:(hv + 1) * rows8]
        rank = jnp.zeros((rows8, TM), F32)
        for k in range(1, EXPERTS_PER_GROUP):
            pk = partner(sb8, k)
            rank = rank + jnp.where(member + k >= EXPERTS_PER_GROUP, ahead(pk, sb8, True), ahead(pk, sb8, False))
        sel.append(rank < 1.5)
        gscore.append(group_sum(jnp.where(sel[hv], sb8, 0.0)))
    gate8 = []
    chosen_f = []
    for hv in range(2):
        mine, far = gscore[hv], gscore[1 - hv]
        near = other_group(mine)
        rank = jnp.where(second, ahead(near, mine, True), ahead(near, mine, False))
        rank = rank + ahead(far, mine, hv == 1) + ahead(other_group(far), mine, hv == 1)
        chosen = rank < 0.5
        chosen_f.append(jnp.where(chosen, 1.0, 0.0))
        s8 = s_all[hv * rows8:(hv + 1) * rows8]
        gate8.append(jnp.where(chosen, jnp.where(sel[hv], s8, 0.0), 0.0))
    total = group_sum(gate8[0]) + group_sum(gate8[1])
    total = total + other_group(total)
    folded = gate8[0] + gate8[1]
    gates4 = ((folded + other_group(folded)) * (1.0 / total))[0:EXPERTS_PER_GROUP]

    cgf = [chosen_f[0][0:1], chosen_f[0][EXPERTS_PER_GROUP:EXPERTS_PER_GROUP + 1],
           chosen_f[1][0:1], chosen_f[1][EXPERTS_PER_GROUP:EXPERTS_PER_GROUP + 1]]
    onehot_g = jnp.concatenate(cgf + [jnp.zeros((8 - N_GROUPS, TM), F32)], axis=0).astype(BF16)
    before = (lax.broadcasted_iota(jnp.int32, (TM, TM), 0) < lax.broadcasted_iota(jnp.int32, (TM, TM), 1))
    prior = _dot(onehot_g, jnp.where(before, 1.0, 0.0).astype(BF16))
    lp = jnp.zeros((1, TM), F32)
    start = jnp.zeros((1, 1), F32)
    counts = []
    for g in range(N_GROUPS):
        n_g = jnp.sum(cgf[g], axis=-1, keepdims=True)
        counts.append(n_g)
        lp = lp + cgf[g] * (start + prior[g:g + 1, :])
        start = start + jnp.floor((n_g + (BF16_ROWS - 1)) * (1.0 / BF16_ROWS)) * BF16_ROWS
    lp_i = lp.astype(jnp.int32)
    perm = jnp.where(lax.broadcasted_iota(jnp.int32, (LT, TM), 0) == lp_i, 1.0, 0.0).astype(BF16)

    hl_ref[0, :, :D_MODEL] = _dot(perm, h_hi).astype(BF16)
    g_hi, g_lo = _split_bf16(gates4)
    gate_rows = jnp.concatenate([g_hi.astype(F32), g_lo.astype(F32), jnp.zeros((GATE_W - 8, TM), F32)],
                                axis=0).astype(BF16)
    hl_ref[0, :, D_MODEL:] = _dot_nt(perm, gate_rows).astype(BF16)

    lp_hi = jnp.floor(lp * (1.0 / 32.0))
    lp_parts = jnp.concatenate([lp_hi, lp - 32.0 * lp_hi, jnp.zeros((LANES - 2, TM), F32)], axis=0).astype(BF16)
    eye = jnp.where(lax.broadcasted_iota(jnp.int32, (TM, TM), 0) == lax.broadcasted_iota(jnp.int32, (TM, TM), 1),
                    1.0, 0.0).astype(BF16)
    lp_ref[...] = _dot_nt(eye, lp_parts)

    cnt = jnp.concatenate([jnp.broadcast_to(c, (1, LANES)) for c in counts]
                          + [jnp.zeros((8 - N_GROUPS, LANES), F32)], axis=0)
    cnt_ref[0] = cnt.astype(jnp.int32)

    h_prev[...] = h_new
    lt_prev[...] = lt_new
    _store_pair(is_ctx, xmc_ref, xms_ref, xm)


def _outproj(cfg, l, a_pair, c_pair, x_pair, mod, n2, w_out, w_router, b_router):
    t, nt, nct = cfg.t_all, cfg.n_tiles, cfg.n_ctx_tiles
    tile_a = lambda i: jnp.minimum(i, nt - 1)
    tile_b = lambda i: jnp.maximum(i - 1, 0)
    return pl.pallas_call(
        functools.partial(_outproj_kernel, n_tiles=nt, n_ctx_tiles=nct),
        grid=(nt + 1,),
        in_specs=_pair_specs(cfg, Q_W, tile_a) + _pair_specs(cfg, CONV_CH, tile_a)
        + _pair_specs(cfg, D_MODEL, tile_a) + [
            pl.BlockSpec((1, 1, 6, D_MODEL), lambda i: (l, _mod_row(cfg, tile_a(i)), 0, 0)),
            pl.BlockSpec((1, 1, D_MODEL), lambda i: (l, 0, 0)),
            pl.BlockSpec((1, D_MODEL, D_MODEL), lambda i: (l, 0, 0)),
            pl.BlockSpec((D_MODEL, LANES), lambda i: (0, 0)),
            pl.BlockSpec((N_EXPERTS, 1), lambda i: (0, 0)),
        ],
        out_specs=_pair_specs(cfg, D_MODEL, tile_a) + [
            pl.BlockSpec((1, LT, XS_W), lambda i: (tile_b(i), 0, 0)),
            pl.BlockSpec((TM, LANES), lambda i: (tile_b(i), 0)),
            pl.BlockSpec((1, 8, LANES), lambda i: (tile_b(i), 0, 0)),
        ],
        out_shape=_pair_shapes(cfg, D_MODEL, F32) + [
            jax.ShapeDtypeStruct((nt, LT, XS_W), BF16),
            jax.ShapeDtypeStruct((t, LANES), F32),
            jax.ShapeDtypeStruct((nt, 8, LANES), jnp.int32),
        ],
        scratch_shapes=[pltpu.VMEM((D_MODEL, D_MODEL), BF16), pltpu.VMEM((TM, D_MODEL), BF16),
                        pltpu.VMEM((N_EXPERTS, TM), F32)],
        compiler_params=_cparams(("arbitrary",)),
        name="outproj",
    )(*a_pair, *c_pair, *x_pair, mod, n2, w_out, w_router, b_router)


def _segment_pieces(n, src_row, dst_row, src_ref, dst_ref, sem, fn):
    for size in SEG_SIZES:
        off = n & ~(2 * size - 1)

        @pl.when((n & size) != 0)
        def _():
            fn(pltpu.make_async_copy(
                src_ref.at[pl.ds(pl.multiple_of(src_row + off, BF16_ROWS), size)],
                dst_ref.at[pl.ds(pl.multiple_of(dst_row + off, BF16_ROWS), size)],
                sem))


def _start(cp):
    cp.start()


def _wait(cp):
    cp.wait()


def _moe_kernel(tg_ref, nv_ref, ssrc_ref, sdst_ref, slen_ref, tlo_ref, thi_ref,
                hl_ref, wg_ref, wu_ref, wd_ref, y_ref, xbuf, sem):
    del tg_ref
    i = pl.program_id(0)
    nv = nv_ref[0]

    def tile_copies(tile, slot, fn):
        base = tile * MOE_TM

        def body(s, carry):
            d = sdst_ref[s]
            lo = jnp.maximum(d, base)
            hi = jnp.minimum(d + slen_ref[s], base + MOE_TM)
            n = jnp.maximum(hi - lo, 0)
            _segment_pieces(n, ssrc_ref[s] + (lo - d), lo - base, hl_ref, xbuf.at[slot], sem.at[slot], fn)
            return carry

        lax.fori_loop(tlo_ref[tile], thi_ref[tile], body, 0)

    @pl.when(i == 0)
    def _():
        xbuf[...] = jnp.zeros_like(xbuf)
        tile_copies(0, 0, _start)

    @pl.when(i + 1 < nv)
    def _():
        tile_copies(i + 1, (i + 1) % 2, _start)

    @pl.when(i >= nv)
    def _():
        y_ref[...] = jnp.zeros_like(y_ref)

    @pl.when(i < nv)
    def _():
        slot = i % 2
        tile_copies(i, slot, _wait)
        xg = xbuf[slot]
        x = xg[:, :D_MODEL]
        gcols = xg[:, D_MODEL:].astype(F32)
        hid = []
        for j in range(EXPERTS_PER_GROUP):
            gate = gcols[:, j:j + 1] + gcols[:, EXPERTS_PER_GROUP + j:EXPERTS_PER_GROUP + j + 1]
            hg = _dot(x, wg_ref[0, j])
            hu = _dot(x, wu_ref[0, j])
            hid.append(((hg * _sigmoid(hg)) * hu * gate).astype(BF16))
        hcat = jnp.concatenate(hid, axis=1)
        wd = wd_ref[0].reshape(EXPERTS_PER_GROUP * D_FF, D_MODEL)
        y_ref[...] = _dot(hcat, wd).astype(BF16)


def _moe(cfg, l, hl, plan, w_gate, w_up, w_down):
    nmt = cfg.moe_tiles

    def w_map(i, tg, nv, *_):
        return (l * N_GROUPS + tg[jnp.minimum(i, nv[0] - 1)], 0, 0, 0)

    return pl.pallas_call(
        _moe_kernel,
        grid_spec=pltpu.PrefetchScalarGridSpec(
            num_scalar_prefetch=7,
            grid=(nmt,),
            in_specs=[
                pl.BlockSpec(memory_space=pl.ANY),
                pl.BlockSpec((1, EXPERTS_PER_GROUP, D_MODEL, D_FF), w_map),
                pl.BlockSpec((1, EXPERTS_PER_GROUP, D_MODEL, D_FF), w_map),
                pl.BlockSpec((1, EXPERTS_PER_GROUP, D_FF, D_MODEL), w_map),
            ],
            out_specs=pl.BlockSpec((MOE_TM, D_MODEL), lambda i, *_: (i, 0)),
            scratch_shapes=[pltpu.VMEM((2, MOE_TM, XS_W), BF16), pltpu.SemaphoreType.DMA((2,))],
        ),
        out_shape=jax.ShapeDtypeStruct((nmt * MOE_TM, D_MODEL), BF16),
        compiler_params=_cparams(("arbitrary",)),
        name="moe",
    )(plan.tile_group, plan.n_valid, plan.seg_tiled, plan.seg_sorted, plan.seg_len, plan.tile_lo, plan.tile_hi,
      hl, w_gate, w_up, w_down)


def _combine_kernel(stiled_ref, ssorted_ref, slen_ref, xmc_ref, xms_ref, ys_ref, lp_ref, mod_ref,
                    oc_ref, os_ref, ybuf, sem, *, n_tiles, n_ctx_tiles):
    t = pl.program_id(0)

    def tile_copies(tile, slot, fn):
        for g in range(N_GROUPS):
            s = g * n_tiles + tile
            _segment_pieces(slen_ref[s], ssorted_ref[s], stiled_ref[s] - tile * LT,
                            ys_ref, ybuf.at[slot], sem.at[slot], fn)

    @pl.when(t == 0)
    def _():
        ybuf[...] = jnp.zeros_like(ybuf)
        tile_copies(0, 0, _start)

    @pl.when(t + 1 < n_tiles)
    def _():
        tile_copies(t + 1, (t + 1) % 2, _start)

    slot = t % 2
    tile_copies(t, slot, _wait)
    lp = lp_ref[...]
    pos = (lp[:, 0:1] * 32.0 + lp[:, 1:2]).astype(jnp.int32)
    perm_t = jnp.where(lax.broadcasted_iota(jnp.int32, (TM, LT), 1) == pos, 1.0, 0.0).astype(BF16)
    y = _dot(perm_t, ybuf[slot])
    g2 = mod_ref[0, 0][5:6]
    is_ctx = t < n_ctx_tiles
    _store_pair(is_ctx, oc_ref, os_ref, _load_pair(is_ctx, xmc_ref, xms_ref) + g2 * y)


def _combine(cfg, l, xm_pair, ys, lp, mod, plan):
    nt = cfg.n_tiles
    return pl.pallas_call(
        functools.partial(_combine_kernel, n_tiles=nt, n_ctx_tiles=cfg.n_ctx_tiles),
        grid_spec=pltpu.PrefetchScalarGridSpec(
            num_scalar_prefetch=3,
            grid=(nt,),
            in_specs=_pair_specs(cfg, D_MODEL) + [
                pl.BlockSpec(memory_space=pl.ANY),
                pl.BlockSpec((TM, LANES), lambda i, *_: (i, 0)),
                pl.BlockSpec((1, 1, 6, D_MODEL), lambda i, *_: (l, _mod_row(cfg, i), 0, 0)),
            ],
            out_specs=_pair_specs(cfg, D_MODEL),
            scratch_shapes=[pltpu.VMEM((2, LT, D_MODEL), BF16), pltpu.SemaphoreType.DMA((2,))],
        ),
        out_shape=_pair_shapes(cfg, D_MODEL, F32),
        compiler_params=_cparams(("arbitrary",)),
        name="combine",
    )(plan.seg_tiled, plan.seg_sorted, plan.seg_len, *xm_pair, ys, lp, mod)


class Plan(NamedTuple):
    seg_tiled: jax.Array
    seg_sorted: jax.Array
    seg_len: jax.Array
    tile_group: jax.Array
    n_valid: jax.Array
    tile_lo: jax.Array
    tile_hi: jax.Array


def _plan(cfg, counts):
    nt = cfg.n_tiles
    m = (counts + (BF16_ROWS - 1)) // BF16_ROWS * BF16_ROWS
    local_off = jnp.cumsum(m, axis=1) - m
    within_group = jnp.cumsum(m, axis=0) - m
    group_rows = jnp.sum(m, axis=0)
    group_tiles = (group_rows + (MOE_TM - 1)) // MOE_TM
    tiles_end = jnp.cumsum(group_tiles)
    group_start = (tiles_end - group_tiles) * MOE_TM
    seg_sorted = (group_start[None, :] + within_group).T.reshape(-1).astype(jnp.int32)
    seg_tiled = (jnp.arange(nt, dtype=jnp.int32)[:, None] * LT + local_off).T.reshape(-1).astype(jnp.int32)
    seg_len = m.T.reshape(-1).astype(jnp.int32)
    n_valid = tiles_end[-1:].astype(jnp.int32)
    tile_ids = jnp.arange(cfg.moe_tiles, dtype=jnp.int32)
    tile_group = jnp.minimum(jnp.sum(tile_ids[:, None] >= tiles_end[None, :], axis=1), N_GROUPS - 1).astype(jnp.int32)
    tile_base = tile_ids * MOE_TM
    seg_end = seg_sorted + seg_len
    tile_lo = jnp.sum(seg_end[None, :] <= tile_base[:, None], axis=1).astype(jnp.int32)
    tile_hi = jnp.sum(seg_sorted[None, :] < tile_base[:, None] + MOE_TM, axis=1).astype(jnp.int32)
    return Plan(seg_tiled, seg_sorted, seg_len, tile_group, n_valid, tile_lo, tile_hi)


def _rope_tables(n):
    pos = jnp.arange(n)
    rc = jnp.stack([(pos // GRID_W).astype(F32), (pos % GRID_W).astype(F32)], axis=1)
    freqs = ROPE_THETA ** (-jnp.arange(AXIS_PAIRS, dtype=F32) / AXIS_PAIRS)
    ang = rc[:, :, None] * freqs[None, None, :]
    cos = jnp.cos(ang)[:, :, None, :]
    sin = jnp.sin(ang)[:, :, None, :] * jnp.array([-1.0, 1.0], F32)[None, None, :, None]
    cos = jnp.broadcast_to(cos, (n, 2, 2, AXIS_PAIRS)).reshape(n, HEAD_DIM)
    sin = jnp.broadcast_to(sin, (n, 2, 2, AXIS_PAIRS)).reshape(n, HEAD_DIM)
    reps = LANES // HEAD_DIM
    return jnp.tile(cos, (1, reps)), jnp.tile(sin, (1, reps))


def _block_diag_mean(width):
    r = jnp.arange(width)
    return jnp.where((r[:, None] // HEAD_DIM) == (r[None, :] // HEAD_DIM), 1.0 / HEAD_DIM, 0.0).astype(BF16)


def _forward(cfg, x_prompt, x_sample, cache_k, cache_v, c, c_ctx, w_mod, b_mod, norm1_g, norm2_g,
             w_in, q_norm_g, k_norm_g, conv_dw_w, conv_dw_b, conv_ln_g, conv_ln_b, w_out,
             w_router, b_router, w_gate, w_up, w_down):
    depth = cfg.depth
    assert cfg.ctx_seq == CONV_ROWS and cfg.t_ctx % TM == 0 and cfg.smp_seq % TM == 0
    assert cfg.t_ctx % cfg.smp_seq == 0 and cfg.smp_batch + 1 <= 8

    x = (x_prompt.reshape(cfg.t_ctx, D_MODEL), x_sample.reshape(-1, D_MODEL))
    cvec = jnp.zeros((8, D_MODEL), F32).at[0].set(c_ctx).at[1:1 + cfg.smp_batch].set(c)
    mod = _modulation(cvec, w_mod, b_mod)

    n1 = norm1_g.reshape(depth, 1, D_MODEL)
    n2 = norm2_g.reshape(depth, 1, D_MODEL)
    gain = jnp.concatenate([jnp.tile(q_norm_g, (1, N_Q_HEADS)), jnp.tile(k_norm_g, (1, N_KV_HEADS))],
                           axis=1).reshape(depth, 1, QK_W)
    wg_b = w_gate.astype(BF16).reshape(depth * N_GROUPS, EXPERTS_PER_GROUP, D_MODEL, D_FF)
    wu_b = w_up.astype(BF16).reshape(depth * N_GROUPS, EXPERTS_PER_GROUP, D_MODEL, D_FF)
    wd_b = w_down.astype(BF16).reshape(depth * N_GROUPS, EXPERTS_PER_GROUP, D_FF, D_MODEL)
    taps = _conv_tap_matrices(conv_dw_w)
    dw_b = conv_dw_b.reshape(depth, 1, CONV_CH)
    ln_g = conv_ln_g.reshape(depth, 1, CONV_CH)
    ln_b = conv_ln_b.reshape(depth, 1, CONV_CH)
    wr = jnp.pad(w_router, ((0, 0), (0, LANES - N_EXPERTS)))
    br = b_router.reshape(N_EXPERTS, 1)
    cos, sin = _rope_tables(cfg.smp_seq)
    bdq = _block_diag_mean(Q_W // 2)
    bdk = _block_diag_mean(KV_W)
    ck = cache_k.reshape(cfg.smp_batch, depth, cfg.past, KV_W)
    cv = cache_v.reshape(cfg.smp_batch, depth, cfg.past, KV_W)

    new_k = jnp.zeros((cfg.ctx_batch, depth, cfg.ctx_seq, KV_W), F32)
    new_v = jnp.zeros((cfg.ctx_batch, depth, cfg.ctx_seq, KV_W), F32)
    for l in range(depth):
        q, k, v, z, new_k, new_v = _inproj(cfg, l, x, mod, n1, w_in, gain, bdq, bdk, cos, sin, new_k, new_v)
        a_pair, c_pair = _mixer(cfg, l, q, k, v, ck, cv, z, taps, dw_b, ln_g, ln_b)
        xmc, xms, hl, lp, cnt = _outproj(cfg, l, a_pair, c_pair, x, mod, n2, w_out, wr, br)
        plan = _plan(cfg, cnt[:, :N_GROUPS, 0])
        ys = _moe(cfg, l, hl.reshape(cfg.n_tiles * LT, XS_W), plan, wg_b, wu_b, wd_b)
        x = _combine(cfg, l, (xmc, xms), ys, lp, mod, plan)

    y_prompt = x[0].reshape(cfg.ctx_batch, cfg.ctx_seq, D_MODEL)
    y_sample = x[1].reshape(cfg.smp_batch, cfg.smp_seq, D_MODEL)
    cache_shape = (cfg.ctx_batch, depth, cfg.ctx_seq, N_KV_HEADS, HEAD_DIM)
    return y_prompt, y_sample, new_k.reshape(cache_shape), new_v.reshape(cache_shape)


def kernel(x_prompt, x_sample, cache_k, cache_v, c, c_ctx, w_mod, b_mod, norm1_g, norm2_g, w_in, q_norm_g,
           k_norm_g, conv_dw_w, conv_dw_b, conv_ln_g, conv_ln_b, w_out, w_router, b_router, w_gate, w_up, w_down):
    cfg = Cfg(ctx_batch=x_prompt.shape[0], ctx_seq=x_prompt.shape[1], smp_batch=x_sample.shape[0],
              smp_seq=x_sample.shape[1], past=cache_k.shape[2], depth=w_mod.shape[0])
    return _forward(cfg, x_prompt, x_sample, cache_k, cache_v, c, c_ctx, w_mod, b_mod, norm1_g, norm2_g,
                    w_in, q_norm_g, k_norm_g, conv_dw_w, conv_dw_b, conv_ln_g, conv_ln_b, w_out,
                    w_router, b_router, w_gate, w_up, w_down)
```

```python
import functools
import math
from typing import NamedTuple

import jax
import jax.numpy as jnp
from jax import lax
from jax.experimental import pallas as pl
from jax.experimental.pallas import tpu as pltpu

F32 = jnp.float32
BF16 = jnp.bfloat16

D_MODEL = 1024
HEAD_DIM = 64
N_Q_HEADS = 8
N_KV_HEADS = 2
Q_PER_KV = N_Q_HEADS // N_KV_HEADS
Q_W = N_Q_HEADS * HEAD_DIM
KV_W = N_KV_HEADS * HEAD_DIM
QK_W = Q_W + KV_W
CONV_CH = D_MODEL // 2
CONV_K = 31
CONV_PAD = CONV_K // 2
IN_COLS = Q_W + 2 * KV_W + 2 * CONV_CH
GRID_W = 64
AXIS_PAIRS = HEAD_DIM // 4
ROPE_THETA = 10000.0
N_EXPERTS = 16
N_GROUPS = 4
EXPERTS_PER_GROUP = N_EXPERTS // N_GROUPS
D_FF = 512
EPS = 1e-6

LANES = 128
BF16_ROWS = 16
TM = 512
LT = 640
MOE_TM = 512
CONV_ROWS = 256
HALO = 16
CONV_INNER = CONV_ROWS + 16
PAD_ROWS = 24 + CONV_INNER + 8
GATE_W = LANES
XS_W = D_MODEL + GATE_W
SEG_SIZES = (512, 256, 128, 64, 32, 16)
Q_SCALE = (1.0 / math.sqrt(HEAD_DIM)) * math.log2(math.e)
VMEM_LIMIT = 48 * 1024 * 1024
MIXER_VMEM_LIMIT = 56 * 1024 * 1024


class Cfg(NamedTuple):
    ctx_batch: int
    ctx_seq: int
    smp_batch: int
    smp_seq: int
    past: int
    depth: int

    @property
    def t_ctx(self):
        return self.ctx_batch * self.ctx_seq

    @property
    def t_all(self):
        return self.t_ctx + self.smp_batch * self.smp_seq

    @property
    def n_tiles(self):
        return self.t_all // TM

    @property
    def n_ctx_tiles(self):
        return self.t_ctx // TM

    @property
    def smp_tiles(self):
        return self.smp_seq // TM

    @property
    def moe_tiles(self):
        rows = self.t_all + self.n_tiles * N_GROUPS * (BF16_ROWS - 1) + N_GROUPS * (MOE_TM - 1)
        return -(-rows // MOE_TM)


def _mod_row(cfg, i):
    return jnp.where(i < cfg.n_ctx_tiles, 0, 1 + (i - cfg.n_ctx_tiles) // cfg.smp_tiles)


def _pair_specs(cfg, width, tile_of=lambda i: i):
    nct = cfg.n_ctx_tiles
    return [pl.BlockSpec((TM, width), lambda i, *_: (jnp.minimum(tile_of(i), nct - 1), 0)),
            pl.BlockSpec((TM, width), lambda i, *_: (jnp.maximum(tile_of(i) - nct, 0), 0))]


def _pair_shapes(cfg, width, dtype):
    return [jax.ShapeDtypeStruct((cfg.t_ctx, width), dtype),
            jax.ShapeDtypeStruct((cfg.t_all - cfg.t_ctx, width), dtype)]


def _load_pair(is_ctx, c_ref, s_ref):
    return jnp.where(is_ctx, c_ref[...], s_ref[...])


def _store_pair(is_ctx, c_ref, s_ref, val):
    @pl.when(is_ctx)
    def _():
        c_ref[...] = val

    @pl.when(jnp.logical_not(is_ctx))
    def _():
        s_ref[...] = val


def _cparams(sem, vmem=VMEM_LIMIT):
    return pltpu.CompilerParams(dimension_semantics=sem, vmem_limit_bytes=vmem)


def _dot(a, b):
    return jnp.dot(a, b, preferred_element_type=F32)


def _dot_nt(a, b):
    return lax.dot_general(a, b, (((1,), (1,)), ((), ())), preferred_element_type=F32)


def _sigmoid(x):
    return 1.0 / (1.0 + jnp.exp(-x))


def _mod_kernel(c_ref, w_ref, b_ref, o_ref):
    c = c_ref[...]
    a = (c * _sigmoid(c)).astype(BF16)
    o_ref[0] = _dot(a, w_ref[0].astype(BF16)) + b_ref[0]


def _modulation(cvec, w_mod, b_mod):
    depth = w_mod.shape[0]
    out = pl.pallas_call(
        _mod_kernel,
        grid=(depth, 6),
        in_specs=[
            pl.BlockSpec((8, D_MODEL), lambda l, j: (0, 0)),
            pl.BlockSpec((1, D_MODEL, D_MODEL), lambda l, j: (l, 0, j)),
            pl.BlockSpec((1, 1, D_MODEL), lambda l, j: (l, 0, j)),
        ],
        out_specs=pl.BlockSpec((1, 8, D_MODEL), lambda l, j: (l, 0, j)),
        out_shape=jax.ShapeDtypeStruct((depth, 8, 6 * D_MODEL), F32),
        compiler_params=_cparams(("arbitrary", "arbitrary")),
        name="modulation",
    )(cvec, w_mod, b_mod.reshape(depth, 1, 6 * D_MODEL))
    return out.reshape(depth, 8, 6, D_MODEL)


def _inproj_kernel(xc_ref, xs_ref, mod_ref, n1_ref, w_ref, gain_ref, bdq_ref, bdk_ref, cos_ref, sin_ref,
                   kc_in, vc_in, q_ref, k_ref, v_ref, z_ref, kc_ref, vc_ref, w_bf, proj_prev,
                   *, n_tiles, n_ctx_tiles, ctx_seq):
    del kc_in, vc_in
    step = pl.program_id(0)

    @pl.when(step == 0)
    def _():
        w_bf[...] = w_ref[0].astype(BF16)
        proj_prev[...] = jnp.zeros_like(proj_prev)

    x = _load_pair(jnp.minimum(step, n_tiles - 1) < n_ctx_tiles, xc_ref, xs_ref)
    mod = mod_ref[0, 0]
    sh1, sc1 = mod[0:1], mod[1:2]
    ms = jnp.mean(x * x, axis=-1, keepdims=True)
    h = (x * lax.rsqrt(ms + EPS) * n1_ref[0]) * (1.0 + sc1) + sh1
    proj_new = _dot(h.astype(BF16), w_bf[...])

    qk = proj_prev[:, :QK_W]
    sq = (qk * qk).astype(BF16)
    half = Q_W // 2
    msq = jnp.concatenate([_dot(sq[:, :half], bdq_ref[...]),
                           _dot(sq[:, half:Q_W], bdq_ref[...]),
                           _dot(sq[:, Q_W:], bdk_ref[...])], axis=1)
    qkn = qk * lax.rsqrt(msq + EPS) * gain_ref[0]
    v = proj_prev[:, QK_W:QK_W + KV_W]
    v_ref[...] = v
    a = proj_prev[:, QK_W + KV_W:QK_W + KV_W + CONV_CH]
    b = proj_prev[:, QK_W + KV_W + CONV_CH:]
    z_ref[...] = a * _sigmoid(b)
    cos = cos_ref[...]
    sin = sin_ref[...]
    first = (lax.broadcasted_iota(jnp.int32, (TM, LANES), 1) % 32) < 16
    for c in range(QK_W // LANES):
        blk = qkn[:, c * LANES:(c + 1) * LANES]
        partner = jnp.where(first, pltpu.roll(blk, LANES - 16, 1), pltpu.roll(blk, 16, 1))
        rot = blk * cos + partner * sin
        if c < Q_W // LANES:
            q_ref[:, c * LANES:(c + 1) * LANES] = (rot * Q_SCALE).astype(BF16)
        else:
            k = rot
            k_ref[...] = k

    proj_prev[...] = proj_new

    @pl.when(step - 1 < n_ctx_tiles)
    def _():
        for s in range(TM // ctx_seq):
            kc_ref[s, 0] = k[s * ctx_seq:(s + 1) * ctx_seq]
            vc_ref[s, 0] = v[s * ctx_seq:(s + 1) * ctx_seq]


def _inproj(cfg, l, x_pair, mod, n1, w_in, gain, bdq, bdk, cos, sin, kc, vc):
    t = cfg.t_all
    nt, nct, st = cfg.n_tiles, cfg.n_ctx_tiles, cfg.smp_tiles
    tile_a = lambda i: jnp.minimum(i, nt - 1)
    tile_b = lambda i: jnp.maximum(i - 1, 0)
    rope_map = lambda i: (jnp.where(tile_b(i) < nct, 0, 1 + (tile_b(i) - nct) % st), 0)
    seqs = TM // cfg.ctx_seq
    cache_spec = lambda: pl.BlockSpec((seqs, 1, cfg.ctx_seq, KV_W),
                                      lambda i: (jnp.minimum(tile_b(i), nct - 1), l, 0, 0))
    return pl.pallas_call(
        functools.partial(_inproj_kernel, n_tiles=nt, n_ctx_tiles=nct, ctx_seq=cfg.ctx_seq),
        grid=(nt + 1,),
        in_specs=_pair_specs(cfg, D_MODEL, tile_a) + [
            pl.BlockSpec((1, 1, 6, D_MODEL), lambda i: (l, _mod_row(cfg, tile_a(i)), 0, 0)),
            pl.BlockSpec((1, 1, D_MODEL), lambda i: (l, 0, 0)),
            pl.BlockSpec((1, D_MODEL, IN_COLS), lambda i: (l, 0, 0)),
            pl.BlockSpec((1, 1, QK_W), lambda i: (l, 0, 0)),
            pl.BlockSpec((Q_W // 2, Q_W // 2), lambda i: (0, 0)),
            pl.BlockSpec((KV_W, KV_W), lambda i: (0, 0)),
            pl.BlockSpec((TM, LANES), rope_map),
            pl.BlockSpec((TM, LANES), rope_map),
            pl.BlockSpec(memory_space=pl.ANY),
            pl.BlockSpec(memory_space=pl.ANY),
        ],
        out_specs=[
            pl.BlockSpec((TM, Q_W), lambda i: (tile_b(i), 0)),
            pl.BlockSpec((TM, KV_W), lambda i: (tile_b(i), 0)),
            pl.BlockSpec((TM, KV_W), lambda i: (tile_b(i), 0)),
            pl.BlockSpec((TM, CONV_CH), lambda i: (tile_b(i), 0)),
            cache_spec(), cache_spec(),
        ],
        out_shape=[
            jax.ShapeDtypeStruct((t, Q_W), BF16),
            jax.ShapeDtypeStruct((t, KV_W), F32),
            jax.ShapeDtypeStruct((t, KV_W), F32),
            jax.ShapeDtypeStruct((t, CONV_CH), F32),
            jax.ShapeDtypeStruct(kc.shape, F32),
            jax.ShapeDtypeStruct(vc.shape, F32),
        ],
        scratch_shapes=[pltpu.VMEM((D_MODEL, IN_COLS), BF16), pltpu.VMEM((TM, IN_COLS), F32)],
        input_output_aliases={10: 4, 11: 5},
        compiler_params=_cparams(("arbitrary",)),
        name="inproj",
    )(*x_pair, mod, n1, w_in, gain, bdq, bdk, cos, sin, kc, vc)


def _mixer_kernel(*refs, rq, n_past, seq_blocks):
    if n_past:
        q_ref, k_ref, v_ref, ck_ref, cv_ref = refs[:5]
        refs = refs[5:]
    else:
        q_ref, k_ref, v_ref = refs[:3]
        refs = refs[3:]
    z_ref, zp_ref, zn_ref, w_ref, b_ref, g_ref, beta_ref = refs[:7]
    refs = refs[7:]
    if n_past:
        for src, dst in zip(refs[0:3], refs[5:8]):
            dst[...] = src[...].astype(BF16)
        refs = refs[3:5] + refs[8:]
    o_ref, co_ref, krep, vrep, pad, taps = refs
    head_w = Q_PER_KV * HEAD_DIM
    j = pl.program_id(1)

    @pl.when(jnp.logical_and(pl.program_id(0) == 0, j == 0))
    def _():
        diag = (lax.broadcasted_iota(jnp.int32, (LANES, LANES), 0)
                == lax.broadcasted_iota(jnp.int32, (LANES, LANES), 1))
        for c in range(CONV_CH // LANES):
            for p in range(4):
                for q in range(4):
                    for half in range(2):
                        tap = w_ref[0, 8 * q + 2 * p + half:8 * q + 2 * p + half + 1, c * LANES:(c + 1) * LANES]
                        taps[c, p, q * LANES:(q + 1) * LANES, half * LANES:(half + 1) * LANES] = (
                            jnp.where(diag, tap, 0.0).astype(BF16))

    @pl.when(j == 0)
    def _():
        src = lax.broadcasted_iota(jnp.int32, (KV_W, head_w), 0)
        dst = lax.broadcasted_iota(jnp.int32, (KV_W, head_w), 1) % HEAD_DIM
        n_new = k_ref.shape[0]
        for h in range(N_KV_HEADS):
            rep = jnp.where(src == dst + h * HEAD_DIM, 1.0, 0.0).astype(BF16)
            if n_past:
                krep[h, :n_past, :] = _dot(ck_ref[0, 0].astype(BF16), rep).astype(BF16)
                vrep[h, :n_past, :] = _dot(cv_ref[0, 0].astype(BF16), rep).astype(BF16)
            krep[h, n_past:n_past + n_new, :] = _dot(k_ref[...].astype(BF16), rep).astype(BF16)
            vrep[h, n_past:n_past + n_new, :] = _dot(v_ref[...].astype(BF16), rep).astype(BF16)

    end = HALO + CONV_ROWS
    if seq_blocks > 1:
        pad[0:HALO, :] = jnp.where(j > 0, zp_ref[...], 0.0)
        pad[end:end + HALO, :] = jnp.where(j < seq_blocks - 1, zn_ref[...], 0.0)
    else:
        pad[0:HALO, :] = jnp.zeros((HALO, CONV_CH), F32)
        pad[end:end + HALO, :] = jnp.zeros((HALO, CONV_CH), F32)
    pad[end + HALO:, :] = jnp.zeros((PAD_ROWS - end - HALO, CONV_CH), F32)
    pad[HALO:end, :] = z_ref[...]

    lane_head = lax.broadcasted_iota(jnp.int32, (rq, head_w), 1) // HEAD_DIM
    for h in range(N_KV_HEADS):
        qh = q_ref[:, h * head_w:(h + 1) * head_w]
        kh = krep[h]
        vh = vrep[h]
        acc = jnp.zeros((rq, head_w), F32)
        for g in range(Q_PER_KV):
            mask = lane_head == g
            qm = jnp.where(mask, qh, jnp.zeros_like(qh))
            s = _dot_nt(qm, kh)
            m = jnp.max(s, axis=-1, keepdims=True)
            p = jnp.exp2(s - m)
            denom = jnp.sum(p, axis=-1, keepdims=True)
            o = _dot(p.astype(BF16), vh)
            acc = jnp.where(mask, o * (1.0 / denom), acc)
        o_ref[:, h * head_w:(h + 1) * head_w] = acc.astype(BF16)
    conv_cols = [_conv_cols(pad, taps, c) for c in range(CONV_CH // LANES)]
    co_ref[...] = _conv_finish(conv_cols, b_ref, g_ref, beta_ref)


def _mixer(cfg, l, q, k, v, cache_k, cache_v, z, taps, dw_b, ln_g, ln_b, expert_w):
    head_w = Q_PER_KV * HEAD_DIM
    rq = CONV_ROWS
    hb = CONV_ROWS // HALO
    last = cfg.t_all // HALO - 1
    vec = lambda: pl.BlockSpec((1, 1, CONV_CH), lambda b, j: (l, 0, 0))

    def conv_specs(row_block):
        return [
            pl.BlockSpec((CONV_ROWS, CONV_CH), lambda b, j: (row_block(b, j), 0)),
            pl.BlockSpec((HALO, CONV_CH), lambda b, j: (jnp.maximum(row_block(b, j) * hb - 1, 0), 0)),
            pl.BlockSpec((HALO, CONV_CH), lambda b, j: (jnp.minimum(row_block(b, j) * hb + hb, last), 0)),
            pl.BlockSpec((1, 32, CONV_CH), lambda b, j: (l, 0, 0)),
            vec(), vec(), vec(),
        ]

    def scratch(sk):
        return [pltpu.VMEM((N_KV_HEADS, sk, head_w), BF16), pltpu.VMEM((N_KV_HEADS, sk, head_w), BF16),
                pltpu.VMEM((PAD_ROWS, CONV_CH), F32),
                pltpu.VMEM((CONV_CH // LANES, 4, 4 * LANES, 2 * LANES), BF16)]

    sc = cfg.ctx_seq
    a_ctx, c_ctx = pl.pallas_call(
        functools.partial(_mixer_kernel, rq=rq, n_past=0, seq_blocks=1),
        grid=(cfg.ctx_batch, 1),
        in_specs=[
            pl.BlockSpec((sc, Q_W), lambda b, j: (b, 0)),
            pl.BlockSpec((sc, KV_W), lambda b, j: (b, 0)),
            pl.BlockSpec((sc, KV_W), lambda b, j: (b, 0)),
        ] + conv_specs(lambda b, j: b),
        out_specs=[pl.BlockSpec((sc, Q_W), lambda b, j: (b, 0)),
                   pl.BlockSpec((sc, CONV_CH), lambda b, j: (b, 0))],
        out_shape=[jax.ShapeDtypeStruct((cfg.t_ctx, Q_W), BF16),
                   jax.ShapeDtypeStruct((cfg.t_ctx, CONV_CH), BF16)],
        scratch_shapes=scratch(sc),
        compiler_params=_cparams(("arbitrary", "arbitrary")),
        name="mixer_ctx",
    )(q, k, v, z, z, z, taps, dw_b, ln_g, ln_b)
    ss = cfg.smp_seq
    sk = cfg.past + ss
    kv_base = cfg.t_ctx // ss
    q_base = cfg.t_ctx // rq
    nqb = ss // rq
    t_smp = cfg.smp_batch * ss
    steps = cfg.smp_batch * nqb
    w_rows = [w.shape[0] // (cfg.depth * steps) for w in expert_w]
    w_specs = lambda off: [pl.BlockSpec((r, w.shape[1]), lambda b, j: (off + b * nqb + j, 0))
                           for r, w in zip(w_rows, expert_w)]
    a_smp, c_smp, *expert_bf = pl.pallas_call(
        functools.partial(_mixer_kernel, rq=rq, n_past=cfg.past, seq_blocks=nqb),
        grid=(cfg.smp_batch, nqb),
        in_specs=[
            pl.BlockSpec((rq, Q_W), lambda b, j: (q_base + b * nqb + j, 0)),
            pl.BlockSpec((ss, KV_W), lambda b, j: (kv_base + b, 0)),
            pl.BlockSpec((ss, KV_W), lambda b, j: (kv_base + b, 0)),
            pl.BlockSpec((1, 1, cfg.past, KV_W), lambda b, j: (b, l, 0, 0)),
            pl.BlockSpec((1, 1, cfg.past, KV_W), lambda b, j: (b, l, 0, 0)),
        ] + conv_specs(lambda b, j: q_base + b * nqb + j) + w_specs(l * steps),
        out_specs=[pl.BlockSpec((rq, Q_W), lambda b, j: (b * nqb + j, 0)),
                   pl.BlockSpec((rq, CONV_CH), lambda b, j: (b * nqb + j, 0))] + w_specs(0),
        out_shape=[jax.ShapeDtypeStruct((t_smp, Q_W), BF16),
                   jax.ShapeDtypeStruct((t_smp, CONV_CH), BF16)]
        + [jax.ShapeDtypeStruct((r * steps, w.shape[1]), BF16) for r, w in zip(w_rows, expert_w)],
        scratch_shapes=scratch(sk),
        compiler_params=_cparams(("arbitrary", "arbitrary"), MIXER_VMEM_LIMIT),
        name="mixer_smp",
    )(q, k, v, cache_k, cache_v, z, z, z, taps, dw_b, ln_g, ln_b, *expert_w)
    return (a_ctx, a_smp), (c_ctx, c_smp), expert_bf


def _conv_cols(pad, taps, c):
    cols = slice(c * LANES, (c + 1) * LANES)
    lhs = jnp.concatenate([pad[8 * q:8 * q + CONV_INNER, cols].astype(BF16) for q in range(4)], axis=1)
    acc = None
    for p in range(4):
        pair = _dot(lhs, taps[c, p])
        for half in range(2):
            o = 2 * p + half
            shifted = pair[o:o + CONV_ROWS, half * LANES:(half + 1) * LANES]
            acc = shifted if acc is None else acc + shifted
    return acc


def _conv_tap_rows(conv_dw_w):
    return jnp.pad(conv_dw_w, ((0, 0), (HALO - CONV_PAD, 32 - CONV_K - (HALO - CONV_PAD)), (0, 0)))


def _conv_finish(blocks, b_ref, g_ref, beta_ref):
    y = jnp.concatenate(blocks, axis=1) + b_ref[0]
    mu = jnp.mean(y, axis=-1, keepdims=True)
    yc = y - mu
    var = jnp.mean(yc * yc, axis=-1, keepdims=True)
    yn = yc * lax.rsqrt(var + EPS) * g_ref[0] + beta_ref[0]
    return (yn * _sigmoid(yn)).astype(BF16)


def _split_bf16(x):
    hi = x.astype(BF16)
    lo = (x - hi.astype(F32)).astype(BF16)
    return hi, lo


def _outproj_kernel(ac_ref, as_ref, cc_ref, cs_ref, xc_ref, xs_ref, mod_ref, n2_ref, wo_ref, wr_ref, br_ref,
                    xmc_ref, xms_ref, hl_ref, lp_ref, cnt_ref, wo_bf, h_prev, lt_prev, *, n_tiles, n_ctx_tiles):
    step = pl.program_id(0)

    @pl.when(step == 0)
    def _():
        wo_bf[...] = wo_ref[0].astype(BF16)
        h_prev[...] = jnp.zeros_like(h_prev)
        lt_prev[...] = jnp.zeros_like(lt_prev)

    is_ctx = jnp.minimum(step, n_tiles - 1) < n_ctx_tiles
    ac = jnp.concatenate([_load_pair(is_ctx, ac_ref, as_ref), _load_pair(is_ctx, cc_ref, cs_ref)], axis=1)
    y = _dot(ac, wo_bf[...])
    mod = mod_ref[0, 0]
    g1, sh2, sc2 = mod[2:3], mod[3:4], mod[4:5]
    xm = _load_pair(is_ctx, xc_ref, xs_ref) + g1 * y
    ms = jnp.mean(xm * xm, axis=-1, keepdims=True)
    h2 = (xm * lax.rsqrt(ms + EPS) * n2_ref[0]) * (1.0 + sc2) + sh2
    h_new, h_lo = _split_bf16(h2)
    w_hi, w_lo = _split_bf16(wr_ref[...])
    logits = _dot(h_new, w_hi) + (_dot(h_new, w_lo) + _dot(h_lo, w_hi))
    lt_new = logits.T[0:N_EXPERTS, :]

    h_hi = h_prev[...]
    s_all = _sigmoid(lt_prev[...])
    sb_all = s_all + br_ref[...]

    rows8 = 2 * EXPERTS_PER_GROUP
    row = lax.broadcasted_iota(jnp.int32, (rows8, TM), 0)
    member = row % EXPERTS_PER_GROUP
    second = row >= EXPERTS_PER_GROUP

    def partner(x, k):
        return jnp.where(member + k < EXPERTS_PER_GROUP,
                         pltpu.roll(x, rows8 - k, 0), pltpu.roll(x, EXPERTS_PER_GROUP - k, 0))

    def other_group(x):
        return pltpu.roll(x, EXPERTS_PER_GROUP, 0)

    def group_sum(x):
        return x + partner(x, 1) + (partner(x, 2) + partner(x, 3))

    def ahead(other, mine, other_first):
        return jnp.where((other >= mine) if other_first else (other > mine), 1.0, 0.0)

    sel, gscore = [], []
    for hv in range(2):
        sb8 = sb_all[hv * rows8:(hv + 1) * rows8]
        rank = jnp.zeros((rows8, TM), F32)
        for k in range(1, EXPERTS_PER_GROUP):
            pk = partner(sb8, k)
            rank = rank + jnp.where(member + k >= EXPERTS_PER_GROUP, ahead(pk, sb8, True), ahead(pk, sb8, False))
        sel.append(rank < 1.5)
        gscore.append(group_sum(jnp.where(sel[hv], sb8, 0.0)))
    gate8 = []
    chosen_f = []
    for hv in range(2):
        mine, far = gscore[hv], gscore[1 - hv]
        near = other_group(mine)
        rank = jnp.where(second, ahead(near, mine, True), ahead(near, mine, False))
        rank = rank + ahead(far, mine, hv == 1) + ahead(other_group(far), mine, hv == 1)
        chosen = rank < 0.5
        chosen_f.append(jnp.where(chosen, 1.0, 0.0))
        s8 = s_all[hv * rows8:(hv + 1) * rows8]
        gate8.append(jnp.where(chosen, jnp.where(sel[hv], s8, 0.0), 0.0))
    total = group_sum(gate8[0]) + group_sum(gate8[1])
    total = total + other_group(total)
    folded = gate8[0] + gate8[1]
    gates4 = ((folded + other_group(folded)) * (1.0 / total))[0:EXPERTS_PER_GROUP]

    cgf = [chosen_f[0][0:1], chosen_f[0][EXPERTS_PER_GROUP:EXPERTS_PER_GROUP + 1],
           chosen_f[1][0:1], chosen_f[1][EXPERTS_PER_GROUP:EXPERTS_PER_GROUP + 1]]
    onehot_g = jnp.concatenate(cgf + [jnp.zeros((8 - N_GROUPS, TM), F32)], axis=0).astype(BF16)
    before = (lax.broadcasted_iota(jnp.int32, (TM, TM), 0) < lax.broadcasted_iota(jnp.int32, (TM, TM), 1))
    prior = _dot(onehot_g, jnp.where(before, 1.0, 0.0).astype(BF16))
    lp = jnp.zeros((1, TM), F32)
    start = jnp.zeros((1, 1), F32)
    counts = []
    for g in range(N_GROUPS):
        n_g = jnp.sum(cgf[g], axis=-1, keepdims=True)
        counts.append(n_g)
        lp = lp + cgf[g] * (start + prior[g:g + 1, :])
        start = start + jnp.floor((n_g + (BF16_ROWS - 1)) * (1.0 / BF16_ROWS)) * BF16_ROWS
    lp_i = lp.astype(jnp.int32)
    perm = jnp.where(lax.broadcasted_iota(jnp.int32, (LT, TM), 0) == lp_i, 1.0, 0.0).astype(BF16)

    hl_ref[0, :, :D_MODEL] = _dot(perm, h_hi).astype(BF16)
    g_hi, g_lo = _split_bf16(gates4)
    gate_rows = jnp.concatenate([g_hi.astype(F32), g_lo.astype(F32), jnp.zeros((GATE_W - 8, TM), F32)],
                                axis=0).astype(BF16)
    hl_ref[0, :, D_MODEL:] = _dot_nt(perm, gate_rows).astype(BF16)

    lp_hi = jnp.floor(lp * (1.0 / 32.0))
    lp_parts = jnp.concatenate([lp_hi, lp - 32.0 * lp_hi, jnp.zeros((LANES - 2, TM), F32)], axis=0).astype(BF16)
    eye = jnp.where(lax.broadcasted_iota(jnp.int32, (TM, TM), 0) == lax.broadcasted_iota(jnp.int32, (TM, TM), 1),
                    1.0, 0.0).astype(BF16)
    lp_ref[...] = _dot_nt(eye, lp_parts)

    cnt = jnp.concatenate([jnp.broadcast_to(c, (1, LANES)) for c in counts]
                          + [jnp.zeros((8 - N_GROUPS, LANES), F32)], axis=0)
    cnt_ref[0] = cnt.astype(jnp.int32)

    h_prev[...] = h_new
    lt_prev[...] = lt_new
    _store_pair(is_ctx, xmc_ref, xms_ref, xm)


def _outproj(cfg, l, a_pair, c_pair, x_pair, mod, n2, w_out, w_router, b_router):
    t, nt, nct = cfg.t_all, cfg.n_tiles, cfg.n_ctx_tiles
    tile_a = lambda i: jnp.minimum(i, nt - 1)
    tile_b = lambda i: jnp.maximum(i - 1, 0)
    return pl.pallas_call(
        functools.partial(_outproj_kernel, n_tiles=nt, n_ctx_tiles=nct),
        grid=(nt + 1,),
        in_specs=_pair_specs(cfg, Q_W, tile_a) + _pair_specs(cfg, CONV_CH, tile_a)
        + _pair_specs(cfg, D_MODEL, tile_a) + [
            pl.BlockSpec((1, 1, 6, D_MODEL), lambda i: (l, _mod_row(cfg, tile_a(i)), 0, 0)),
            pl.BlockSpec((1, 1, D_MODEL), lambda i: (l, 0, 0)),
            pl.BlockSpec((1, D_MODEL, D_MODEL), lambda i: (l, 0, 0)),
            pl.BlockSpec((D_MODEL, LANES), lambda i: (0, 0)),
            pl.BlockSpec((N_EXPERTS, 1), lambda i: (0, 0)),
        ],
        out_specs=_pair_specs(cfg, D_MODEL, tile_a) + [
            pl.BlockSpec((1, LT, XS_W), lambda i: (tile_b(i), 0, 0)),
            pl.BlockSpec((TM, LANES), lambda i: (tile_b(i), 0)),
            pl.BlockSpec((1, 8, LANES), lambda i: (tile_b(i), 0, 0)),
        ],
        out_shape=_pair_shapes(cfg, D_MODEL, F32) + [
            jax.ShapeDtypeStruct((nt, LT, XS_W), BF16),
            jax.ShapeDtypeStruct((t, LANES), F32),
            jax.ShapeDtypeStruct((nt, 8, LANES), jnp.int32),
        ],
        scratch_shapes=[pltpu.VMEM((D_MODEL, D_MODEL), BF16), pltpu.VMEM((TM, D_MODEL), BF16),
                        pltpu.VMEM((N_EXPERTS, TM), F32)],
        compiler_params=_cparams(("arbitrary",)),
        name="outproj",
    )(*a_pair, *c_pair, *x_pair, mod, n2, w_out, w_router, b_router)


def _segment_pieces(n, src_row, dst_row, src_ref, dst_ref, sem, fn):
    for size in SEG_SIZES:
        off = n & ~(2 * size - 1)

        @pl.when((n & size) != 0)
        def _():
            fn(pltpu.make_async_copy(
                src_ref.at[pl.ds(pl.multiple_of(src_row + off, BF16_ROWS), size)],
                dst_ref.at[pl.ds(pl.multiple_of(dst_row + off, BF16_ROWS), size)],
                sem))


def _start(cp):
    cp.start()


def _wait(cp):
    cp.wait()


def _moe_kernel(tg_ref, nv_ref, ssrc_ref, sdst_ref, slen_ref, tlo_ref, thi_ref,
                hl_ref, wg_ref, wu_ref, wd_ref, y_ref, xbuf, sem):
    del tg_ref
    i = pl.program_id(0)
    nv = nv_ref[0]

    def tile_copies(tile, slot, fn):
        base = tile * MOE_TM

        def body(s, carry):
            d = sdst_ref[s]
            lo = jnp.maximum(d, base)
            hi = jnp.minimum(d + slen_ref[s], base + MOE_TM)
            n = jnp.maximum(hi - lo, 0)
            _segment_pieces(n, ssrc_ref[s] + (lo - d), lo - base, hl_ref, xbuf.at[slot], sem.at[slot], fn)
            return carry

        lax.fori_loop(tlo_ref[tile], thi_ref[tile], body, 0)

    @pl.when(i == 0)
    def _():
        xbuf[...] = jnp.zeros_like(xbuf)
        tile_copies(0, 0, _start)

    @pl.when(i + 1 < nv)
    def _():
        tile_copies(i + 1, (i + 1) % 2, _start)

    @pl.when(i >= nv)
    def _():
        y_ref[...] = jnp.zeros_like(y_ref)

    @pl.when(i < nv)
    def _():
        slot = i % 2
        tile_copies(i, slot, _wait)
        xg = xbuf[slot]
        x = xg[:, :D_MODEL]
        gcols = xg[:, D_MODEL:].astype(F32)
        hid = []
        for j in range(EXPERTS_PER_GROUP):
            gate = gcols[:, j:j + 1] + gcols[:, EXPERTS_PER_GROUP + j:EXPERTS_PER_GROUP + j + 1]
            hg = _dot(x, wg_ref[0, j])
            hu = _dot(x, wu_ref[0, j])
            hid.append(((hg * _sigmoid(hg)) * hu * gate).astype(BF16))
        hcat = jnp.concatenate(hid, axis=1)
        wd = wd_ref[0].reshape(EXPERTS_PER_GROUP * D_FF, D_MODEL)
        y_ref[...] = _dot(hcat, wd).astype(BF16)


def _moe(cfg, hl, plan, w_gate, w_up, w_down):
    nmt = cfg.moe_tiles

    def w_map(i, tg, nv, *_):
        return (tg[jnp.minimum(i, nv[0] - 1)], 0, 0, 0)

    w_gate = w_gate.reshape(N_GROUPS, EXPERTS_PER_GROUP, D_MODEL, D_FF)
    w_up = w_up.reshape(N_GROUPS, EXPERTS_PER_GROUP, D_MODEL, D_FF)
    w_down = w_down.reshape(N_GROUPS, EXPERTS_PER_GROUP, D_FF, D_MODEL)

    return pl.pallas_call(
        _moe_kernel,
        grid_spec=pltpu.PrefetchScalarGridSpec(
            num_scalar_prefetch=7,
            grid=(nmt,),
            in_specs=[
                pl.BlockSpec(memory_space=pl.ANY),
                pl.BlockSpec((1, EXPERTS_PER_GROUP, D_MODEL, D_FF), w_map),
                pl.BlockSpec((1, EXPERTS_PER_GROUP, D_MODEL, D_FF), w_map),
                pl.BlockSpec((1, EXPERTS_PER_GROUP, D_FF, D_MODEL), w_map),
            ],
            out_specs=pl.BlockSpec((MOE_TM, D_MODEL), lambda i, *_: (i, 0)),
            scratch_shapes=[pltpu.VMEM((2, MOE_TM, XS_W), BF16), pltpu.SemaphoreType.DMA((2,))],
        ),
        out_shape=jax.ShapeDtypeStruct((nmt * MOE_TM, D_MODEL), BF16),
        compiler_params=_cparams(("arbitrary",)),
        name="moe",
    )(plan.tile_group, plan.n_valid, plan.seg_tiled, plan.seg_sorted, plan.seg_len, plan.tile_lo, plan.tile_hi,
      hl, w_gate, w_up, w_down)


def _combine_kernel(stiled_ref, ssorted_ref, slen_ref, xmc_ref, xms_ref, ys_ref, lp_ref, mod_ref,
                    oc_ref, os_ref, ybuf, sem, *, n_tiles, n_ctx_tiles):
    t = pl.program_id(0)

    def tile_copies(tile, slot, fn):
        for g in range(N_GROUPS):
            s = g * n_tiles + tile
            _segment_pieces(slen_ref[s], ssorted_ref[s], stiled_ref[s] - tile * LT,
                            ys_ref, ybuf.at[slot], sem.at[slot], fn)

    @pl.when(t == 0)
    def _():
        ybuf[...] = jnp.zeros_like(ybuf)
        tile_copies(0, 0, _start)

    @pl.when(t + 1 < n_tiles)
    def _():
        tile_copies(t + 1, (t + 1) % 2, _start)

    slot = t % 2
    tile_copies(t, slot, _wait)
    lp = lp_ref[...]
    pos = (lp[:, 0:1] * 32.0 + lp[:, 1:2]).astype(jnp.int32)
    perm_t = jnp.where(lax.broadcasted_iota(jnp.int32, (TM, LT), 1) == pos, 1.0, 0.0).astype(BF16)
    y = _dot(perm_t, ybuf[slot])
    g2 = mod_ref[0, 0][5:6]
    is_ctx = t < n_ctx_tiles
    _store_pair(is_ctx, oc_ref, os_ref, _load_pair(is_ctx, xmc_ref, xms_ref) + g2 * y)


def _combine(cfg, l, xm_pair, ys, lp, mod, plan):
    nt = cfg.n_tiles
    return pl.pallas_call(
        functools.partial(_combine_kernel, n_tiles=nt, n_ctx_tiles=cfg.n_ctx_tiles),
        grid_spec=pltpu.PrefetchScalarGridSpec(
            num_scalar_prefetch=3,
            grid=(nt,),
            in_specs=_pair_specs(cfg, D_MODEL) + [
                pl.BlockSpec(memory_space=pl.ANY),
                pl.BlockSpec((TM, LANES), lambda i, *_: (i, 0)),
                pl.BlockSpec((1, 1, 6, D_MODEL), lambda i, *_: (l, _mod_row(cfg, i), 0, 0)),
            ],
            out_specs=_pair_specs(cfg, D_MODEL),
            scratch_shapes=[pltpu.VMEM((2, LT, D_MODEL), BF16), pltpu.SemaphoreType.DMA((2,))],
        ),
        out_shape=_pair_shapes(cfg, D_MODEL, F32),
        compiler_params=_cparams(("arbitrary",)),
        name="combine",
    )(plan.seg_tiled, plan.seg_sorted, plan.seg_len, *xm_pair, ys, lp, mod)


class Plan(NamedTuple):
    seg_tiled: jax.Array
    seg_sorted: jax.Array
    seg_len: jax.Array
    tile_group: jax.Array
    n_valid: jax.Array
    tile_lo: jax.Array
    tile_hi: jax.Array


def _plan(cfg, counts):
    nt = cfg.n_tiles
    m = (counts + (BF16_ROWS - 1)) // BF16_ROWS * BF16_ROWS
    local_off = jnp.cumsum(m, axis=1) - m
    within_group = jnp.cumsum(m, axis=0) - m
    group_rows = jnp.sum(m, axis=0)
    group_tiles = (group_rows + (MOE_TM - 1)) // MOE_TM
    tiles_end = jnp.cumsum(group_tiles)
    group_start = (tiles_end - group_tiles) * MOE_TM
    seg_sorted = (group_start[None, :] + within_group).T.reshape(-1).astype(jnp.int32)
    seg_tiled = (jnp.arange(nt, dtype=jnp.int32)[:, None] * LT + local_off).T.reshape(-1).astype(jnp.int32)
    seg_len = m.T.reshape(-1).astype(jnp.int32)
    n_valid = tiles_end[-1:].astype(jnp.int32)
    tile_ids = jnp.arange(cfg.moe_tiles, dtype=jnp.int32)
    tile_group = jnp.minimum(jnp.sum(tile_ids[:, None] >= tiles_end[None, :], axis=1), N_GROUPS - 1).astype(jnp.int32)
    tile_base = tile_ids * MOE_TM
    seg_end = seg_sorted + seg_len
    tile_lo = jnp.sum(seg_end[None, :] <= tile_base[:, None], axis=1).astype(jnp.int32)
    tile_hi = jnp.sum(seg_sorted[None, :] < tile_base[:, None] + MOE_TM, axis=1).astype(jnp.int32)
    return Plan(seg_tiled, seg_sorted, seg_len, tile_group, n_valid, tile_lo, tile_hi)


def _rope_tables(n):
    pos = jnp.arange(n)
    rc = jnp.stack([(pos // GRID_W).astype(F32), (pos % GRID_W).astype(F32)], axis=1)
    freqs = ROPE_THETA ** (-jnp.arange(AXIS_PAIRS, dtype=F32) / AXIS_PAIRS)
    ang = rc[:, :, None] * freqs[None, None, :]
    cos = jnp.cos(ang)[:, :, None, :]
    sin = jnp.sin(ang)[:, :, None, :] * jnp.array([-1.0, 1.0], F32)[None, None, :, None]
    cos = jnp.broadcast_to(cos, (n, 2, 2, AXIS_PAIRS)).reshape(n, HEAD_DIM)
    sin = jnp.broadcast_to(sin, (n, 2, 2, AXIS_PAIRS)).reshape(n, HEAD_DIM)
    reps = LANES // HEAD_DIM
    cos = jnp.concatenate([jnp.ones((TM, HEAD_DIM), F32), cos], axis=0)
    sin = jnp.concatenate([jnp.zeros((TM, HEAD_DIM), F32), sin], axis=0)
    return jnp.tile(cos, (1, reps)), jnp.tile(sin, (1, reps))


def _block_diag_mean(width):
    r = jnp.arange(width)
    return jnp.where((r[:, None] // HEAD_DIM) == (r[None, :] // HEAD_DIM), 1.0 / HEAD_DIM, 0.0).astype(BF16)


def _forward(cfg, x_prompt, x_sample, cache_k, cache_v, c, c_ctx, w_mod, b_mod, norm1_g, norm2_g,
             w_in, q_norm_g, k_norm_g, conv_dw_w, conv_dw_b, conv_ln_g, conv_ln_b, w_out,
             w_router, b_router, w_gate, w_up, w_down):
    depth = cfg.depth
    assert cfg.ctx_seq == CONV_ROWS and cfg.t_ctx % TM == 0 and cfg.smp_seq % TM == 0
    assert cfg.t_ctx % cfg.smp_seq == 0 and cfg.smp_batch + 1 <= 8

    x = (x_prompt.reshape(cfg.t_ctx, D_MODEL), x_sample.reshape(-1, D_MODEL))
    cvec = jnp.zeros((8, D_MODEL), F32).at[0].set(c_ctx).at[1:1 + cfg.smp_batch].set(c)
    mod = _modulation(cvec, w_mod, b_mod)

    n1 = norm1_g.reshape(depth, 1, D_MODEL)
    n2 = norm2_g.reshape(depth, 1, D_MODEL)
    gain = jnp.concatenate([jnp.tile(q_norm_g, (1, N_Q_HEADS)), jnp.tile(k_norm_g, (1, N_KV_HEADS))],
                           axis=1).reshape(depth, 1, QK_W)
    expert_w = (w_gate.reshape(-1, D_FF), w_up.reshape(-1, D_FF), w_down.reshape(-1, D_MODEL))
    taps = _conv_tap_rows(conv_dw_w)
    dw_b = conv_dw_b.reshape(depth, 1, CONV_CH)
    ln_g = conv_ln_g.reshape(depth, 1, CONV_CH)
    ln_b = conv_ln_b.reshape(depth, 1, CONV_CH)
    wr = jnp.pad(w_router, ((0, 0), (0, LANES - N_EXPERTS)))
    br = b_router.reshape(N_EXPERTS, 1)
    cos, sin = _rope_tables(cfg.smp_seq)
    bdq = _block_diag_mean(Q_W // 2)
    bdk = _block_diag_mean(KV_W)
    ck = cache_k.reshape(cfg.smp_batch, depth, cfg.past, KV_W)
    cv = cache_v.reshape(cfg.smp_batch, depth, cfg.past, KV_W)

    new_k = jnp.zeros((cfg.ctx_batch, depth, cfg.ctx_seq, KV_W), F32)
    new_v = jnp.zeros((cfg.ctx_batch, depth, cfg.ctx_seq, KV_W), F32)
    for l in range(depth):
        q, k, v, z, new_k, new_v = _inproj(cfg, l, x, mod, n1, w_in, gain, bdq, bdk, cos, sin, new_k, new_v)
        a_pair, c_pair, expert_bf = _mixer(cfg, l, q, k, v, ck, cv, z, taps, dw_b, ln_g, ln_b, expert_w)
        xmc, xms, hl, lp, cnt = _outproj(cfg, l, a_pair, c_pair, x, mod, n2, w_out, wr, br)
        plan = _plan(cfg, cnt[:, :N_GROUPS, 0])
        ys = _moe(cfg, hl.reshape(cfg.n_tiles * LT, XS_W), plan, *expert_bf)
        x = _combine(cfg, l, (xmc, xms), ys, lp, mod, plan)

    y_prompt = x[0].reshape(cfg.ctx_batch, cfg.ctx_seq, D_MODEL)
    y_sample = x[1].reshape(cfg.smp_batch, cfg.smp_seq, D_MODEL)
    cache_shape = (cfg.ctx_batch, depth, cfg.ctx_seq, N_KV_HEADS, HEAD_DIM)
    return y_prompt, y_sample, new_k.reshape(cache_shape), new_v.reshape(cache_shape)


def kernel(x_prompt, x_sample, cache_k, cache_v, c, c_ctx, w_mod, b_mod, norm1_g, norm2_g, w_in, q_norm_g,
           k_norm_g, conv_dw_w, conv_dw_b, conv_ln_g, conv_ln_b, w_out, w_router, b_router, w_gate, w_up, w_down):
    cfg = Cfg(ctx_batch=x_prompt.shape[0], ctx_seq=x_prompt.shape[1], smp_batch=x_sample.shape[0],
              smp_seq=x_sample.shape[1], past=cache_k.shape[2], depth=w_mod.shape[0])
    return _forward(cfg, x_prompt, x_sample, cache_k, cache_v, c, c_ctx, w_mod, b_mod, norm1_g, norm2_g,
                    w_in, q_norm_g, k_norm_g, conv_dw_w, conv_dw_b, conv_ln_g, conv_ln_b, w_out,
                    w_router, b_router, w_gate, w_up, w_down)
```

```python
import functools
import math
from typing import NamedTuple

import jax
import jax.numpy as jnp
from jax import lax
from jax.experimental import pallas as pl
from jax.experimental.pallas import tpu as pltpu

F32 = jnp.float32
BF16 = jnp.bfloat16

D_MODEL = 1024
HEAD_DIM = 64
N_Q_HEADS = 8
N_KV_HEADS = 2
Q_PER_KV = N_Q_HEADS // N_KV_HEADS
Q_W = N_Q_HEADS * HEAD_DIM
KV_W = N_KV_HEADS * HEAD_DIM
QK_W = Q_W + KV_W
CONV_CH = D_MODEL // 2
CONV_K = 31
CONV_PAD = CONV_K // 2
IN_COLS = Q_W + 2 * KV_W + 2 * CONV_CH
GRID_W = 64
AXIS_PAIRS = HEAD_DIM // 4
ROPE_THETA = 10000.0
N_EXPERTS = 16
N_GROUPS = 4
EXPERTS_PER_GROUP = N_EXPERTS // N_GROUPS
D_FF = 512
EPS = 1e-6

LANES = 128
BF16_ROWS = 16
TM = 512
TOP_K = 2
LT = 1280
MOE_TM = 512
CONV_ROWS = 256
HALO = 16
CONV_INNER = CONV_ROWS + 16
PAD_ROWS = 24 + CONV_INNER + 8
GATE_W = LANES
XS_W = D_MODEL + GATE_W
SEG_SIZES = (512, 256, 128, 64, 32, 16)
Q_SCALE = (1.0 / math.sqrt(HEAD_DIM)) * math.log2(math.e)
VMEM_LIMIT = 48 * 1024 * 1024
MIXER_VMEM_LIMIT = 56 * 1024 * 1024


class Cfg(NamedTuple):
    ctx_batch: int
    ctx_seq: int
    smp_batch: int
    smp_seq: int
    past: int
    depth: int

    @property
    def t_ctx(self):
        return self.ctx_batch * self.ctx_seq

    @property
    def t_all(self):
        return self.t_ctx + self.smp_batch * self.smp_seq

    @property
    def n_tiles(self):
        return self.t_all // TM

    @property
    def n_ctx_tiles(self):
        return self.t_ctx // TM

    @property
    def smp_tiles(self):
        return self.smp_seq // TM

    @property
    def moe_tiles(self):
        rows = TOP_K * self.t_all + self.n_tiles * N_EXPERTS * (BF16_ROWS - 1) + N_EXPERTS * (MOE_TM - 1)
        return -(-rows // MOE_TM)


def _mod_row(cfg, i):
    return jnp.where(i < cfg.n_ctx_tiles, 0, 1 + (i - cfg.n_ctx_tiles) // cfg.smp_tiles)


def _pair_specs(cfg, width, tile_of=lambda i: i):
    nct = cfg.n_ctx_tiles
    return [pl.BlockSpec((TM, width), lambda i, *_: (jnp.minimum(tile_of(i), nct - 1), 0)),
            pl.BlockSpec((TM, width), lambda i, *_: (jnp.maximum(tile_of(i) - nct, 0), 0))]


def _pair_shapes(cfg, width, dtype):
    return [jax.ShapeDtypeStruct((cfg.t_ctx, width), dtype),
            jax.ShapeDtypeStruct((cfg.t_all - cfg.t_ctx, width), dtype)]


def _load_pair(is_ctx, c_ref, s_ref):
    return jnp.where(is_ctx, c_ref[...], s_ref[...])


def _store_pair(is_ctx, c_ref, s_ref, val):
    @pl.when(is_ctx)
    def _():
        c_ref[...] = val

    @pl.when(jnp.logical_not(is_ctx))
    def _():
        s_ref[...] = val


def _cparams(sem, vmem=VMEM_LIMIT):
    return pltpu.CompilerParams(dimension_semantics=sem, vmem_limit_bytes=vmem)


def _dot(a, b):
    return jnp.dot(a, b, preferred_element_type=F32)


def _dot_nt(a, b):
    return lax.dot_general(a, b, (((1,), (1,)), ((), ())), preferred_element_type=F32)


def _sigmoid(x):
    return 1.0 / (1.0 + jnp.exp(-x))


def _mod_kernel(c_ref, w_ref, b_ref, o_ref):
    c = c_ref[...]
    a = (c * _sigmoid(c)).astype(BF16)
    o_ref[0] = _dot(a, w_ref[0].astype(BF16)) + b_ref[0]


def _modulation(cvec, w_mod, b_mod):
    depth = w_mod.shape[0]
    out = pl.pallas_call(
        _mod_kernel,
        grid=(depth, 6),
        in_specs=[
            pl.BlockSpec((8, D_MODEL), lambda l, j: (0, 0)),
            pl.BlockSpec((1, D_MODEL, D_MODEL), lambda l, j: (l, 0, j)),
            pl.BlockSpec((1, 1, D_MODEL), lambda l, j: (l, 0, j)),
        ],
        out_specs=pl.BlockSpec((1, 8, D_MODEL), lambda l, j: (l, 0, j)),
        out_shape=jax.ShapeDtypeStruct((depth, 8, 6 * D_MODEL), F32),
        compiler_params=_cparams(("arbitrary", "arbitrary")),
        name="modulation",
    )(cvec, w_mod, b_mod.reshape(depth, 1, 6 * D_MODEL))
    return out.reshape(depth, 8, 6, D_MODEL)


def _inproj_kernel(xc_ref, xs_ref, mod_ref, n1_ref, w_ref, gain_ref, bdq_ref, bdk_ref, cos_ref, sin_ref,
                   kc_in, vc_in, q_ref, k_ref, v_ref, z_ref, kc_ref, vc_ref, w_bf, proj_prev,
                   *, n_tiles, n_ctx_tiles, ctx_seq):
    del kc_in, vc_in
    step = pl.program_id(0)

    @pl.when(step == 0)
    def _():
        w_bf[...] = w_ref[0].astype(BF16)
        proj_prev[...] = jnp.zeros_like(proj_prev)

    x = _load_pair(jnp.minimum(step, n_tiles - 1) < n_ctx_tiles, xc_ref, xs_ref)
    mod = mod_ref[0, 0]
    sh1, sc1 = mod[0:1], mod[1:2]
    ms = jnp.mean(x * x, axis=-1, keepdims=True)
    h = (x * lax.rsqrt(ms + EPS) * n1_ref[0]) * (1.0 + sc1) + sh1
    proj_new = _dot(h.astype(BF16), w_bf[...])

    qk = proj_prev[:, :QK_W]
    sq = (qk * qk).astype(BF16)
    half = Q_W // 2
    msq = jnp.concatenate([_dot(sq[:, :half], bdq_ref[...]),
                           _dot(sq[:, half:Q_W], bdq_ref[...]),
                           _dot(sq[:, Q_W:], bdk_ref[...])], axis=1)
    qkn = qk * lax.rsqrt(msq + EPS) * gain_ref[0]
    v = proj_prev[:, QK_W:QK_W + KV_W]
    v_ref[...] = v
    a = proj_prev[:, QK_W + KV_W:QK_W + KV_W + CONV_CH]
    b = proj_prev[:, QK_W + KV_W + CONV_CH:]
    z_ref[...] = a * _sigmoid(b)
    cos = cos_ref[...]
    sin = sin_ref[...]
    first = (lax.broadcasted_iota(jnp.int32, (TM, LANES), 1) % 32) < 16
    for c in range(QK_W // LANES):
        blk = qkn[:, c * LANES:(c + 1) * LANES]
        partner = jnp.where(first, pltpu.roll(blk, LANES - 16, 1), pltpu.roll(blk, 16, 1))
        rot = blk * cos + partner * sin
        if c < Q_W // LANES:
            q_ref[:, c * LANES:(c + 1) * LANES] = (rot * Q_SCALE).astype(BF16)
        else:
            k = rot
            k_ref[...] = k

    proj_prev[...] = proj_new

    @pl.when(step - 1 < n_ctx_tiles)
    def _():
        for s in range(TM // ctx_seq):
            kc_ref[s, 0] = k[s * ctx_seq:(s + 1) * ctx_seq]
            vc_ref[s, 0] = v[s * ctx_seq:(s + 1) * ctx_seq]


def _inproj(cfg, l, x_pair, mod, n1, w_in, gain, bdq, bdk, cos, sin, kc, vc):
    t = cfg.t_all
    nt, nct, st = cfg.n_tiles, cfg.n_ctx_tiles, cfg.smp_tiles
    tile_a = lambda i: jnp.minimum(i, nt - 1)
    tile_b = lambda i: jnp.maximum(i - 1, 0)
    rope_map = lambda i: (jnp.where(tile_b(i) < nct, 0, 1 + (tile_b(i) - nct) % st), 0)
    seqs = TM // cfg.ctx_seq
    cache_spec = lambda: pl.BlockSpec((seqs, 1, cfg.ctx_seq, KV_W),
                                      lambda i: (jnp.minimum(tile_b(i), nct - 1), l, 0, 0))
    return pl.pallas_call(
        functools.partial(_inproj_kernel, n_tiles=nt, n_ctx_tiles=nct, ctx_seq=cfg.ctx_seq),
        grid=(nt + 1,),
        in_specs=_pair_specs(cfg, D_MODEL, tile_a) + [
            pl.BlockSpec((1, 1, 6, D_MODEL), lambda i: (l, _mod_row(cfg, tile_a(i)), 0, 0)),
            pl.BlockSpec((1, 1, D_MODEL), lambda i: (l, 0, 0)),
            pl.BlockSpec((1, D_MODEL, IN_COLS), lambda i: (l, 0, 0)),
            pl.BlockSpec((1, 1, QK_W), lambda i: (l, 0, 0)),
            pl.BlockSpec((Q_W // 2, Q_W // 2), lambda i: (0, 0)),
            pl.BlockSpec((KV_W, KV_W), lambda i: (0, 0)),
            pl.BlockSpec((TM, LANES), rope_map),
            pl.BlockSpec((TM, LANES), rope_map),
            pl.BlockSpec(memory_space=pl.ANY),
            pl.BlockSpec(memory_space=pl.ANY),
        ],
        out_specs=[
            pl.BlockSpec((TM, Q_W), lambda i: (tile_b(i), 0)),
            pl.BlockSpec((TM, KV_W), lambda i: (tile_b(i), 0)),
            pl.BlockSpec((TM, KV_W), lambda i: (tile_b(i), 0)),
            pl.BlockSpec((TM, CONV_CH), lambda i: (tile_b(i), 0)),
            cache_spec(), cache_spec(),
        ],
        out_shape=[
            jax.ShapeDtypeStruct((t, Q_W), BF16),
            jax.ShapeDtypeStruct((t, KV_W), F32),
            jax.ShapeDtypeStruct((t, KV_W), F32),
            jax.ShapeDtypeStruct((t, CONV_CH), F32),
            jax.ShapeDtypeStruct(kc.shape, F32),
            jax.ShapeDtypeStruct(vc.shape, F32),
        ],
        scratch_shapes=[pltpu.VMEM((D_MODEL, IN_COLS), BF16), pltpu.VMEM((TM, IN_COLS), F32)],
        input_output_aliases={10: 4, 11: 5},
        compiler_params=_cparams(("arbitrary",)),
        name="inproj",
    )(*x_pair, mod, n1, w_in, gain, bdq, bdk, cos, sin, kc, vc)


def _mixer_kernel(*refs, rq, n_past, seq_blocks):
    if n_past:
        q_ref, k_ref, v_ref, ck_ref, cv_ref = refs[:5]
        refs = refs[5:]
    else:
        q_ref, k_ref, v_ref = refs[:3]
        refs = refs[3:]
    z_ref, zp_ref, zn_ref, w_ref, b_ref, g_ref, beta_ref = refs[:7]
    refs = refs[7:]
    if n_past:
        for src, dst in zip(refs[0:3], refs[5:8]):
            dst[...] = src[...].astype(BF16)
        refs = refs[3:5] + refs[8:]
    o_ref, co_ref, krep, vrep, pad, taps = refs
    head_w = Q_PER_KV * HEAD_DIM
    j = pl.program_id(1)

    @pl.when(jnp.logical_and(pl.program_id(0) == 0, j == 0))
    def _():
        diag = (lax.broadcasted_iota(jnp.int32, (LANES, LANES), 0)
                == lax.broadcasted_iota(jnp.int32, (LANES, LANES), 1))
        for c in range(CONV_CH // LANES):
            for p in range(4):
                for q in range(4):
                    for half in range(2):
                        tap = w_ref[0, 8 * q + 2 * p + half:8 * q + 2 * p + half + 1, c * LANES:(c + 1) * LANES]
                        taps[c, p, q * LANES:(q + 1) * LANES, half * LANES:(half + 1) * LANES] = (
                            jnp.where(diag, tap, 0.0).astype(BF16))

    @pl.when(j == 0)
    def _():
        src = lax.broadcasted_iota(jnp.int32, (KV_W, head_w), 0)
        dst = lax.broadcasted_iota(jnp.int32, (KV_W, head_w), 1) % HEAD_DIM
        n_new = k_ref.shape[0]
        for h in range(N_KV_HEADS):
            rep = jnp.where(src == dst + h * HEAD_DIM, 1.0, 0.0).astype(BF16)
            if n_past:
                krep[h, :n_past, :] = _dot(ck_ref[0, 0].astype(BF16), rep).astype(BF16)
                vrep[h, :n_past, :] = _dot(cv_ref[0, 0].astype(BF16), rep).astype(BF16)
            krep[h, n_past:n_past + n_new, :] = _dot(k_ref[...].astype(BF16), rep).astype(BF16)
            vrep[h, n_past:n_past + n_new, :] = _dot(v_ref[...].astype(BF16), rep).astype(BF16)

    end = HALO + CONV_ROWS
    if seq_blocks > 1:
        pad[0:HALO, :] = jnp.where(j > 0, zp_ref[...], 0.0)
        pad[end:end + HALO, :] = jnp.where(j < seq_blocks - 1, zn_ref[...], 0.0)
    else:
        pad[0:HALO, :] = jnp.zeros((HALO, CONV_CH), F32)
        pad[end:end + HALO, :] = jnp.zeros((HALO, CONV_CH), F32)
    pad[end + HALO:, :] = jnp.zeros((PAD_ROWS - end - HALO, CONV_CH), F32)
    pad[HALO:end, :] = z_ref[...]

    lane_head = lax.broadcasted_iota(jnp.int32, (rq, head_w), 1) // HEAD_DIM
    for h in range(N_KV_HEADS):
        qh = q_ref[:, h * head_w:(h + 1) * head_w]
        kh = krep[h]
        vh = vrep[h]
        acc = jnp.zeros((rq, head_w), F32)
        for g in range(Q_PER_KV):
            mask = lane_head == g
            qm = jnp.where(mask, qh, jnp.zeros_like(qh))
            s = _dot_nt(qm, kh)
            m = jnp.max(s, axis=-1, keepdims=True)
            p = jnp.exp2(s - m)
            denom = jnp.sum(p, axis=-1, keepdims=True)
            o = _dot(p.astype(BF16), vh)
            acc = jnp.where(mask, o * (1.0 / denom), acc)
        o_ref[:, h * head_w:(h + 1) * head_w] = acc.astype(BF16)
    conv_cols = [_conv_cols(pad, taps, c) for c in range(CONV_CH // LANES)]
    co_ref[...] = _conv_finish(conv_cols, b_ref, g_ref, beta_ref)


def _mixer(cfg, l, q, k, v, cache_k, cache_v, z, taps, dw_b, ln_g, ln_b, expert_w):
    head_w = Q_PER_KV * HEAD_DIM
    rq = CONV_ROWS
    hb = CONV_ROWS // HALO
    last = cfg.t_all // HALO - 1
    vec = lambda: pl.BlockSpec((1, 1, CONV_CH), lambda b, j: (l, 0, 0))

    def conv_specs(row_block):
        return [
            pl.BlockSpec((CONV_ROWS, CONV_CH), lambda b, j: (row_block(b, j), 0)),
            pl.BlockSpec((HALO, CONV_CH), lambda b, j: (jnp.maximum(row_block(b, j) * hb - 1, 0), 0)),
            pl.BlockSpec((HALO, CONV_CH), lambda b, j: (jnp.minimum(row_block(b, j) * hb + hb, last), 0)),
            pl.BlockSpec((1, 32, CONV_CH), lambda b, j: (l, 0, 0)),
            vec(), vec(), vec(),
        ]

    def scratch(sk):
        return [pltpu.VMEM((N_KV_HEADS, sk, head_w), BF16), pltpu.VMEM((N_KV_HEADS, sk, head_w), BF16),
                pltpu.VMEM((PAD_ROWS, CONV_CH), F32),
                pltpu.VMEM((CONV_CH // LANES, 4, 4 * LANES, 2 * LANES), BF16)]

    sc = cfg.ctx_seq
    a_ctx, c_ctx = pl.pallas_call(
        functools.partial(_mixer_kernel, rq=rq, n_past=0, seq_blocks=1),
        grid=(cfg.ctx_batch, 1),
        in_specs=[
            pl.BlockSpec((sc, Q_W), lambda b, j: (b, 0)),
            pl.BlockSpec((sc, KV_W), lambda b, j: (b, 0)),
            pl.BlockSpec((sc, KV_W), lambda b, j: (b, 0)),
        ] + conv_specs(lambda b, j: b),
        out_specs=[pl.BlockSpec((sc, Q_W), lambda b, j: (b, 0)),
                   pl.BlockSpec((sc, CONV_CH), lambda b, j: (b, 0))],
        out_shape=[jax.ShapeDtypeStruct((cfg.t_ctx, Q_W), BF16),
                   jax.ShapeDtypeStruct((cfg.t_ctx, CONV_CH), BF16)],
        scratch_shapes=scratch(sc),
        compiler_params=_cparams(("arbitrary", "arbitrary")),
        name="mixer_ctx",
    )(q, k, v, z, z, z, taps, dw_b, ln_g, ln_b)
    ss = cfg.smp_seq
    sk = cfg.past + ss
    kv_base = cfg.t_ctx // ss
    q_base = cfg.t_ctx // rq
    nqb = ss // rq
    t_smp = cfg.smp_batch * ss
    steps = cfg.smp_batch * nqb
    w_rows = [w.shape[0] // (cfg.depth * steps) for w in expert_w]
    w_specs = lambda off: [pl.BlockSpec((r, w.shape[1]), lambda b, j: (off + b * nqb + j, 0))
                           for r, w in zip(w_rows, expert_w)]
    a_smp, c_smp, *expert_bf = pl.pallas_call(
        functools.partial(_mixer_kernel, rq=rq, n_past=cfg.past, seq_blocks=nqb),
        grid=(cfg.smp_batch, nqb),
        in_specs=[
            pl.BlockSpec((rq, Q_W), lambda b, j: (q_base + b * nqb + j, 0)),
            pl.BlockSpec((ss, KV_W), lambda b, j: (kv_base + b, 0)),
            pl.BlockSpec((ss, KV_W), lambda b, j: (kv_base + b, 0)),
            pl.BlockSpec((1, 1, cfg.past, KV_W), lambda b, j: (b, l, 0, 0)),
            pl.BlockSpec((1, 1, cfg.past, KV_W), lambda b, j: (b, l, 0, 0)),
        ] + conv_specs(lambda b, j: q_base + b * nqb + j) + w_specs(l * steps),
        out_specs=[pl.BlockSpec((rq, Q_W), lambda b, j: (b * nqb + j, 0)),
                   pl.BlockSpec((rq, CONV_CH), lambda b, j: (b * nqb + j, 0))] + w_specs(0),
        out_shape=[jax.ShapeDtypeStruct((t_smp, Q_W), BF16),
                   jax.ShapeDtypeStruct((t_smp, CONV_CH), BF16)]
        + [jax.ShapeDtypeStruct((r * steps, w.shape[1]), BF16) for r, w in zip(w_rows, expert_w)],
        scratch_shapes=scratch(sk),
        compiler_params=_cparams(("arbitrary", "arbitrary"), MIXER_VMEM_LIMIT),
        name="mixer_smp",
    )(q, k, v, cache_k, cache_v, z, z, z, taps, dw_b, ln_g, ln_b, *expert_w)
    return (a_ctx, a_smp), (c_ctx, c_smp), expert_bf


def _conv_cols(pad, taps, c):
    cols = slice(c * LANES, (c + 1) * LANES)
    lhs = jnp.concatenate([pad[8 * q:8 * q + CONV_INNER, cols].astype(BF16) for q in range(4)], axis=1)
    acc = None
    for p in range(4):
        pair = _dot(lhs, taps[c, p])
        for half in range(2):
            o = 2 * p + half
            shifted = pair[o:o + CONV_ROWS, half * LANES:(half + 1) * LANES]
            acc = shifted if acc is None else acc + shifted
    return acc


def _conv_tap_rows(conv_dw_w):
    return jnp.pad(conv_dw_w, ((0, 0), (HALO - CONV_PAD, 32 - CONV_K - (HALO - CONV_PAD)), (0, 0)))


def _conv_finish(blocks, b_ref, g_ref, beta_ref):
    y = jnp.concatenate(blocks, axis=1) + b_ref[0]
    mu = jnp.mean(y, axis=-1, keepdims=True)
    yc = y - mu
    var = jnp.mean(yc * yc, axis=-1, keepdims=True)
    yn = yc * lax.rsqrt(var + EPS) * g_ref[0] + beta_ref[0]
    return (yn * _sigmoid(yn)).astype(BF16)


def _split_bf16(x):
    hi = x.astype(BF16)
    lo = (x - hi.astype(F32)).astype(BF16)
    return hi, lo


def _outproj_kernel(ac_ref, as_ref, cc_ref, cs_ref, xc_ref, xs_ref, mod_ref, n2_ref, wo_ref, wr_ref, br_ref,
                    xmc_ref, xms_ref, hl_ref, lp_ref, cnt_ref, wo_bf, h_prev, lt_prev, *, n_tiles, n_ctx_tiles):
    step = pl.program_id(0)

    @pl.when(step == 0)
    def _():
        wo_bf[...] = wo_ref[0].astype(BF16)
        h_prev[...] = jnp.zeros_like(h_prev)
        lt_prev[...] = jnp.zeros_like(lt_prev)

    is_ctx = jnp.minimum(step, n_tiles - 1) < n_ctx_tiles
    ac = jnp.concatenate([_load_pair(is_ctx, ac_ref, as_ref), _load_pair(is_ctx, cc_ref, cs_ref)], axis=1)
    y = _dot(ac, wo_bf[...])
    mod = mod_ref[0, 0]
    g1, sh2, sc2 = mod[2:3], mod[3:4], mod[4:5]
    xm = _load_pair(is_ctx, xc_ref, xs_ref) + g1 * y
    ms = jnp.mean(xm * xm, axis=-1, keepdims=True)
    h2 = (xm * lax.rsqrt(ms + EPS) * n2_ref[0]) * (1.0 + sc2) + sh2
    h_new, h_lo = _split_bf16(h2)
    w_hi, w_lo = _split_bf16(wr_ref[...])
    logits = _dot(h_new, w_hi) + (_dot(h_new, w_lo) + _dot(h_lo, w_hi))
    lt_new = logits.T[0:N_EXPERTS, :]

    h_hi = h_prev[...]
    s_all = _sigmoid(lt_prev[...])
    sb_all = s_all + br_ref[...]

    rows8 = 2 * EXPERTS_PER_GROUP
    row = lax.broadcasted_iota(jnp.int32, (rows8, TM), 0)
    member = row % EXPERTS_PER_GROUP
    second = row >= EXPERTS_PER_GROUP

    def partner(x, k):
        return jnp.where(member + k < EXPERTS_PER_GROUP,
                         pltpu.roll(x, rows8 - k, 0), pltpu.roll(x, EXPERTS_PER_GROUP - k, 0))

    def other_group(x):
        return pltpu.roll(x, EXPERTS_PER_GROUP, 0)

    def group_sum(x):
        return x + partner(x, 1) + (partner(x, 2) + partner(x, 3))

    def ahead(other, mine, other_first):
        return jnp.where((other >= mine) if other_first else (other > mine), 1.0, 0.0)

    sel, gscore = [], []
    for hv in range(2):
        sb8 = sb_all[hv * rows8:(hv + 1) * rows8]
        rank = jnp.zeros((rows8, TM), F32)
        for k in range(1, EXPERTS_PER_GROUP):
            pk = partner(sb8, k)
            rank = rank + jnp.where(member + k >= EXPERTS_PER_GROUP, ahead(pk, sb8, True), ahead(pk, sb8, False))
        sel.append(rank < 1.5)
        gscore.append(group_sum(jnp.where(sel[hv], sb8, 0.0)))
    gate8, used8 = [], []
    for hv in range(2):
        mine, far = gscore[hv], gscore[1 - hv]
        near = other_group(mine)
        rank = jnp.where(second, ahead(near, mine, True), ahead(near, mine, False))
        rank = rank + ahead(far, mine, hv == 1) + ahead(other_group(far), mine, hv == 1)
        chosen = rank < 0.5
        used8.append(jnp.where(chosen, jnp.where(sel[hv], 1.0, 0.0), 0.0))
        gate8.append(used8[hv] * s_all[hv * rows8:(hv + 1) * rows8])
    total = group_sum(gate8[0]) + group_sum(gate8[1])
    total = total + other_group(total)
    inv_total = 1.0 / total
    gate16 = jnp.concatenate([gate8[0] * inv_total, gate8[1] * inv_total], axis=0)
    used16 = jnp.concatenate(used8, axis=0)
    used = used16 > 0.5

    before = (lax.broadcasted_iota(jnp.int32, (TM, TM), 0) < lax.broadcasted_iota(jnp.int32, (TM, TM), 1))
    prior = _dot(used16.astype(BF16), jnp.where(before, 1.0, 0.0).astype(BF16))
    n_e = jnp.sum(used16, axis=-1, keepdims=True)
    m_e = jnp.floor((n_e + (BF16_ROWS - 1)) * (1.0 / BF16_ROWS)) * BF16_ROWS
    lower = (lax.broadcasted_iota(jnp.int32, (N_EXPERTS, N_EXPERTS), 0)
             > lax.broadcasted_iota(jnp.int32, (N_EXPERTS, N_EXPERTS), 1))
    start = _dot(jnp.where(lower, 1.0, 0.0).astype(BF16),
                 jnp.broadcast_to(m_e, (N_EXPERTS, LANES)).astype(BF16))[:, 0:1]
    pos16 = start + prior
    pos_a = jnp.min(jnp.where(used, pos16, float(LT)), axis=0, keepdims=True)
    pos_b = jnp.max(jnp.where(used, pos16, -1.0), axis=0, keepdims=True)
    gate_a = jnp.sum(jnp.where(used, jnp.where(pos16 == pos_a, gate16, 0.0), 0.0), axis=0, keepdims=True)
    gate_b = jnp.sum(jnp.where(used, jnp.where(pos16 == pos_b, gate16, 0.0), 0.0), axis=0, keepdims=True)
    rows = lax.broadcasted_iota(jnp.int32, (LT, TM), 0)
    perm_a = jnp.where(rows == pos_a.astype(jnp.int32), 1.0, 0.0).astype(BF16)
    perm_b = jnp.where(rows == pos_b.astype(jnp.int32), 1.0, 0.0).astype(BF16)

    hl_ref[0, :, :D_MODEL] = _dot(perm_a + perm_b, h_hi).astype(BF16)

    def gate_rows(g):
        g_hi, g_lo = _split_bf16(g)
        return jnp.concatenate([g_hi.astype(F32), g_lo.astype(F32), jnp.zeros((GATE_W - 2, TM), F32)],
                               axis=0).astype(BF16)

    hl_ref[0, :, D_MODEL:] = (_dot_nt(perm_a, gate_rows(gate_a)) + _dot_nt(perm_b, gate_rows(gate_b))).astype(BF16)

    def split32(p):
        hi = jnp.floor(p * (1.0 / 32.0))
        return [hi, p - 32.0 * hi]

    lp_parts = jnp.concatenate(split32(pos_a) + split32(pos_b) + [jnp.zeros((LANES - 4, TM), F32)],
                               axis=0).astype(BF16)
    eye = jnp.where(lax.broadcasted_iota(jnp.int32, (TM, TM), 0) == lax.broadcasted_iota(jnp.int32, (TM, TM), 1),
                    1.0, 0.0).astype(BF16)
    lp_ref[...] = _dot_nt(eye, lp_parts)

    cnt_ref[0] = jnp.broadcast_to(n_e, (N_EXPERTS, LANES)).astype(jnp.int32)

    h_prev[...] = h_new
    lt_prev[...] = lt_new
    _store_pair(is_ctx, xmc_ref, xms_ref, xm)


def _outproj(cfg, l, a_pair, c_pair, x_pair, mod, n2, w_out, w_router, b_router):
    t, nt, nct = cfg.t_all, cfg.n_tiles, cfg.n_ctx_tiles
    tile_a = lambda i: jnp.minimum(i, nt - 1)
    tile_b = lambda i: jnp.maximum(i - 1, 0)
    return pl.pallas_call(
        functools.partial(_outproj_kernel, n_tiles=nt, n_ctx_tiles=nct),
        grid=(nt + 1,),
        in_specs=_pair_specs(cfg, Q_W, tile_a) + _pair_specs(cfg, CONV_CH, tile_a)
        + _pair_specs(cfg, D_MODEL, tile_a) + [
            pl.BlockSpec((1, 1, 6, D_MODEL), lambda i: (l, _mod_row(cfg, tile_a(i)), 0, 0)),
            pl.BlockSpec((1, 1, D_MODEL), lambda i: (l, 0, 0)),
            pl.BlockSpec((1, D_MODEL, D_MODEL), lambda i: (l, 0, 0)),
            pl.BlockSpec((D_MODEL, LANES), lambda i: (0, 0)),
            pl.BlockSpec((N_EXPERTS, 1), lambda i: (0, 0)),
        ],
        out_specs=_pair_specs(cfg, D_MODEL, tile_a) + [
            pl.BlockSpec((1, LT, XS_W), lambda i: (tile_b(i), 0, 0)),
            pl.BlockSpec((TM, LANES), lambda i: (tile_b(i), 0)),
            pl.BlockSpec((1, N_EXPERTS, LANES), lambda i: (tile_b(i), 0, 0)),
        ],
        out_shape=_pair_shapes(cfg, D_MODEL, F32) + [
            jax.ShapeDtypeStruct((nt, LT, XS_W), BF16),
            jax.ShapeDtypeStruct((t, LANES), F32),
            jax.ShapeDtypeStruct((nt, N_EXPERTS, LANES), jnp.int32),
        ],
        scratch_shapes=[pltpu.VMEM((D_MODEL, D_MODEL), BF16), pltpu.VMEM((TM, D_MODEL), BF16),
                        pltpu.VMEM((N_EXPERTS, TM), F32)],
        compiler_params=_cparams(("arbitrary",)),
        name="outproj",
    )(*a_pair, *c_pair, *x_pair, mod, n2, w_out, w_router, b_router)


def _segment_pieces(n, src_row, dst_row, src_ref, dst_ref, sem, fn):
    for size in SEG_SIZES:
        off = n & ~(2 * size - 1)

        @pl.when((n & size) != 0)
        def _():
            fn(pltpu.make_async_copy(
                src_ref.at[pl.ds(pl.multiple_of(src_row + off, BF16_ROWS), size)],
                dst_ref.at[pl.ds(pl.multiple_of(dst_row + off, BF16_ROWS), size)],
                sem))


def _start(cp):
    cp.start()


def _wait(cp):
    cp.wait()


def _moe_kernel(tg_ref, nv_ref, ssrc_ref, sdst_ref, slen_ref, tlo_ref, thi_ref,
                hl_ref, wg_ref, wu_ref, wd_ref, y_ref, xbuf, sem):
    del tg_ref
    i = pl.program_id(0)
    nv = nv_ref[0]

    def tile_copies(tile, slot, fn):
        base = tile * MOE_TM

        def body(s, carry):
            d = sdst_ref[s]
            lo = jnp.maximum(d, base)
            hi = jnp.minimum(d + slen_ref[s], base + MOE_TM)
            n = jnp.maximum(hi - lo, 0)
            _segment_pieces(n, ssrc_ref[s] + (lo - d), lo - base, hl_ref, xbuf.at[slot], sem.at[slot], fn)
            return carry

        lax.fori_loop(tlo_ref[tile], thi_ref[tile], body, 0)

    @pl.when(i == 0)
    def _():
        xbuf[...] = jnp.zeros_like(xbuf)
        tile_copies(0, 0, _start)

    @pl.when(i + 1 < nv)
    def _():
        tile_copies(i + 1, (i + 1) % 2, _start)

    @pl.when(i >= nv)
    def _():
        y_ref[...] = jnp.zeros_like(y_ref)

    @pl.when(i < nv)
    def _():
        slot = i % 2
        tile_copies(i, slot, _wait)
        xg = xbuf[slot]
        x = xg[:, :D_MODEL]
        gate = xg[:, D_MODEL:D_MODEL + 1].astype(F32) + xg[:, D_MODEL + 1:D_MODEL + 2].astype(F32)
        hg = _dot(x, wg_ref[0])
        hu = _dot(x, wu_ref[0])
        hid = ((hg * _sigmoid(hg)) * hu * gate).astype(BF16)
        y_ref[...] = _dot(hid, wd_ref[0]).astype(BF16)


def _moe(cfg, hl, plan, w_gate, w_up, w_down):
    nmt = cfg.moe_tiles

    def w_map(i, tg, nv, *_):
        return (tg[jnp.minimum(i, nv[0] - 1)], 0, 0)

    w_gate = w_gate.reshape(N_EXPERTS, D_MODEL, D_FF)
    w_up = w_up.reshape(N_EXPERTS, D_MODEL, D_FF)
    w_down = w_down.reshape(N_EXPERTS, D_FF, D_MODEL)

    return pl.pallas_call(
        _moe_kernel,
        grid_spec=pltpu.PrefetchScalarGridSpec(
            num_scalar_prefetch=7,
            grid=(nmt,),
            in_specs=[
                pl.BlockSpec(memory_space=pl.ANY),
                pl.BlockSpec((1, D_MODEL, D_FF), w_map),
                pl.BlockSpec((1, D_MODEL, D_FF), w_map),
                pl.BlockSpec((1, D_FF, D_MODEL), w_map),
            ],
            out_specs=pl.BlockSpec((MOE_TM, D_MODEL), lambda i, *_: (i, 0)),
            scratch_shapes=[pltpu.VMEM((2, MOE_TM, XS_W), BF16), pltpu.SemaphoreType.DMA((2,))],
        ),
        out_shape=jax.ShapeDtypeStruct((nmt * MOE_TM, D_MODEL), BF16),
        compiler_params=_cparams(("arbitrary",)),
        name="moe",
    )(plan.tile_expert, plan.n_valid, plan.seg_tiled, plan.seg_sorted, plan.seg_len, plan.tile_lo, plan.tile_hi,
      hl, w_gate, w_up, w_down)


def _combine_kernel(stiled_ref, ssorted_ref, slen_ref, xmc_ref, xms_ref, ys_ref, lp_ref, mod_ref,
                    oc_ref, os_ref, ybuf, sem, *, n_tiles, n_ctx_tiles):
    t = pl.program_id(0)

    def start_tile(tile, slot):
        def body(e, carry):
            s = e * n_tiles + tile
            _segment_pieces(slen_ref[s], ssorted_ref[s], stiled_ref[s] - tile * LT,
                            ys_ref, ybuf.at[slot], sem.at[slot], _start)
            return carry

        lax.fori_loop(0, N_EXPERTS, body, 0)

    def wait_tile(tile, slot):
        total = lax.fori_loop(0, N_EXPERTS, lambda e, acc: acc + slen_ref[e * n_tiles + tile], 0)
        for size in (2 * SEG_SIZES[0],) + SEG_SIZES:
            @pl.when((total & size) != 0)
            def _():
                pltpu.make_async_copy(ys_ref.at[pl.ds(0, size)], ybuf.at[slot, pl.ds(0, size)], sem.at[slot]).wait()

    @pl.when(t == 0)
    def _():
        ybuf[...] = jnp.zeros_like(ybuf)
        start_tile(0, 0)

    @pl.when(t + 1 < n_tiles)
    def _():
        start_tile(t + 1, (t + 1) % 2)

    slot = t % 2
    wait_tile(t, slot)
    lp = lp_ref[...]
    pos_a = (lp[:, 0:1] * 32.0 + lp[:, 1:2]).astype(jnp.int32)
    pos_b = (lp[:, 2:3] * 32.0 + lp[:, 3:4]).astype(jnp.int32)
    cols = lax.broadcasted_iota(jnp.int32, (TM, LT), 1)
    perm_t = jnp.where(cols == pos_a, 1.0, jnp.where(cols == pos_b, 1.0, 0.0)).astype(BF16)
    y = _dot(perm_t, ybuf[slot])
    g2 = mod_ref[0, 0][5:6]
    is_ctx = t < n_ctx_tiles
    _store_pair(is_ctx, oc_ref, os_ref, _load_pair(is_ctx, xmc_ref, xms_ref) + g2 * y)


def _combine(cfg, l, xm_pair, ys, lp, mod, plan):
    nt = cfg.n_tiles
    return pl.pallas_call(
        functools.partial(_combine_kernel, n_tiles=nt, n_ctx_tiles=cfg.n_ctx_tiles),
        grid_spec=pltpu.PrefetchScalarGridSpec(
            num_scalar_prefetch=3,
            grid=(nt,),
            in_specs=_pair_specs(cfg, D_MODEL) + [
                pl.BlockSpec(memory_space=pl.ANY),
                pl.BlockSpec((TM, LANES), lambda i, *_: (i, 0)),
                pl.BlockSpec((1, 1, 6, D_MODEL), lambda i, *_: (l, _mod_row(cfg, i), 0, 0)),
            ],
            out_specs=_pair_specs(cfg, D_MODEL),
            scratch_shapes=[pltpu.VMEM((2, LT, D_MODEL), BF16), pltpu.SemaphoreType.DMA((2,))],
        ),
        out_shape=_pair_shapes(cfg, D_MODEL, F32),
        compiler_params=_cparams(("arbitrary",)),
        name="combine",
    )(plan.seg_tiled, plan.seg_sorted, plan.seg_len, *xm_pair, ys, lp, mod)


class Plan(NamedTuple):
    seg_tiled: jax.Array
    seg_sorted: jax.Array
    seg_len: jax.Array
    tile_expert: jax.Array
    n_valid: jax.Array
    tile_lo: jax.Array
    tile_hi: jax.Array


def _plan(cfg, counts):
    nt = cfg.n_tiles
    m = (counts + (BF16_ROWS - 1)) // BF16_ROWS * BF16_ROWS
    local_off = jnp.cumsum(m, axis=1) - m
    within_expert = jnp.cumsum(m, axis=0) - m
    expert_rows = jnp.sum(m, axis=0)
    expert_tiles = (expert_rows + (MOE_TM - 1)) // MOE_TM
    tiles_end = jnp.cumsum(expert_tiles)
    expert_start = (tiles_end - expert_tiles) * MOE_TM
    seg_sorted = (expert_start[None, :] + within_expert).T.reshape(-1).astype(jnp.int32)
    seg_tiled = (jnp.arange(nt, dtype=jnp.int32)[:, None] * LT + local_off).T.reshape(-1).astype(jnp.int32)
    seg_len = m.T.reshape(-1).astype(jnp.int32)
    n_valid = tiles_end[-1:].astype(jnp.int32)
    tile_ids = jnp.arange(cfg.moe_tiles, dtype=jnp.int32)
    tile_expert = jnp.minimum(jnp.sum(tile_ids[:, None] >= tiles_end[None, :], axis=1),
                              N_EXPERTS - 1).astype(jnp.int32)
    tile_base = tile_ids * MOE_TM
    seg_end = seg_sorted + seg_len
    tile_lo = jnp.sum(seg_end[None, :] <= tile_base[:, None], axis=1).astype(jnp.int32)
    tile_hi = jnp.sum(seg_sorted[None, :] < tile_base[:, None] + MOE_TM, axis=1).astype(jnp.int32)
    return Plan(seg_tiled, seg_sorted, seg_len, tile_expert, n_valid, tile_lo, tile_hi)


def _rope_tables(n):
    pos = jnp.arange(n)
    rc = jnp.stack([(pos // GRID_W).astype(F32), (pos % GRID_W).astype(F32)], axis=1)
    freqs = ROPE_THETA ** (-jnp.arange(AXIS_PAIRS, dtype=F32) / AXIS_PAIRS)
    ang = rc[:, :, None] * freqs[None, None, :]
    cos = jnp.cos(ang)[:, :, None, :]
    sin = jnp.sin(ang)[:, :, None, :] * jnp.array([-1.0, 1.0], F32)[None, None, :, None]
    cos = jnp.broadcast_to(cos, (n, 2, 2, AXIS_PAIRS)).reshape(n, HEAD_DIM)
    sin = jnp.broadcast_to(sin, (n, 2, 2, AXIS_PAIRS)).reshape(n, HEAD_DIM)
    reps = LANES // HEAD_DIM
    cos = jnp.concatenate([jnp.ones((TM, HEAD_DIM), F32), cos], axis=0)
    sin = jnp.concatenate([jnp.zeros((TM, HEAD_DIM), F32), sin], axis=0)
    return jnp.tile(cos, (1, reps)), jnp.tile(sin, (1, reps))


def _block_diag_mean(width):
    r = jnp.arange(width)
    return jnp.where((r[:, None] // HEAD_DIM) == (r[None, :] // HEAD_DIM), 1.0 / HEAD_DIM, 0.0).astype(BF16)


def _forward(cfg, x_prompt, x_sample, cache_k, cache_v, c, c_ctx, w_mod, b_mod, norm1_g, norm2_g,
             w_in, q_norm_g, k_norm_g, conv_dw_w, conv_dw_b, conv_ln_g, conv_ln_b, w_out,
             w_router, b_router, w_gate, w_up, w_down):
    depth = cfg.depth
    assert cfg.ctx_seq == CONV_ROWS and cfg.t_ctx % TM == 0 and cfg.smp_seq % TM == 0
    assert cfg.t_ctx % cfg.smp_seq == 0 and cfg.smp_batch + 1 <= 8

    x = (x_prompt.reshape(cfg.t_ctx, D_MODEL), x_sample.reshape(-1, D_MODEL))
    cvec = jnp.zeros((8, D_MODEL), F32).at[0].set(c_ctx).at[1:1 + cfg.smp_batch].set(c)
    mod = _modulation(cvec, w_mod, b_mod)

    n1 = norm1_g.reshape(depth, 1, D_MODEL)
    n2 = norm2_g.reshape(depth, 1, D_MODEL)
    gain = jnp.concatenate([jnp.tile(q_norm_g, (1, N_Q_HEADS)), jnp.tile(k_norm_g, (1, N_KV_HEADS))],
                           axis=1).reshape(depth, 1, QK_W)
    expert_w = (w_gate.reshape(-1, D_FF), w_up.reshape(-1, D_FF), w_down.reshape(-1, D_MODEL))
    taps = _conv_tap_rows(conv_dw_w)
    dw_b = conv_dw_b.reshape(depth, 1, CONV_CH)
    ln_g = conv_ln_g.reshape(depth, 1, CONV_CH)
    ln_b = conv_ln_b.reshape(depth, 1, CONV_CH)
    wr = jnp.pad(w_router, ((0, 0), (0, LANES - N_EXPERTS)))
    br = b_router.reshape(N_EXPERTS, 1)
    cos, sin = _rope_tables(cfg.smp_seq)
    bdq = _block_diag_mean(Q_W // 2)
    bdk = _block_diag_mean(KV_W)
    ck = cache_k.reshape(cfg.smp_batch, depth, cfg.past, KV_W)
    cv = cache_v.reshape(cfg.smp_batch, depth, cfg.past, KV_W)

    new_k = jnp.zeros((cfg.ctx_batch, depth, cfg.ctx_seq, KV_W), F32)
    new_v = jnp.zeros((cfg.ctx_batch, depth, cfg.ctx_seq, KV_W), F32)
    for l in range(depth):
        q, k, v, z, new_k, new_v = _inproj(cfg, l, x, mod, n1, w_in, gain, bdq, bdk, cos, sin, new_k, new_v)
        a_pair, c_pair, expert_bf = _mixer(cfg, l, q, k, v, ck, cv, z, taps, dw_b, ln_g, ln_b, expert_w)
        xmc, xms, hl, lp, cnt = _outproj(cfg, l, a_pair, c_pair, x, mod, n2, w_out, wr, br)
        plan = _plan(cfg, cnt[:, :, 0])
        ys = _moe(cfg, hl.reshape(cfg.n_tiles * LT, XS_W), plan, *expert_bf)
        x = _combine(cfg, l, (xmc, xms), ys, lp, mod, plan)

    y_prompt = x[0].reshape(cfg.ctx_batch, cfg.ctx_seq, D_MODEL)
    y_sample = x[1].reshape(cfg.smp_batch, cfg.smp_seq, D_MODEL)
    cache_shape = (cfg.ctx_batch, depth, cfg.ctx_seq, N_KV_HEADS, HEAD_DIM)
    return y_prompt, y_sample, new_k.reshape(cache_shape), new_v.reshape(cache_shape)


def kernel(x_prompt, x_sample, cache_k, cache_v, c, c_ctx, w_mod, b_mod, norm1_g, norm2_g, w_in, q_norm_g,
           k_norm_g, conv_dw_w, conv_dw_b, conv_ln_g, conv_ln_b, w_out, w_router, b_router, w_gate, w_up, w_down):
    cfg = Cfg(ctx_batch=x_prompt.shape[0], ctx_seq=x_prompt.shape[1], smp_batch=x_sample.shape[0],
              smp_seq=x_sample.shape[1], past=cache_k.shape[2], depth=w_mod.shape[0])
    return _forward(cfg, x_prompt, x_sample, cache_k, cache_v, c, c_ctx, w_mod, b_mod, norm1_g, norm2_g,
                    w_in, q_norm_g, k_norm_g, conv_dw_w, conv_dw_b, conv_ln_g, conv_ln_b, w_out,
                    w_router, b_router, w_gate, w_up, w_down)
```

```python
import functools
import math
from typing import NamedTuple

import jax
import jax.numpy as jnp
from jax import lax
from jax.experimental import pallas as pl
from jax.experimental.pallas import tpu as pltpu

F32 = jnp.float32
BF16 = jnp.bfloat16

D_MODEL = 1024
HEAD_DIM = 64
N_Q_HEADS = 8
N_KV_HEADS = 2
Q_PER_KV = N_Q_HEADS // N_KV_HEADS
Q_W = N_Q_HEADS * HEAD_DIM
KV_W = N_KV_HEADS * HEAD_DIM
QK_W = Q_W + KV_W
CONV_CH = D_MODEL // 2
CONV_K = 31
CONV_PAD = CONV_K // 2
IN_COLS = Q_W + 2 * KV_W + 2 * CONV_CH
GRID_W = 64
AXIS_PAIRS = HEAD_DIM // 4
ROPE_THETA = 10000.0
N_EXPERTS = 16
N_GROUPS = 4
EXPERTS_PER_GROUP = N_EXPERTS // N_GROUPS
D_FF = 512
EPS = 1e-6

LANES = 128
BF16_ROWS = 16
TM = 512
TOP_K = 2
LT = 1280
MOE_TM = 512
CONV_ROWS = 256
HALO = 16
CONV_INNER = CONV_ROWS + 16
PAD_ROWS = 24 + CONV_INNER + 8
SEG_SIZES = (512, 256, 128, 64, 32, 16)
SEG_SPLIT = 128
Q_SCALE = (1.0 / math.sqrt(HEAD_DIM)) * math.log2(math.e)
VMEM_LIMIT = 48 * 1024 * 1024
MIXER_VMEM_LIMIT = 56 * 1024 * 1024


class Cfg(NamedTuple):
    ctx_batch: int
    ctx_seq: int
    smp_batch: int
    smp_seq: int
    past: int
    depth: int

    @property
    def t_ctx(self):
        return self.ctx_batch * self.ctx_seq

    @property
    def t_all(self):
        return self.t_ctx + self.smp_batch * self.smp_seq

    @property
    def n_tiles(self):
        return self.t_all // TM

    @property
    def n_ctx_tiles(self):
        return self.t_ctx // TM

    @property
    def smp_tiles(self):
        return self.smp_seq // TM

    @property
    def moe_tiles(self):
        rows = TOP_K * self.t_all + self.n_tiles * N_EXPERTS * (BF16_ROWS - 1) + N_EXPERTS * (MOE_TM - 1)
        return -(-rows // MOE_TM)


def _mod_row(cfg, i):
    return jnp.where(i < cfg.n_ctx_tiles, 0, 1 + (i - cfg.n_ctx_tiles) // cfg.smp_tiles)


def _pair_specs(cfg, width, tile_of=lambda i: i):
    nct = cfg.n_ctx_tiles
    return [pl.BlockSpec((TM, width), lambda i, *_: (jnp.minimum(tile_of(i), nct - 1), 0)),
            pl.BlockSpec((TM, width), lambda i, *_: (jnp.maximum(tile_of(i) - nct, 0), 0))]


def _pair_shapes(cfg, width, dtype):
    return [jax.ShapeDtypeStruct((cfg.t_ctx, width), dtype),
            jax.ShapeDtypeStruct((cfg.t_all - cfg.t_ctx, width), dtype)]


def _load_pair(is_ctx, c_ref, s_ref):
    return jnp.where(is_ctx, c_ref[...], s_ref[...])


def _store_pair(is_ctx, c_ref, s_ref, val):
    @pl.when(is_ctx)
    def _():
        c_ref[...] = val

    @pl.when(jnp.logical_not(is_ctx))
    def _():
        s_ref[...] = val


def _cparams(sem, vmem=VMEM_LIMIT):
    return pltpu.CompilerParams(dimension_semantics=sem, vmem_limit_bytes=vmem)


def _dot(a, b):
    return jnp.dot(a, b, preferred_element_type=F32)


def _dot_nt(a, b):
    return lax.dot_general(a, b, (((1,), (1,)), ((), ())), preferred_element_type=F32)


def _sigmoid(x):
    return 1.0 / (1.0 + jnp.exp(-x))


def _mod_kernel(c_ref, w_ref, b_ref, o_ref):
    c = c_ref[...]
    a = (c * _sigmoid(c)).astype(BF16)
    o_ref[0] = _dot(a, w_ref[0].astype(BF16)) + b_ref[0]


def _modulation(cvec, w_mod, b_mod):
    depth = w_mod.shape[0]
    out = pl.pallas_call(
        _mod_kernel,
        grid=(depth, 6),
        in_specs=[
            pl.BlockSpec((8, D_MODEL), lambda l, j: (0, 0)),
            pl.BlockSpec((1, D_MODEL, D_MODEL), lambda l, j: (l, 0, j)),
            pl.BlockSpec((1, 1, D_MODEL), lambda l, j: (l, 0, j)),
        ],
        out_specs=pl.BlockSpec((1, 8, D_MODEL), lambda l, j: (l, 0, j)),
        out_shape=jax.ShapeDtypeStruct((depth, 8, 6 * D_MODEL), F32),
        compiler_params=_cparams(("arbitrary", "arbitrary")),
        name="modulation",
    )(cvec, w_mod, b_mod.reshape(depth, 1, 6 * D_MODEL))
    return out.reshape(depth, 8, 6, D_MODEL)


def _inproj_kernel(xc_ref, xs_ref, mod_ref, n1_ref, w_ref, gain_ref, bdq_ref, bdk_ref, cos_ref, sin_ref,
                   kc_in, vc_in, q_ref, k_ref, v_ref, z_ref, kc_ref, vc_ref, w_bf, proj_prev,
                   *, n_tiles, n_ctx_tiles, ctx_seq):
    del kc_in, vc_in
    step = pl.program_id(0)

    @pl.when(step == 0)
    def _():
        w_bf[...] = w_ref[0].astype(BF16)
        proj_prev[...] = jnp.zeros_like(proj_prev)

    x = _load_pair(jnp.minimum(step, n_tiles - 1) < n_ctx_tiles, xc_ref, xs_ref)
    mod = mod_ref[0, 0]
    sh1, sc1 = mod[0:1], mod[1:2]
    ms = jnp.mean(x * x, axis=-1, keepdims=True)
    h = (x * lax.rsqrt(ms + EPS) * n1_ref[0]) * (1.0 + sc1) + sh1
    proj_new = _dot(h.astype(BF16), w_bf[...])

    qk = proj_prev[:, :QK_W]
    sq = (qk * qk).astype(BF16)
    half = Q_W // 2
    msq = jnp.concatenate([_dot(sq[:, :half], bdq_ref[...]),
                           _dot(sq[:, half:Q_W], bdq_ref[...]),
                           _dot(sq[:, Q_W:], bdk_ref[...])], axis=1)
    qkn = qk * lax.rsqrt(msq + EPS) * gain_ref[0]
    v = proj_prev[:, QK_W:QK_W + KV_W]
    v_ref[...] = v
    a = proj_prev[:, QK_W + KV_W:QK_W + KV_W + CONV_CH]
    b = proj_prev[:, QK_W + KV_W + CONV_CH:]
    z_ref[...] = a * _sigmoid(b)
    cos = cos_ref[...]
    sin = sin_ref[...]
    first = (lax.broadcasted_iota(jnp.int32, (TM, LANES), 1) % 32) < 16
    for c in range(QK_W // LANES):
        blk = qkn[:, c * LANES:(c + 1) * LANES]
        partner = jnp.where(first, pltpu.roll(blk, LANES - 16, 1), pltpu.roll(blk, 16, 1))
        rot = blk * cos + partner * sin
        if c < Q_W // LANES:
            q_ref[:, c * LANES:(c + 1) * LANES] = (rot * Q_SCALE).astype(BF16)
        else:
            k = rot
            k_ref[...] = k

    proj_prev[...] = proj_new

    @pl.when(step - 1 < n_ctx_tiles)
    def _():
        for s in range(TM // ctx_seq):
            kc_ref[s, 0] = k[s * ctx_seq:(s + 1) * ctx_seq]
            vc_ref[s, 0] = v[s * ctx_seq:(s + 1) * ctx_seq]


def _inproj(cfg, l, x_pair, mod, n1, w_in, gain, bdq, bdk, cos, sin, kc, vc):
    t = cfg.t_all
    nt, nct, st = cfg.n_tiles, cfg.n_ctx_tiles, cfg.smp_tiles
    tile_a = lambda i: jnp.minimum(i, nt - 1)
    tile_b = lambda i: jnp.maximum(i - 1, 0)
    rope_map = lambda i: (jnp.where(tile_b(i) < nct, 0, 1 + (tile_b(i) - nct) % st), 0)
    seqs = TM // cfg.ctx_seq
    cache_spec = lambda: pl.BlockSpec((seqs, 1, cfg.ctx_seq, KV_W),
                                      lambda i: (jnp.minimum(tile_b(i), nct - 1), l, 0, 0))
    return pl.pallas_call(
        functools.partial(_inproj_kernel, n_tiles=nt, n_ctx_tiles=nct, ctx_seq=cfg.ctx_seq),
        grid=(nt + 1,),
        in_specs=_pair_specs(cfg, D_MODEL, tile_a) + [
            pl.BlockSpec((1, 1, 6, D_MODEL), lambda i: (l, _mod_row(cfg, tile_a(i)), 0, 0)),
            pl.BlockSpec((1, 1, D_MODEL), lambda i: (l, 0, 0)),
            pl.BlockSpec((1, D_MODEL, IN_COLS), lambda i: (l, 0, 0)),
            pl.BlockSpec((1, 1, QK_W), lambda i: (l, 0, 0)),
            pl.BlockSpec((Q_W // 2, Q_W // 2), lambda i: (0, 0)),
            pl.BlockSpec((KV_W, KV_W), lambda i: (0, 0)),
            pl.BlockSpec((TM, LANES), rope_map),
            pl.BlockSpec((TM, LANES), rope_map),
            pl.BlockSpec(memory_space=pl.ANY),
            pl.BlockSpec(memory_space=pl.ANY),
        ],
        out_specs=[
            pl.BlockSpec((TM, Q_W), lambda i: (tile_b(i), 0)),
            pl.BlockSpec((TM, KV_W), lambda i: (tile_b(i), 0)),
            pl.BlockSpec((TM, KV_W), lambda i: (tile_b(i), 0)),
            pl.BlockSpec((TM, CONV_CH), lambda i: (tile_b(i), 0)),
            cache_spec(), cache_spec(),
        ],
        out_shape=[
            jax.ShapeDtypeStruct((t, Q_W), BF16),
            jax.ShapeDtypeStruct((t, KV_W), F32),
            jax.ShapeDtypeStruct((t, KV_W), F32),
            jax.ShapeDtypeStruct((t, CONV_CH), F32),
            jax.ShapeDtypeStruct(kc.shape, F32),
            jax.ShapeDtypeStruct(vc.shape, F32),
        ],
        scratch_shapes=[pltpu.VMEM((D_MODEL, IN_COLS), BF16), pltpu.VMEM((TM, IN_COLS), F32)],
        input_output_aliases={10: 4, 11: 5},
        compiler_params=_cparams(("arbitrary",)),
        name="inproj",
    )(*x_pair, mod, n1, w_in, gain, bdq, bdk, cos, sin, kc, vc)


def _mixer_kernel(*refs, rq, n_past, seq_blocks):
    if n_past:
        q_ref, k_ref, v_ref, ck_ref, cv_ref = refs[:5]
        refs = refs[5:]
    else:
        q_ref, k_ref, v_ref = refs[:3]
        refs = refs[3:]
    z_ref, zp_ref, zn_ref, w_ref, b_ref, g_ref, beta_ref = refs[:7]
    refs = refs[7:]
    if n_past:
        for src, dst in zip(refs[0:3], refs[5:8]):
            dst[...] = src[...].astype(BF16)
        refs = refs[3:5] + refs[8:]
    o_ref, co_ref, krep, vrep, pad, taps = refs
    head_w = Q_PER_KV * HEAD_DIM
    j = pl.program_id(1)

    @pl.when(jnp.logical_and(pl.program_id(0) == 0, j == 0))
    def _():
        diag = (lax.broadcasted_iota(jnp.int32, (LANES, LANES), 0)
                == lax.broadcasted_iota(jnp.int32, (LANES, LANES), 1))
        for c in range(CONV_CH // LANES):
            for p in range(4):
                for q in range(4):
                    for half in range(2):
                        tap = w_ref[0, 8 * q + 2 * p + half:8 * q + 2 * p + half + 1, c * LANES:(c + 1) * LANES]
                        taps[c, p, q * LANES:(q + 1) * LANES, half * LANES:(half + 1) * LANES] = (
                            jnp.where(diag, tap, 0.0).astype(BF16))

    @pl.when(j == 0)
    def _():
        src = lax.broadcasted_iota(jnp.int32, (KV_W, head_w), 0)
        dst = lax.broadcasted_iota(jnp.int32, (KV_W, head_w), 1) % HEAD_DIM
        n_new = k_ref.shape[0]
        for h in range(N_KV_HEADS):
            rep = jnp.where(src == dst + h * HEAD_DIM, 1.0, 0.0).astype(BF16)
            if n_past:
                krep[h, :n_past, :] = _dot(ck_ref[0, 0].astype(BF16), rep).astype(BF16)
                vrep[h, :n_past, :] = _dot(cv_ref[0, 0].astype(BF16), rep).astype(BF16)
            krep[h, n_past:n_past + n_new, :] = _dot(k_ref[...].astype(BF16), rep).astype(BF16)
            vrep[h, n_past:n_past + n_new, :] = _dot(v_ref[...].astype(BF16), rep).astype(BF16)

    end = HALO + CONV_ROWS
    if seq_blocks > 1:
        pad[0:HALO, :] = jnp.where(j > 0, zp_ref[...], 0.0)
        pad[end:end + HALO, :] = jnp.where(j < seq_blocks - 1, zn_ref[...], 0.0)
    else:
        pad[0:HALO, :] = jnp.zeros((HALO, CONV_CH), F32)
        pad[end:end + HALO, :] = jnp.zeros((HALO, CONV_CH), F32)
    pad[end + HALO:, :] = jnp.zeros((PAD_ROWS - end - HALO, CONV_CH), F32)
    pad[HALO:end, :] = z_ref[...]

    lane_head = lax.broadcasted_iota(jnp.int32, (rq, head_w), 1) // HEAD_DIM
    for h in range(N_KV_HEADS):
        qh = q_ref[:, h * head_w:(h + 1) * head_w]
        kh = krep[h]
        vh = vrep[h]
        acc = jnp.zeros((rq, head_w), F32)
        for g in range(Q_PER_KV):
            mask = lane_head == g
            qm = jnp.where(mask, qh, jnp.zeros_like(qh))
            s = _dot_nt(qm, kh)
            m = jnp.max(s, axis=-1, keepdims=True)
            p = jnp.exp2(s - m)
            denom = jnp.sum(p, axis=-1, keepdims=True)
            o = _dot(p.astype(BF16), vh)
            acc = jnp.where(mask, o * (1.0 / denom), acc)
        o_ref[:, h * head_w:(h + 1) * head_w] = acc.astype(BF16)
    conv_cols = [_conv_cols(pad, taps, c) for c in range(CONV_CH // LANES)]
    co_ref[...] = _conv_finish(conv_cols, b_ref, g_ref, beta_ref)


def _mixer(cfg, l, q, k, v, cache_k, cache_v, z, taps, dw_b, ln_g, ln_b, expert_w):
    head_w = Q_PER_KV * HEAD_DIM
    rq = CONV_ROWS
    hb = CONV_ROWS // HALO
    last = cfg.t_all // HALO - 1
    vec = lambda: pl.BlockSpec((1, 1, CONV_CH), lambda b, j: (l, 0, 0))

    def conv_specs(row_block):
        return [
            pl.BlockSpec((CONV_ROWS, CONV_CH), lambda b, j: (row_block(b, j), 0)),
            pl.BlockSpec((HALO, CONV_CH), lambda b, j: (jnp.maximum(row_block(b, j) * hb - 1, 0), 0)),
            pl.BlockSpec((HALO, CONV_CH), lambda b, j: (jnp.minimum(row_block(b, j) * hb + hb, last), 0)),
            pl.BlockSpec((1, 32, CONV_CH), lambda b, j: (l, 0, 0)),
            vec(), vec(), vec(),
        ]

    def scratch(sk):
        return [pltpu.VMEM((N_KV_HEADS, sk, head_w), BF16), pltpu.VMEM((N_KV_HEADS, sk, head_w), BF16),
                pltpu.VMEM((PAD_ROWS, CONV_CH), F32),
                pltpu.VMEM((CONV_CH // LANES, 4, 4 * LANES, 2 * LANES), BF16)]

    sc = cfg.ctx_seq
    a_ctx, c_ctx = pl.pallas_call(
        functools.partial(_mixer_kernel, rq=rq, n_past=0, seq_blocks=1),
        grid=(cfg.ctx_batch, 1),
        in_specs=[
            pl.BlockSpec((sc, Q_W), lambda b, j: (b, 0)),
            pl.BlockSpec((sc, KV_W), lambda b, j: (b, 0)),
            pl.BlockSpec((sc, KV_W), lambda b, j: (b, 0)),
        ] + conv_specs(lambda b, j: b),
        out_specs=[pl.BlockSpec((sc, Q_W), lambda b, j: (b, 0)),
                   pl.BlockSpec((sc, CONV_CH), lambda b, j: (b, 0))],
        out_shape=[jax.ShapeDtypeStruct((cfg.t_ctx, Q_W), BF16),
                   jax.ShapeDtypeStruct((cfg.t_ctx, CONV_CH), BF16)],
        scratch_shapes=scratch(sc),
        compiler_params=_cparams(("arbitrary", "arbitrary")),
        name="mixer_ctx",
    )(q, k, v, z, z, z, taps, dw_b, ln_g, ln_b)
    ss = cfg.smp_seq
    sk = cfg.past + ss
    kv_base = cfg.t_ctx // ss
    q_base = cfg.t_ctx // rq
    nqb = ss // rq
    t_smp = cfg.smp_batch * ss
    steps = cfg.smp_batch * nqb
    w_rows = [w.shape[0] // (cfg.depth * steps) for w in expert_w]
    w_specs = lambda off: [pl.BlockSpec((r, w.shape[1]), lambda b, j: (off + b * nqb + j, 0))
                           for r, w in zip(w_rows, expert_w)]
    a_smp, c_smp, *expert_bf = pl.pallas_call(
        functools.partial(_mixer_kernel, rq=rq, n_past=cfg.past, seq_blocks=nqb),
        grid=(cfg.smp_batch, nqb),
        in_specs=[
            pl.BlockSpec((rq, Q_W), lambda b, j: (q_base + b * nqb + j, 0)),
            pl.BlockSpec((ss, KV_W), lambda b, j: (kv_base + b, 0)),
            pl.BlockSpec((ss, KV_W), lambda b, j: (kv_base + b, 0)),
            pl.BlockSpec((1, 1, cfg.past, KV_W), lambda b, j: (b, l, 0, 0)),
            pl.BlockSpec((1, 1, cfg.past, KV_W), lambda b, j: (b, l, 0, 0)),
        ] + conv_specs(lambda b, j: q_base + b * nqb + j) + w_specs(l * steps),
        out_specs=[pl.BlockSpec((rq, Q_W), lambda b, j: (b * nqb + j, 0)),
                   pl.BlockSpec((rq, CONV_CH), lambda b, j: (b * nqb + j, 0))] + w_specs(0),
        out_shape=[jax.ShapeDtypeStruct((t_smp, Q_W), BF16),
                   jax.ShapeDtypeStruct((t_smp, CONV_CH), BF16)]
        + [jax.ShapeDtypeStruct((r * steps, w.shape[1]), BF16) for r, w in zip(w_rows, expert_w)],
        scratch_shapes=scratch(sk),
        compiler_params=_cparams(("arbitrary", "arbitrary"), MIXER_VMEM_LIMIT),
        name="mixer_smp",
    )(q, k, v, cache_k, cache_v, z, z, z, taps, dw_b, ln_g, ln_b, *expert_w)
    return (a_ctx, a_smp), (c_ctx, c_smp), expert_bf


def _conv_cols(pad, taps, c):
    cols = slice(c * LANES, (c + 1) * LANES)
    lhs = jnp.concatenate([pad[8 * q:8 * q + CONV_INNER, cols].astype(BF16) for q in range(4)], axis=1)
    acc = None
    for p in range(4):
        pair = _dot(lhs, taps[c, p])
        for half in range(2):
            o = 2 * p + half
            shifted = pair[o:o + CONV_ROWS, half * LANES:(half + 1) * LANES]
            acc = shifted if acc is None else acc + shifted
    return acc


def _conv_tap_rows(conv_dw_w):
    return jnp.pad(conv_dw_w, ((0, 0), (HALO - CONV_PAD, 32 - CONV_K - (HALO - CONV_PAD)), (0, 0)))


def _conv_finish(blocks, b_ref, g_ref, beta_ref):
    y = jnp.concatenate(blocks, axis=1) + b_ref[0]
    mu = jnp.mean(y, axis=-1, keepdims=True)
    yc = y - mu
    var = jnp.mean(yc * yc, axis=-1, keepdims=True)
    yn = yc * lax.rsqrt(var + EPS) * g_ref[0] + beta_ref[0]
    return (yn * _sigmoid(yn)).astype(BF16)


def _split_bf16(x):
    hi = x.astype(BF16)
    lo = (x - hi.astype(F32)).astype(BF16)
    return hi, lo


def _outproj_kernel(ac_ref, as_ref, cc_ref, cs_ref, xc_ref, xs_ref, mod_ref, n2_ref, wo_ref, wr_ref, br_ref,
                    xmc_ref, xms_ref, hl_ref, lp_ref, cnt_ref, wo_bf, h_prev, lt_prev, *, n_tiles, n_ctx_tiles):
    step = pl.program_id(0)

    @pl.when(step == 0)
    def _():
        wo_bf[...] = wo_ref[0].astype(BF16)
        h_prev[...] = jnp.zeros_like(h_prev)
        lt_prev[...] = jnp.zeros_like(lt_prev)

    is_ctx = jnp.minimum(step, n_tiles - 1) < n_ctx_tiles
    ac = jnp.concatenate([_load_pair(is_ctx, ac_ref, as_ref), _load_pair(is_ctx, cc_ref, cs_ref)], axis=1)
    y = _dot(ac, wo_bf[...])
    mod = mod_ref[0, 0]
    g1, sh2, sc2 = mod[2:3], mod[3:4], mod[4:5]
    xm = _load_pair(is_ctx, xc_ref, xs_ref) + g1 * y
    ms = jnp.mean(xm * xm, axis=-1, keepdims=True)
    h2 = (xm * lax.rsqrt(ms + EPS) * n2_ref[0]) * (1.0 + sc2) + sh2
    h_new, h_lo = _split_bf16(h2)
    w_hi, w_lo = _split_bf16(wr_ref[...])
    logits = _dot(h_new, w_hi) + (_dot(h_new, w_lo) + _dot(h_lo, w_hi))
    lt_new = logits.T[0:N_EXPERTS, :]

    h_hi = h_prev[...]
    s_all = _sigmoid(lt_prev[...])
    sb_all = s_all + br_ref[...]

    rows8 = 2 * EXPERTS_PER_GROUP
    row = lax.broadcasted_iota(jnp.int32, (rows8, TM), 0)
    member = row % EXPERTS_PER_GROUP
    second = row >= EXPERTS_PER_GROUP

    def partner(x, k):
        return jnp.where(member + k < EXPERTS_PER_GROUP,
                         pltpu.roll(x, rows8 - k, 0), pltpu.roll(x, EXPERTS_PER_GROUP - k, 0))

    def other_group(x):
        return pltpu.roll(x, EXPERTS_PER_GROUP, 0)

    def group_sum(x):
        return x + partner(x, 1) + (partner(x, 2) + partner(x, 3))

    def ahead(other, mine, other_first):
        return jnp.where((other >= mine) if other_first else (other > mine), 1.0, 0.0)

    sel, gscore = [], []
    for hv in range(2):
        sb8 = sb_all[hv * rows8:(hv + 1) * rows8]
        rank = jnp.zeros((rows8, TM), F32)
        for k in range(1, EXPERTS_PER_GROUP):
            pk = partner(sb8, k)
            rank = rank + jnp.where(member + k >= EXPERTS_PER_GROUP, ahead(pk, sb8, True), ahead(pk, sb8, False))
        sel.append(rank < 1.5)
        gscore.append(group_sum(jnp.where(sel[hv], sb8, 0.0)))
    gate8, used8 = [], []
    for hv in range(2):
        mine, far = gscore[hv], gscore[1 - hv]
        near = other_group(mine)
        rank = jnp.where(second, ahead(near, mine, True), ahead(near, mine, False))
        rank = rank + ahead(far, mine, hv == 1) + ahead(other_group(far), mine, hv == 1)
        chosen = rank < 0.5
        used8.append(jnp.where(chosen, jnp.where(sel[hv], 1.0, 0.0), 0.0))
        gate8.append(used8[hv] * s_all[hv * rows8:(hv + 1) * rows8])
    total = group_sum(gate8[0]) + group_sum(gate8[1])
    total = total + other_group(total)
    inv_total = 1.0 / total
    gate16 = jnp.concatenate([gate8[0] * inv_total, gate8[1] * inv_total], axis=0)
    used16 = jnp.concatenate(used8, axis=0)
    used = used16 > 0.5

    before = (lax.broadcasted_iota(jnp.int32, (TM, TM), 0) < lax.broadcasted_iota(jnp.int32, (TM, TM), 1))
    prior = _dot(used16.astype(BF16), jnp.where(before, 1.0, 0.0).astype(BF16))
    n_e = jnp.sum(used16, axis=-1, keepdims=True)
    m_e = jnp.floor((n_e + (BF16_ROWS - 1)) * (1.0 / BF16_ROWS)) * BF16_ROWS
    lower = (lax.broadcasted_iota(jnp.int32, (N_EXPERTS, N_EXPERTS), 0)
             > lax.broadcasted_iota(jnp.int32, (N_EXPERTS, N_EXPERTS), 1))
    start = _dot(jnp.where(lower, 1.0, 0.0).astype(BF16),
                 jnp.broadcast_to(m_e, (N_EXPERTS, LANES)).astype(BF16))[:, 0:1]
    pos16 = start + prior
    pos_a = jnp.min(jnp.where(used, pos16, float(LT)), axis=0, keepdims=True)
    pos_b = jnp.max(jnp.where(used, pos16, -1.0), axis=0, keepdims=True)
    gate_a = jnp.sum(jnp.where(used, jnp.where(pos16 == pos_a, gate16, 0.0), 0.0), axis=0, keepdims=True)
    gate_b = jnp.sum(jnp.where(used, jnp.where(pos16 == pos_b, gate16, 0.0), 0.0), axis=0, keepdims=True)
    rows = lax.broadcasted_iota(jnp.int32, (LT, TM), 0)
    perm_a = jnp.where(rows == pos_a.astype(jnp.int32), 1.0, 0.0).astype(BF16)
    perm_b = jnp.where(rows == pos_b.astype(jnp.int32), 1.0, 0.0).astype(BF16)

    hl_ref[0] = _dot(perm_a + perm_b, h_hi).astype(BF16)

    def split32(p):
        hi = jnp.floor(p * (1.0 / 32.0))
        return [hi, p - 32.0 * hi]

    def split3(g):
        hi = g.astype(BF16).astype(F32)
        mid = (g - hi).astype(BF16).astype(F32)
        lo = ((g - hi) - mid).astype(BF16).astype(F32)
        return [hi, mid, lo]

    parts = split32(pos_a) + split32(pos_b) + split3(gate_a) + split3(gate_b)
    lp_parts = jnp.concatenate(parts + [jnp.zeros((LANES - len(parts), TM), F32)], axis=0).astype(BF16)
    eye = jnp.where(lax.broadcasted_iota(jnp.int32, (TM, TM), 0) == lax.broadcasted_iota(jnp.int32, (TM, TM), 1),
                    1.0, 0.0).astype(BF16)
    lp_ref[...] = _dot_nt(eye, lp_parts)

    cnt_ref[0] = jnp.broadcast_to(n_e, (N_EXPERTS, LANES)).astype(jnp.int32)

    h_prev[...] = h_new
    lt_prev[...] = lt_new
    _store_pair(is_ctx, xmc_ref, xms_ref, xm)


def _outproj(cfg, l, a_pair, c_pair, x_pair, mod, n2, w_out, w_router, b_router):
    t, nt, nct = cfg.t_all, cfg.n_tiles, cfg.n_ctx_tiles
    tile_a = lambda i: jnp.minimum(i, nt - 1)
    tile_b = lambda i: jnp.maximum(i - 1, 0)
    return pl.pallas_call(
        functools.partial(_outproj_kernel, n_tiles=nt, n_ctx_tiles=nct),
        grid=(nt + 1,),
        in_specs=_pair_specs(cfg, Q_W, tile_a) + _pair_specs(cfg, CONV_CH, tile_a)
        + _pair_specs(cfg, D_MODEL, tile_a) + [
            pl.BlockSpec((1, 1, 6, D_MODEL), lambda i: (l, _mod_row(cfg, tile_a(i)), 0, 0)),
            pl.BlockSpec((1, 1, D_MODEL), lambda i: (l, 0, 0)),
            pl.BlockSpec((1, D_MODEL, D_MODEL), lambda i: (l, 0, 0)),
            pl.BlockSpec((D_MODEL, LANES), lambda i: (0, 0)),
            pl.BlockSpec((N_EXPERTS, 1), lambda i: (0, 0)),
        ],
        out_specs=_pair_specs(cfg, D_MODEL, tile_a) + [
            pl.BlockSpec((1, LT, D_MODEL), lambda i: (tile_b(i), 0, 0)),
            pl.BlockSpec((TM, LANES), lambda i: (tile_b(i), 0)),
            pl.BlockSpec((1, N_EXPERTS, LANES), lambda i: (tile_b(i), 0, 0)),
        ],
        out_shape=_pair_shapes(cfg, D_MODEL, F32) + [
            jax.ShapeDtypeStruct((nt, LT, D_MODEL), BF16),
            jax.ShapeDtypeStruct((t, LANES), F32),
            jax.ShapeDtypeStruct((nt, N_EXPERTS, LANES), jnp.int32),
        ],
        scratch_shapes=[pltpu.VMEM((D_MODEL, D_MODEL), BF16), pltpu.VMEM((TM, D_MODEL), BF16),
                        pltpu.VMEM((N_EXPERTS, TM), F32)],
        compiler_params=_cparams(("arbitrary",)),
        name="outproj",
    )(*a_pair, *c_pair, *x_pair, mod, n2, w_out, w_router, b_router)


def _start_pieces(n, src_row, dst_row, src_ref, dst_ref, sem):
    def piece(size):
        off = n & ~(2 * size - 1)

        @pl.when((n & size) != 0)
        def _():
            pltpu.make_async_copy(
                src_ref.at[pl.ds(pl.multiple_of(src_row + off, BF16_ROWS), size)],
                dst_ref.at[pl.ds(pl.multiple_of(dst_row + off, BF16_ROWS), size)],
                sem).start()

    @pl.when(n >= SEG_SPLIT)
    def _():
        for size in SEG_SIZES:
            if size >= SEG_SPLIT:
                piece(size)

    for size in SEG_SIZES:
        if size < SEG_SPLIT:
            piece(size)


def _wait_rows(total, sizes, src_ref, dst_ref, sem):
    for size in sizes:
        @pl.when((total & size) != 0)
        def _():
            pltpu.make_async_copy(src_ref.at[pl.ds(0, size)], dst_ref.at[pl.ds(0, size)], sem).wait()


def _moe_kernel(tg_ref, nv_ref, ssrc_ref, sdst_ref, slen_ref, tlo_ref, thi_ref,
                hl_ref, wg_ref, wu_ref, wd_ref, y_ref, xbuf, sem):
    del tg_ref
    i = pl.program_id(0)
    nv = nv_ref[0]

    def overlap(s, tile):
        base = tile * MOE_TM
        d = sdst_ref[s]
        lo = jnp.maximum(d, base)
        hi = jnp.minimum(d + slen_ref[s], base + MOE_TM)
        return jnp.maximum(hi - lo, 0), lo

    def start_tile(tile, slot):
        def body(s, carry):
            n, lo = overlap(s, tile)
            _start_pieces(n, ssrc_ref[s] + (lo - sdst_ref[s]), lo - tile * MOE_TM,
                          hl_ref, xbuf.at[slot], sem.at[slot])
            return carry

        lax.fori_loop(tlo_ref[tile], thi_ref[tile], body, 0)

    def wait_tile(tile, slot):
        total = lax.fori_loop(tlo_ref[tile], thi_ref[tile], lambda s, acc: acc + overlap(s, tile)[0], 0)
        _wait_rows(total, SEG_SIZES, hl_ref, xbuf.at[slot], sem.at[slot])

    @pl.when(i == 0)
    def _():
        xbuf[...] = jnp.zeros_like(xbuf)
        start_tile(0, 0)

    @pl.when(i + 1 < nv)
    def _():
        start_tile(i + 1, (i + 1) % 2)

    @pl.when(i >= nv)
    def _():
        y_ref[...] = jnp.zeros_like(y_ref)

    @pl.when(i < nv)
    def _():
        slot = i % 2
        wait_tile(i, slot)
        x = xbuf[slot]
        hg = _dot(x, wg_ref[0])
        hu = _dot(x, wu_ref[0])
        hid = ((hg * _sigmoid(hg)) * hu).astype(BF16)
        y_ref[...] = _dot(hid, wd_ref[0]).astype(BF16)


def _moe(cfg, hl, plan, w_gate, w_up, w_down):
    nmt = cfg.moe_tiles

    def w_map(i, tg, nv, *_):
        return (tg[jnp.minimum(i, nv[0] - 1)], 0, 0)

    w_gate = w_gate.reshape(N_EXPERTS, D_MODEL, D_FF)
    w_up = w_up.reshape(N_EXPERTS, D_MODEL, D_FF)
    w_down = w_down.reshape(N_EXPERTS, D_FF, D_MODEL)

    return pl.pallas_call(
        _moe_kernel,
        grid_spec=pltpu.PrefetchScalarGridSpec(
            num_scalar_prefetch=7,
            grid=(nmt,),
            in_specs=[
                pl.BlockSpec(memory_space=pl.ANY),
                pl.BlockSpec((1, D_MODEL, D_FF), w_map),
                pl.BlockSpec((1, D_MODEL, D_FF), w_map),
                pl.BlockSpec((1, D_FF, D_MODEL), w_map),
            ],
            out_specs=pl.BlockSpec((MOE_TM, D_MODEL), lambda i, *_: (i, 0)),
            scratch_shapes=[pltpu.VMEM((2, MOE_TM, D_MODEL), BF16), pltpu.SemaphoreType.DMA((2,))],
        ),
        out_shape=jax.ShapeDtypeStruct((nmt * MOE_TM, D_MODEL), BF16),
        compiler_params=_cparams(("arbitrary",)),
        name="moe",
    )(plan.tile_expert, plan.n_valid, plan.seg_tiled, plan.seg_sorted, plan.seg_len, plan.tile_lo, plan.tile_hi,
      hl, w_gate, w_up, w_down)


def _combine_kernel(stiled_ref, ssorted_ref, slen_ref, xmc_ref, xms_ref, ys_ref, lp_ref, mod_ref,
                    oc_ref, os_ref, ybuf, sem, *, n_tiles, n_ctx_tiles):
    t = pl.program_id(0)

    def start_tile(tile, slot):
        def body(e, carry):
            s = e * n_tiles + tile
            _start_pieces(slen_ref[s], ssorted_ref[s], stiled_ref[s] - tile * LT,
                          ys_ref, ybuf.at[slot], sem.at[slot])
            return carry

        lax.fori_loop(0, N_EXPERTS, body, 0)

    def wait_tile(tile, slot):
        total = lax.fori_loop(0, N_EXPERTS, lambda e, acc: acc + slen_ref[e * n_tiles + tile], 0)
        _wait_rows(total, (2 * SEG_SIZES[0],) + SEG_SIZES, ys_ref, ybuf.at[slot], sem.at[slot])

    @pl.when(t == 0)
    def _():
        ybuf[...] = jnp.zeros_like(ybuf)
        start_tile(0, 0)

    @pl.when(t + 1 < n_tiles)
    def _():
        start_tile(t + 1, (t + 1) % 2)

    slot = t % 2
    wait_tile(t, slot)
    lp = lp_ref[...]
    pos_a = (lp[:, 0:1] * 32.0 + lp[:, 1:2]).astype(jnp.int32)
    pos_b = (lp[:, 2:3] * 32.0 + lp[:, 3:4]).astype(jnp.int32)
    gate_a = lp[:, 4:5] + lp[:, 5:6] + lp[:, 6:7]
    gate_b = lp[:, 7:8] + lp[:, 8:9] + lp[:, 9:10]
    cols = lax.broadcasted_iota(jnp.int32, (TM, LT), 1)
    mix = jnp.where(cols == pos_a, gate_a, jnp.where(cols == pos_b, gate_b, 0.0)).astype(BF16)
    y = _dot(mix, ybuf[slot])
    g2 = mod_ref[0, 0][5:6]
    is_ctx = t < n_ctx_tiles
    _store_pair(is_ctx, oc_ref, os_ref, _load_pair(is_ctx, xmc_ref, xms_ref) + g2 * y)


def _combine(cfg, l, xm_pair, ys, lp, mod, plan):
    nt = cfg.n_tiles
    return pl.pallas_call(
        functools.partial(_combine_kernel, n_tiles=nt, n_ctx_tiles=cfg.n_ctx_tiles),
        grid_spec=pltpu.PrefetchScalarGridSpec(
            num_scalar_prefetch=3,
            grid=(nt,),
            in_specs=_pair_specs(cfg, D_MODEL) + [
                pl.BlockSpec(memory_space=pl.ANY),
                pl.BlockSpec((TM, LANES), lambda i, *_: (i, 0)),
                pl.BlockSpec((1, 1, 6, D_MODEL), lambda i, *_: (l, _mod_row(cfg, i), 0, 0)),
            ],
            out_specs=_pair_specs(cfg, D_MODEL),
            scratch_shapes=[pltpu.VMEM((2, LT, D_MODEL), BF16), pltpu.SemaphoreType.DMA((2,))],
        ),
        out_shape=_pair_shapes(cfg, D_MODEL, F32),
        compiler_params=_cparams(("arbitrary",)),
        name="combine",
    )(plan.seg_tiled, plan.seg_sorted, plan.seg_len, *xm_pair, ys, lp, mod)


class Plan(NamedTuple):
    seg_tiled: jax.Array
    seg_sorted: jax.Array
    seg_len: jax.Array
    tile_expert: jax.Array
    n_valid: jax.Array
    tile_lo: jax.Array
    tile_hi: jax.Array


def _plan(cfg, counts):
    nt = cfg.n_tiles
    m = (counts + (BF16_ROWS - 1)) // BF16_ROWS * BF16_ROWS
    local_off = jnp.cumsum(m, axis=1) - m
    within_expert = jnp.cumsum(m, axis=0) - m
    expert_rows = jnp.sum(m, axis=0)
    expert_tiles = (expert_rows + (MOE_TM - 1)) // MOE_TM
    tiles_end = jnp.cumsum(expert_tiles)
    expert_start = (tiles_end - expert_tiles) * MOE_TM
    seg_sorted = (expert_start[None, :] + within_expert).T.reshape(-1).astype(jnp.int32)
    seg_tiled = (jnp.arange(nt, dtype=jnp.int32)[:, None] * LT + local_off).T.reshape(-1).astype(jnp.int32)
    seg_len = m.T.reshape(-1).astype(jnp.int32)
    n_valid = tiles_end[-1:].astype(jnp.int32)
    tile_ids = jnp.arange(cfg.moe_tiles, dtype=jnp.int32)
    tile_expert = jnp.minimum(jnp.sum(tile_ids[:, None] >= tiles_end[None, :], axis=1),
                              N_EXPERTS - 1).astype(jnp.int32)
    tile_base = tile_ids * MOE_TM
    seg_end = seg_sorted + seg_len
    tile_lo = jnp.sum(seg_end[None, :] <= tile_base[:, None], axis=1).astype(jnp.int32)
    tile_hi = jnp.sum(seg_sorted[None, :] < tile_base[:, None] + MOE_TM, axis=1).astype(jnp.int32)
    return Plan(seg_tiled, seg_sorted, seg_len, tile_expert, n_valid, tile_lo, tile_hi)


def _rope_tables(n):
    pos = jnp.arange(n)
    rc = jnp.stack([(pos // GRID_W).astype(F32), (pos % GRID_W).astype(F32)], axis=1)
    freqs = ROPE_THETA ** (-jnp.arange(AXIS_PAIRS, dtype=F32) / AXIS_PAIRS)
    ang = rc[:, :, None] * freqs[None, None, :]
    cos = jnp.cos(ang)[:, :, None, :]
    sin = jnp.sin(ang)[:, :, None, :] * jnp.array([-1.0, 1.0], F32)[None, None, :, None]
    cos = jnp.broadcast_to(cos, (n, 2, 2, AXIS_PAIRS)).reshape(n, HEAD_DIM)
    sin = jnp.broadcast_to(sin, (n, 2, 2, AXIS_PAIRS)).reshape(n, HEAD_DIM)
    reps = LANES // HEAD_DIM
    cos = jnp.concatenate([jnp.ones((TM, HEAD_DIM), F32), cos], axis=0)
    sin = jnp.concatenate([jnp.zeros((TM, HEAD_DIM), F32), sin], axis=0)
    return jnp.tile(cos, (1, reps)), jnp.tile(sin, (1, reps))


def _block_diag_mean(width):
    r = jnp.arange(width)
    return jnp.where((r[:, None] // HEAD_DIM) == (r[None, :] // HEAD_DIM), 1.0 / HEAD_DIM, 0.0).astype(BF16)


def _forward(cfg, x_prompt, x_sample, cache_k, cache_v, c, c_ctx, w_mod, b_mod, norm1_g, norm2_g,
             w_in, q_norm_g, k_norm_g, conv_dw_w, conv_dw_b, conv_ln_g, conv_ln_b, w_out,
             w_router, b_router, w_gate, w_up, w_down):
    depth = cfg.depth
    assert cfg.ctx_seq == CONV_ROWS and cfg.t_ctx % TM == 0 and cfg.smp_seq % TM == 0
    assert cfg.t_ctx % cfg.smp_seq == 0 and cfg.smp_batch + 1 <= 8

    x = (x_prompt.reshape(cfg.t_ctx, D_MODEL), x_sample.reshape(-1, D_MODEL))
    cvec = jnp.zeros((8, D_MODEL), F32).at[0].set(c_ctx).at[1:1 + cfg.smp_batch].set(c)
    mod = _modulation(cvec, w_mod, b_mod)

    n1 = norm1_g.reshape(depth, 1, D_MODEL)
    n2 = norm2_g.reshape(depth, 1, D_MODEL)
    gain = jnp.concatenate([jnp.tile(q_norm_g, (1, N_Q_HEADS)), jnp.tile(k_norm_g, (1, N_KV_HEADS))],
                           axis=1).reshape(depth, 1, QK_W)
    expert_w = (w_gate.reshape(-1, D_FF), w_up.reshape(-1, D_FF), w_down.reshape(-1, D_MODEL))
    taps = _conv_tap_rows(conv_dw_w)
    dw_b = conv_dw_b.reshape(depth, 1, CONV_CH)
    ln_g = conv_ln_g.reshape(depth, 1, CONV_CH)
    ln_b = conv_ln_b.reshape(depth, 1, CONV_CH)
    wr = jnp.pad(w_router, ((0, 0), (0, LANES - N_EXPERTS)))
    br = b_router.reshape(N_EXPERTS, 1)
    cos, sin = _rope_tables(cfg.smp_seq)
    bdq = _block_diag_mean(Q_W // 2)
    bdk = _block_diag_mean(KV_W)
    ck = cache_k.reshape(cfg.smp_batch, depth, cfg.past, KV_W)
    cv = cache_v.reshape(cfg.smp_batch, depth, cfg.past, KV_W)

    new_k = jnp.zeros((cfg.ctx_batch, depth, cfg.ctx_seq, KV_W), F32)
    new_v = jnp.zeros((cfg.ctx_batch, depth, cfg.ctx_seq, KV_W), F32)
    for l in range(depth):
        q, k, v, z, new_k, new_v = _inproj(cfg, l, x, mod, n1, w_in, gain, bdq, bdk, cos, sin, new_k, new_v)
        a_pair, c_pair, expert_bf = _mixer(cfg, l, q, k, v, ck, cv, z, taps, dw_b, ln_g, ln_b, expert_w)
        xmc, xms, hl, lp, cnt = _outproj(cfg, l, a_pair, c_pair, x, mod, n2, w_out, wr, br)
        plan = _plan(cfg, cnt[:, :, 0])
        ys = _moe(cfg, hl.reshape(cfg.n_tiles * LT, D_MODEL), plan, *expert_bf)
        x = _combine(cfg, l, (xmc, xms), ys, lp, mod, plan)

    y_prompt = x[0].reshape(cfg.ctx_batch, cfg.ctx_seq, D_MODEL)
    y_sample = x[1].reshape(cfg.smp_batch, cfg.smp_seq, D_MODEL)
    cache_shape = (cfg.ctx_batch, depth, cfg.ctx_seq, N_KV_HEADS, HEAD_DIM)
    return y_prompt, y_sample, new_k.reshape(cache_shape), new_v.reshape(cache_shape)


def kernel(x_prompt, x_sample, cache_k, cache_v, c, c_ctx, w_mod, b_mod, norm1_g, norm2_g, w_in, q_norm_g,
           k_norm_g, conv_dw_w, conv_dw_b, conv_ln_g, conv_ln_b, w_out, w_router, b_router, w_gate, w_up, w_down):
    cfg = Cfg(ctx_batch=x_prompt.shape[0], ctx_seq=x_prompt.shape[1], smp_batch=x_sample.shape[0],
              smp_seq=x_sample.shape[1], past=cache_k.shape[2], depth=w_mod.shape[0])
    return _forward(cfg, x_prompt, x_sample, cache_k, cache_v, c, c_ctx, w_mod, b_mod, norm1_g, norm2_g,
                    w_in, q_norm_g, k_norm_g, conv_dw_w, conv_dw_b, conv_ln_g, conv_ln_b, w_out,
                    w_router, b_router, w_gate, w_up, w_down)
```

```python
import functools
import math
from typing import NamedTuple

import jax
import jax.numpy as jnp
from jax import lax
from jax.experimental import pallas as pl
from jax.experimental.pallas import tpu as pltpu

F32 = jnp.float32
BF16 = jnp.bfloat16

D_MODEL = 1024
HEAD_DIM = 64
N_Q_HEADS = 8
N_KV_HEADS = 2
Q_PER_KV = N_Q_HEADS // N_KV_HEADS
Q_W = N_Q_HEADS * HEAD_DIM
KV_W = N_KV_HEADS * HEAD_DIM
QK_W = Q_W + KV_W
CONV_CH = D_MODEL // 2
CONV_K = 31
CONV_PAD = CONV_K // 2
IN_COLS = Q_W + 2 * KV_W + 2 * CONV_CH
GRID_W = 64
AXIS_PAIRS = HEAD_DIM // 4
ROPE_THETA = 10000.0
N_EXPERTS = 16
N_GROUPS = 4
EXPERTS_PER_GROUP = N_EXPERTS // N_GROUPS
D_FF = 512
EPS = 1e-6

LANES = 128
BF16_ROWS = 16
TM = 512
TOP_K = 2
LT = 1280
MOE_TM = 512
MOE_PAIR = 2
CONV_ROWS = 256
HALO = 16
CONV_INNER = CONV_ROWS + 16
PAD_ROWS = 24 + CONV_INNER + 8
SEG_SIZES = (512, 256, 128, 64, 32, 16)
SEG_SPLIT = 128
Q_SCALE = (1.0 / math.sqrt(HEAD_DIM)) * math.log2(math.e)
VMEM_LIMIT = 48 * 1024 * 1024
MIXER_VMEM_LIMIT = 56 * 1024 * 1024


class Cfg(NamedTuple):
    ctx_batch: int
    ctx_seq: int
    smp_batch: int
    smp_seq: int
    past: int
    depth: int

    @property
    def t_ctx(self):
        return self.ctx_batch * self.ctx_seq

    @property
    def t_all(self):
        return self.t_ctx + self.smp_batch * self.smp_seq

    @property
    def n_tiles(self):
        return self.t_all // TM

    @property
    def n_ctx_tiles(self):
        return self.t_ctx // TM

    @property
    def smp_tiles(self):
        return self.smp_seq // TM

    @property
    def moe_tiles(self):
        rows = TOP_K * self.t_all + self.n_tiles * N_EXPERTS * (BF16_ROWS - 1) + N_EXPERTS * (MOE_TM - 1)
        step_rows = MOE_PAIR * MOE_TM
        return -(-rows // step_rows) * MOE_PAIR


def _mod_row(cfg, i):
    return jnp.where(i < cfg.n_ctx_tiles, 0, 1 + (i - cfg.n_ctx_tiles) // cfg.smp_tiles)


def _pair_specs(cfg, width, tile_of=lambda i: i):
    nct = cfg.n_ctx_tiles
    return [pl.BlockSpec((TM, width), lambda i, *_: (jnp.minimum(tile_of(i), nct - 1), 0)),
            pl.BlockSpec((TM, width), lambda i, *_: (jnp.maximum(tile_of(i) - nct, 0), 0))]


def _pair_shapes(cfg, width, dtype):
    return [jax.ShapeDtypeStruct((cfg.t_ctx, width), dtype),
            jax.ShapeDtypeStruct((cfg.t_all - cfg.t_ctx, width), dtype)]


def _load_pair(is_ctx, c_ref, s_ref):
    return jnp.where(is_ctx, c_ref[...], s_ref[...])


def _store_pair(is_ctx, c_ref, s_ref, val):
    @pl.when(is_ctx)
    def _():
        c_ref[...] = val

    @pl.when(jnp.logical_not(is_ctx))
    def _():
        s_ref[...] = val


def _cparams(sem, vmem=VMEM_LIMIT):
    return pltpu.CompilerParams(dimension_semantics=sem, vmem_limit_bytes=vmem)


def _dot(a, b):
    return jnp.dot(a, b, preferred_element_type=F32)


def _dot_nt(a, b):
    return lax.dot_general(a, b, (((1,), (1,)), ((), ())), preferred_element_type=F32)


def _sigmoid(x):
    return 1.0 / (1.0 + jnp.exp(-x))


def _mod_kernel(c_ref, w_ref, b_ref, o_ref):
    c = c_ref[...]
    a = (c * _sigmoid(c)).astype(BF16)
    o_ref[0] = _dot(a, w_ref[0].astype(BF16)) + b_ref[0]


def _modulation(cvec, w_mod, b_mod):
    depth = w_mod.shape[0]
    out = pl.pallas_call(
        _mod_kernel,
        grid=(depth, 6),
        in_specs=[
            pl.BlockSpec((8, D_MODEL), lambda l, j: (0, 0)),
            pl.BlockSpec((1, D_MODEL, D_MODEL), lambda l, j: (l, 0, j)),
            pl.BlockSpec((1, 1, D_MODEL), lambda l, j: (l, 0, j)),
        ],
        out_specs=pl.BlockSpec((1, 8, D_MODEL), lambda l, j: (l, 0, j)),
        out_shape=jax.ShapeDtypeStruct((depth, 8, 6 * D_MODEL), F32),
        compiler_params=_cparams(("arbitrary", "arbitrary")),
        name="modulation",
    )(cvec, w_mod, b_mod.reshape(depth, 1, 6 * D_MODEL))
    return out.reshape(depth, 8, 6, D_MODEL)


def _inproj_kernel(xc_ref, xs_ref, mod_ref, n1_ref, w_ref, gain_ref, bdq_ref, bdk_ref, cos_ref, sin_ref,
                   kc_in, vc_in, q_ref, k_ref, v_ref, z_ref, kc_ref, vc_ref, w_bf, proj_prev,
                   *, n_tiles, n_ctx_tiles, ctx_seq):
    del kc_in, vc_in
    step = pl.program_id(0)

    @pl.when(step == 0)
    def _():
        w_bf[...] = w_ref[0].astype(BF16)
        proj_prev[...] = jnp.zeros_like(proj_prev)

    x = _load_pair(jnp.minimum(step, n_tiles - 1) < n_ctx_tiles, xc_ref, xs_ref)
    mod = mod_ref[0, 0]
    sh1, sc1 = mod[0:1], mod[1:2]
    ms = jnp.mean(x * x, axis=-1, keepdims=True)
    h = (x * lax.rsqrt(ms + EPS) * n1_ref[0]) * (1.0 + sc1) + sh1
    proj_new = _dot(h.astype(BF16), w_bf[...])

    qk = proj_prev[:, :QK_W]
    sq = (qk * qk).astype(BF16)
    half = Q_W // 2
    msq = jnp.concatenate([_dot(sq[:, :half], bdq_ref[...]),
                           _dot(sq[:, half:Q_W], bdq_ref[...]),
                           _dot(sq[:, Q_W:], bdk_ref[...])], axis=1)
    qkn = qk * lax.rsqrt(msq + EPS) * gain_ref[0]
    v = proj_prev[:, QK_W:QK_W + KV_W]
    v_ref[...] = v
    a = proj_prev[:, QK_W + KV_W:QK_W + KV_W + CONV_CH]
    b = proj_prev[:, QK_W + KV_W + CONV_CH:]
    z_ref[...] = a * _sigmoid(b)
    cos = cos_ref[...]
    sin = sin_ref[...]
    first = (lax.broadcasted_iota(jnp.int32, (TM, LANES), 1) % 32) < 16
    for c in range(QK_W // LANES):
        blk = qkn[:, c * LANES:(c + 1) * LANES]
        partner = jnp.where(first, pltpu.roll(blk, LANES - 16, 1), pltpu.roll(blk, 16, 1))
        rot = blk * cos + partner * sin
        if c < Q_W // LANES:
            q_ref[:, c * LANES:(c + 1) * LANES] = (rot * Q_SCALE).astype(BF16)
        else:
            k = rot
            k_ref[...] = k

    proj_prev[...] = proj_new

    @pl.when(step - 1 < n_ctx_tiles)
    def _():
        for s in range(TM // ctx_seq):
            kc_ref[s, 0] = k[s * ctx_seq:(s + 1) * ctx_seq]
            vc_ref[s, 0] = v[s * ctx_seq:(s + 1) * ctx_seq]


def _inproj(cfg, l, x_pair, mod, n1, w_in, gain, bdq, bdk, cos, sin, kc, vc):
    t = cfg.t_all
    nt, nct, st = cfg.n_tiles, cfg.n_ctx_tiles, cfg.smp_tiles
    tile_a = lambda i: jnp.minimum(i, nt - 1)
    tile_b = lambda i: jnp.maximum(i - 1, 0)
    rope_map = lambda i: (jnp.where(tile_b(i) < nct, 0, 1 + (tile_b(i) - nct) % st), 0)
    seqs = TM // cfg.ctx_seq
    cache_spec = lambda: pl.BlockSpec((seqs, 1, cfg.ctx_seq, KV_W),
                                      lambda i: (jnp.minimum(tile_b(i), nct - 1), l, 0, 0))
    return pl.pallas_call(
        functools.partial(_inproj_kernel, n_tiles=nt, n_ctx_tiles=nct, ctx_seq=cfg.ctx_seq),
        grid=(nt + 1,),
        in_specs=_pair_specs(cfg, D_MODEL, tile_a) + [
            pl.BlockSpec((1, 1, 6, D_MODEL), lambda i: (l, _mod_row(cfg, tile_a(i)), 0, 0)),
            pl.BlockSpec((1, 1, D_MODEL), lambda i: (l, 0, 0)),
            pl.BlockSpec((1, D_MODEL, IN_COLS), lambda i: (l, 0, 0)),
            pl.BlockSpec((1, 1, QK_W), lambda i: (l, 0, 0)),
            pl.BlockSpec((Q_W // 2, Q_W // 2), lambda i: (0, 0)),
            pl.BlockSpec((KV_W, KV_W), lambda i: (0, 0)),
            pl.BlockSpec((TM, LANES), rope_map),
            pl.BlockSpec((TM, LANES), rope_map),
            pl.BlockSpec(memory_space=pl.ANY),
            pl.BlockSpec(memory_space=pl.ANY),
        ],
        out_specs=[
            pl.BlockSpec((TM, Q_W), lambda i: (tile_b(i), 0)),
            pl.BlockSpec((TM, KV_W), lambda i: (tile_b(i), 0)),
            pl.BlockSpec((TM, KV_W), lambda i: (tile_b(i), 0)),
            pl.BlockSpec((TM, CONV_CH), lambda i: (tile_b(i), 0)),
            cache_spec(), cache_spec(),
        ],
        out_shape=[
            jax.ShapeDtypeStruct((t, Q_W), BF16),
            jax.ShapeDtypeStruct((t, KV_W), F32),
            jax.ShapeDtypeStruct((t, KV_W), F32),
            jax.ShapeDtypeStruct((t, CONV_CH), F32),
            jax.ShapeDtypeStruct(kc.shape, F32),
            jax.ShapeDtypeStruct(vc.shape, F32),
        ],
        scratch_shapes=[pltpu.VMEM((D_MODEL, IN_COLS), BF16), pltpu.VMEM((TM, IN_COLS), F32)],
        input_output_aliases={10: 4, 11: 5},
        compiler_params=_cparams(("arbitrary",)),
        name="inproj",
    )(*x_pair, mod, n1, w_in, gain, bdq, bdk, cos, sin, kc, vc)


def _mixer_kernel(*refs, rq, n_past, seq_blocks):
    if n_past:
        q_ref, k_ref, v_ref, ck_ref, cv_ref = refs[:5]
        refs = refs[5:]
    else:
        q_ref, k_ref, v_ref = refs[:3]
        refs = refs[3:]
    z_ref, zp_ref, zn_ref, w_ref, b_ref, g_ref, beta_ref = refs[:7]
    refs = refs[7:]
    if n_past:
        for src, dst in zip(refs[0:3], refs[5:8]):
            dst[...] = src[...].astype(BF16)
        refs = refs[3:5] + refs[8:]
    o_ref, co_ref, krep, vrep, pad, taps = refs
    head_w = Q_PER_KV * HEAD_DIM
    j = pl.program_id(1)

    @pl.when(jnp.logical_and(pl.program_id(0) == 0, j == 0))
    def _():
        diag = (lax.broadcasted_iota(jnp.int32, (LANES, LANES), 0)
                == lax.broadcasted_iota(jnp.int32, (LANES, LANES), 1))
        for c in range(CONV_CH // LANES):
            for p in range(4):
                for q in range(4):
                    for half in range(2):
                        tap = w_ref[0, 8 * q + 2 * p + half:8 * q + 2 * p + half + 1, c * LANES:(c + 1) * LANES]
                        taps[c, p, q * LANES:(q + 1) * LANES, half * LANES:(half + 1) * LANES] = (
                            jnp.where(diag, tap, 0.0).astype(BF16))

    @pl.when(j == 0)
    def _():
        src = lax.broadcasted_iota(jnp.int32, (KV_W, head_w), 0)
        dst = lax.broadcasted_iota(jnp.int32, (KV_W, head_w), 1) % HEAD_DIM
        n_new = k_ref.shape[0]
        for h in range(N_KV_HEADS):
            rep = jnp.where(src == dst + h * HEAD_DIM, 1.0, 0.0).astype(BF16)
            if n_past:
                krep[h, :n_past, :] = _dot(ck_ref[0, 0].astype(BF16), rep).astype(BF16)
                vrep[h, :n_past, :] = _dot(cv_ref[0, 0].astype(BF16), rep).astype(BF16)
            krep[h, n_past:n_past + n_new, :] = _dot(k_ref[...].astype(BF16), rep).astype(BF16)
            vrep[h, n_past:n_past + n_new, :] = _dot(v_ref[...].astype(BF16), rep).astype(BF16)

    end = HALO + CONV_ROWS
    if seq_blocks > 1:
        pad[0:HALO, :] = jnp.where(j > 0, zp_ref[...], 0.0)
        pad[end:end + HALO, :] = jnp.where(j < seq_blocks - 1, zn_ref[...], 0.0)
    else:
        pad[0:HALO, :] = jnp.zeros((HALO, CONV_CH), F32)
        pad[end:end + HALO, :] = jnp.zeros((HALO, CONV_CH), F32)
    pad[end + HALO:, :] = jnp.zeros((PAD_ROWS - end - HALO, CONV_CH), F32)
    pad[HALO:end, :] = z_ref[...]

    lane_head = lax.broadcasted_iota(jnp.int32, (rq, head_w), 1) // HEAD_DIM
    for h in range(N_KV_HEADS):
        qh = q_ref[:, h * head_w:(h + 1) * head_w]
        kh = krep[h]
        vh = vrep[h]
        acc = jnp.zeros((rq, head_w), F32)
        for g in range(Q_PER_KV):
            mask = lane_head == g
            qm = jnp.where(mask, qh, jnp.zeros_like(qh))
            s = _dot_nt(qm, kh)
            m = jnp.max(s, axis=-1, keepdims=True)
            p = jnp.exp2(s - m)
            denom = jnp.sum(p, axis=-1, keepdims=True)
            o = _dot(p.astype(BF16), vh)
            acc = jnp.where(mask, o * (1.0 / denom), acc)
        o_ref[:, h * head_w:(h + 1) * head_w] = acc.astype(BF16)
    conv_cols = [_conv_cols(pad, taps, c) for c in range(CONV_CH // LANES)]
    co_ref[...] = _conv_finish(conv_cols, b_ref, g_ref, beta_ref)


def _mixer(cfg, l, q, k, v, cache_k, cache_v, z, taps, dw_b, ln_g, ln_b, expert_w):
    head_w = Q_PER_KV * HEAD_DIM
    rq = CONV_ROWS
    hb = CONV_ROWS // HALO
    last = cfg.t_all // HALO - 1
    vec = lambda: pl.BlockSpec((1, 1, CONV_CH), lambda b, j: (l, 0, 0))

    def conv_specs(row_block):
        return [
            pl.BlockSpec((CONV_ROWS, CONV_CH), lambda b, j: (row_block(b, j), 0)),
            pl.BlockSpec((HALO, CONV_CH), lambda b, j: (jnp.maximum(row_block(b, j) * hb - 1, 0), 0)),
            pl.BlockSpec((HALO, CONV_CH), lambda b, j: (jnp.minimum(row_block(b, j) * hb + hb, last), 0)),
            pl.BlockSpec((1, 32, CONV_CH), lambda b, j: (l, 0, 0)),
            vec(), vec(), vec(),
        ]

    def scratch(sk):
        return [pltpu.VMEM((N_KV_HEADS, sk, head_w), BF16), pltpu.VMEM((N_KV_HEADS, sk, head_w), BF16),
                pltpu.VMEM((PAD_ROWS, CONV_CH), F32),
                pltpu.VMEM((CONV_CH // LANES, 4, 4 * LANES, 2 * LANES), BF16)]

    sc = cfg.ctx_seq
    a_ctx, c_ctx = pl.pallas_call(
        functools.partial(_mixer_kernel, rq=rq, n_past=0, seq_blocks=1),
        grid=(cfg.ctx_batch, 1),
        in_specs=[
            pl.BlockSpec((sc, Q_W), lambda b, j: (b, 0)),
            pl.BlockSpec((sc, KV_W), lambda b, j: (b, 0)),
            pl.BlockSpec((sc, KV_W), lambda b, j: (b, 0)),
        ] + conv_specs(lambda b, j: b),
        out_specs=[pl.BlockSpec((sc, Q_W), lambda b, j: (b, 0)),
                   pl.BlockSpec((sc, CONV_CH), lambda b, j: (b, 0))],
        out_shape=[jax.ShapeDtypeStruct((cfg.t_ctx, Q_W), BF16),
                   jax.ShapeDtypeStruct((cfg.t_ctx, CONV_CH), BF16)],
        scratch_shapes=scratch(sc),
        compiler_params=_cparams(("arbitrary", "arbitrary")),
        name="mixer_ctx",
    )(q, k, v, z, z, z, taps, dw_b, ln_g, ln_b)
    ss = cfg.smp_seq
    sk = cfg.past + ss
    kv_base = cfg.t_ctx // ss
    q_base = cfg.t_ctx // rq
    nqb = ss // rq
    t_smp = cfg.smp_batch * ss
    steps = cfg.smp_batch * nqb
    w_rows = [w.shape[0] // (cfg.depth * steps) for w in expert_w]
    w_specs = lambda off: [pl.BlockSpec((r, w.shape[1]), lambda b, j: (off + b * nqb + j, 0))
                           for r, w in zip(w_rows, expert_w)]
    a_smp, c_smp, *expert_bf = pl.pallas_call(
        functools.partial(_mixer_kernel, rq=rq, n_past=cfg.past, seq_blocks=nqb),
        grid=(cfg.smp_batch, nqb),
        in_specs=[
            pl.BlockSpec((rq, Q_W), lambda b, j: (q_base + b * nqb + j, 0)),
            pl.BlockSpec((ss, KV_W), lambda b, j: (kv_base + b, 0)),
            pl.BlockSpec((ss, KV_W), lambda b, j: (kv_base + b, 0)),
            pl.BlockSpec((1, 1, cfg.past, KV_W), lambda b, j: (b, l, 0, 0)),
            pl.BlockSpec((1, 1, cfg.past, KV_W), lambda b, j: (b, l, 0, 0)),
        ] + conv_specs(lambda b, j: q_base + b * nqb + j) + w_specs(l * steps),
        out_specs=[pl.BlockSpec((rq, Q_W), lambda b, j: (b * nqb + j, 0)),
                   pl.BlockSpec((rq, CONV_CH), lambda b, j: (b * nqb + j, 0))] + w_specs(0),
        out_shape=[jax.ShapeDtypeStruct((t_smp, Q_W), BF16),
                   jax.ShapeDtypeStruct((t_smp, CONV_CH), BF16)]
        + [jax.ShapeDtypeStruct((r * steps, w.shape[1]), BF16) for r, w in zip(w_rows, expert_w)],
        scratch_shapes=scratch(sk),
        compiler_params=_cparams(("arbitrary", "arbitrary"), MIXER_VMEM_LIMIT),
        name="mixer_smp",
    )(q, k, v, cache_k, cache_v, z, z, z, taps, dw_b, ln_g, ln_b, *expert_w)
    return (a_ctx, a_smp), (c_ctx, c_smp), expert_bf


def _conv_cols(pad, taps, c):
    cols = slice(c * LANES, (c + 1) * LANES)
    lhs = jnp.concatenate([pad[8 * q:8 * q + CONV_INNER, cols].astype(BF16) for q in range(4)], axis=1)
    acc = None
    for p in range(4):
        pair = _dot(lhs, taps[c, p])
        for half in range(2):
            o = 2 * p + half
            shifted = pair[o:o + CONV_ROWS, half * LANES:(half + 1) * LANES]
            acc = shifted if acc is None else acc + shifted
    return acc


def _conv_tap_rows(conv_dw_w):
    return jnp.pad(conv_dw_w, ((0, 0), (HALO - CONV_PAD, 32 - CONV_K - (HALO - CONV_PAD)), (0, 0)))


def _conv_finish(blocks, b_ref, g_ref, beta_ref):
    y = jnp.concatenate(blocks, axis=1) + b_ref[0]
    mu = jnp.mean(y, axis=-1, keepdims=True)
    yc = y - mu
    var = jnp.mean(yc * yc, axis=-1, keepdims=True)
    yn = yc * lax.rsqrt(var + EPS) * g_ref[0] + beta_ref[0]
    return (yn * _sigmoid(yn)).astype(BF16)


def _split_bf16(x):
    hi = x.astype(BF16)
    lo = (x - hi.astype(F32)).astype(BF16)
    return hi, lo


def _outproj_kernel(ac_ref, as_ref, cc_ref, cs_ref, xc_ref, xs_ref, mod_ref, n2_ref, wo_ref, wr_ref, br_ref,
                    xmc_ref, xms_ref, hl_ref, lp_ref, cnt_ref, wo_bf, h_prev, lt_prev, *, n_tiles, n_ctx_tiles):
    step = pl.program_id(0)

    @pl.when(step == 0)
    def _():
        wo_bf[...] = wo_ref[0].astype(BF16)
        h_prev[...] = jnp.zeros_like(h_prev)
        lt_prev[...] = jnp.zeros_like(lt_prev)

    is_ctx = jnp.minimum(step, n_tiles - 1) < n_ctx_tiles
    ac = jnp.concatenate([_load_pair(is_ctx, ac_ref, as_ref), _load_pair(is_ctx, cc_ref, cs_ref)], axis=1)
    y = _dot(ac, wo_bf[...])
    mod = mod_ref[0, 0]
    g1, sh2, sc2 = mod[2:3], mod[3:4], mod[4:5]
    xm = _load_pair(is_ctx, xc_ref, xs_ref) + g1 * y
    ms = jnp.mean(xm * xm, axis=-1, keepdims=True)
    h2 = (xm * lax.rsqrt(ms + EPS) * n2_ref[0]) * (1.0 + sc2) + sh2
    h_new, h_lo = _split_bf16(h2)
    w_hi, w_lo = _split_bf16(wr_ref[...])
    logits = _dot(h_new, w_hi) + (_dot(h_new, w_lo) + _dot(h_lo, w_hi))
    lt_new = logits.T[0:N_EXPERTS, :]

    h_hi = h_prev[...]
    s_all = _sigmoid(lt_prev[...])
    sb_all = s_all + br_ref[...]

    rows8 = 2 * EXPERTS_PER_GROUP
    row = lax.broadcasted_iota(jnp.int32, (rows8, TM), 0)
    member = row % EXPERTS_PER_GROUP
    second = row >= EXPERTS_PER_GROUP

    def partner(x, k):
        return jnp.where(member + k < EXPERTS_PER_GROUP,
                         pltpu.roll(x, rows8 - k, 0), pltpu.roll(x, EXPERTS_PER_GROUP - k, 0))

    def other_group(x):
        return pltpu.roll(x, EXPERTS_PER_GROUP, 0)

    def group_sum(x):
        return x + partner(x, 1) + (partner(x, 2) + partner(x, 3))

    def ahead(other, mine, other_first):
        return jnp.where((other >= mine) if other_first else (other > mine), 1.0, 0.0)

    sel, gscore = [], []
    for hv in range(2):
        sb8 = sb_all[hv * rows8:(hv + 1) * rows8]
        rank = jnp.zeros((rows8, TM), F32)
        for k in range(1, EXPERTS_PER_GROUP):
            pk = partner(sb8, k)
            rank = rank + jnp.where(member + k >= EXPERTS_PER_GROUP, ahead(pk, sb8, True), ahead(pk, sb8, False))
        sel.append(rank < 1.5)
        gscore.append(group_sum(jnp.where(sel[hv], sb8, 0.0)))
    gate8, used8 = [], []
    for hv in range(2):
        mine, far = gscore[hv], gscore[1 - hv]
        near = other_group(mine)
        rank = jnp.where(second, ahead(near, mine, True), ahead(near, mine, False))
        rank = rank + ahead(far, mine, hv == 1) + ahead(other_group(far), mine, hv == 1)
        chosen = rank < 0.5
        used8.append(jnp.where(chosen, jnp.where(sel[hv], 1.0, 0.0), 0.0))
        gate8.append(used8[hv] * s_all[hv * rows8:(hv + 1) * rows8])
    total = group_sum(gate8[0]) + group_sum(gate8[1])
    total = total + other_group(total)
    inv_total = 1.0 / total
    gate16 = jnp.concatenate([gate8[0] * inv_total, gate8[1] * inv_total], axis=0)
    used16 = jnp.concatenate(used8, axis=0)
    used = used16 > 0.5

    before = (lax.broadcasted_iota(jnp.int32, (TM, TM), 0) < lax.broadcasted_iota(jnp.int32, (TM, TM), 1))
    prior = _dot(used16.astype(BF16), jnp.where(before, 1.0, 0.0).astype(BF16))
    n_e = jnp.sum(used16, axis=-1, keepdims=True)
    m_e = jnp.floor((n_e + (BF16_ROWS - 1)) * (1.0 / BF16_ROWS)) * BF16_ROWS
    lower = (lax.broadcasted_iota(jnp.int32, (N_EXPERTS, N_EXPERTS), 0)
             > lax.broadcasted_iota(jnp.int32, (N_EXPERTS, N_EXPERTS), 1))
    start = _dot(jnp.where(lower, 1.0, 0.0).astype(BF16),
                 jnp.broadcast_to(m_e, (N_EXPERTS, LANES)).astype(BF16))[:, 0:1]
    pos16 = start + prior
    pos_a = jnp.min(jnp.where(used, pos16, float(LT)), axis=0, keepdims=True)
    pos_b = jnp.max(jnp.where(used, pos16, -1.0), axis=0, keepdims=True)
    gate_a = jnp.sum(jnp.where(used, jnp.where(pos16 == pos_a, gate16, 0.0), 0.0), axis=0, keepdims=True)
    gate_b = jnp.sum(jnp.where(used, jnp.where(pos16 == pos_b, gate16, 0.0), 0.0), axis=0, keepdims=True)
    rows = lax.broadcasted_iota(jnp.int32, (LT, TM), 0)
    perm_a = jnp.where(rows == pos_a.astype(jnp.int32), 1.0, 0.0).astype(BF16)
    perm_b = jnp.where(rows == pos_b.astype(jnp.int32), 1.0, 0.0).astype(BF16)

    hl_ref[0] = _dot(perm_a + perm_b, h_hi).astype(BF16)

    def split32(p):
        hi = jnp.floor(p * (1.0 / 32.0))
        return [hi, p - 32.0 * hi]

    def split3(g):
        hi = g.astype(BF16).astype(F32)
        mid = (g - hi).astype(BF16).astype(F32)
        lo = ((g - hi) - mid).astype(BF16).astype(F32)
        return [hi, mid, lo]

    parts = split32(pos_a) + split32(pos_b) + split3(gate_a) + split3(gate_b)
    lp_parts = jnp.concatenate(parts + [jnp.zeros((LANES - len(parts), TM), F32)], axis=0).astype(BF16)
    eye = jnp.where(lax.broadcasted_iota(jnp.int32, (TM, TM), 0) == lax.broadcasted_iota(jnp.int32, (TM, TM), 1),
                    1.0, 0.0).astype(BF16)
    lp_ref[...] = _dot_nt(eye, lp_parts)

    cnt_ref[0] = jnp.broadcast_to(n_e, (N_EXPERTS, LANES)).astype(jnp.int32)

    h_prev[...] = h_new
    lt_prev[...] = lt_new
    _store_pair(is_ctx, xmc_ref, xms_ref, xm)


def _outproj(cfg, l, a_pair, c_pair, x_pair, mod, n2, w_out, w_router, b_router):
    t, nt, nct = cfg.t_all, cfg.n_tiles, cfg.n_ctx_tiles
    tile_a = lambda i: jnp.minimum(i, nt - 1)
    tile_b = lambda i: jnp.maximum(i - 1, 0)
    return pl.pallas_call(
        functools.partial(_outproj_kernel, n_tiles=nt, n_ctx_tiles=nct),
        grid=(nt + 1,),
        in_specs=_pair_specs(cfg, Q_W, tile_a) + _pair_specs(cfg, CONV_CH, tile_a)
        + _pair_specs(cfg, D_MODEL, tile_a) + [
            pl.BlockSpec((1, 1, 6, D_MODEL), lambda i: (l, _mod_row(cfg, tile_a(i)), 0, 0)),
            pl.BlockSpec((1, 1, D_MODEL), lambda i: (l, 0, 0)),
            pl.BlockSpec((1, D_MODEL, D_MODEL), lambda i: (l, 0, 0)),
            pl.BlockSpec((D_MODEL, LANES), lambda i: (0, 0)),
            pl.BlockSpec((N_EXPERTS, 1), lambda i: (0, 0)),
        ],
        out_specs=_pair_specs(cfg, D_MODEL, tile_a) + [
            pl.BlockSpec((1, LT, D_MODEL), lambda i: (tile_b(i), 0, 0)),
            pl.BlockSpec((TM, LANES), lambda i: (tile_b(i), 0)),
            pl.BlockSpec((1, N_EXPERTS, LANES), lambda i: (tile_b(i), 0, 0)),
        ],
        out_shape=_pair_shapes(cfg, D_MODEL, F32) + [
            jax.ShapeDtypeStruct((nt, LT, D_MODEL), BF16),
            jax.ShapeDtypeStruct((t, LANES), F32),
            jax.ShapeDtypeStruct((nt, N_EXPERTS, LANES), jnp.int32),
        ],
        scratch_shapes=[pltpu.VMEM((D_MODEL, D_MODEL), BF16), pltpu.VMEM((TM, D_MODEL), BF16),
                        pltpu.VMEM((N_EXPERTS, TM), F32)],
        compiler_params=_cparams(("arbitrary",)),
        name="outproj",
    )(*a_pair, *c_pair, *x_pair, mod, n2, w_out, w_router, b_router)


def _start_pieces(n, src_row, dst_row, src_ref, dst_ref, sem):
    def piece(size):
        off = n & ~(2 * size - 1)

        @pl.when((n & size) != 0)
        def _():
            pltpu.make_async_copy(
                src_ref.at[pl.ds(pl.multiple_of(src_row + off, BF16_ROWS), size)],
                dst_ref.at[pl.ds(pl.multiple_of(dst_row + off, BF16_ROWS), size)],
                sem).start()

    @pl.when(n >= SEG_SPLIT)
    def _():
        for size in SEG_SIZES:
            if size >= SEG_SPLIT:
                piece(size)

    for size in SEG_SIZES:
        if size < SEG_SPLIT:
            piece(size)


def _wait_rows(total, sizes, src_ref, dst_ref, sem):
    for size in sizes:
        @pl.when((total & size) != 0)
        def _():
            pltpu.make_async_copy(src_ref.at[pl.ds(0, size)], dst_ref.at[pl.ds(0, size)], sem).wait()


def _moe_kernel(tg_ref, nv_ref, ssrc_ref, sdst_ref, slen_ref, tlo_ref, thi_ref, hl_ref, *refs):
    del tg_ref
    w_refs = [refs[3 * k:3 * k + 3] for k in range(MOE_PAIR)]
    y_ref, xbuf, sem = refs[3 * MOE_PAIR:]
    i = pl.program_id(0)
    nv = nv_ref[0]

    def overlap(s, tile):
        base = tile * MOE_TM
        d = sdst_ref[s]
        lo = jnp.maximum(d, base)
        hi = jnp.minimum(d + slen_ref[s], base + MOE_TM)
        return jnp.maximum(hi - lo, 0), lo

    def start_step(step, slot):
        for k in range(MOE_PAIR):
            tile = step * MOE_PAIR + k

            @pl.when(tile < nv)
            def _():
                def body(s, carry):
                    n, lo = overlap(s, tile)
                    _start_pieces(n, ssrc_ref[s] + (lo - sdst_ref[s]), lo - tile * MOE_TM,
                                  hl_ref, xbuf.at[slot, k], sem.at[slot])
                    return carry

                lax.fori_loop(tlo_ref[tile], thi_ref[tile], body, 0)

    def wait_step(step, slot):
        for k in range(MOE_PAIR):
            tile = step * MOE_PAIR + k
            total = lax.fori_loop(tlo_ref[tile], thi_ref[tile], lambda s, acc: acc + overlap(s, tile)[0], 0)
            _wait_rows(total, SEG_SIZES, hl_ref, xbuf.at[slot, k], sem.at[slot])

    @pl.when(i == 0)
    def _():
        xbuf[...] = jnp.zeros_like(xbuf)
        start_step(0, 0)

    @pl.when((i + 1) * MOE_PAIR < nv)
    def _():
        start_step(i + 1, (i + 1) % 2)

    @pl.when(i * MOE_PAIR >= nv)
    def _():
        y_ref[...] = jnp.zeros_like(y_ref)

    @pl.when(i * MOE_PAIR < nv)
    def _():
        slot = i % 2
        wait_step(i, slot)
        for k, (wg_ref, wu_ref, wd_ref) in enumerate(w_refs):
            x = xbuf[slot, k]
            hg = _dot(x, wg_ref[0])
            hu = _dot(x, wu_ref[0])
            hid = ((hg * _sigmoid(hg)) * hu).astype(BF16)
            y_ref[k * MOE_TM:(k + 1) * MOE_TM, :] = _dot(hid, wd_ref[0]).astype(BF16)


def _moe(cfg, hl, plan, w_gate, w_up, w_down):
    n_steps = cfg.moe_tiles // MOE_PAIR

    def w_map(k):
        return lambda i, tg, nv, *_: (tg[jnp.minimum(i * MOE_PAIR + k, nv[0] - 1)], 0, 0)

    w_gate = w_gate.reshape(N_EXPERTS, D_MODEL, D_FF)
    w_up = w_up.reshape(N_EXPERTS, D_MODEL, D_FF)
    w_down = w_down.reshape(N_EXPERTS, D_FF, D_MODEL)
    w_specs = []
    for k in range(MOE_PAIR):
        w_specs += [pl.BlockSpec((1, D_MODEL, D_FF), w_map(k)), pl.BlockSpec((1, D_MODEL, D_FF), w_map(k)),
                    pl.BlockSpec((1, D_FF, D_MODEL), w_map(k))]

    return pl.pallas_call(
        _moe_kernel,
        grid_spec=pltpu.PrefetchScalarGridSpec(
            num_scalar_prefetch=7,
            grid=(n_steps,),
            in_specs=[pl.BlockSpec(memory_space=pl.ANY)] + w_specs,
            out_specs=pl.BlockSpec((MOE_PAIR * MOE_TM, D_MODEL), lambda i, *_: (i, 0)),
            scratch_shapes=[pltpu.VMEM((2, MOE_PAIR, MOE_TM, D_MODEL), BF16), pltpu.SemaphoreType.DMA((2,))],
        ),
        out_shape=jax.ShapeDtypeStruct((n_steps * MOE_PAIR * MOE_TM, D_MODEL), BF16),
        compiler_params=_cparams(("arbitrary",)),
        name="moe",
    )(plan.tile_expert, plan.n_valid, plan.seg_tiled, plan.seg_sorted, plan.seg_len, plan.tile_lo, plan.tile_hi,
      hl, *([w_gate, w_up, w_down] * MOE_PAIR))


def _combine_kernel(stiled_ref, ssorted_ref, slen_ref, xmc_ref, xms_ref, ys_ref, lp_ref, mod_ref,
                    oc_ref, os_ref, ybuf, sem, *, n_tiles, n_ctx_tiles):
    t = pl.program_id(0)

    def start_tile(tile, slot):
        def body(e, carry):
            s = e * n_tiles + tile
            _start_pieces(slen_ref[s], ssorted_ref[s], stiled_ref[s] - tile * LT,
                          ys_ref, ybuf.at[slot], sem.at[slot])
            return carry

        lax.fori_loop(0, N_EXPERTS, body, 0)

    def wait_tile(tile, slot):
        total = lax.fori_loop(0, N_EXPERTS, lambda e, acc: acc + slen_ref[e * n_tiles + tile], 0)
        _wait_rows(total, (2 * SEG_SIZES[0],) + SEG_SIZES, ys_ref, ybuf.at[slot], sem.at[slot])

    @pl.when(t == 0)
    def _():
        ybuf[...] = jnp.zeros_like(ybuf)
        start_tile(0, 0)

    @pl.when(t + 1 < n_tiles)
    def _():
        start_tile(t + 1, (t + 1) % 2)

    slot = t % 2
    wait_tile(t, slot)
    lp = lp_ref[...]
    pos_a = (lp[:, 0:1] * 32.0 + lp[:, 1:2]).astype(jnp.int32)
    pos_b = (lp[:, 2:3] * 32.0 + lp[:, 3:4]).astype(jnp.int32)
    gate_a = lp[:, 4:5] + lp[:, 5:6] + lp[:, 6:7]
    gate_b = lp[:, 7:8] + lp[:, 8:9] + lp[:, 9:10]
    cols = lax.broadcasted_iota(jnp.int32, (TM, LT), 1)
    mix = jnp.where(cols == pos_a, gate_a, jnp.where(cols == pos_b, gate_b, 0.0)).astype(BF16)
    y = _dot(mix, ybuf[slot])
    g2 = mod_ref[0, 0][5:6]
    is_ctx = t < n_ctx_tiles
    _store_pair(is_ctx, oc_ref, os_ref, _load_pair(is_ctx, xmc_ref, xms_ref) + g2 * y)


def _combine(cfg, l, xm_pair, ys, lp, mod, plan):
    nt = cfg.n_tiles
    return pl.pallas_call(
        functools.partial(_combine_kernel, n_tiles=nt, n_ctx_tiles=cfg.n_ctx_tiles),
        grid_spec=pltpu.PrefetchScalarGridSpec(
            num_scalar_prefetch=3,
            grid=(nt,),
            in_specs=_pair_specs(cfg, D_MODEL) + [
                pl.BlockSpec(memory_space=pl.ANY),
                pl.BlockSpec((TM, LANES), lambda i, *_: (i, 0)),
                pl.BlockSpec((1, 1, 6, D_MODEL), lambda i, *_: (l, _mod_row(cfg, i), 0, 0)),
            ],
            out_specs=_pair_specs(cfg, D_MODEL),
            scratch_shapes=[pltpu.VMEM((2, LT, D_MODEL), BF16), pltpu.SemaphoreType.DMA((2,))],
        ),
        out_shape=_pair_shapes(cfg, D_MODEL, F32),
        compiler_params=_cparams(("arbitrary",)),
        name="combine",
    )(plan.seg_tiled, plan.seg_sorted, plan.seg_len, *xm_pair, ys, lp, mod)


class Plan(NamedTuple):
    seg_tiled: jax.Array
    seg_sorted: jax.Array
    seg_len: jax.Array
    tile_expert: jax.Array
    n_valid: jax.Array
    tile_lo: jax.Array
    tile_hi: jax.Array


def _plan(cfg, counts):
    nt = cfg.n_tiles
    m = (counts + (BF16_ROWS - 1)) // BF16_ROWS * BF16_ROWS
    local_off = jnp.cumsum(m, axis=1) - m
    within_expert = jnp.cumsum(m, axis=0) - m
    expert_rows = jnp.sum(m, axis=0)
    expert_tiles = (expert_rows + (MOE_TM - 1)) // MOE_TM
    tiles_end = jnp.cumsum(expert_tiles)
    expert_start = (tiles_end - expert_tiles) * MOE_TM
    seg_sorted = (expert_start[None, :] + within_expert).T.reshape(-1).astype(jnp.int32)
    seg_tiled = (jnp.arange(nt, dtype=jnp.int32)[:, None] * LT + local_off).T.reshape(-1).astype(jnp.int32)
    seg_len = m.T.reshape(-1).astype(jnp.int32)
    n_valid = tiles_end[-1:].astype(jnp.int32)
    tile_ids = jnp.arange(cfg.moe_tiles, dtype=jnp.int32)
    tile_expert = jnp.minimum(jnp.sum(tile_ids[:, None] >= tiles_end[None, :], axis=1),
                              N_EXPERTS - 1).astype(jnp.int32)
    tile_base = tile_ids * MOE_TM
    seg_end = seg_sorted + seg_len
    tile_lo = jnp.sum(seg_end[None, :] <= tile_base[:, None], axis=1).astype(jnp.int32)
    tile_hi = jnp.sum(seg_sorted[None, :] < tile_base[:, None] + MOE_TM, axis=1).astype(jnp.int32)
    return Plan(seg_tiled, seg_sorted, seg_len, tile_expert, n_valid, tile_lo, tile_hi)


def _rope_tables(n):
    pos = jnp.arange(n)
    rc = jnp.stack([(pos // GRID_W).astype(F32), (pos % GRID_W).astype(F32)], axis=1)
    freqs = ROPE_THETA ** (-jnp.arange(AXIS_PAIRS, dtype=F32) / AXIS_PAIRS)
    ang = rc[:, :, None] * freqs[None, None, :]
    cos = jnp.cos(ang)[:, :, None, :]
    sin = jnp.sin(ang)[:, :, None, :] * jnp.array([-1.0, 1.0], F32)[None, None, :, None]
    cos = jnp.broadcast_to(cos, (n, 2, 2, AXIS_PAIRS)).reshape(n, HEAD_DIM)
    sin = jnp.broadcast_to(sin, (n, 2, 2, AXIS_PAIRS)).reshape(n, HEAD_DIM)
    reps = LANES // HEAD_DIM
    cos = jnp.concatenate([jnp.ones((TM, HEAD_DIM), F32), cos], axis=0)
    sin = jnp.concatenate([jnp.zeros((TM, HEAD_DIM), F32), sin], axis=0)
    return jnp.tile(cos, (1, reps)), jnp.tile(sin, (1, reps))


def _block_diag_mean(width):
    r = jnp.arange(width)
    return jnp.where((r[:, None] // HEAD_DIM) == (r[None, :] // HEAD_DIM), 1.0 / HEAD_DIM, 0.0).astype(BF16)


def _forward(cfg, x_prompt, x_sample, cache_k, cache_v, c, c_ctx, w_mod, b_mod, norm1_g, norm2_g,
             w_in, q_norm_g, k_norm_g, conv_dw_w, conv_dw_b, conv_ln_g, conv_ln_b, w_out,
             w_router, b_router, w_gate, w_up, w_down):
    depth = cfg.depth
    assert cfg.ctx_seq == CONV_ROWS and cfg.t_ctx % TM == 0 and cfg.smp_seq % TM == 0
    assert cfg.t_ctx % cfg.smp_seq == 0 and cfg.smp_batch + 1 <= 8

    x = (x_prompt.reshape(cfg.t_ctx, D_MODEL), x_sample.reshape(-1, D_MODEL))
    cvec = jnp.zeros((8, D_MODEL), F32).at[0].set(c_ctx).at[1:1 + cfg.smp_batch].set(c)
    mod = _modulation(cvec, w_mod, b_mod)

    n1 = norm1_g.reshape(depth, 1, D_MODEL)
    n2 = norm2_g.reshape(depth, 1, D_MODEL)
    gain = jnp.concatenate([jnp.tile(q_norm_g, (1, N_Q_HEADS)), jnp.tile(k_norm_g, (1, N_KV_HEADS))],
                           axis=1).reshape(depth, 1, QK_W)
    expert_w = (w_gate.reshape(-1, D_FF), w_up.reshape(-1, D_FF), w_down.reshape(-1, D_MODEL))
    taps = _conv_tap_rows(conv_dw_w)
    dw_b = conv_dw_b.reshape(depth, 1, CONV_CH)
    ln_g = conv_ln_g.reshape(depth, 1, CONV_CH)
    ln_b = conv_ln_b.reshape(depth, 1, CONV_CH)
    wr = jnp.pad(w_router, ((0, 0), (0, LANES - N_EXPERTS)))
    br = b_router.reshape(N_EXPERTS, 1)
    cos, sin = _rope_tables(cfg.smp_seq)
    bdq = _block_diag_mean(Q_W // 2)
    bdk = _block_diag_mean(KV_W)
    ck = cache_k.reshape(cfg.smp_batch, depth, cfg.past, KV_W)
    cv = cache_v.reshape(cfg.smp_batch, depth, cfg.past, KV_W)

    new_k = jnp.zeros((cfg.ctx_batch, depth, cfg.ctx_seq, KV_W), F32)
    new_v = jnp.zeros((cfg.ctx_batch, depth, cfg.ctx_seq, KV_W), F32)
    for l in range(depth):
        q, k, v, z, new_k, new_v = _inproj(cfg, l, x, mod, n1, w_in, gain, bdq, bdk, cos, sin, new_k, new_v)
        a_pair, c_pair, expert_bf = _mixer(cfg, l, q, k, v, ck, cv, z, taps, dw_b, ln_g, ln_b, expert_w)
        xmc, xms, hl, lp, cnt = _outproj(cfg, l, a_pair, c_pair, x, mod, n2, w_out, wr, br)
        plan = _plan(cfg, cnt[:, :, 0])
        ys = _moe(cfg, hl.reshape(cfg.n_tiles * LT, D_MODEL), plan, *expert_bf)
        x = _combine(cfg, l, (xmc, xms), ys, lp, mod, plan)

    y_prompt = x[0].reshape(cfg.ctx_batch, cfg.ctx_seq, D_MODEL)
    y_sample = x[1].reshape(cfg.smp_batch, cfg.smp_seq, D_MODEL)
    cache_shape = (cfg.ctx_batch, depth, cfg.ctx_seq, N_KV_HEADS, HEAD_DIM)
    return y_prompt, y_sample, new_k.reshape(cache_shape), new_v.reshape(cache_shape)


def kernel(x_prompt, x_sample, cache_k, cache_v, c, c_ctx, w_mod, b_mod, norm1_g, norm2_g, w_in, q_norm_g,
           k_norm_g, conv_dw_w, conv_dw_b, conv_ln_g, conv_ln_b, w_out, w_router, b_router, w_gate, w_up, w_down):
    cfg = Cfg(ctx_batch=x_prompt.shape[0], ctx_seq=x_prompt.shape[1], smp_batch=x_sample.shape[0],
              smp_seq=x_sample.shape[1], past=cache_k.shape[2], depth=w_mod.shape[0])
    return _forward(cfg, x_prompt, x_sample, cache_k, cache_v, c, c_ctx, w_mod, b_mod, norm1_g, norm2_g,
                    w_in, q_norm_g, k_norm_g, conv_dw_w, conv_dw_b, conv_ln_g, conv_ln_b, w_out,
                    w_router, b_router, w_gate, w_up, w_down)
```

```python
import functools
import math
from typing import NamedTuple

import jax
import jax.numpy as jnp
from jax import lax
from jax.experimental import pallas as pl
from jax.experimental.pallas import tpu as pltpu

F32 = jnp.float32
BF16 = jnp.bfloat16

D_MODEL = 1024
HEAD_DIM = 64
N_Q_HEADS = 8
N_KV_HEADS = 2
Q_PER_KV = N_Q_HEADS // N_KV_HEADS
Q_W = N_Q_HEADS * HEAD_DIM
KV_W = N_KV_HEADS * HEAD_DIM
QK_W = Q_W + KV_W
CONV_CH = D_MODEL // 2
CONV_K = 31
CONV_PAD = CONV_K // 2
IN_COLS = Q_W + 2 * KV_W + 2 * CONV_CH
GRID_W = 64
AXIS_PAIRS = HEAD_DIM // 4
ROPE_THETA = 10000.0
N_EXPERTS = 16
N_GROUPS = 4
EXPERTS_PER_GROUP = N_EXPERTS // N_GROUPS
D_FF = 512
EPS = 1e-6

LANES = 128
BF16_ROWS = 16
TM = 512
TOP_K = 2
LT = 1280
MOE_TM = 512
MOE_PAIR = 2
CONV_ROWS = 256
HALO = 16
CONV_INNER = CONV_ROWS + 16
PAD_ROWS = 24 + CONV_INNER + 8
SEG_SIZES = (512, 256, 128, 64, 32, 16)
SEG_SPLIT = 128
Q_SCALE = (1.0 / math.sqrt(HEAD_DIM)) * math.log2(math.e)
VMEM_LIMIT = 48 * 1024 * 1024
MIXER_VMEM_LIMIT = 56 * 1024 * 1024


class Cfg(NamedTuple):
    ctx_batch: int
    ctx_seq: int
    smp_batch: int
    smp_seq: int
    past: int
    depth: int

    @property
    def t_ctx(self):
        return self.ctx_batch * self.ctx_seq

    @property
    def t_all(self):
        return self.t_ctx + self.smp_batch * self.smp_seq

    @property
    def n_tiles(self):
        return self.t_all // TM

    @property
    def n_ctx_tiles(self):
        return self.t_ctx // TM

    @property
    def smp_tiles(self):
        return self.smp_seq // TM

    @property
    def moe_tiles(self):
        rows = TOP_K * self.t_all + self.n_tiles * N_EXPERTS * (BF16_ROWS - 1) + N_EXPERTS * (MOE_TM - 1)
        step_rows = MOE_PAIR * MOE_TM
        return -(-rows // step_rows) * MOE_PAIR


def _mod_row(cfg, i):
    return jnp.where(i < cfg.n_ctx_tiles, 0, 1 + (i - cfg.n_ctx_tiles) // cfg.smp_tiles)


def _pair_specs(cfg, width, tile_of=lambda i: i):
    nct = cfg.n_ctx_tiles
    return [pl.BlockSpec((TM, width), lambda i, *_: (jnp.minimum(tile_of(i), nct - 1), 0)),
            pl.BlockSpec((TM, width), lambda i, *_: (jnp.maximum(tile_of(i) - nct, 0), 0))]


def _pair_shapes(cfg, width, dtype):
    return [jax.ShapeDtypeStruct((cfg.t_ctx, width), dtype),
            jax.ShapeDtypeStruct((cfg.t_all - cfg.t_ctx, width), dtype)]


def _load_pair(is_ctx, c_ref, s_ref):
    return jnp.where(is_ctx, c_ref[...], s_ref[...])


def _store_pair(is_ctx, c_ref, s_ref, val):
    @pl.when(is_ctx)
    def _():
        c_ref[...] = val

    @pl.when(jnp.logical_not(is_ctx))
    def _():
        s_ref[...] = val


def _stream_specs(cfg, width, paired, tile_of=lambda i: i):
    if paired:
        return _pair_specs(cfg, width, tile_of)
    return [pl.BlockSpec((TM, width), lambda i, *_: (tile_of(i), 0))]


def _stream_shapes(cfg, width, dtype, paired):
    if paired:
        return _pair_shapes(cfg, width, dtype)
    return [jax.ShapeDtypeStruct((cfg.t_all, width), dtype)]


def _load_stream(is_ctx, refs):
    return _load_pair(is_ctx, *refs) if len(refs) == 2 else refs[0][...]


def _cparams(sem, vmem=VMEM_LIMIT):
    return pltpu.CompilerParams(dimension_semantics=sem, vmem_limit_bytes=vmem)


def _dot(a, b):
    return jnp.dot(a, b, preferred_element_type=F32)


def _dot_nt(a, b):
    return lax.dot_general(a, b, (((1,), (1,)), ((), ())), preferred_element_type=F32)


def _sigmoid(x):
    return 1.0 / (1.0 + jnp.exp(-x))


def _mod_kernel(c_ref, w_ref, b_ref, o_ref):
    c = c_ref[...]
    a = (c * _sigmoid(c)).astype(BF16)
    o_ref[0] = _dot(a, w_ref[0].astype(BF16)) + b_ref[0]


def _modulation(cvec, w_mod, b_mod):
    depth = w_mod.shape[0]
    out = pl.pallas_call(
        _mod_kernel,
        grid=(depth, 6),
        in_specs=[
            pl.BlockSpec((8, D_MODEL), lambda l, j: (0, 0)),
            pl.BlockSpec((1, D_MODEL, D_MODEL), lambda l, j: (l, 0, j)),
            pl.BlockSpec((1, 1, D_MODEL), lambda l, j: (l, 0, j)),
        ],
        out_specs=pl.BlockSpec((1, 8, D_MODEL), lambda l, j: (l, 0, j)),
        out_shape=jax.ShapeDtypeStruct((depth, 8, 6 * D_MODEL), F32),
        compiler_params=_cparams(("arbitrary", "arbitrary")),
        name="modulation",
    )(cvec, w_mod, b_mod.reshape(depth, 1, 6 * D_MODEL))
    return out.reshape(depth, 8, 6, D_MODEL)


def _inproj_kernel(*refs, n_tiles, n_ctx_tiles, ctx_seq, n_x):
    x_refs = refs[:n_x]
    (mod_ref, n1_ref, w_ref, gain_ref, bdq_ref, bdk_ref, cos_ref, sin_ref,
     kc_in, vc_in, q_ref, k_ref, v_ref, z_ref, kc_ref, vc_ref, w_bf, proj_prev) = refs[n_x:]
    del kc_in, vc_in
    step = pl.program_id(0)

    @pl.when(step == 0)
    def _():
        w_bf[...] = w_ref[0].astype(BF16)
        proj_prev[...] = jnp.zeros_like(proj_prev)

    x = _load_stream(jnp.minimum(step, n_tiles - 1) < n_ctx_tiles, x_refs)
    mod = mod_ref[0, 0]
    sh1, sc1 = mod[0:1], mod[1:2]
    ms = jnp.mean(x * x, axis=-1, keepdims=True)
    h = (x * lax.rsqrt(ms + EPS) * n1_ref[0]) * (1.0 + sc1) + sh1
    proj_new = _dot(h.astype(BF16), w_bf[...])

    qk = proj_prev[:, :QK_W]
    sq = (qk * qk).astype(BF16)
    half = Q_W // 2
    msq = jnp.concatenate([_dot(sq[:, :half], bdq_ref[...]),
                           _dot(sq[:, half:Q_W], bdq_ref[...]),
                           _dot(sq[:, Q_W:], bdk_ref[...])], axis=1)
    qkn = qk * lax.rsqrt(msq + EPS) * gain_ref[0]
    v = proj_prev[:, QK_W:QK_W + KV_W]
    v_ref[...] = v
    a = proj_prev[:, QK_W + KV_W:QK_W + KV_W + CONV_CH]
    b = proj_prev[:, QK_W + KV_W + CONV_CH:]
    z_ref[...] = a * _sigmoid(b)
    cos = cos_ref[...]
    sin = sin_ref[...]
    first = (lax.broadcasted_iota(jnp.int32, (TM, LANES), 1) % 32) < 16
    for c in range(QK_W // LANES):
        blk = qkn[:, c * LANES:(c + 1) * LANES]
        partner = jnp.where(first, pltpu.roll(blk, LANES - 16, 1), pltpu.roll(blk, 16, 1))
        rot = blk * cos + partner * sin
        if c < Q_W // LANES:
            q_ref[:, c * LANES:(c + 1) * LANES] = (rot * Q_SCALE).astype(BF16)
        else:
            k = rot
            k_ref[...] = k

    proj_prev[...] = proj_new

    @pl.when(step - 1 < n_ctx_tiles)
    def _():
        for s in range(TM // ctx_seq):
            kc_ref[s, 0] = k[s * ctx_seq:(s + 1) * ctx_seq]
            vc_ref[s, 0] = v[s * ctx_seq:(s + 1) * ctx_seq]


def _inproj(cfg, l, x, mod, n1, w_in, gain, bdq, bdk, cos, sin, kc, vc):
    t = cfg.t_all
    n_x = len(x)
    nt, nct, st = cfg.n_tiles, cfg.n_ctx_tiles, cfg.smp_tiles
    tile_a = lambda i: jnp.minimum(i, nt - 1)
    tile_b = lambda i: jnp.maximum(i - 1, 0)
    rope_map = lambda i: (jnp.where(tile_b(i) < nct, 0, 1 + (tile_b(i) - nct) % st), 0)
    seqs = TM // cfg.ctx_seq
    cache_spec = lambda: pl.BlockSpec((seqs, 1, cfg.ctx_seq, KV_W),
                                      lambda i: (jnp.minimum(tile_b(i), nct - 1), l, 0, 0))
    return pl.pallas_call(
        functools.partial(_inproj_kernel, n_tiles=nt, n_ctx_tiles=nct, ctx_seq=cfg.ctx_seq, n_x=n_x),
        grid=(nt + 1,),
        in_specs=_stream_specs(cfg, D_MODEL, n_x == 2, tile_a) + [
            pl.BlockSpec((1, 1, 6, D_MODEL), lambda i: (l, _mod_row(cfg, tile_a(i)), 0, 0)),
            pl.BlockSpec((1, 1, D_MODEL), lambda i: (l, 0, 0)),
            pl.BlockSpec((1, D_MODEL, IN_COLS), lambda i: (l, 0, 0)),
            pl.BlockSpec((1, 1, QK_W), lambda i: (l, 0, 0)),
            pl.BlockSpec((Q_W // 2, Q_W // 2), lambda i: (0, 0)),
            pl.BlockSpec((KV_W, KV_W), lambda i: (0, 0)),
            pl.BlockSpec((TM, LANES), rope_map),
            pl.BlockSpec((TM, LANES), rope_map),
            pl.BlockSpec(memory_space=pl.ANY),
            pl.BlockSpec(memory_space=pl.ANY),
        ],
        out_specs=[
            pl.BlockSpec((TM, Q_W), lambda i: (tile_b(i), 0)),
            pl.BlockSpec((TM, KV_W), lambda i: (tile_b(i), 0)),
            pl.BlockSpec((TM, KV_W), lambda i: (tile_b(i), 0)),
            pl.BlockSpec((TM, CONV_CH), lambda i: (tile_b(i), 0)),
            cache_spec(), cache_spec(),
        ],
        out_shape=[
            jax.ShapeDtypeStruct((t, Q_W), BF16),
            jax.ShapeDtypeStruct((t, KV_W), F32),
            jax.ShapeDtypeStruct((t, KV_W), F32),
            jax.ShapeDtypeStruct((t, CONV_CH), F32),
            jax.ShapeDtypeStruct(kc.shape, F32),
            jax.ShapeDtypeStruct(vc.shape, F32),
        ],
        scratch_shapes=[pltpu.VMEM((D_MODEL, IN_COLS), BF16), pltpu.VMEM((TM, IN_COLS), F32)],
        input_output_aliases={n_x + 8: 4, n_x + 9: 5},
        compiler_params=_cparams(("arbitrary",)),
        name="inproj",
    )(*x, mod, n1, w_in, gain, bdq, bdk, cos, sin, kc, vc)


def _mixer_kernel(*refs, rq, n_past, seq_blocks):
    if n_past:
        q_ref, k_ref, v_ref, ck_ref, cv_ref = refs[:5]
        refs = refs[5:]
    else:
        q_ref, k_ref, v_ref = refs[:3]
        refs = refs[3:]
    z_ref, zp_ref, zn_ref, w_ref, b_ref, g_ref, beta_ref = refs[:7]
    refs = refs[7:]
    if n_past:
        for src, dst in zip(refs[0:3], refs[5:8]):
            dst[...] = src[...].astype(BF16)
        refs = refs[3:5] + refs[8:]
    o_ref, co_ref, krep, vrep, pad, taps = refs
    head_w = Q_PER_KV * HEAD_DIM
    j = pl.program_id(1)

    @pl.when(jnp.logical_and(pl.program_id(0) == 0, j == 0))
    def _():
        diag = (lax.broadcasted_iota(jnp.int32, (LANES, LANES), 0)
                == lax.broadcasted_iota(jnp.int32, (LANES, LANES), 1))
        for c in range(CONV_CH // LANES):
            for p in range(4):
                for q in range(4):
                    for half in range(2):
                        tap = w_ref[0, 8 * q + 2 * p + half:8 * q + 2 * p + half + 1, c * LANES:(c + 1) * LANES]
                        taps[c, p, q * LANES:(q + 1) * LANES, half * LANES:(half + 1) * LANES] = (
                            jnp.where(diag, tap, 0.0).astype(BF16))

    @pl.when(j == 0)
    def _():
        src = lax.broadcasted_iota(jnp.int32, (KV_W, head_w), 0)
        dst = lax.broadcasted_iota(jnp.int32, (KV_W, head_w), 1) % HEAD_DIM
        n_new = k_ref.shape[0]
        for h in range(N_KV_HEADS):
            rep = jnp.where(src == dst + h * HEAD_DIM, 1.0, 0.0).astype(BF16)
            if n_past:
                krep[h, :n_past, :] = _dot(ck_ref[0, 0].astype(BF16), rep).astype(BF16)
                vrep[h, :n_past, :] = _dot(cv_ref[0, 0].astype(BF16), rep).astype(BF16)
            krep[h, n_past:n_past + n_new, :] = _dot(k_ref[...].astype(BF16), rep).astype(BF16)
            vrep[h, n_past:n_past + n_new, :] = _dot(v_ref[...].astype(BF16), rep).astype(BF16)

    end = HALO + CONV_ROWS
    if seq_blocks > 1:
        pad[0:HALO, :] = jnp.where(j > 0, zp_ref[...], 0.0)
        pad[end:end + HALO, :] = jnp.where(j < seq_blocks - 1, zn_ref[...], 0.0)
    else:
        pad[0:HALO, :] = jnp.zeros((HALO, CONV_CH), F32)
        pad[end:end + HALO, :] = jnp.zeros((HALO, CONV_CH), F32)
    pad[end + HALO:, :] = jnp.zeros((PAD_ROWS - end - HALO, CONV_CH), F32)
    pad[HALO:end, :] = z_ref[...]

    lane_head = lax.broadcasted_iota(jnp.int32, (rq, head_w), 1) // HEAD_DIM
    for h in range(N_KV_HEADS):
        qh = q_ref[:, h * head_w:(h + 1) * head_w]
        kh = krep[h]
        vh = vrep[h]
        acc = jnp.zeros((rq, head_w), F32)
        for g in range(Q_PER_KV):
            mask = lane_head == g
            qm = jnp.where(mask, qh, jnp.zeros_like(qh))
            s = _dot_nt(qm, kh)
            m = jnp.max(s, axis=-1, keepdims=True)
            p = jnp.exp2(s - m)
            denom = jnp.sum(p, axis=-1, keepdims=True)
            o = _dot(p.astype(BF16), vh)
            acc = jnp.where(mask, o * (1.0 / denom), acc)
        o_ref[:, h * head_w:(h + 1) * head_w] = acc.astype(BF16)
    conv_cols = [_conv_cols(pad, taps, c) for c in range(CONV_CH // LANES)]
    co_ref[...] = _conv_finish(conv_cols, b_ref, g_ref, beta_ref)


def _mixer(cfg, l, q, k, v, cache_k, cache_v, z, taps, dw_b, ln_g, ln_b, expert_w):
    head_w = Q_PER_KV * HEAD_DIM
    rq = CONV_ROWS
    hb = CONV_ROWS // HALO
    last = cfg.t_all // HALO - 1
    vec = lambda: pl.BlockSpec((1, 1, CONV_CH), lambda b, j: (l, 0, 0))

    def conv_specs(row_block):
        return [
            pl.BlockSpec((CONV_ROWS, CONV_CH), lambda b, j: (row_block(b, j), 0)),
            pl.BlockSpec((HALO, CONV_CH), lambda b, j: (jnp.maximum(row_block(b, j) * hb - 1, 0), 0)),
            pl.BlockSpec((HALO, CONV_CH), lambda b, j: (jnp.minimum(row_block(b, j) * hb + hb, last), 0)),
            pl.BlockSpec((1, 32, CONV_CH), lambda b, j: (l, 0, 0)),
            vec(), vec(), vec(),
        ]

    def scratch(sk):
        return [pltpu.VMEM((N_KV_HEADS, sk, head_w), BF16), pltpu.VMEM((N_KV_HEADS, sk, head_w), BF16),
                pltpu.VMEM((PAD_ROWS, CONV_CH), F32),
                pltpu.VMEM((CONV_CH // LANES, 4, 4 * LANES, 2 * LANES), BF16)]

    sc = cfg.ctx_seq
    a_ctx, c_ctx = pl.pallas_call(
        functools.partial(_mixer_kernel, rq=rq, n_past=0, seq_blocks=1),
        grid=(cfg.ctx_batch, 1),
        in_specs=[
            pl.BlockSpec((sc, Q_W), lambda b, j: (b, 0)),
            pl.BlockSpec((sc, KV_W), lambda b, j: (b, 0)),
            pl.BlockSpec((sc, KV_W), lambda b, j: (b, 0)),
        ] + conv_specs(lambda b, j: b),
        out_specs=[pl.BlockSpec((sc, Q_W), lambda b, j: (b, 0)),
                   pl.BlockSpec((sc, CONV_CH), lambda b, j: (b, 0))],
        out_shape=[jax.ShapeDtypeStruct((cfg.t_ctx, Q_W), BF16),
                   jax.ShapeDtypeStruct((cfg.t_ctx, CONV_CH), BF16)],
        scratch_shapes=scratch(sc),
        compiler_params=_cparams(("arbitrary", "arbitrary")),
        name="mixer_ctx",
    )(q, k, v, z, z, z, taps, dw_b, ln_g, ln_b)
    ss = cfg.smp_seq
    sk = cfg.past + ss
    kv_base = cfg.t_ctx // ss
    q_base = cfg.t_ctx // rq
    nqb = ss // rq
    t_smp = cfg.smp_batch * ss
    steps = cfg.smp_batch * nqb
    w_rows = [w.shape[0] // (cfg.depth * steps) for w in expert_w]
    w_specs = lambda off: [pl.BlockSpec((r, w.shape[1]), lambda b, j: (off + b * nqb + j, 0))
                           for r, w in zip(w_rows, expert_w)]
    a_smp, c_smp, *expert_bf = pl.pallas_call(
        functools.partial(_mixer_kernel, rq=rq, n_past=cfg.past, seq_blocks=nqb),
        grid=(cfg.smp_batch, nqb),
        in_specs=[
            pl.BlockSpec((rq, Q_W), lambda b, j: (q_base + b * nqb + j, 0)),
            pl.BlockSpec((ss, KV_W), lambda b, j: (kv_base + b, 0)),
            pl.BlockSpec((ss, KV_W), lambda b, j: (kv_base + b, 0)),
            pl.BlockSpec((1, 1, cfg.past, KV_W), lambda b, j: (b, l, 0, 0)),
            pl.BlockSpec((1, 1, cfg.past, KV_W), lambda b, j: (b, l, 0, 0)),
        ] + conv_specs(lambda b, j: q_base + b * nqb + j) + w_specs(l * steps),
        out_specs=[pl.BlockSpec((rq, Q_W), lambda b, j: (b * nqb + j, 0)),
                   pl.BlockSpec((rq, CONV_CH), lambda b, j: (b * nqb + j, 0))] + w_specs(0),
        out_shape=[jax.ShapeDtypeStruct((t_smp, Q_W), BF16),
                   jax.ShapeDtypeStruct((t_smp, CONV_CH), BF16)]
        + [jax.ShapeDtypeStruct((r * steps, w.shape[1]), BF16) for r, w in zip(w_rows, expert_w)],
        scratch_shapes=scratch(sk),
        compiler_params=_cparams(("arbitrary", "arbitrary"), MIXER_VMEM_LIMIT),
        name="mixer_smp",
    )(q, k, v, cache_k, cache_v, z, z, z, taps, dw_b, ln_g, ln_b, *expert_w)
    return (a_ctx, a_smp), (c_ctx, c_smp), expert_bf


def _conv_cols(pad, taps, c):
    cols = slice(c * LANES, (c + 1) * LANES)
    lhs = jnp.concatenate([pad[8 * q:8 * q + CONV_INNER, cols].astype(BF16) for q in range(4)], axis=1)
    acc = None
    for p in range(4):
        pair = _dot(lhs, taps[c, p])
        for half in range(2):
            o = 2 * p + half
            shifted = pair[o:o + CONV_ROWS, half * LANES:(half + 1) * LANES]
            acc = shifted if acc is None else acc + shifted
    return acc


def _conv_tap_rows(conv_dw_w):
    return jnp.pad(conv_dw_w, ((0, 0), (HALO - CONV_PAD, 32 - CONV_K - (HALO - CONV_PAD)), (0, 0)))


def _conv_finish(blocks, b_ref, g_ref, beta_ref):
    y = jnp.concatenate(blocks, axis=1) + b_ref[0]
    mu = jnp.mean(y, axis=-1, keepdims=True)
    yc = y - mu
    var = jnp.mean(yc * yc, axis=-1, keepdims=True)
    yn = yc * lax.rsqrt(var + EPS) * g_ref[0] + beta_ref[0]
    return (yn * _sigmoid(yn)).astype(BF16)


def _split_bf16(x):
    hi = x.astype(BF16)
    lo = (x - hi.astype(F32)).astype(BF16)
    return hi, lo


def _outproj_kernel(ac_ref, as_ref, cc_ref, cs_ref, *refs, n_tiles, n_ctx_tiles, n_x):
    x_refs = refs[:n_x]
    (mod_ref, n2_ref, wo_ref, wr_ref, br_ref, xm_ref, hl_ref, lp_ref, cnt_ref,
     wo_bf, h_prev, lt_prev, before_bf, eye_bf, rows_i) = refs[n_x:]
    step = pl.program_id(0)

    @pl.when(step == 0)
    def _():
        wo_bf[...] = wo_ref[0].astype(BF16)
        h_prev[...] = jnp.zeros_like(h_prev)
        lt_prev[...] = jnp.zeros_like(lt_prev)
        r = lax.broadcasted_iota(jnp.int32, (TM, TM), 0)
        c = lax.broadcasted_iota(jnp.int32, (TM, TM), 1)
        before_bf[...] = jnp.where(r < c, 1.0, 0.0).astype(BF16)
        eye_bf[...] = jnp.where(r == c, 1.0, 0.0).astype(BF16)
        rows_i[...] = lax.broadcasted_iota(jnp.int32, (LT, TM), 0)

    is_ctx = jnp.minimum(step, n_tiles - 1) < n_ctx_tiles
    ac = jnp.concatenate([_load_pair(is_ctx, ac_ref, as_ref), _load_pair(is_ctx, cc_ref, cs_ref)], axis=1)
    y = _dot(ac, wo_bf[...])
    mod = mod_ref[0, 0]
    g1, sh2, sc2 = mod[2:3], mod[3:4], mod[4:5]
    xm = _load_stream(is_ctx, x_refs) + g1 * y
    xm_ref[...] = xm
    ms = jnp.mean(xm * xm, axis=-1, keepdims=True)
    h2 = (xm * lax.rsqrt(ms + EPS) * n2_ref[0]) * (1.0 + sc2) + sh2
    h_new, h_lo = _split_bf16(h2)
    w_hi, w_lo = _split_bf16(wr_ref[...])
    logits = _dot(h_new, w_hi) + (_dot(h_new, w_lo) + _dot(h_lo, w_hi))
    lt_new = logits.T[0:N_EXPERTS, :]

    h_hi = h_prev[...]
    s_all = _sigmoid(lt_prev[...])
    sb_all = s_all + br_ref[...]

    rows8 = 2 * EXPERTS_PER_GROUP
    row = lax.broadcasted_iota(jnp.int32, (rows8, TM), 0)
    member = row % EXPERTS_PER_GROUP
    second = row >= EXPERTS_PER_GROUP

    def partner(x, k):
        return jnp.where(member + k < EXPERTS_PER_GROUP,
                         pltpu.roll(x, rows8 - k, 0), pltpu.roll(x, EXPERTS_PER_GROUP - k, 0))

    def other_group(x):
        return pltpu.roll(x, EXPERTS_PER_GROUP, 0)

    def group_sum(x):
        return x + partner(x, 1) + (partner(x, 2) + partner(x, 3))

    def ahead(other, mine, other_first):
        return jnp.where((other >= mine) if other_first else (other > mine), 1.0, 0.0)

    sel, gscore = [], []
    for hv in range(2):
        sb8 = sb_all[hv * rows8:(hv + 1) * rows8]
        rank = jnp.zeros((rows8, TM), F32)
        for k in range(1, EXPERTS_PER_GROUP):
            pk = partner(sb8, k)
            rank = rank + jnp.where(member + k >= EXPERTS_PER_GROUP, ahead(pk, sb8, True), ahead(pk, sb8, False))
        sel.append(rank < 1.5)
        gscore.append(group_sum(jnp.where(sel[hv], sb8, 0.0)))
    gate8, used8 = [], []
    for hv in range(2):
        mine, far = gscore[hv], gscore[1 - hv]
        near = other_group(mine)
        rank = jnp.where(second, ahead(near, mine, True), ahead(near, mine, False))
        rank = rank + ahead(far, mine, hv == 1) + ahead(other_group(far), mine, hv == 1)
        chosen = rank < 0.5
        used8.append(jnp.where(chosen, jnp.where(sel[hv], 1.0, 0.0), 0.0))
        gate8.append(used8[hv] * s_all[hv * rows8:(hv + 1) * rows8])
    total = group_sum(gate8[0]) + group_sum(gate8[1])
    total = total + other_group(total)
    inv_total = 1.0 / total
    gate16 = jnp.concatenate([gate8[0] * inv_total, gate8[1] * inv_total], axis=0)
    used16 = jnp.concatenate(used8, axis=0)
    used = used16 > 0.5

    prior = _dot(used16.astype(BF16), before_bf[...])
    n_e = jnp.sum(used16, axis=-1, keepdims=True)
    m_e = jnp.floor((n_e + (BF16_ROWS - 1)) * (1.0 / BF16_ROWS)) * BF16_ROWS
    lower = (lax.broadcasted_iota(jnp.int32, (N_EXPERTS, N_EXPERTS), 0)
             > lax.broadcasted_iota(jnp.int32, (N_EXPERTS, N_EXPERTS), 1))
    start = _dot(jnp.where(lower, 1.0, 0.0).astype(BF16),
                 jnp.broadcast_to(m_e, (N_EXPERTS, LANES)).astype(BF16))[:, 0:1]
    pos16 = start + prior
    pos_a = jnp.min(jnp.where(used, pos16, float(LT)), axis=0, keepdims=True)
    pos_b = jnp.max(jnp.where(used, pos16, -1.0), axis=0, keepdims=True)
    gate_a = jnp.sum(jnp.where(used, jnp.where(pos16 == pos_a, gate16, 0.0), 0.0), axis=0, keepdims=True)
    gate_b = jnp.sum(jnp.where(used, jnp.where(pos16 == pos_b, gate16, 0.0), 0.0), axis=0, keepdims=True)
    rows = rows_i[...]
    perm = jnp.where(rows == pos_a.astype(jnp.int32), 1.0,
                     jnp.where(rows == pos_b.astype(jnp.int32), 1.0, 0.0)).astype(BF16)
    hl_ref[0] = _dot(perm, h_hi).astype(BF16)

    def split32(p):
        hi = jnp.floor(p * (1.0 / 32.0))
        return [hi, p - 32.0 * hi]

    def split3(g):
        hi = g.astype(BF16).astype(F32)
        mid = (g - hi).astype(BF16).astype(F32)
        lo = ((g - hi) - mid).astype(BF16).astype(F32)
        return [hi, mid, lo]

    parts = split32(pos_a) + split32(pos_b) + split3(gate_a) + split3(gate_b)
    lp_parts = jnp.concatenate(parts + [jnp.zeros((LANES - len(parts), TM), F32)], axis=0).astype(BF16)
    lp_ref[...] = _dot_nt(eye_bf[...], lp_parts)

    cnt_ref[0] = jnp.broadcast_to(n_e, (N_EXPERTS, LANES)).astype(jnp.int32)

    h_prev[...] = h_new
    lt_prev[...] = lt_new


def _outproj(cfg, l, a_pair, c_pair, x, mod, n2, w_out, w_router, b_router):
    t, nt, nct = cfg.t_all, cfg.n_tiles, cfg.n_ctx_tiles
    n_x = len(x)
    tile_a = lambda i: jnp.minimum(i, nt - 1)
    tile_b = lambda i: jnp.maximum(i - 1, 0)
    return pl.pallas_call(
        functools.partial(_outproj_kernel, n_tiles=nt, n_ctx_tiles=nct, n_x=n_x),
        grid=(nt + 1,),
        in_specs=_pair_specs(cfg, Q_W, tile_a) + _pair_specs(cfg, CONV_CH, tile_a)
        + _stream_specs(cfg, D_MODEL, n_x == 2, tile_a) + [
            pl.BlockSpec((1, 1, 6, D_MODEL), lambda i: (l, _mod_row(cfg, tile_a(i)), 0, 0)),
            pl.BlockSpec((1, 1, D_MODEL), lambda i: (l, 0, 0)),
            pl.BlockSpec((1, D_MODEL, D_MODEL), lambda i: (l, 0, 0)),
            pl.BlockSpec((D_MODEL, LANES), lambda i: (0, 0)),
            pl.BlockSpec((N_EXPERTS, 1), lambda i: (0, 0)),
        ],
        out_specs=_stream_specs(cfg, D_MODEL, False, tile_a) + [
            pl.BlockSpec((1, LT, D_MODEL), lambda i: (tile_b(i), 0, 0)),
            pl.BlockSpec((TM, LANES), lambda i: (tile_b(i), 0)),
            pl.BlockSpec((1, N_EXPERTS, LANES), lambda i: (tile_b(i), 0, 0)),
        ],
        out_shape=_stream_shapes(cfg, D_MODEL, F32, False) + [
            jax.ShapeDtypeStruct((nt, LT, D_MODEL), BF16),
            jax.ShapeDtypeStruct((t, LANES), F32),
            jax.ShapeDtypeStruct((nt, N_EXPERTS, LANES), jnp.int32),
        ],
        scratch_shapes=[pltpu.VMEM((D_MODEL, D_MODEL), BF16), pltpu.VMEM((TM, D_MODEL), BF16),
                        pltpu.VMEM((N_EXPERTS, TM), F32), pltpu.VMEM((TM, TM), BF16), pltpu.VMEM((TM, TM), BF16),
                        pltpu.VMEM((LT, TM), jnp.int32)],
        compiler_params=_cparams(("arbitrary",)),
        name="outproj",
    )(*a_pair, *c_pair, *x, mod, n2, w_out, w_router, b_router)


def _start_pieces(n, src_row, dst_row, src_ref, dst_ref, sem):
    def piece(size):
        off = n & ~(2 * size - 1)

        @pl.when((n & size) != 0)
        def _():
            pltpu.make_async_copy(
                src_ref.at[pl.ds(pl.multiple_of(src_row + off, BF16_ROWS), size)],
                dst_ref.at[pl.ds(pl.multiple_of(dst_row + off, BF16_ROWS), size)],
                sem).start()

    @pl.when(n >= SEG_SPLIT)
    def _():
        for size in SEG_SIZES:
            if size >= SEG_SPLIT:
                piece(size)

    for size in SEG_SIZES:
        if size < SEG_SPLIT:
            piece(size)


def _wait_rows(total, sizes, src_ref, dst_ref, sem):
    for size in sizes:
        @pl.when((total & size) != 0)
        def _():
            pltpu.make_async_copy(src_ref.at[pl.ds(0, size)], dst_ref.at[pl.ds(0, size)], sem).wait()


def _moe_kernel(tg_ref, nv_ref, ssrc_ref, sdst_ref, slen_ref, tlo_ref, thi_ref, hl_ref, *refs):
    del tg_ref
    w_refs = [refs[3 * k:3 * k + 3] for k in range(MOE_PAIR)]
    y_ref, xbuf, sem = refs[3 * MOE_PAIR:]
    i = pl.program_id(0)
    nv = nv_ref[0]

    def overlap(s, tile):
        base = tile * MOE_TM
        d = sdst_ref[s]
        lo = jnp.maximum(d, base)
        hi = jnp.minimum(d + slen_ref[s], base + MOE_TM)
        return jnp.maximum(hi - lo, 0), lo

    def start_step(step, slot):
        for k in range(MOE_PAIR):
            tile = step * MOE_PAIR + k

            @pl.when(tile < nv)
            def _():
                def body(s, carry):
                    n, lo = overlap(s, tile)
                    _start_pieces(n, ssrc_ref[s] + (lo - sdst_ref[s]), lo - tile * MOE_TM,
                                  hl_ref, xbuf.at[slot, k], sem.at[slot])
                    return carry

                lax.fori_loop(tlo_ref[tile], thi_ref[tile], body, 0)

    def wait_step(step, slot):
        for k in range(MOE_PAIR):
            tile = step * MOE_PAIR + k
            total = lax.fori_loop(tlo_ref[tile], thi_ref[tile], lambda s, acc: acc + overlap(s, tile)[0], 0)
            _wait_rows(total, SEG_SIZES, hl_ref, xbuf.at[slot, k], sem.at[slot])

    @pl.when(i == 0)
    def _():
        xbuf[...] = jnp.zeros_like(xbuf)
        start_step(0, 0)

    @pl.when((i + 1) * MOE_PAIR < nv)
    def _():
        start_step(i + 1, (i + 1) % 2)

    @pl.when(i * MOE_PAIR >= nv)
    def _():
        y_ref[...] = jnp.zeros_like(y_ref)

    @pl.when(i * MOE_PAIR < nv)
    def _():
        slot = i % 2
        wait_step(i, slot)
        for k, (wg_ref, wu_ref, wd_ref) in enumerate(w_refs):
            x = xbuf[slot, k]
            hg = _dot(x, wg_ref[0])
            hu = _dot(x, wu_ref[0])
            hid = ((hg * _sigmoid(hg)) * hu).astype(BF16)
            y_ref[k * MOE_TM:(k + 1) * MOE_TM, :] = _dot(hid, wd_ref[0]).astype(BF16)


def _moe(cfg, hl, plan, w_gate, w_up, w_down):
    n_steps = cfg.moe_tiles // MOE_PAIR

    def w_map(k):
        return lambda i, tg, nv, *_: (tg[jnp.minimum(i * MOE_PAIR + k, nv[0] - 1)], 0, 0)

    w_gate = w_gate.reshape(N_EXPERTS, D_MODEL, D_FF)
    w_up = w_up.reshape(N_EXPERTS, D_MODEL, D_FF)
    w_down = w_down.reshape(N_EXPERTS, D_FF, D_MODEL)
    w_specs = []
    for k in range(MOE_PAIR):
        w_specs += [pl.BlockSpec((1, D_MODEL, D_FF), w_map(k)), pl.BlockSpec((1, D_MODEL, D_FF), w_map(k)),
                    pl.BlockSpec((1, D_FF, D_MODEL), w_map(k))]

    return pl.pallas_call(
        _moe_kernel,
        grid_spec=pltpu.PrefetchScalarGridSpec(
            num_scalar_prefetch=7,
            grid=(n_steps,),
            in_specs=[pl.BlockSpec(memory_space=pl.ANY)] + w_specs,
            out_specs=pl.BlockSpec((MOE_PAIR * MOE_TM, D_MODEL), lambda i, *_: (i, 0)),
            scratch_shapes=[pltpu.VMEM((2, MOE_PAIR, MOE_TM, D_MODEL), BF16), pltpu.SemaphoreType.DMA((2,))],
        ),
        out_shape=jax.ShapeDtypeStruct((n_steps * MOE_PAIR * MOE_TM, D_MODEL), BF16),
        compiler_params=_cparams(("arbitrary",)),
        name="moe",
    )(plan.tile_expert, plan.n_valid, plan.seg_tiled, plan.seg_sorted, plan.seg_len, plan.tile_lo, plan.tile_hi,
      hl, *([w_gate, w_up, w_down] * MOE_PAIR))


def _combine_kernel(stiled_ref, ssorted_ref, slen_ref, xm_ref, ys_ref, lp_ref, mod_ref, *refs,
                    n_tiles, n_ctx_tiles):
    o_refs, (ybuf, sem) = refs[:-2], refs[-2:]
    t = pl.program_id(0)

    def start_tile(tile, slot):
        def body(e, carry):
            s = e * n_tiles + tile
            _start_pieces(slen_ref[s], ssorted_ref[s], stiled_ref[s] - tile * LT,
                          ys_ref, ybuf.at[slot], sem.at[slot])
            return carry

        lax.fori_loop(0, N_EXPERTS, body, 0)

    def wait_tile(tile, slot):
        total = lax.fori_loop(0, N_EXPERTS, lambda e, acc: acc + slen_ref[e * n_tiles + tile], 0)
        _wait_rows(total, (2 * SEG_SIZES[0],) + SEG_SIZES, ys_ref, ybuf.at[slot], sem.at[slot])

    @pl.when(t == 0)
    def _():
        ybuf[...] = jnp.zeros_like(ybuf)
        start_tile(0, 0)

    @pl.when(t + 1 < n_tiles)
    def _():
        start_tile(t + 1, (t + 1) % 2)

    slot = t % 2
    wait_tile(t, slot)
    lp = lp_ref[...]
    pos_a = (lp[:, 0:1] * 32.0 + lp[:, 1:2]).astype(jnp.int32)
    pos_b = (lp[:, 2:3] * 32.0 + lp[:, 3:4]).astype(jnp.int32)
    gate_a = lp[:, 4:5] + lp[:, 5:6] + lp[:, 6:7]
    gate_b = lp[:, 7:8] + lp[:, 8:9] + lp[:, 9:10]
    cols = lax.broadcasted_iota(jnp.int32, (TM, LT), 1)
    mix = jnp.where(cols == pos_a, gate_a, jnp.where(cols == pos_b, gate_b, 0.0)).astype(BF16)
    y = _dot(mix, ybuf[slot])
    g2 = mod_ref[0, 0][5:6]
    x_new = xm_ref[...] + g2 * y
    if len(o_refs) == 2:
        _store_pair(t < n_ctx_tiles, *o_refs, x_new)
    else:
        o_refs[0][...] = x_new


def _combine(cfg, l, xm, ys, lp, mod, plan, paired_out):
    nt = cfg.n_tiles
    return pl.pallas_call(
        functools.partial(_combine_kernel, n_tiles=nt, n_ctx_tiles=cfg.n_ctx_tiles),
        grid_spec=pltpu.PrefetchScalarGridSpec(
            num_scalar_prefetch=3,
            grid=(nt,),
            in_specs=[
                pl.BlockSpec((TM, D_MODEL), lambda i, *_: (i, 0)),
                pl.BlockSpec(memory_space=pl.ANY),
                pl.BlockSpec((TM, LANES), lambda i, *_: (i, 0)),
                pl.BlockSpec((1, 1, 6, D_MODEL), lambda i, *_: (l, _mod_row(cfg, i), 0, 0)),
            ],
            out_specs=_stream_specs(cfg, D_MODEL, paired_out),
            scratch_shapes=[pltpu.VMEM((2, LT, D_MODEL), BF16), pltpu.SemaphoreType.DMA((2,))],
        ),
        out_shape=_stream_shapes(cfg, D_MODEL, F32, paired_out),
        compiler_params=_cparams(("arbitrary",)),
        name="combine",
    )(plan.seg_tiled, plan.seg_sorted, plan.seg_len, xm, ys, lp, mod)


class Plan(NamedTuple):
    seg_tiled: jax.Array
    seg_sorted: jax.Array
    seg_len: jax.Array
    tile_expert: jax.Array
    n_valid: jax.Array
    tile_lo: jax.Array
    tile_hi: jax.Array


def _plan(cfg, counts):
    nt = cfg.n_tiles
    m = (counts + (BF16_ROWS - 1)) // BF16_ROWS * BF16_ROWS
    local_off = jnp.cumsum(m, axis=1) - m
    within_expert = jnp.cumsum(m, axis=0) - m
    expert_rows = jnp.sum(m, axis=0)
    expert_tiles = (expert_rows + (MOE_TM - 1)) // MOE_TM
    tiles_end = jnp.cumsum(expert_tiles)
    expert_start = (tiles_end - expert_tiles) * MOE_TM
    seg_sorted = (expert_start[None, :] + within_expert).T.reshape(-1).astype(jnp.int32)
    seg_tiled = (jnp.arange(nt, dtype=jnp.int32)[:, None] * LT + local_off).T.reshape(-1).astype(jnp.int32)
    seg_len = m.T.reshape(-1).astype(jnp.int32)
    n_valid = tiles_end[-1:].astype(jnp.int32)
    tile_ids = jnp.arange(cfg.moe_tiles, dtype=jnp.int32)
    tile_expert = jnp.minimum(jnp.sum(tile_ids[:, None] >= tiles_end[None, :], axis=1),
                              N_EXPERTS - 1).astype(jnp.int32)
    tile_base = tile_ids * MOE_TM
    seg_end = seg_sorted + seg_len
    tile_lo = jnp.sum(seg_end[None, :] <= tile_base[:, None], axis=1).astype(jnp.int32)
    tile_hi = jnp.sum(seg_sorted[None, :] < tile_base[:, None] + MOE_TM, axis=1).astype(jnp.int32)
    return Plan(seg_tiled, seg_sorted, seg_len, tile_expert, n_valid, tile_lo, tile_hi)


def _rope_tables(n):
    pos = jnp.arange(n)
    rc = jnp.stack([(pos // GRID_W).astype(F32), (pos % GRID_W).astype(F32)], axis=1)
    freqs = ROPE_THETA ** (-jnp.arange(AXIS_PAIRS, dtype=F32) / AXIS_PAIRS)
    ang = rc[:, :, None] * freqs[None, None, :]
    cos = jnp.cos(ang)[:, :, None, :]
    sin = jnp.sin(ang)[:, :, None, :] * jnp.array([-1.0, 1.0], F32)[None, None, :, None]
    cos = jnp.broadcast_to(cos, (n, 2, 2, AXIS_PAIRS)).reshape(n, HEAD_DIM)
    sin = jnp.broadcast_to(sin, (n, 2, 2, AXIS_PAIRS)).reshape(n, HEAD_DIM)
    reps = LANES // HEAD_DIM
    cos = jnp.concatenate([jnp.ones((TM, HEAD_DIM), F32), cos], axis=0)
    sin = jnp.concatenate([jnp.zeros((TM, HEAD_DIM), F32), sin], axis=0)
    return jnp.tile(cos, (1, reps)), jnp.tile(sin, (1, reps))


def _block_diag_mean(width):
    r = jnp.arange(width)
    return jnp.where((r[:, None] // HEAD_DIM) == (r[None, :] // HEAD_DIM), 1.0 / HEAD_DIM, 0.0).astype(BF16)


def _forward(cfg, x_prompt, x_sample, cache_k, cache_v, c, c_ctx, w_mod, b_mod, norm1_g, norm2_g,
             w_in, q_norm_g, k_norm_g, conv_dw_w, conv_dw_b, conv_ln_g, conv_ln_b, w_out,
             w_router, b_router, w_gate, w_up, w_down):
    depth = cfg.depth
    assert cfg.ctx_seq == CONV_ROWS and cfg.t_ctx % TM == 0 and cfg.smp_seq % TM == 0
    assert cfg.t_ctx % cfg.smp_seq == 0 and cfg.smp_batch + 1 <= 8

    x = [x_prompt.reshape(cfg.t_ctx, D_MODEL), x_sample.reshape(-1, D_MODEL)]
    cvec = jnp.zeros((8, D_MODEL), F32).at[0].set(c_ctx).at[1:1 + cfg.smp_batch].set(c)
    mod = _modulation(cvec, w_mod, b_mod)

    n1 = norm1_g.reshape(depth, 1, D_MODEL)
    n2 = norm2_g.reshape(depth, 1, D_MODEL)
    gain = jnp.concatenate([jnp.tile(q_norm_g, (1, N_Q_HEADS)), jnp.tile(k_norm_g, (1, N_KV_HEADS))],
                           axis=1).reshape(depth, 1, QK_W)
    expert_w = (w_gate.reshape(-1, D_FF), w_up.reshape(-1, D_FF), w_down.reshape(-1, D_MODEL))
    taps = _conv_tap_rows(conv_dw_w)
    dw_b = conv_dw_b.reshape(depth, 1, CONV_CH)
    ln_g = conv_ln_g.reshape(depth, 1, CONV_CH)
    ln_b = conv_ln_b.reshape(depth, 1, CONV_CH)
    wr = jnp.pad(w_router, ((0, 0), (0, LANES - N_EXPERTS)))
    br = b_router.reshape(N_EXPERTS, 1)
    cos, sin = _rope_tables(cfg.smp_seq)
    bdq = _block_diag_mean(Q_W // 2)
    bdk = _block_diag_mean(KV_W)
    ck = cache_k.reshape(cfg.smp_batch, depth, cfg.past, KV_W)
    cv = cache_v.reshape(cfg.smp_batch, depth, cfg.past, KV_W)

    new_k = jnp.zeros((cfg.ctx_batch, depth, cfg.ctx_seq, KV_W), F32)
    new_v = jnp.zeros((cfg.ctx_batch, depth, cfg.ctx_seq, KV_W), F32)
    for l in range(depth):
        q, k, v, z, new_k, new_v = _inproj(cfg, l, x, mod, n1, w_in, gain, bdq, bdk, cos, sin, new_k, new_v)
        a_pair, c_pair, expert_bf = _mixer(cfg, l, q, k, v, ck, cv, z, taps, dw_b, ln_g, ln_b, expert_w)
        xm, hl, lp, cnt = _outproj(cfg, l, a_pair, c_pair, x, mod, n2, w_out, wr, br)
        plan = _plan(cfg, cnt[:, :, 0])
        ys = _moe(cfg, hl.reshape(cfg.n_tiles * LT, D_MODEL), plan, *expert_bf)
        x = _combine(cfg, l, xm, ys, lp, mod, plan, paired_out=(l == depth - 1))

    y_prompt = x[0].reshape(cfg.ctx_batch, cfg.ctx_seq, D_MODEL)
    y_sample = x[1].reshape(cfg.smp_batch, cfg.smp_seq, D_MODEL)
    cache_shape = (cfg.ctx_batch, depth, cfg.ctx_seq, N_KV_HEADS, HEAD_DIM)
    return y_prompt, y_sample, new_k.reshape(cache_shape), new_v.reshape(cache_shape)


def kernel(x_prompt, x_sample, cache_k, cache_v, c, c_ctx, w_mod, b_mod, norm1_g, norm2_g, w_in, q_norm_g,
           k_norm_g, conv_dw_w, conv_dw_b, conv_ln_g, conv_ln_b, w_out, w_router, b_router, w_gate, w_up, w_down):
    cfg = Cfg(ctx_batch=x_prompt.shape[0], ctx_seq=x_prompt.shape[1], smp_batch=x_sample.shape[0],
              smp_seq=x_sample.shape[1], past=cache_k.shape[2], depth=w_mod.shape[0])
    return _forward(cfg, x_prompt, x_sample, cache_k, cache_v, c, c_ctx, w_mod, b_mod, norm1_g, norm2_g,
                    w_in, q_norm_g, k_norm_g, conv_dw_w, conv_dw_b, conv_ln_g, conv_ln_b, w_out,
                    w_router, b_router, w_gate, w_up, w_down)
```

```python
import functools
import math
from typing import NamedTuple

import jax
import jax.numpy as jnp
from jax import lax
from jax.experimental import pallas as pl
from jax.experimental.pallas import tpu as pltpu

F32 = jnp.float32
BF16 = jnp.bfloat16

D_MODEL = 1024
HEAD_DIM = 64
N_Q_HEADS = 8
N_KV_HEADS = 2
Q_PER_KV = N_Q_HEADS // N_KV_HEADS
Q_W = N_Q_HEADS * HEAD_DIM
KV_W = N_KV_HEADS * HEAD_DIM
QK_W = Q_W + KV_W
CONV_CH = D_MODEL // 2
CONV_K = 31
CONV_PAD = CONV_K // 2
IN_COLS = Q_W + 2 * KV_W + 2 * CONV_CH
GRID_W = 64
AXIS_PAIRS = HEAD_DIM // 4
ROPE_THETA = 10000.0
N_EXPERTS = 16
N_GROUPS = 4
EXPERTS_PER_GROUP = N_EXPERTS // N_GROUPS
D_FF = 512
EPS = 1e-6

LANES = 128
BF16_ROWS = 16
TM = 512
TOP_K = 2
LT = 1280
MOE_TM = 512
MOE_PAIR = 4
CONV_ROWS = 256
CTX_SEQS = 4
HALO = 16
CONV_INNER = CONV_ROWS + 16
PAD_ROWS = 24 + CONV_INNER + 8
SEG_SIZES = (512, 256, 128, 64, 32, 16)
SEG_SPLIT = 128
Q_SCALE = (1.0 / math.sqrt(HEAD_DIM)) * math.log2(math.e)
VMEM_LIMIT = 48 * 1024 * 1024
MIXER_VMEM_LIMIT = 56 * 1024 * 1024


class Cfg(NamedTuple):
    ctx_batch: int
    ctx_seq: int
    smp_batch: int
    smp_seq: int
    past: int
    depth: int

    @property
    def t_ctx(self):
        return self.ctx_batch * self.ctx_seq

    @property
    def t_all(self):
        return self.t_ctx + self.smp_batch * self.smp_seq

    @property
    def n_tiles(self):
        return self.t_all // TM

    @property
    def n_ctx_tiles(self):
        return self.t_ctx // TM

    @property
    def smp_tiles(self):
        return self.smp_seq // TM

    @property
    def moe_tiles(self):
        rows = TOP_K * self.t_all + self.n_tiles * N_EXPERTS * (BF16_ROWS - 1) + N_EXPERTS * (MOE_TM - 1)
        step_rows = MOE_PAIR * MOE_TM
        return -(-rows // step_rows) * MOE_PAIR


def _mod_row(cfg, i):
    return jnp.where(i < cfg.n_ctx_tiles, 0, 1 + (i - cfg.n_ctx_tiles) // cfg.smp_tiles)


def _pair_specs(cfg, width, tile_of=lambda i: i):
    nct = cfg.n_ctx_tiles
    return [pl.BlockSpec((TM, width), lambda i, *_: (jnp.minimum(tile_of(i), nct - 1), 0)),
            pl.BlockSpec((TM, width), lambda i, *_: (jnp.maximum(tile_of(i) - nct, 0), 0))]


def _pair_shapes(cfg, width, dtype):
    return [jax.ShapeDtypeStruct((cfg.t_ctx, width), dtype),
            jax.ShapeDtypeStruct((cfg.t_all - cfg.t_ctx, width), dtype)]


def _load_pair(is_ctx, c_ref, s_ref):
    return jnp.where(is_ctx, c_ref[...], s_ref[...])


def _store_pair(is_ctx, c_ref, s_ref, val):
    @pl.when(is_ctx)
    def _():
        c_ref[...] = val

    @pl.when(jnp.logical_not(is_ctx))
    def _():
        s_ref[...] = val


def _stream_specs(cfg, width, paired, tile_of=lambda i: i):
    if paired:
        return _pair_specs(cfg, width, tile_of)
    return [pl.BlockSpec((TM, width), lambda i, *_: (tile_of(i), 0))]


def _stream_shapes(cfg, width, dtype, paired):
    if paired:
        return _pair_shapes(cfg, width, dtype)
    return [jax.ShapeDtypeStruct((cfg.t_all, width), dtype)]


def _load_stream(is_ctx, refs):
    return _load_pair(is_ctx, *refs) if len(refs) == 2 else refs[0][...]


def _cparams(sem, vmem=VMEM_LIMIT):
    return pltpu.CompilerParams(dimension_semantics=sem, vmem_limit_bytes=vmem)


def _dot(a, b):
    return jnp.dot(a, b, preferred_element_type=F32)


def _dot_nt(a, b):
    return lax.dot_general(a, b, (((1,), (1,)), ((), ())), preferred_element_type=F32)


def _sigmoid(x):
    return 1.0 / (1.0 + jnp.exp(-x))


def _mod_kernel(c_ref, w_ref, b_ref, o_ref):
    c = c_ref[...]
    a = (c * _sigmoid(c)).astype(BF16)
    o_ref[0] = _dot(a, w_ref[0].astype(BF16)) + b_ref[0]


def _modulation(cvec, w_mod, b_mod):
    depth = w_mod.shape[0]
    out = pl.pallas_call(
        _mod_kernel,
        grid=(depth, 6),
        in_specs=[
            pl.BlockSpec((8, D_MODEL), lambda l, j: (0, 0)),
            pl.BlockSpec((1, D_MODEL, D_MODEL), lambda l, j: (l, 0, j)),
            pl.BlockSpec((1, 1, D_MODEL), lambda l, j: (l, 0, j)),
        ],
        out_specs=pl.BlockSpec((1, 8, D_MODEL), lambda l, j: (l, 0, j)),
        out_shape=jax.ShapeDtypeStruct((depth, 8, 6 * D_MODEL), F32),
        compiler_params=_cparams(("arbitrary", "arbitrary")),
        name="modulation",
    )(cvec, w_mod, b_mod.reshape(depth, 1, 6 * D_MODEL))
    return out.reshape(depth, 8, 6, D_MODEL)


def _inproj_kernel(*refs, n_tiles, n_ctx_tiles, ctx_seq, n_x):
    x_refs = refs[:n_x]
    (mod_ref, n1_ref, w_ref, gain_ref, bdq_ref, bdk_ref, cos_ref, sin_ref,
     kc_in, vc_in, q_ref, k_ref, v_ref, z_ref, kc_ref, vc_ref, w_bf, proj_prev) = refs[n_x:]
    del kc_in, vc_in
    step = pl.program_id(0)

    @pl.when(step == 0)
    def _():
        w_bf[...] = w_ref[0].astype(BF16)
        proj_prev[...] = jnp.zeros_like(proj_prev)

    x = _load_stream(jnp.minimum(step, n_tiles - 1) < n_ctx_tiles, x_refs)
    mod = mod_ref[0, 0]
    sh1, sc1 = mod[0:1], mod[1:2]
    ms = jnp.mean(x * x, axis=-1, keepdims=True)
    h = (x * lax.rsqrt(ms + EPS) * n1_ref[0]) * (1.0 + sc1) + sh1
    proj_new = _dot(h.astype(BF16), w_bf[...])

    qk = proj_prev[:, :QK_W]
    sq = (qk * qk).astype(BF16)
    half = Q_W // 2
    msq = jnp.concatenate([_dot(sq[:, :half], bdq_ref[...]),
                           _dot(sq[:, half:Q_W], bdq_ref[...]),
                           _dot(sq[:, Q_W:], bdk_ref[...])], axis=1)
    qkn = qk * lax.rsqrt(msq + EPS) * gain_ref[0]
    v = proj_prev[:, QK_W:QK_W + KV_W]
    v_ref[...] = v
    a = proj_prev[:, QK_W + KV_W:QK_W + KV_W + CONV_CH]
    b = proj_prev[:, QK_W + KV_W + CONV_CH:]
    z_ref[...] = a * _sigmoid(b)
    cos = cos_ref[...]
    sin = sin_ref[...]
    first = (lax.broadcasted_iota(jnp.int32, (TM, LANES), 1) % 32) < 16
    for c in range(QK_W // LANES):
        blk = qkn[:, c * LANES:(c + 1) * LANES]
        partner = jnp.where(first, pltpu.roll(blk, LANES - 16, 1), pltpu.roll(blk, 16, 1))
        rot = blk * cos + partner * sin
        if c < Q_W // LANES:
            q_ref[:, c * LANES:(c + 1) * LANES] = (rot * Q_SCALE).astype(BF16)
        else:
            k = rot
            k_ref[...] = k

    proj_prev[...] = proj_new

    @pl.when(step - 1 < n_ctx_tiles)
    def _():
        for s in range(TM // ctx_seq):
            kc_ref[s, 0] = k[s * ctx_seq:(s + 1) * ctx_seq]
            vc_ref[s, 0] = v[s * ctx_seq:(s + 1) * ctx_seq]


def _inproj(cfg, l, x, mod, n1, w_in, gain, bdq, bdk, cos, sin, kc, vc):
    t = cfg.t_all
    n_x = len(x)
    nt, nct, st = cfg.n_tiles, cfg.n_ctx_tiles, cfg.smp_tiles
    tile_a = lambda i: jnp.minimum(i, nt - 1)
    tile_b = lambda i: jnp.maximum(i - 1, 0)
    rope_map = lambda i: (jnp.where(tile_b(i) < nct, 0, 1 + (tile_b(i) - nct) % st), 0)
    seqs = TM // cfg.ctx_seq
    cache_spec = lambda: pl.BlockSpec((seqs, 1, cfg.ctx_seq, KV_W),
                                      lambda i: (jnp.minimum(tile_b(i), nct - 1), l, 0, 0))
    return pl.pallas_call(
        functools.partial(_inproj_kernel, n_tiles=nt, n_ctx_tiles=nct, ctx_seq=cfg.ctx_seq, n_x=n_x),
        grid=(nt + 1,),
        in_specs=_stream_specs(cfg, D_MODEL, n_x == 2, tile_a) + [
            pl.BlockSpec((1, 1, 6, D_MODEL), lambda i: (l, _mod_row(cfg, tile_a(i)), 0, 0)),
            pl.BlockSpec((1, 1, D_MODEL), lambda i: (l, 0, 0)),
            pl.BlockSpec((1, D_MODEL, IN_COLS), lambda i: (l, 0, 0)),
            pl.BlockSpec((1, 1, QK_W), lambda i: (l, 0, 0)),
            pl.BlockSpec((Q_W // 2, Q_W // 2), lambda i: (0, 0)),
            pl.BlockSpec((KV_W, KV_W), lambda i: (0, 0)),
            pl.BlockSpec((TM, LANES), rope_map),
            pl.BlockSpec((TM, LANES), rope_map),
            pl.BlockSpec(memory_space=pl.ANY),
            pl.BlockSpec(memory_space=pl.ANY),
        ],
        out_specs=[
            pl.BlockSpec((TM, Q_W), lambda i: (tile_b(i), 0)),
            pl.BlockSpec((TM, KV_W), lambda i: (tile_b(i), 0)),
            pl.BlockSpec((TM, KV_W), lambda i: (tile_b(i), 0)),
            pl.BlockSpec((TM, CONV_CH), lambda i: (tile_b(i), 0)),
            cache_spec(), cache_spec(),
        ],
        out_shape=[
            jax.ShapeDtypeStruct((t, Q_W), BF16),
            jax.ShapeDtypeStruct((t, KV_W), F32),
            jax.ShapeDtypeStruct((t, KV_W), F32),
            jax.ShapeDtypeStruct((t, CONV_CH), F32),
            jax.ShapeDtypeStruct(kc.shape, F32),
            jax.ShapeDtypeStruct(vc.shape, F32),
        ],
        scratch_shapes=[pltpu.VMEM((D_MODEL, IN_COLS), BF16), pltpu.VMEM((TM, IN_COLS), F32)],
        input_output_aliases={n_x + 8: 4, n_x + 9: 5},
        compiler_params=_cparams(("arbitrary",)),
        name="inproj",
    )(*x, mod, n1, w_in, gain, bdq, bdk, cos, sin, kc, vc)


def _mixer_kernel(*refs, rq, n_past, seq_blocks, n_seq):
    if n_past:
        q_ref, k_ref, v_ref, ck_ref, cv_ref = refs[:5]
        refs = refs[5:]
    else:
        q_ref, k_ref, v_ref = refs[:3]
        refs = refs[3:]
    z_ref, zp_ref, zn_ref, w_ref, b_ref, g_ref, beta_ref = refs[:7]
    refs = refs[7:]
    if n_past:
        for src, dst in zip(refs[0:3], refs[5:8]):
            dst[...] = src[...].astype(BF16)
        refs = refs[3:5] + refs[8:]
    o_ref, co_ref, krep, vrep, pad, taps = refs
    head_w = Q_PER_KV * HEAD_DIM
    j = pl.program_id(1)

    @pl.when(jnp.logical_and(pl.program_id(0) == 0, j == 0))
    def _():
        diag = (lax.broadcasted_iota(jnp.int32, (LANES, LANES), 0)
                == lax.broadcasted_iota(jnp.int32, (LANES, LANES), 1))
        for c in range(CONV_CH // LANES):
            for p in range(4):
                for q in range(4):
                    for half in range(2):
                        tap = w_ref[0, 8 * q + 2 * p + half:8 * q + 2 * p + half + 1, c * LANES:(c + 1) * LANES]
                        taps[c, p, q * LANES:(q + 1) * LANES, half * LANES:(half + 1) * LANES] = (
                            jnp.where(diag, tap, 0.0).astype(BF16))

    n_new = k_ref.shape[0] // n_seq

    @pl.when(j == 0)
    def _():
        src = lax.broadcasted_iota(jnp.int32, (KV_W, head_w), 0)
        dst = lax.broadcasted_iota(jnp.int32, (KV_W, head_w), 1) % HEAD_DIM
        for h in range(N_KV_HEADS):
            rep = jnp.where(src == dst + h * HEAD_DIM, 1.0, 0.0).astype(BF16)
            if n_past:
                krep[0, h, :n_past, :] = _dot(ck_ref[0, 0].astype(BF16), rep).astype(BF16)
                vrep[0, h, :n_past, :] = _dot(cv_ref[0, 0].astype(BF16), rep).astype(BF16)
            for s in range(n_seq):
                rows = slice(s * n_new, (s + 1) * n_new)
                krep[s, h, n_past:n_past + n_new, :] = _dot(k_ref[rows, :].astype(BF16), rep).astype(BF16)
                vrep[s, h, n_past:n_past + n_new, :] = _dot(v_ref[rows, :].astype(BF16), rep).astype(BF16)

    lane_head = lax.broadcasted_iota(jnp.int32, (rq, head_w), 1) // HEAD_DIM
    end = HALO + CONV_ROWS
    for s in range(n_seq):
        rows = slice(s * rq, (s + 1) * rq)
        if seq_blocks > 1:
            pad[0:HALO, :] = jnp.where(j > 0, zp_ref[...], 0.0)
            pad[end:end + HALO, :] = jnp.where(j < seq_blocks - 1, zn_ref[...], 0.0)
        else:
            pad[0:HALO, :] = jnp.zeros((HALO, CONV_CH), F32)
            pad[end:end + HALO, :] = jnp.zeros((HALO, CONV_CH), F32)
        pad[end + HALO:, :] = jnp.zeros((PAD_ROWS - end - HALO, CONV_CH), F32)
        pad[HALO:end, :] = z_ref[rows, :]

        for h in range(N_KV_HEADS):
            qh = q_ref[rows, h * head_w:(h + 1) * head_w]
            kh = krep[s, h]
            vh = vrep[s, h]
            acc = jnp.zeros((rq, head_w), F32)
            for g in range(Q_PER_KV):
                mask = lane_head == g
                qm = jnp.where(mask, qh, jnp.zeros_like(qh))
                sc = _dot_nt(qm, kh)
                m = jnp.max(sc, axis=-1, keepdims=True)
                p = jnp.exp2(sc - m)
                denom = jnp.sum(p, axis=-1, keepdims=True)
                o = _dot(p.astype(BF16), vh)
                acc = jnp.where(mask, o * (1.0 / denom), acc)
            o_ref[rows, h * head_w:(h + 1) * head_w] = acc.astype(BF16)
        conv_cols = [_conv_cols(pad, taps, c) for c in range(CONV_CH // LANES)]
        co_ref[rows, :] = _conv_finish(conv_cols, b_ref, g_ref, beta_ref)


def _mixer(cfg, l, q, k, v, cache_k, cache_v, z, taps, dw_b, ln_g, ln_b, expert_w):
    head_w = Q_PER_KV * HEAD_DIM
    rq = CONV_ROWS
    hb = CONV_ROWS // HALO
    last = cfg.t_all // HALO - 1
    vec = lambda: pl.BlockSpec((1, 1, CONV_CH), lambda b, j: (l, 0, 0))

    def conv_specs(row_block, n_seq=1):
        return [
            pl.BlockSpec((n_seq * CONV_ROWS, CONV_CH), lambda b, j: (row_block(b, j), 0)),
            pl.BlockSpec((HALO, CONV_CH), lambda b, j: (jnp.maximum(row_block(b, j) * hb - 1, 0), 0)),
            pl.BlockSpec((HALO, CONV_CH), lambda b, j: (jnp.minimum(row_block(b, j) * hb + hb, last), 0)),
            pl.BlockSpec((1, 32, CONV_CH), lambda b, j: (l, 0, 0)),
            vec(), vec(), vec(),
        ]

    def scratch(sk, n_seq=1):
        return [pltpu.VMEM((n_seq, N_KV_HEADS, sk, head_w), BF16), pltpu.VMEM((n_seq, N_KV_HEADS, sk, head_w), BF16),
                pltpu.VMEM((PAD_ROWS, CONV_CH), F32),
                pltpu.VMEM((CONV_CH // LANES, 4, 4 * LANES, 2 * LANES), BF16)]

    sc = cfg.ctx_seq
    rows = CTX_SEQS * sc
    a_ctx, c_ctx = pl.pallas_call(
        functools.partial(_mixer_kernel, rq=rq, n_past=0, seq_blocks=1, n_seq=CTX_SEQS),
        grid=(cfg.ctx_batch // CTX_SEQS, 1),
        in_specs=[
            pl.BlockSpec((rows, Q_W), lambda b, j: (b, 0)),
            pl.BlockSpec((rows, KV_W), lambda b, j: (b, 0)),
            pl.BlockSpec((rows, KV_W), lambda b, j: (b, 0)),
        ] + conv_specs(lambda b, j: b, CTX_SEQS),
        out_specs=[pl.BlockSpec((rows, Q_W), lambda b, j: (b, 0)),
                   pl.BlockSpec((rows, CONV_CH), lambda b, j: (b, 0))],
        out_shape=[jax.ShapeDtypeStruct((cfg.t_ctx, Q_W), BF16),
                   jax.ShapeDtypeStruct((cfg.t_ctx, CONV_CH), BF16)],
        scratch_shapes=scratch(sc, CTX_SEQS),
        compiler_params=_cparams(("arbitrary", "arbitrary")),
        name="mixer_ctx",
    )(q, k, v, z, z, z, taps, dw_b, ln_g, ln_b)
    ss = cfg.smp_seq
    sk = cfg.past + ss
    kv_base = cfg.t_ctx // ss
    q_base = cfg.t_ctx // rq
    nqb = ss // rq
    t_smp = cfg.smp_batch * ss
    steps = cfg.smp_batch * nqb
    w_rows = [w.shape[0] // (cfg.depth * steps) for w in expert_w]
    w_specs = lambda off: [pl.BlockSpec((r, w.shape[1]), lambda b, j: (off + b * nqb + j, 0))
                           for r, w in zip(w_rows, expert_w)]
    a_smp, c_smp, *expert_bf = pl.pallas_call(
        functools.partial(_mixer_kernel, rq=rq, n_past=cfg.past, seq_blocks=nqb, n_seq=1),
        grid=(cfg.smp_batch, nqb),
        in_specs=[
            pl.BlockSpec((rq, Q_W), lambda b, j: (q_base + b * nqb + j, 0)),
            pl.BlockSpec((ss, KV_W), lambda b, j: (kv_base + b, 0)),
            pl.BlockSpec((ss, KV_W), lambda b, j: (kv_base + b, 0)),
            pl.BlockSpec((1, 1, cfg.past, KV_W), lambda b, j: (b, l, 0, 0)),
            pl.BlockSpec((1, 1, cfg.past, KV_W), lambda b, j: (b, l, 0, 0)),
        ] + conv_specs(lambda b, j: q_base + b * nqb + j) + w_specs(l * steps),
        out_specs=[pl.BlockSpec((rq, Q_W), lambda b, j: (b * nqb + j, 0)),
                   pl.BlockSpec((rq, CONV_CH), lambda b, j: (b * nqb + j, 0))] + w_specs(0),
        out_shape=[jax.ShapeDtypeStruct((t_smp, Q_W), BF16),
                   jax.ShapeDtypeStruct((t_smp, CONV_CH), BF16)]
        + [jax.ShapeDtypeStruct((r * steps, w.shape[1]), BF16) for r, w in zip(w_rows, expert_w)],
        scratch_shapes=scratch(sk),
        compiler_params=_cparams(("arbitrary", "arbitrary"), MIXER_VMEM_LIMIT),
        name="mixer_smp",
    )(q, k, v, cache_k, cache_v, z, z, z, taps, dw_b, ln_g, ln_b, *expert_w)
    return (a_ctx, a_smp), (c_ctx, c_smp), expert_bf


def _conv_cols(pad, taps, c):
    cols = slice(c * LANES, (c + 1) * LANES)
    lhs = jnp.concatenate([pad[8 * q:8 * q + CONV_INNER, cols].astype(BF16) for q in range(4)], axis=1)
    acc = None
    for p in range(4):
        pair = _dot(lhs, taps[c, p])
        for half in range(2):
            o = 2 * p + half
            shifted = pair[o:o + CONV_ROWS, half * LANES:(half + 1) * LANES]
            acc = shifted if acc is None else acc + shifted
    return acc


def _conv_tap_rows(conv_dw_w):
    return jnp.pad(conv_dw_w, ((0, 0), (HALO - CONV_PAD, 32 - CONV_K - (HALO - CONV_PAD)), (0, 0)))


def _conv_finish(blocks, b_ref, g_ref, beta_ref):
    y = jnp.concatenate(blocks, axis=1) + b_ref[0]
    mu = jnp.mean(y, axis=-1, keepdims=True)
    yc = y - mu
    var = jnp.mean(yc * yc, axis=-1, keepdims=True)
    yn = yc * lax.rsqrt(var + EPS) * g_ref[0] + beta_ref[0]
    return (yn * _sigmoid(yn)).astype(BF16)


def _split_bf16(x):
    hi = x.astype(BF16)
    lo = (x - hi.astype(F32)).astype(BF16)
    return hi, lo


def _outproj_kernel(ac_ref, as_ref, cc_ref, cs_ref, *refs, n_tiles, n_ctx_tiles, n_x):
    x_refs = refs[:n_x]
    (mod_ref, n2_ref, wo_ref, wr_ref, br_ref, xm_ref, hl_ref, lp_ref, cnt_ref,
     wo_bf, h_prev, lt_prev, before_bf, eye_bf, rows_i) = refs[n_x:]
    step = pl.program_id(0)

    @pl.when(step == 0)
    def _():
        wo_bf[...] = wo_ref[0].astype(BF16)
        h_prev[...] = jnp.zeros_like(h_prev)
        lt_prev[...] = jnp.zeros_like(lt_prev)
        r = lax.broadcasted_iota(jnp.int32, (TM, TM), 0)
        c = lax.broadcasted_iota(jnp.int32, (TM, TM), 1)
        before_bf[...] = jnp.where(r < c, 1.0, 0.0).astype(BF16)
        eye_bf[...] = jnp.where(r == c, 1.0, 0.0).astype(BF16)
        rows_i[...] = lax.broadcasted_iota(jnp.int32, (LT, TM), 0)

    is_ctx = jnp.minimum(step, n_tiles - 1) < n_ctx_tiles
    ac = jnp.concatenate([_load_pair(is_ctx, ac_ref, as_ref), _load_pair(is_ctx, cc_ref, cs_ref)], axis=1)
    y = _dot(ac, wo_bf[...])
    mod = mod_ref[0, 0]
    g1, sh2, sc2 = mod[2:3], mod[3:4], mod[4:5]
    xm = _load_stream(is_ctx, x_refs) + g1 * y
    xm_ref[...] = xm
    ms = jnp.mean(xm * xm, axis=-1, keepdims=True)
    h2 = (xm * lax.rsqrt(ms + EPS) * n2_ref[0]) * (1.0 + sc2) + sh2
    h_new, h_lo = _split_bf16(h2)
    w_hi, w_lo = _split_bf16(wr_ref[...])
    logits = _dot(h_new, w_hi) + (_dot(h_new, w_lo) + _dot(h_lo, w_hi))
    lt_new = logits.T[0:N_EXPERTS, :]

    h_hi = h_prev[...]
    s_all = _sigmoid(lt_prev[...])
    sb_all = s_all + br_ref[...]

    rows8 = 2 * EXPERTS_PER_GROUP
    row = lax.broadcasted_iota(jnp.int32, (rows8, TM), 0)
    member = row % EXPERTS_PER_GROUP
    second = row >= EXPERTS_PER_GROUP

    def partner(x, k):
        return jnp.where(member + k < EXPERTS_PER_GROUP,
                         pltpu.roll(x, rows8 - k, 0), pltpu.roll(x, EXPERTS_PER_GROUP - k, 0))

    def other_group(x):
        return pltpu.roll(x, EXPERTS_PER_GROUP, 0)

    def group_sum(x):
        return x + partner(x, 1) + (partner(x, 2) + partner(x, 3))

    def ahead(other, mine, other_first):
        return jnp.where((other >= mine) if other_first else (other > mine), 1.0, 0.0)

    sel, gscore = [], []
    for hv in range(2):
        sb8 = sb_all[hv * rows8:(hv + 1) * rows8]
        rank = jnp.zeros((rows8, TM), F32)
        for k in range(1, EXPERTS_PER_GROUP):
            pk = partner(sb8, k)
            rank = rank + jnp.where(member + k >= EXPERTS_PER_GROUP, ahead(pk, sb8, True), ahead(pk, sb8, False))
        sel.append(rank < 1.5)
        gscore.append(group_sum(jnp.where(sel[hv], sb8, 0.0)))
    gate8, used8 = [], []
    for hv in range(2):
        mine, far = gscore[hv], gscore[1 - hv]
        near = other_group(mine)
        rank = jnp.where(second, ahead(near, mine, True), ahead(near, mine, False))
        rank = rank + ahead(far, mine, hv == 1) + ahead(other_group(far), mine, hv == 1)
        chosen = rank < 0.5
        used8.append(jnp.where(chosen, jnp.where(sel[hv], 1.0, 0.0), 0.0))
        gate8.append(used8[hv] * s_all[hv * rows8:(hv + 1) * rows8])
    total = group_sum(gate8[0]) + group_sum(gate8[1])
    total = total + other_group(total)
    inv_total = 1.0 / total
    gate16 = jnp.concatenate([gate8[0] * inv_total, gate8[1] * inv_total], axis=0)
    used16 = jnp.concatenate(used8, axis=0)
    used = used16 > 0.5

    prior = _dot(used16.astype(BF16), before_bf[...])
    n_e = jnp.sum(used16, axis=-1, keepdims=True)
    m_e = jnp.floor((n_e + (BF16_ROWS - 1)) * (1.0 / BF16_ROWS)) * BF16_ROWS
    lower = (lax.broadcasted_iota(jnp.int32, (N_EXPERTS, N_EXPERTS), 0)
             > lax.broadcasted_iota(jnp.int32, (N_EXPERTS, N_EXPERTS), 1))
    start = _dot(jnp.where(lower, 1.0, 0.0).astype(BF16),
                 jnp.broadcast_to(m_e, (N_EXPERTS, LANES)).astype(BF16))[:, 0:1]
    pos16 = start + prior
    pos_a = jnp.min(jnp.where(used, pos16, float(LT)), axis=0, keepdims=True)
    pos_b = jnp.max(jnp.where(used, pos16, -1.0), axis=0, keepdims=True)
    gate_a = jnp.sum(jnp.where(used, jnp.where(pos16 == pos_a, gate16, 0.0), 0.0), axis=0, keepdims=True)
    gate_b = jnp.sum(jnp.where(used, jnp.where(pos16 == pos_b, gate16, 0.0), 0.0), axis=0, keepdims=True)
    rows = rows_i[...]
    perm = jnp.where(rows == pos_a.astype(jnp.int32), 1.0,
                     jnp.where(rows == pos_b.astype(jnp.int32), 1.0, 0.0)).astype(BF16)
    hl_ref[0] = _dot(perm, h_hi).astype(BF16)

    def split32(p):
        hi = jnp.floor(p * (1.0 / 32.0))
        return [hi, p - 32.0 * hi]

    def split3(g):
        hi = g.astype(BF16).astype(F32)
        mid = (g - hi).astype(BF16).astype(F32)
        lo = ((g - hi) - mid).astype(BF16).astype(F32)
        return [hi, mid, lo]

    parts = split32(pos_a) + split32(pos_b) + split3(gate_a) + split3(gate_b)
    lp_parts = jnp.concatenate(parts + [jnp.zeros((LANES - len(parts), TM), F32)], axis=0).astype(BF16)
    lp_ref[...] = _dot_nt(eye_bf[...], lp_parts)

    cnt_ref[0] = jnp.broadcast_to(n_e, (N_EXPERTS, LANES)).astype(jnp.int32)

    h_prev[...] = h_new
    lt_prev[...] = lt_new


def _outproj(cfg, l, a_pair, c_pair, x, mod, n2, w_out, w_router, b_router):
    t, nt, nct = cfg.t_all, cfg.n_tiles, cfg.n_ctx_tiles
    n_x = len(x)
    tile_a = lambda i: jnp.minimum(i, nt - 1)
    tile_b = lambda i: jnp.maximum(i - 1, 0)
    return pl.pallas_call(
        functools.partial(_outproj_kernel, n_tiles=nt, n_ctx_tiles=nct, n_x=n_x),
        grid=(nt + 1,),
        in_specs=_pair_specs(cfg, Q_W, tile_a) + _pair_specs(cfg, CONV_CH, tile_a)
        + _stream_specs(cfg, D_MODEL, n_x == 2, tile_a) + [
            pl.BlockSpec((1, 1, 6, D_MODEL), lambda i: (l, _mod_row(cfg, tile_a(i)), 0, 0)),
            pl.BlockSpec((1, 1, D_MODEL), lambda i: (l, 0, 0)),
            pl.BlockSpec((1, D_MODEL, D_MODEL), lambda i: (l, 0, 0)),
            pl.BlockSpec((D_MODEL, LANES), lambda i: (0, 0)),
            pl.BlockSpec((N_EXPERTS, 1), lambda i: (0, 0)),
        ],
        out_specs=_stream_specs(cfg, D_MODEL, False, tile_a) + [
            pl.BlockSpec((1, LT, D_MODEL), lambda i: (tile_b(i), 0, 0)),
            pl.BlockSpec((TM, LANES), lambda i: (tile_b(i), 0)),
            pl.BlockSpec((1, N_EXPERTS, LANES), lambda i: (tile_b(i), 0, 0)),
        ],
        out_shape=_stream_shapes(cfg, D_MODEL, F32, False) + [
            jax.ShapeDtypeStruct((nt, LT, D_MODEL), BF16),
            jax.ShapeDtypeStruct((t, LANES), F32),
            jax.ShapeDtypeStruct((nt, N_EXPERTS, LANES), jnp.int32),
        ],
        scratch_shapes=[pltpu.VMEM((D_MODEL, D_MODEL), BF16), pltpu.VMEM((TM, D_MODEL), BF16),
                        pltpu.VMEM((N_EXPERTS, TM), F32), pltpu.VMEM((TM, TM), BF16), pltpu.VMEM((TM, TM), BF16),
                        pltpu.VMEM((LT, TM), jnp.int32)],
        compiler_params=_cparams(("arbitrary",)),
        name="outproj",
    )(*a_pair, *c_pair, *x, mod, n2, w_out, w_router, b_router)


def _start_pieces(n, src_row, dst_row, src_ref, dst_ref, sem):
    def piece(size):
        off = n & ~(2 * size - 1)

        @pl.when((n & size) != 0)
        def _():
            pltpu.make_async_copy(
                src_ref.at[pl.ds(pl.multiple_of(src_row + off, BF16_ROWS), size)],
                dst_ref.at[pl.ds(pl.multiple_of(dst_row + off, BF16_ROWS), size)],
                sem).start()

    @pl.when(n >= SEG_SPLIT)
    def _():
        for size in SEG_SIZES:
            if size >= SEG_SPLIT:
                piece(size)

    for size in SEG_SIZES:
        if size < SEG_SPLIT:
            piece(size)


def _wait_rows(total, sizes, src_ref, dst_ref, sem):
    for size in sizes:
        @pl.when((total & size) != 0)
        def _():
            pltpu.make_async_copy(src_ref.at[pl.ds(0, size)], dst_ref.at[pl.ds(0, size)], sem).wait()


def _moe_kernel(tg_ref, nv_ref, ssrc_ref, sdst_ref, slen_ref, tlo_ref, thi_ref, hl_ref, *refs):
    del tg_ref
    w_refs = [refs[3 * k:3 * k + 3] for k in range(MOE_PAIR)]
    y_ref, xbuf, sem = refs[3 * MOE_PAIR:]
    i = pl.program_id(0)
    nv = nv_ref[0]

    def overlap(s, tile):
        base = tile * MOE_TM
        d = sdst_ref[s]
        lo = jnp.maximum(d, base)
        hi = jnp.minimum(d + slen_ref[s], base + MOE_TM)
        return jnp.maximum(hi - lo, 0), lo

    def start_step(step, slot):
        for k in range(MOE_PAIR):
            tile = step * MOE_PAIR + k

            @pl.when(tile < nv)
            def _():
                def body(s, carry):
                    n, lo = overlap(s, tile)
                    _start_pieces(n, ssrc_ref[s] + (lo - sdst_ref[s]), lo - tile * MOE_TM,
                                  hl_ref, xbuf.at[slot, k], sem.at[slot])
                    return carry

                lax.fori_loop(tlo_ref[tile], thi_ref[tile], body, 0)

    def wait_step(step, slot):
        for k in range(MOE_PAIR):
            tile = step * MOE_PAIR + k
            total = lax.fori_loop(tlo_ref[tile], thi_ref[tile], lambda s, acc: acc + overlap(s, tile)[0], 0)
            _wait_rows(total, SEG_SIZES, hl_ref, xbuf.at[slot, k], sem.at[slot])

    @pl.when(i == 0)
    def _():
        xbuf[...] = jnp.zeros_like(xbuf)
        start_step(0, 0)

    @pl.when((i + 1) * MOE_PAIR < nv)
    def _():
        start_step(i + 1, (i + 1) % 2)

    @pl.when(i * MOE_PAIR >= nv)
    def _():
        y_ref[...] = jnp.zeros_like(y_ref)

    @pl.when(i * MOE_PAIR < nv)
    def _():
        slot = i % 2
        wait_step(i, slot)
        for k, (wg_ref, wu_ref, wd_ref) in enumerate(w_refs):
            x = xbuf[slot, k]
            hg = _dot(x, wg_ref[0])
            hu = _dot(x, wu_ref[0])
            hid = ((hg * _sigmoid(hg)) * hu).astype(BF16)
            y_ref[k * MOE_TM:(k + 1) * MOE_TM, :] = _dot(hid, wd_ref[0]).astype(BF16)


def _moe(cfg, hl, plan, w_gate, w_up, w_down):
    n_steps = cfg.moe_tiles // MOE_PAIR

    def w_map(k):
        return lambda i, tg, nv, *_: (tg[jnp.minimum(i * MOE_PAIR + k, nv[0] - 1)], 0, 0)

    w_gate = w_gate.reshape(N_EXPERTS, D_MODEL, D_FF)
    w_up = w_up.reshape(N_EXPERTS, D_MODEL, D_FF)
    w_down = w_down.reshape(N_EXPERTS, D_FF, D_MODEL)
    w_specs = []
    for k in range(MOE_PAIR):
        w_specs += [pl.BlockSpec((1, D_MODEL, D_FF), w_map(k)), pl.BlockSpec((1, D_MODEL, D_FF), w_map(k)),
                    pl.BlockSpec((1, D_FF, D_MODEL), w_map(k))]

    return pl.pallas_call(
        _moe_kernel,
        grid_spec=pltpu.PrefetchScalarGridSpec(
            num_scalar_prefetch=7,
            grid=(n_steps,),
            in_specs=[pl.BlockSpec(memory_space=pl.ANY)] + w_specs,
            out_specs=pl.BlockSpec((MOE_PAIR * MOE_TM, D_MODEL), lambda i, *_: (i, 0)),
            scratch_shapes=[pltpu.VMEM((2, MOE_PAIR, MOE_TM, D_MODEL), BF16), pltpu.SemaphoreType.DMA((2,))],
        ),
        out_shape=jax.ShapeDtypeStruct((n_steps * MOE_PAIR * MOE_TM, D_MODEL), BF16),
        compiler_params=_cparams(("arbitrary",)),
        name="moe",
    )(plan.tile_expert, plan.n_valid, plan.seg_tiled, plan.seg_sorted, plan.seg_len, plan.tile_lo, plan.tile_hi,
      hl, *([w_gate, w_up, w_down] * MOE_PAIR))


def _combine_kernel(stiled_ref, ssorted_ref, slen_ref, xm_ref, ys_ref, lp_ref, mod_ref, *refs,
                    n_tiles, n_ctx_tiles):
    o_refs, (ybuf, sem) = refs[:-2], refs[-2:]
    t = pl.program_id(0)

    def start_tile(tile, slot):
        def body(e, carry):
            s = e * n_tiles + tile
            _start_pieces(slen_ref[s], ssorted_ref[s], stiled_ref[s] - tile * LT,
                          ys_ref, ybuf.at[slot], sem.at[slot])
            return carry

        lax.fori_loop(0, N_EXPERTS, body, 0)

    def wait_tile(tile, slot):
        total = lax.fori_loop(0, N_EXPERTS, lambda e, acc: acc + slen_ref[e * n_tiles + tile], 0)
        _wait_rows(total, (2 * SEG_SIZES[0],) + SEG_SIZES, ys_ref, ybuf.at[slot], sem.at[slot])

    @pl.when(t == 0)
    def _():
        ybuf[...] = jnp.zeros_like(ybuf)
        start_tile(0, 0)

    @pl.when(t + 1 < n_tiles)
    def _():
        start_tile(t + 1, (t + 1) % 2)

    slot = t % 2
    wait_tile(t, slot)
    lp = lp_ref[...]
    pos_a = (lp[:, 0:1] * 32.0 + lp[:, 1:2]).astype(jnp.int32)
    pos_b = (lp[:, 2:3] * 32.0 + lp[:, 3:4]).astype(jnp.int32)
    gate_a = lp[:, 4:5] + lp[:, 5:6] + lp[:, 6:7]
    gate_b = lp[:, 7:8] + lp[:, 8:9] + lp[:, 9:10]
    cols = lax.broadcasted_iota(jnp.int32, (TM, LT), 1)
    mix = jnp.where(cols == pos_a, gate_a, jnp.where(cols == pos_b, gate_b, 0.0)).astype(BF16)
    y = _dot(mix, ybuf[slot])
    g2 = mod_ref[0, 0][5:6]
    x_new = xm_ref[...] + g2 * y
    if len(o_refs) == 2:
        _store_pair(t < n_ctx_tiles, *o_refs, x_new)
    else:
        o_refs[0][...] = x_new


def _combine(cfg, l, xm, ys, lp, mod, plan, paired_out):
    nt = cfg.n_tiles
    return pl.pallas_call(
        functools.partial(_combine_kernel, n_tiles=nt, n_ctx_tiles=cfg.n_ctx_tiles),
        grid_spec=pltpu.PrefetchScalarGridSpec(
            num_scalar_prefetch=3,
            grid=(nt,),
            in_specs=[
                pl.BlockSpec((TM, D_MODEL), lambda i, *_: (i, 0)),
                pl.BlockSpec(memory_space=pl.ANY),
                pl.BlockSpec((TM, LANES), lambda i, *_: (i, 0)),
                pl.BlockSpec((1, 1, 6, D_MODEL), lambda i, *_: (l, _mod_row(cfg, i), 0, 0)),
            ],
            out_specs=_stream_specs(cfg, D_MODEL, paired_out),
            scratch_shapes=[pltpu.VMEM((2, LT, D_MODEL), BF16), pltpu.SemaphoreType.DMA((2,))],
        ),
        out_shape=_stream_shapes(cfg, D_MODEL, F32, paired_out),
        compiler_params=_cparams(("arbitrary",)),
        name="combine",
    )(plan.seg_tiled, plan.seg_sorted, plan.seg_len, xm, ys, lp, mod)


class Plan(NamedTuple):
    seg_tiled: jax.Array
    seg_sorted: jax.Array
    seg_len: jax.Array
    tile_expert: jax.Array
    n_valid: jax.Array
    tile_lo: jax.Array
    tile_hi: jax.Array


def _plan(cfg, counts):
    nt = cfg.n_tiles
    m = (counts + (BF16_ROWS - 1)) // BF16_ROWS * BF16_ROWS
    local_off = jnp.cumsum(m, axis=1) - m
    within_expert = jnp.cumsum(m, axis=0) - m
    expert_rows = jnp.sum(m, axis=0)
    expert_tiles = (expert_rows + (MOE_TM - 1)) // MOE_TM
    tiles_end = jnp.cumsum(expert_tiles)
    expert_start = (tiles_end - expert_tiles) * MOE_TM
    seg_sorted = (expert_start[None, :] + within_expert).T.reshape(-1).astype(jnp.int32)
    seg_tiled = (jnp.arange(nt, dtype=jnp.int32)[:, None] * LT + local_off).T.reshape(-1).astype(jnp.int32)
    seg_len = m.T.reshape(-1).astype(jnp.int32)
    n_valid = tiles_end[-1:].astype(jnp.int32)
    tile_ids = jnp.arange(cfg.moe_tiles, dtype=jnp.int32)
    tile_expert = jnp.minimum(jnp.sum(tile_ids[:, None] >= tiles_end[None, :], axis=1),
                              N_EXPERTS - 1).astype(jnp.int32)
    tile_base = tile_ids * MOE_TM
    seg_end = seg_sorted + seg_len
    tile_lo = jnp.sum(seg_end[None, :] <= tile_base[:, None], axis=1).astype(jnp.int32)
    tile_hi = jnp.sum(seg_sorted[None, :] < tile_base[:, None] + MOE_TM, axis=1).astype(jnp.int32)
    return Plan(seg_tiled, seg_sorted, seg_len, tile_expert, n_valid, tile_lo, tile_hi)


def _rope_tables(n):
    pos = jnp.arange(n)
    rc = jnp.stack([(pos // GRID_W).astype(F32), (pos % GRID_W).astype(F32)], axis=1)
    freqs = ROPE_THETA ** (-jnp.arange(AXIS_PAIRS, dtype=F32) / AXIS_PAIRS)
    ang = rc[:, :, None] * freqs[None, None, :]
    cos = jnp.cos(ang)[:, :, None, :]
    sin = jnp.sin(ang)[:, :, None, :] * jnp.array([-1.0, 1.0], F32)[None, None, :, None]
    cos = jnp.broadcast_to(cos, (n, 2, 2, AXIS_PAIRS)).reshape(n, HEAD_DIM)
    sin = jnp.broadcast_to(sin, (n, 2, 2, AXIS_PAIRS)).reshape(n, HEAD_DIM)
    reps = LANES // HEAD_DIM
    cos = jnp.concatenate([jnp.ones((TM, HEAD_DIM), F32), cos], axis=0)
    sin = jnp.concatenate([jnp.zeros((TM, HEAD_DIM), F32), sin], axis=0)
    return jnp.tile(cos, (1, reps)), jnp.tile(sin, (1, reps))


def _block_diag_mean(width):
    r = jnp.arange(width)
    return jnp.where((r[:, None] // HEAD_DIM) == (r[None, :] // HEAD_DIM), 1.0 / HEAD_DIM, 0.0).astype(BF16)


def _forward(cfg, x_prompt, x_sample, cache_k, cache_v, c, c_ctx, w_mod, b_mod, norm1_g, norm2_g,
             w_in, q_norm_g, k_norm_g, conv_dw_w, conv_dw_b, conv_ln_g, conv_ln_b, w_out,
             w_router, b_router, w_gate, w_up, w_down):
    depth = cfg.depth
    assert cfg.ctx_seq == CONV_ROWS and cfg.t_ctx % TM == 0 and cfg.smp_seq % TM == 0
    assert cfg.t_ctx % cfg.smp_seq == 0 and cfg.smp_batch + 1 <= 8 and cfg.ctx_batch % CTX_SEQS == 0

    x = [x_prompt.reshape(cfg.t_ctx, D_MODEL), x_sample.reshape(-1, D_MODEL)]
    cvec = jnp.zeros((8, D_MODEL), F32).at[0].set(c_ctx).at[1:1 + cfg.smp_batch].set(c)
    mod = _modulation(cvec, w_mod, b_mod)

    n1 = norm1_g.reshape(depth, 1, D_MODEL)
    n2 = norm2_g.reshape(depth, 1, D_MODEL)
    gain = jnp.concatenate([jnp.tile(q_norm_g, (1, N_Q_HEADS)), jnp.tile(k_norm_g, (1, N_KV_HEADS))],
                           axis=1).reshape(depth, 1, QK_W)
    expert_w = (w_gate.reshape(-1, D_FF), w_up.reshape(-1, D_FF), w_down.reshape(-1, D_MODEL))
    taps = _conv_tap_rows(conv_dw_w)
    dw_b = conv_dw_b.reshape(depth, 1, CONV_CH)
    ln_g = conv_ln_g.reshape(depth, 1, CONV_CH)
    ln_b = conv_ln_b.reshape(depth, 1, CONV_CH)
    wr = jnp.pad(w_router, ((0, 0), (0, LANES - N_EXPERTS)))
    br = b_router.reshape(N_EXPERTS, 1)
    cos, sin = _rope_tables(cfg.smp_seq)
    bdq = _block_diag_mean(Q_W // 2)
    bdk = _block_diag_mean(KV_W)
    ck = cache_k.reshape(cfg.smp_batch, depth, cfg.past, KV_W)
    cv = cache_v.reshape(cfg.smp_batch, depth, cfg.past, KV_W)

    new_k = jnp.zeros((cfg.ctx_batch, depth, cfg.ctx_seq, KV_W), F32)
    new_v = jnp.zeros((cfg.ctx_batch, depth, cfg.ctx_seq, KV_W), F32)
    for l in range(depth):
        q, k, v, z, new_k, new_v = _inproj(cfg, l, x, mod, n1, w_in, gain, bdq, bdk, cos, sin, new_k, new_v)
        a_pair, c_pair, expert_bf = _mixer(cfg, l, q, k, v, ck, cv, z, taps, dw_b, ln_g, ln_b, expert_w)
        xm, hl, lp, cnt = _outproj(cfg, l, a_pair, c_pair, x, mod, n2, w_out, wr, br)
        plan = _plan(cfg, cnt[:, :, 0])
        ys = _moe(cfg, hl.reshape(cfg.n_tiles * LT, D_MODEL), plan, *expert_bf)
        x = _combine(cfg, l, xm, ys, lp, mod, plan, paired_out=(l == depth - 1))

    y_prompt = x[0].reshape(cfg.ctx_batch, cfg.ctx_seq, D_MODEL)
    y_sample = x[1].reshape(cfg.smp_batch, cfg.smp_seq, D_MODEL)
    cache_shape = (cfg.ctx_batch, depth, cfg.ctx_seq, N_KV_HEADS, HEAD_DIM)
    return y_prompt, y_sample, new_k.reshape(cache_shape), new_v.reshape(cache_shape)


def kernel(x_prompt, x_sample, cache_k, cache_v, c, c_ctx, w_mod, b_mod, norm1_g, norm2_g, w_in, q_norm_g,
           k_norm_g, conv_dw_w, conv_dw_b, conv_ln_g, conv_ln_b, w_out, w_router, b_router, w_gate, w_up, w_down):
    cfg = Cfg(ctx_batch=x_prompt.shape[0], ctx_seq=x_prompt.shape[1], smp_batch=x_sample.shape[0],
              smp_seq=x_sample.shape[1], past=cache_k.shape[2], depth=w_mod.shape[0])
    return _forward(cfg, x_prompt, x_sample, cache_k, cache_v, c, c_ctx, w_mod, b_mod, norm1_g, norm2_g,
                    w_in, q_norm_g, k_norm_g, conv_dw_w, conv_dw_b, conv_ln_g, conv_ln_b, w_out,
                    w_router, b_router, w_gate, w_up, w_down)
```

```python
import functools
import math
from typing import NamedTuple

import jax
import jax.numpy as jnp
from jax import lax
from jax.experimental import pallas as pl
from jax.experimental.pallas import tpu as pltpu

F32 = jnp.float32
BF16 = jnp.bfloat16

D_MODEL = 1024
HEAD_DIM = 64
N_Q_HEADS = 8
N_KV_HEADS = 2
Q_PER_KV = N_Q_HEADS // N_KV_HEADS
Q_W = N_Q_HEADS * HEAD_DIM
KV_W = N_KV_HEADS * HEAD_DIM
QK_W = Q_W + KV_W
CONV_CH = D_MODEL // 2
CONV_K = 31
CONV_PAD = CONV_K // 2
IN_COLS = Q_W + 2 * KV_W + 2 * CONV_CH
GRID_W = 64
AXIS_PAIRS = HEAD_DIM // 4
ROPE_THETA = 10000.0
N_EXPERTS = 16
N_GROUPS = 4
EXPERTS_PER_GROUP = N_EXPERTS // N_GROUPS
D_FF = 512
EPS = 1e-6

LANES = 128
BF16_ROWS = 16
TM = 512
TOP_K = 2
LT = 1280
MOE_TM = 512
MOE_PAIR = 2
CONV_ROWS = 256
CTX_SEQS = 4
HALO = 16
CONV_INNER = CONV_ROWS + 16
PAD_ROWS = 24 + CONV_INNER + 8
SEG_SIZES = (512, 256, 128, 64, 32, 16)
SEG_SPLIT = 128
Q_SCALE = (1.0 / math.sqrt(HEAD_DIM)) * math.log2(math.e)
VMEM_LIMIT = 48 * 1024 * 1024
MIXER_VMEM_LIMIT = 56 * 1024 * 1024


class Cfg(NamedTuple):
    ctx_batch: int
    ctx_seq: int
    smp_batch: int
    smp_seq: int
    past: int
    depth: int

    @property
    def t_ctx(self):
        return self.ctx_batch * self.ctx_seq

    @property
    def t_all(self):
        return self.t_ctx + self.smp_batch * self.smp_seq

    @property
    def n_tiles(self):
        return self.t_all // TM

    @property
    def n_ctx_tiles(self):
        return self.t_ctx // TM

    @property
    def smp_tiles(self):
        return self.smp_seq // TM

    @property
    def moe_tiles(self):
        rows = TOP_K * self.t_all + self.n_tiles * N_EXPERTS * (BF16_ROWS - 1) + N_EXPERTS * (MOE_TM - 1)
        step_rows = MOE_PAIR * MOE_TM
        return -(-rows // step_rows) * MOE_PAIR


def _mod_row(cfg, i):
    return jnp.where(i < cfg.n_ctx_tiles, 0, 1 + (i - cfg.n_ctx_tiles) // cfg.smp_tiles)


def _pair_specs(cfg, width, tile_of=lambda i: i):
    nct = cfg.n_ctx_tiles
    return [pl.BlockSpec((TM, width), lambda i, *_: (jnp.minimum(tile_of(i), nct - 1), 0)),
            pl.BlockSpec((TM, width), lambda i, *_: (jnp.maximum(tile_of(i) - nct, 0), 0))]


def _pair_shapes(cfg, width, dtype):
    return [jax.ShapeDtypeStruct((cfg.t_ctx, width), dtype),
            jax.ShapeDtypeStruct((cfg.t_all - cfg.t_ctx, width), dtype)]


def _load_pair(is_ctx, c_ref, s_ref):
    return jnp.where(is_ctx, c_ref[...], s_ref[...])


def _store_pair(is_ctx, c_ref, s_ref, val):
    @pl.when(is_ctx)
    def _():
        c_ref[...] = val

    @pl.when(jnp.logical_not(is_ctx))
    def _():
        s_ref[...] = val


def _stream_specs(cfg, width, paired, tile_of=lambda i: i):
    if paired:
        return _pair_specs(cfg, width, tile_of)
    return [pl.BlockSpec((TM, width), lambda i, *_: (tile_of(i), 0))]


def _stream_shapes(cfg, width, dtype, paired):
    if paired:
        return _pair_shapes(cfg, width, dtype)
    return [jax.ShapeDtypeStruct((cfg.t_all, width), dtype)]


def _load_stream(is_ctx, refs):
    return _load_pair(is_ctx, *refs) if len(refs) == 2 else refs[0][...]


def _cparams(sem, vmem=VMEM_LIMIT):
    return pltpu.CompilerParams(dimension_semantics=sem, vmem_limit_bytes=vmem)


def _dot(a, b):
    return jnp.dot(a, b, preferred_element_type=F32)


def _dot_nt(a, b):
    return lax.dot_general(a, b, (((1,), (1,)), ((), ())), preferred_element_type=F32)


def _sigmoid(x):
    return 1.0 / (1.0 + jnp.exp(-x))


def _mod_kernel(c_ref, w_ref, b_ref, o_ref):
    c = c_ref[...]
    a = (c * _sigmoid(c)).astype(BF16)
    o_ref[0] = _dot(a, w_ref[0].astype(BF16)) + b_ref[0]


def _modulation(cvec, w_mod, b_mod):
    depth = w_mod.shape[0]
    out = pl.pallas_call(
        _mod_kernel,
        grid=(depth, 6),
        in_specs=[
            pl.BlockSpec((8, D_MODEL), lambda l, j: (0, 0)),
            pl.BlockSpec((1, D_MODEL, D_MODEL), lambda l, j: (l, 0, j)),
            pl.BlockSpec((1, 1, D_MODEL), lambda l, j: (l, 0, j)),
        ],
        out_specs=pl.BlockSpec((1, 8, D_MODEL), lambda l, j: (l, 0, j)),
        out_shape=jax.ShapeDtypeStruct((depth, 8, 6 * D_MODEL), F32),
        compiler_params=_cparams(("arbitrary", "arbitrary")),
        name="modulation",
    )(cvec, w_mod, b_mod.reshape(depth, 1, 6 * D_MODEL))
    return out.reshape(depth, 8, 6, D_MODEL)


def _inproj_kernel(*refs, n_tiles, n_ctx_tiles, ctx_seq, n_x):
    x_refs = refs[:n_x]
    (mod_ref, n1_ref, w_ref, gain_ref, bdq_ref, bdk_ref, cos_ref, sin_ref,
     kc_in, vc_in, q_ref, k_ref, v_ref, z_ref, kc_ref, vc_ref, w_bf, proj_prev) = refs[n_x:]
    del kc_in, vc_in
    step = pl.program_id(0)

    @pl.when(step == 0)
    def _():
        w_bf[...] = w_ref[0].astype(BF16)
        proj_prev[...] = jnp.zeros_like(proj_prev)

    x = _load_stream(jnp.minimum(step, n_tiles - 1) < n_ctx_tiles, x_refs)
    mod = mod_ref[0, 0]
    sh1, sc1 = mod[0:1], mod[1:2]
    ms = jnp.mean(x * x, axis=-1, keepdims=True)
    h = (x * lax.rsqrt(ms + EPS) * n1_ref[0]) * (1.0 + sc1) + sh1
    proj_new = _dot(h.astype(BF16), w_bf[...])

    qk = proj_prev[:, :QK_W]
    sq = (qk * qk).astype(BF16)
    half = Q_W // 2
    msq = jnp.concatenate([_dot(sq[:, :half], bdq_ref[...]),
                           _dot(sq[:, half:Q_W], bdq_ref[...]),
                           _dot(sq[:, Q_W:], bdk_ref[...])], axis=1)
    qkn = qk * lax.rsqrt(msq + EPS) * gain_ref[0]
    v = proj_prev[:, QK_W:QK_W + KV_W]
    v_ref[...] = v
    a = proj_prev[:, QK_W + KV_W:QK_W + KV_W + CONV_CH]
    b = proj_prev[:, QK_W + KV_W + CONV_CH:]
    z_ref[...] = a * _sigmoid(b)
    cos = cos_ref[...]
    sin = sin_ref[...]
    first = (lax.broadcasted_iota(jnp.int32, (TM, LANES), 1) % 32) < 16
    for c in range(QK_W // LANES):
        blk = qkn[:, c * LANES:(c + 1) * LANES]
        partner = jnp.where(first, pltpu.roll(blk, LANES - 16, 1), pltpu.roll(blk, 16, 1))
        rot = blk * cos + partner * sin
        if c < Q_W // LANES:
            q_ref[:, c * LANES:(c + 1) * LANES] = (rot * Q_SCALE).astype(BF16)
        else:
            k = rot
            k_ref[...] = k

    proj_prev[...] = proj_new

    @pl.when(step - 1 < n_ctx_tiles)
    def _():
        for s in range(TM // ctx_seq):
            kc_ref[s, 0] = k[s * ctx_seq:(s + 1) * ctx_seq]
            vc_ref[s, 0] = v[s * ctx_seq:(s + 1) * ctx_seq]


def _inproj(cfg, l, x, mod, n1, w_in, gain, bdq, bdk, cos, sin, kc, vc):
    t = cfg.t_all
    n_x = len(x)
    nt, nct, st = cfg.n_tiles, cfg.n_ctx_tiles, cfg.smp_tiles
    tile_a = lambda i: jnp.minimum(i, nt - 1)
    tile_b = lambda i: jnp.maximum(i - 1, 0)
    rope_map = lambda i: (jnp.where(tile_b(i) < nct, 0, 1 + (tile_b(i) - nct) % st), 0)
    seqs = TM // cfg.ctx_seq
    cache_spec = lambda: pl.BlockSpec((seqs, 1, cfg.ctx_seq, KV_W),
                                      lambda i: (jnp.minimum(tile_b(i), nct - 1), l, 0, 0))
    return pl.pallas_call(
        functools.partial(_inproj_kernel, n_tiles=nt, n_ctx_tiles=nct, ctx_seq=cfg.ctx_seq, n_x=n_x),
        grid=(nt + 1,),
        in_specs=_stream_specs(cfg, D_MODEL, n_x == 2, tile_a) + [
            pl.BlockSpec((1, 1, 6, D_MODEL), lambda i: (l, _mod_row(cfg, tile_a(i)), 0, 0)),
            pl.BlockSpec((1, 1, D_MODEL), lambda i: (l, 0, 0)),
            pl.BlockSpec((1, D_MODEL, IN_COLS), lambda i: (l, 0, 0)),
            pl.BlockSpec((1, 1, QK_W), lambda i: (l, 0, 0)),
            pl.BlockSpec((Q_W // 2, Q_W // 2), lambda i: (0, 0)),
            pl.BlockSpec((KV_W, KV_W), lambda i: (0, 0)),
            pl.BlockSpec((TM, LANES), rope_map),
            pl.BlockSpec((TM, LANES), rope_map),
            pl.BlockSpec(memory_space=pl.ANY),
            pl.BlockSpec(memory_space=pl.ANY),
        ],
        out_specs=[
            pl.BlockSpec((TM, Q_W), lambda i: (tile_b(i), 0)),
            pl.BlockSpec((TM, KV_W), lambda i: (tile_b(i), 0)),
            pl.BlockSpec((TM, KV_W), lambda i: (tile_b(i), 0)),
            pl.BlockSpec((TM, CONV_CH), lambda i: (tile_b(i), 0)),
            cache_spec(), cache_spec(),
        ],
        out_shape=[
            jax.ShapeDtypeStruct((t, Q_W), BF16),
            jax.ShapeDtypeStruct((t, KV_W), F32),
            jax.ShapeDtypeStruct((t, KV_W), F32),
            jax.ShapeDtypeStruct((t, CONV_CH), F32),
            jax.ShapeDtypeStruct(kc.shape, F32),
            jax.ShapeDtypeStruct(vc.shape, F32),
        ],
        scratch_shapes=[pltpu.VMEM((D_MODEL, IN_COLS), BF16), pltpu.VMEM((TM, IN_COLS), F32)],
        input_output_aliases={n_x + 8: 4, n_x + 9: 5},
        compiler_params=_cparams(("arbitrary",)),
        name="inproj",
    )(*x, mod, n1, w_in, gain, bdq, bdk, cos, sin, kc, vc)


def _mixer_kernel(*refs, rq, n_past, seq_blocks, n_seq):
    if n_past:
        q_ref, k_ref, v_ref, ck_ref, cv_ref = refs[:5]
        refs = refs[5:]
    else:
        q_ref, k_ref, v_ref = refs[:3]
        refs = refs[3:]
    z_ref, zp_ref, zn_ref, w_ref, b_ref, g_ref, beta_ref = refs[:7]
    refs = refs[7:]
    if n_past:
        for src, dst in zip(refs[0:3], refs[5:8]):
            dst[...] = src[...].astype(BF16)
        refs = refs[3:5] + refs[8:]
    o_ref, co_ref, krep, vrep, pad, taps = refs
    head_w = Q_PER_KV * HEAD_DIM
    j = pl.program_id(1)

    @pl.when(jnp.logical_and(pl.program_id(0) == 0, j == 0))
    def _():
        diag = (lax.broadcasted_iota(jnp.int32, (LANES, LANES), 0)
                == lax.broadcasted_iota(jnp.int32, (LANES, LANES), 1))
        for c in range(CONV_CH // LANES):
            for p in range(4):
                for q in range(4):
                    for half in range(2):
                        tap = w_ref[0, 8 * q + 2 * p + half:8 * q + 2 * p + half + 1, c * LANES:(c + 1) * LANES]
                        taps[c, p, q * LANES:(q + 1) * LANES, half * LANES:(half + 1) * LANES] = (
                            jnp.where(diag, tap, 0.0).astype(BF16))

    n_new = k_ref.shape[0] // n_seq

    @pl.when(j == 0)
    def _():
        src = lax.broadcasted_iota(jnp.int32, (KV_W, head_w), 0)
        dst = lax.broadcasted_iota(jnp.int32, (KV_W, head_w), 1) % HEAD_DIM
        for h in range(N_KV_HEADS):
            rep = jnp.where(src == dst + h * HEAD_DIM, 1.0, 0.0).astype(BF16)
            if n_past:
                krep[0, h, :n_past, :] = _dot(ck_ref[0, 0].astype(BF16), rep).astype(BF16)
                vrep[0, h, :n_past, :] = _dot(cv_ref[0, 0].astype(BF16), rep).astype(BF16)
            for s in range(n_seq):
                rows = slice(s * n_new, (s + 1) * n_new)
                krep[s, h, n_past:n_past + n_new, :] = _dot(k_ref[rows, :].astype(BF16), rep).astype(BF16)
                vrep[s, h, n_past:n_past + n_new, :] = _dot(v_ref[rows, :].astype(BF16), rep).astype(BF16)

    lane_head = lax.broadcasted_iota(jnp.int32, (rq, head_w), 1) // HEAD_DIM
    end = HALO + CONV_ROWS
    for s in range(n_seq):
        rows = slice(s * rq, (s + 1) * rq)
        if seq_blocks > 1:
            pad[0:HALO, :] = jnp.where(j > 0, zp_ref[...], 0.0)
            pad[end:end + HALO, :] = jnp.where(j < seq_blocks - 1, zn_ref[...], 0.0)
        else:
            pad[0:HALO, :] = jnp.zeros((HALO, CONV_CH), F32)
            pad[end:end + HALO, :] = jnp.zeros((HALO, CONV_CH), F32)
        pad[end + HALO:, :] = jnp.zeros((PAD_ROWS - end - HALO, CONV_CH), F32)
        pad[HALO:end, :] = z_ref[rows, :]

        for h in range(N_KV_HEADS):
            qh = q_ref[rows, h * head_w:(h + 1) * head_w]
            kh = krep[s, h]
            vh = vrep[s, h]
            acc = jnp.zeros((rq, head_w), F32)
            for g in range(Q_PER_KV):
                mask = lane_head == g
                qm = jnp.where(mask, qh, jnp.zeros_like(qh))
                sc = _dot_nt(qm, kh)
                m = jnp.max(sc, axis=-1, keepdims=True)
                p = jnp.exp2(sc - m)
                denom = jnp.sum(p, axis=-1, keepdims=True)
                o = _dot(p.astype(BF16), vh)
                acc = jnp.where(mask, o * (1.0 / denom), acc)
            o_ref[rows, h * head_w:(h + 1) * head_w] = acc.astype(BF16)
        conv_cols = [_conv_cols(pad, taps, c) for c in range(CONV_CH // LANES)]
        co_ref[rows, :] = _conv_finish(conv_cols, b_ref, g_ref, beta_ref)


def _mixer(cfg, l, q, k, v, cache_k, cache_v, z, taps, dw_b, ln_g, ln_b, expert_w):
    head_w = Q_PER_KV * HEAD_DIM
    rq = CONV_ROWS
    hb = CONV_ROWS // HALO
    last = cfg.t_all // HALO - 1
    vec = lambda: pl.BlockSpec((1, 1, CONV_CH), lambda b, j: (l, 0, 0))

    def conv_specs(row_block, n_seq=1):
        return [
            pl.BlockSpec((n_seq * CONV_ROWS, CONV_CH), lambda b, j: (row_block(b, j), 0)),
            pl.BlockSpec((HALO, CONV_CH), lambda b, j: (jnp.maximum(row_block(b, j) * hb - 1, 0), 0)),
            pl.BlockSpec((HALO, CONV_CH), lambda b, j: (jnp.minimum(row_block(b, j) * hb + hb, last), 0)),
            pl.BlockSpec((1, 32, CONV_CH), lambda b, j: (l, 0, 0)),
            vec(), vec(), vec(),
        ]

    def scratch(sk, n_seq=1):
        return [pltpu.VMEM((n_seq, N_KV_HEADS, sk, head_w), BF16), pltpu.VMEM((n_seq, N_KV_HEADS, sk, head_w), BF16),
                pltpu.VMEM((PAD_ROWS, CONV_CH), F32),
                pltpu.VMEM((CONV_CH // LANES, 4, 4 * LANES, 2 * LANES), BF16)]

    sc = cfg.ctx_seq
    rows = CTX_SEQS * sc
    a_ctx, c_ctx = pl.pallas_call(
        functools.partial(_mixer_kernel, rq=rq, n_past=0, seq_blocks=1, n_seq=CTX_SEQS),
        grid=(cfg.ctx_batch // CTX_SEQS, 1),
        in_specs=[
            pl.BlockSpec((rows, Q_W), lambda b, j: (b, 0)),
            pl.BlockSpec((rows, KV_W), lambda b, j: (b, 0)),
            pl.BlockSpec((rows, KV_W), lambda b, j: (b, 0)),
        ] + conv_specs(lambda b, j: b, CTX_SEQS),
        out_specs=[pl.BlockSpec((rows, Q_W), lambda b, j: (b, 0)),
                   pl.BlockSpec((rows, CONV_CH), lambda b, j: (b, 0))],
        out_shape=[jax.ShapeDtypeStruct((cfg.t_ctx, Q_W), BF16),
                   jax.ShapeDtypeStruct((cfg.t_ctx, CONV_CH), BF16)],
        scratch_shapes=scratch(sc, CTX_SEQS),
        compiler_params=_cparams(("arbitrary", "arbitrary")),
        name="mixer_ctx",
    )(q, k, v, z, z, z, taps, dw_b, ln_g, ln_b)
    ss = cfg.smp_seq
    sk = cfg.past + ss
    kv_base = cfg.t_ctx // ss
    q_base = cfg.t_ctx // rq
    nqb = ss // rq
    t_smp = cfg.smp_batch * ss
    steps = cfg.smp_batch * nqb
    w_rows = [w.shape[0] // (cfg.depth * steps) for w in expert_w]
    w_specs = lambda off: [pl.BlockSpec((r, w.shape[1]), lambda b, j: (off + b * nqb + j, 0))
                           for r, w in zip(w_rows, expert_w)]
    a_smp, c_smp, *expert_bf = pl.pallas_call(
        functools.partial(_mixer_kernel, rq=rq, n_past=cfg.past, seq_blocks=nqb, n_seq=1),
        grid=(cfg.smp_batch, nqb),
        in_specs=[
            pl.BlockSpec((rq, Q_W), lambda b, j: (q_base + b * nqb + j, 0)),
            pl.BlockSpec((ss, KV_W), lambda b, j: (kv_base + b, 0)),
            pl.BlockSpec((ss, KV_W), lambda b, j: (kv_base + b, 0)),
            pl.BlockSpec((1, 1, cfg.past, KV_W), lambda b, j: (b, l, 0, 0)),
            pl.BlockSpec((1, 1, cfg.past, KV_W), lambda b, j: (b, l, 0, 0)),
        ] + conv_specs(lambda b, j: q_base + b * nqb + j) + w_specs(l * steps),
        out_specs=[pl.BlockSpec((rq, Q_W), lambda b, j: (b * nqb + j, 0)),
                   pl.BlockSpec((rq, CONV_CH), lambda b, j: (b * nqb + j, 0))] + w_specs(0),
        out_shape=[jax.ShapeDtypeStruct((t_smp, Q_W), BF16),
                   jax.ShapeDtypeStruct((t_smp, CONV_CH), BF16)]
        + [jax.ShapeDtypeStruct((r * steps, w.shape[1]), BF16) for r, w in zip(w_rows, expert_w)],
        scratch_shapes=scratch(sk),
        compiler_params=_cparams(("arbitrary", "arbitrary"), MIXER_VMEM_LIMIT),
        name="mixer_smp",
    )(q, k, v, cache_k, cache_v, z, z, z, taps, dw_b, ln_g, ln_b, *expert_w)
    return (a_ctx, a_smp), (c_ctx, c_smp), expert_bf


def _conv_cols(pad, taps, c):
    cols = slice(c * LANES, (c + 1) * LANES)
    lhs = jnp.concatenate([pad[8 * q:8 * q + CONV_INNER, cols].astype(BF16) for q in range(4)], axis=1)
    acc = None
    for p in range(4):
        pair = _dot(lhs, taps[c, p])
        for half in range(2):
            o = 2 * p + half
            shifted = pair[o:o + CONV_ROWS, half * LANES:(half + 1) * LANES]
            acc = shifted if acc is None else acc + shifted
    return acc


def _conv_tap_rows(conv_dw_w):
    return jnp.pad(conv_dw_w, ((0, 0), (HALO - CONV_PAD, 32 - CONV_K - (HALO - CONV_PAD)), (0, 0)))


def _conv_finish(blocks, b_ref, g_ref, beta_ref):
    y = jnp.concatenate(blocks, axis=1) + b_ref[0]
    mu = jnp.mean(y, axis=-1, keepdims=True)
    yc = y - mu
    var = jnp.mean(yc * yc, axis=-1, keepdims=True)
    yn = yc * lax.rsqrt(var + EPS) * g_ref[0] + beta_ref[0]
    return (yn * _sigmoid(yn)).astype(BF16)


def _split_bf16(x):
    hi = x.astype(BF16)
    lo = (x - hi.astype(F32)).astype(BF16)
    return hi, lo


def _outproj_kernel(ac_ref, as_ref, cc_ref, cs_ref, *refs, n_tiles, n_ctx_tiles, n_x):
    x_refs = refs[:n_x]
    (mod_ref, n2_ref, wo_ref, wr_ref, br_ref, xm_ref, hl_ref, lp_ref, cnt_ref,
     wo_bf, h_prev, lt_prev, before_bf, eye_bf, rows_i) = refs[n_x:]
    step = pl.program_id(0)

    @pl.when(step == 0)
    def _():
        wo_bf[...] = wo_ref[0].astype(BF16)
        h_prev[...] = jnp.zeros_like(h_prev)
        lt_prev[...] = jnp.zeros_like(lt_prev)
        r = lax.broadcasted_iota(jnp.int32, (TM, TM), 0)
        c = lax.broadcasted_iota(jnp.int32, (TM, TM), 1)
        before_bf[...] = jnp.where(r < c, 1.0, 0.0).astype(BF16)
        eye_bf[...] = jnp.where(r == c, 1.0, 0.0).astype(BF16)
        rows_i[...] = lax.broadcasted_iota(jnp.int32, (LT, TM), 0)

    is_ctx = jnp.minimum(step, n_tiles - 1) < n_ctx_tiles
    ac = jnp.concatenate([_load_pair(is_ctx, ac_ref, as_ref), _load_pair(is_ctx, cc_ref, cs_ref)], axis=1)
    y = _dot(ac, wo_bf[...])
    mod = mod_ref[0, 0]
    g1, sh2, sc2 = mod[2:3], mod[3:4], mod[4:5]
    xm = _load_stream(is_ctx, x_refs) + g1 * y
    xm_ref[...] = xm
    ms = jnp.mean(xm * xm, axis=-1, keepdims=True)
    h2 = (xm * lax.rsqrt(ms + EPS) * n2_ref[0]) * (1.0 + sc2) + sh2
    h_new, h_lo = _split_bf16(h2)
    w_hi, w_lo = _split_bf16(wr_ref[...])
    logits = _dot(h_new, w_hi) + (_dot(h_new, w_lo) + _dot(h_lo, w_hi))
    lt_new = logits.T[0:N_EXPERTS, :]

    h_hi = h_prev[...]
    s_all = _sigmoid(lt_prev[...])
    sb_all = s_all + br_ref[...]

    rows8 = 2 * EXPERTS_PER_GROUP
    row = lax.broadcasted_iota(jnp.int32, (rows8, TM), 0)
    member = row % EXPERTS_PER_GROUP
    second = row >= EXPERTS_PER_GROUP

    def partner(x, k):
        return jnp.where(member + k < EXPERTS_PER_GROUP,
                         pltpu.roll(x, rows8 - k, 0), pltpu.roll(x, EXPERTS_PER_GROUP - k, 0))

    def other_group(x):
        return pltpu.roll(x, EXPERTS_PER_GROUP, 0)

    def group_sum(x):
        return x + partner(x, 1) + (partner(x, 2) + partner(x, 3))

    def ahead(other, mine, other_first):
        return jnp.where((other >= mine) if other_first else (other > mine), 1.0, 0.0)

    sel, gscore = [], []
    for hv in range(2):
        sb8 = sb_all[hv * rows8:(hv + 1) * rows8]
        rank = jnp.zeros((rows8, TM), F32)
        for k in range(1, EXPERTS_PER_GROUP):
            pk = partner(sb8, k)
            rank = rank + jnp.where(member + k >= EXPERTS_PER_GROUP, ahead(pk, sb8, True), ahead(pk, sb8, False))
        sel.append(rank < 1.5)
        gscore.append(group_sum(jnp.where(sel[hv], sb8, 0.0)))
    gate8, used8 = [], []
    for hv in range(2):
        mine, far = gscore[hv], gscore[1 - hv]
        near = other_group(mine)
        rank = jnp.where(second, ahead(near, mine, True), ahead(near, mine, False))
        rank = rank + ahead(far, mine, hv == 1) + ahead(other_group(far), mine, hv == 1)
        chosen = rank < 0.5
        used8.append(jnp.where(chosen, jnp.where(sel[hv], 1.0, 0.0), 0.0))
        gate8.append(used8[hv] * s_all[hv * rows8:(hv + 1) * rows8])
    total = group_sum(gate8[0]) + group_sum(gate8[1])
    total = total + other_group(total)
    inv_total = 1.0 / total
    gate16 = jnp.concatenate([gate8[0] * inv_total, gate8[1] * inv_total], axis=0)
    used16 = jnp.concatenate(used8, axis=0)
    used = used16 > 0.5

    prior = _dot(used16.astype(BF16), before_bf[...])
    n_e = jnp.sum(used16, axis=-1, keepdims=True)
    m_e = jnp.floor((n_e + (BF16_ROWS - 1)) * (1.0 / BF16_ROWS)) * BF16_ROWS
    lower = (lax.broadcasted_iota(jnp.int32, (N_EXPERTS, N_EXPERTS), 0)
             > lax.broadcasted_iota(jnp.int32, (N_EXPERTS, N_EXPERTS), 1))
    start = _dot(jnp.where(lower, 1.0, 0.0).astype(BF16),
                 jnp.broadcast_to(m_e, (N_EXPERTS, LANES)).astype(BF16))[:, 0:1]
    pos16 = start + prior
    pos_a = jnp.min(jnp.where(used, pos16, float(LT)), axis=0, keepdims=True)
    pos_b = jnp.max(jnp.where(used, pos16, -1.0), axis=0, keepdims=True)
    gate_a = jnp.sum(jnp.where(used, jnp.where(pos16 == pos_a, gate16, 0.0), 0.0), axis=0, keepdims=True)
    gate_b = jnp.sum(jnp.where(used, jnp.where(pos16 == pos_b, gate16, 0.0), 0.0), axis=0, keepdims=True)
    rows = rows_i[...]
    perm = jnp.where(rows == pos_a.astype(jnp.int32), 1.0,
                     jnp.where(rows == pos_b.astype(jnp.int32), 1.0, 0.0)).astype(BF16)
    hl_ref[0] = _dot(perm, h_hi).astype(BF16)

    def split32(p):
        hi = jnp.floor(p * (1.0 / 32.0))
        return [hi, p - 32.0 * hi]

    def split3(g):
        hi = g.astype(BF16).astype(F32)
        mid = (g - hi).astype(BF16).astype(F32)
        lo = ((g - hi) - mid).astype(BF16).astype(F32)
        return [hi, mid, lo]

    parts = split32(pos_a) + split32(pos_b) + split3(gate_a) + split3(gate_b)
    lp_parts = jnp.concatenate(parts + [jnp.zeros((LANES - len(parts), TM), F32)], axis=0).astype(BF16)
    lp_ref[...] = _dot_nt(eye_bf[...], lp_parts)

    cnt_ref[0] = jnp.broadcast_to(n_e, (N_EXPERTS, LANES)).astype(jnp.int32)

    h_prev[...] = h_new
    lt_prev[...] = lt_new


def _outproj(cfg, l, a_pair, c_pair, x, mod, n2, w_out, w_router, b_router):
    t, nt, nct = cfg.t_all, cfg.n_tiles, cfg.n_ctx_tiles
    n_x = len(x)
    tile_a = lambda i: jnp.minimum(i, nt - 1)
    tile_b = lambda i: jnp.maximum(i - 1, 0)
    return pl.pallas_call(
        functools.partial(_outproj_kernel, n_tiles=nt, n_ctx_tiles=nct, n_x=n_x),
        grid=(nt + 1,),
        in_specs=_pair_specs(cfg, Q_W, tile_a) + _pair_specs(cfg, CONV_CH, tile_a)
        + _stream_specs(cfg, D_MODEL, n_x == 2, tile_a) + [
            pl.BlockSpec((1, 1, 6, D_MODEL), lambda i: (l, _mod_row(cfg, tile_a(i)), 0, 0)),
            pl.BlockSpec((1, 1, D_MODEL), lambda i: (l, 0, 0)),
            pl.BlockSpec((1, D_MODEL, D_MODEL), lambda i: (l, 0, 0)),
            pl.BlockSpec((D_MODEL, LANES), lambda i: (0, 0)),
            pl.BlockSpec((N_EXPERTS, 1), lambda i: (0, 0)),
        ],
        out_specs=_stream_specs(cfg, D_MODEL, False, tile_a) + [
            pl.BlockSpec((1, LT, D_MODEL), lambda i: (tile_b(i), 0, 0)),
            pl.BlockSpec((TM, LANES), lambda i: (tile_b(i), 0)),
            pl.BlockSpec((1, N_EXPERTS, LANES), lambda i: (tile_b(i), 0, 0)),
        ],
        out_shape=_stream_shapes(cfg, D_MODEL, F32, False) + [
            jax.ShapeDtypeStruct((nt, LT, D_MODEL), BF16),
            jax.ShapeDtypeStruct((t, LANES), F32),
            jax.ShapeDtypeStruct((nt, N_EXPERTS, LANES), jnp.int32),
        ],
        scratch_shapes=[pltpu.VMEM((D_MODEL, D_MODEL), BF16), pltpu.VMEM((TM, D_MODEL), BF16),
                        pltpu.VMEM((N_EXPERTS, TM), F32), pltpu.VMEM((TM, TM), BF16), pltpu.VMEM((TM, TM), BF16),
                        pltpu.VMEM((LT, TM), jnp.int32)],
        compiler_params=_cparams(("arbitrary",)),
        name="outproj",
    )(*a_pair, *c_pair, *x, mod, n2, w_out, w_router, b_router)


def _start_pieces(n, src_row, dst_row, src_ref, dst_ref, sem):
    def piece(size):
        off = n & ~(2 * size - 1)

        @pl.when((n & size) != 0)
        def _():
            pltpu.make_async_copy(
                src_ref.at[pl.ds(pl.multiple_of(src_row + off, BF16_ROWS), size)],
                dst_ref.at[pl.ds(pl.multiple_of(dst_row + off, BF16_ROWS), size)],
                sem).start()

    @pl.when(n >= SEG_SPLIT)
    def _():
        for size in SEG_SIZES:
            if size >= SEG_SPLIT:
                piece(size)

    for size in SEG_SIZES:
        if size < SEG_SPLIT:
            piece(size)


def _wait_rows(total, sizes, src_ref, dst_ref, sem):
    for size in sizes:
        @pl.when((total & size) != 0)
        def _():
            pltpu.make_async_copy(src_ref.at[pl.ds(0, size)], dst_ref.at[pl.ds(0, size)], sem).wait()


def _moe_kernel(tg_ref, nv_ref, ssrc_ref, sdst_ref, slen_ref, tlo_ref, thi_ref, hl_ref, *refs):
    del tg_ref
    w_refs = [refs[3 * k:3 * k + 3] for k in range(MOE_PAIR)]
    y_ref, xbuf, sem = refs[3 * MOE_PAIR:]
    i = pl.program_id(0)
    nv = nv_ref[0]

    def overlap(s, tile):
        base = tile * MOE_TM
        d = sdst_ref[s]
        lo = jnp.maximum(d, base)
        hi = jnp.minimum(d + slen_ref[s], base + MOE_TM)
        return jnp.maximum(hi - lo, 0), lo

    def start_step(step, slot):
        for k in range(MOE_PAIR):
            tile = step * MOE_PAIR + k

            @pl.when(tile < nv)
            def _():
                def body(s, carry):
                    n, lo = overlap(s, tile)
                    _start_pieces(n, ssrc_ref[s] + (lo - sdst_ref[s]), lo - tile * MOE_TM,
                                  hl_ref, xbuf.at[slot, k], sem.at[slot])
                    return carry

                lax.fori_loop(tlo_ref[tile], thi_ref[tile], body, 0)

    def wait_step(step, slot):
        for k in range(MOE_PAIR):
            tile = step * MOE_PAIR + k
            total = lax.fori_loop(tlo_ref[tile], thi_ref[tile], lambda s, acc: acc + overlap(s, tile)[0], 0)
            _wait_rows(total, SEG_SIZES, hl_ref, xbuf.at[slot, k], sem.at[slot])

    @pl.when(i == 0)
    def _():
        xbuf[...] = jnp.zeros_like(xbuf)
        start_step(0, 0)

    @pl.when((i + 1) * MOE_PAIR < nv)
    def _():
        start_step(i + 1, (i + 1) % 2)

    @pl.when(i * MOE_PAIR >= nv)
    def _():
        y_ref[...] = jnp.zeros_like(y_ref)

    @pl.when(i * MOE_PAIR < nv)
    def _():
        slot = i % 2
        wait_step(i, slot)
        for k, (wg_ref, wu_ref, wd_ref) in enumerate(w_refs):
            x = xbuf[slot, k]
            hg = _dot(x, wg_ref[0])
            hu = _dot(x, wu_ref[0])
            hid = ((hg * _sigmoid(hg)) * hu).astype(BF16)
            y_ref[k * MOE_TM:(k + 1) * MOE_TM, :] = _dot(hid, wd_ref[0]).astype(BF16)


def _moe(cfg, hl, plan, w_gate, w_up, w_down):
    n_steps = cfg.moe_tiles // MOE_PAIR

    def w_map(k):
        return lambda i, tg, nv, *_: (tg[jnp.minimum(i * MOE_PAIR + k, nv[0] - 1)], 0, 0)

    w_gate = w_gate.reshape(N_EXPERTS, D_MODEL, D_FF)
    w_up = w_up.reshape(N_EXPERTS, D_MODEL, D_FF)
    w_down = w_down.reshape(N_EXPERTS, D_FF, D_MODEL)
    w_specs = []
    for k in range(MOE_PAIR):
        w_specs += [pl.BlockSpec((1, D_MODEL, D_FF), w_map(k)), pl.BlockSpec((1, D_MODEL, D_FF), w_map(k)),
                    pl.BlockSpec((1, D_FF, D_MODEL), w_map(k))]

    return pl.pallas_call(
        _moe_kernel,
        grid_spec=pltpu.PrefetchScalarGridSpec(
            num_scalar_prefetch=7,
            grid=(n_steps,),
            in_specs=[pl.BlockSpec(memory_space=pl.ANY)] + w_specs,
            out_specs=pl.BlockSpec((MOE_PAIR * MOE_TM, D_MODEL), lambda i, *_: (i, 0)),
            scratch_shapes=[pltpu.VMEM((2, MOE_PAIR, MOE_TM, D_MODEL), BF16), pltpu.SemaphoreType.DMA((2,))],
        ),
        out_shape=jax.ShapeDtypeStruct((n_steps * MOE_PAIR * MOE_TM, D_MODEL), BF16),
        compiler_params=_cparams(("arbitrary",)),
        name="moe",
    )(plan.tile_expert, plan.n_valid, plan.seg_tiled, plan.seg_sorted, plan.seg_len, plan.tile_lo, plan.tile_hi,
      hl, *([w_gate, w_up, w_down] * MOE_PAIR))


def _combine_kernel(stiled_ref, ssorted_ref, slen_ref, xm_ref, ys_ref, lp_ref, mod_ref, *refs,
                    n_tiles, n_ctx_tiles):
    o_refs, (ybuf, sem) = refs[:-2], refs[-2:]
    t = pl.program_id(0)

    def start_tile(tile, slot):
        def body(e, carry):
            s = e * n_tiles + tile
            _start_pieces(slen_ref[s], ssorted_ref[s], stiled_ref[s] - tile * LT,
                          ys_ref, ybuf.at[slot], sem.at[slot])
            return carry

        lax.fori_loop(0, N_EXPERTS, body, 0)

    def wait_tile(tile, slot):
        total = lax.fori_loop(0, N_EXPERTS, lambda e, acc: acc + slen_ref[e * n_tiles + tile], 0)
        _wait_rows(total, (2 * SEG_SIZES[0],) + SEG_SIZES, ys_ref, ybuf.at[slot], sem.at[slot])

    @pl.when(t == 0)
    def _():
        ybuf[...] = jnp.zeros_like(ybuf)
        start_tile(0, 0)

    @pl.when(t + 1 < n_tiles)
    def _():
        start_tile(t + 1, (t + 1) % 2)

    slot = t % 2
    wait_tile(t, slot)
    lp = lp_ref[...]
    pos_a = (lp[:, 0:1] * 32.0 + lp[:, 1:2]).astype(jnp.int32)
    pos_b = (lp[:, 2:3] * 32.0 + lp[:, 3:4]).astype(jnp.int32)
    gate_a = lp[:, 4:5] + lp[:, 5:6] + lp[:, 6:7]
    gate_b = lp[:, 7:8] + lp[:, 8:9] + lp[:, 9:10]
    cols = lax.broadcasted_iota(jnp.int32, (TM, LT), 1)
    mix = jnp.where(cols == pos_a, gate_a, jnp.where(cols == pos_b, gate_b, 0.0)).astype(BF16)
    y = _dot(mix, ybuf[slot])
    g2 = mod_ref[0, 0][5:6]
    x_new = xm_ref[...] + g2 * y
    if len(o_refs) == 2:
        _store_pair(t < n_ctx_tiles, *o_refs, x_new)
    else:
        o_refs[0][...] = x_new


def _combine(cfg, l, xm, ys, lp, mod, plan, paired_out):
    nt = cfg.n_tiles
    return pl.pallas_call(
        functools.partial(_combine_kernel, n_tiles=nt, n_ctx_tiles=cfg.n_ctx_tiles),
        grid_spec=pltpu.PrefetchScalarGridSpec(
            num_scalar_prefetch=3,
            grid=(nt,),
            in_specs=[
                pl.BlockSpec((TM, D_MODEL), lambda i, *_: (i, 0)),
                pl.BlockSpec(memory_space=pl.ANY),
                pl.BlockSpec((TM, LANES), lambda i, *_: (i, 0)),
                pl.BlockSpec((1, 1, 6, D_MODEL), lambda i, *_: (l, _mod_row(cfg, i), 0, 0)),
            ],
            out_specs=_stream_specs(cfg, D_MODEL, paired_out),
            scratch_shapes=[pltpu.VMEM((2, LT, D_MODEL), BF16), pltpu.SemaphoreType.DMA((2,))],
        ),
        out_shape=_stream_shapes(cfg, D_MODEL, F32, paired_out),
        compiler_params=_cparams(("arbitrary",)),
        name="combine",
    )(plan.seg_tiled, plan.seg_sorted, plan.seg_len, xm, ys, lp, mod)


class Plan(NamedTuple):
    seg_tiled: jax.Array
    seg_sorted: jax.Array
    seg_len: jax.Array
    tile_expert: jax.Array
    n_valid: jax.Array
    tile_lo: jax.Array
    tile_hi: jax.Array


def _plan(cfg, counts):
    nt = cfg.n_tiles
    m = (counts + (BF16_ROWS - 1)) // BF16_ROWS * BF16_ROWS
    local_off = jnp.cumsum(m, axis=1) - m
    within_expert = jnp.cumsum(m, axis=0) - m
    expert_rows = jnp.sum(m, axis=0)
    expert_tiles = (expert_rows + (MOE_TM - 1)) // MOE_TM
    tiles_end = jnp.cumsum(expert_tiles)
    expert_start = (tiles_end - expert_tiles) * MOE_TM
    seg_sorted = (expert_start[None, :] + within_expert).T.reshape(-1).astype(jnp.int32)
    seg_tiled = (jnp.arange(nt, dtype=jnp.int32)[:, None] * LT + local_off).T.reshape(-1).astype(jnp.int32)
    seg_len = m.T.reshape(-1).astype(jnp.int32)
    n_valid = tiles_end[-1:].astype(jnp.int32)
    tile_ids = jnp.arange(cfg.moe_tiles, dtype=jnp.int32)
    tile_expert = jnp.minimum(jnp.sum(tile_ids[:, None] >= tiles_end[None, :], axis=1),
                              N_EXPERTS - 1).astype(jnp.int32)
    tile_base = tile_ids * MOE_TM
    seg_end = seg_sorted + seg_len
    tile_lo = jnp.sum(seg_end[None, :] <= tile_base[:, None], axis=1).astype(jnp.int32)
    tile_hi = jnp.sum(seg_sorted[None, :] < tile_base[:, None] + MOE_TM, axis=1).astype(jnp.int32)
    return Plan(seg_tiled, seg_sorted, seg_len, tile_expert, n_valid, tile_lo, tile_hi)


def _rope_tables(n):
    pos = jnp.arange(n)
    rc = jnp.stack([(pos // GRID_W).astype(F32), (pos % GRID_W).astype(F32)], axis=1)
    freqs = ROPE_THETA ** (-jnp.arange(AXIS_PAIRS, dtype=F32) / AXIS_PAIRS)
    ang = rc[:, :, None] * freqs[None, None, :]
    cos = jnp.cos(ang)[:, :, None, :]
    sin = jnp.sin(ang)[:, :, None, :] * jnp.array([-1.0, 1.0], F32)[None, None, :, None]
    cos = jnp.broadcast_to(cos, (n, 2, 2, AXIS_PAIRS)).reshape(n, HEAD_DIM)
    sin = jnp.broadcast_to(sin, (n, 2, 2, AXIS_PAIRS)).reshape(n, HEAD_DIM)
    reps = LANES // HEAD_DIM
    cos = jnp.concatenate([jnp.ones((TM, HEAD_DIM), F32), cos], axis=0)
    sin = jnp.concatenate([jnp.zeros((TM, HEAD_DIM), F32), sin], axis=0)
    return jnp.tile(cos, (1, reps)), jnp.tile(sin, (1, reps))


def _block_diag_mean(width):
    r = jnp.arange(width)
    return jnp.where((r[:, None] // HEAD_DIM) == (r[None, :] // HEAD_DIM), 1.0 / HEAD_DIM, 0.0).astype(BF16)


def _forward(cfg, x_prompt, x_sample, cache_k, cache_v, c, c_ctx, w_mod, b_mod, norm1_g, norm2_g,
             w_in, q_norm_g, k_norm_g, conv_dw_w, conv_dw_b, conv_ln_g, conv_ln_b, w_out,
             w_router, b_router, w_gate, w_up, w_down):
    depth = cfg.depth
    assert cfg.ctx_seq == CONV_ROWS and cfg.t_ctx % TM == 0 and cfg.smp_seq % TM == 0
    assert cfg.t_ctx % cfg.smp_seq == 0 and cfg.smp_batch + 1 <= 8 and cfg.ctx_batch % CTX_SEQS == 0

    x = [x_prompt.reshape(cfg.t_ctx, D_MODEL), x_sample.reshape(-1, D_MODEL)]
    cvec = jnp.zeros((8, D_MODEL), F32).at[0].set(c_ctx).at[1:1 + cfg.smp_batch].set(c)
    mod = _modulation(cvec, w_mod, b_mod)

    n1 = norm1_g.reshape(depth, 1, D_MODEL)
    n2 = norm2_g.reshape(depth, 1, D_MODEL)
    gain = jnp.concatenate([jnp.tile(q_norm_g, (1, N_Q_HEADS)), jnp.tile(k_norm_g, (1, N_KV_HEADS))],
                           axis=1).reshape(depth, 1, QK_W)
    expert_w = (w_gate.reshape(-1, D_FF), w_up.reshape(-1, D_FF), w_down.reshape(-1, D_MODEL))
    taps = _conv_tap_rows(conv_dw_w)
    dw_b = conv_dw_b.reshape(depth, 1, CONV_CH)
    ln_g = conv_ln_g.reshape(depth, 1, CONV_CH)
    ln_b = conv_ln_b.reshape(depth, 1, CONV_CH)
    wr = jnp.pad(w_router, ((0, 0), (0, LANES - N_EXPERTS)))
    br = b_router.reshape(N_EXPERTS, 1)
    cos, sin = _rope_tables(cfg.smp_seq)
    bdq = _block_diag_mean(Q_W // 2)
    bdk = _block_diag_mean(KV_W)
    ck = cache_k.reshape(cfg.smp_batch, depth, cfg.past, KV_W)
    cv = cache_v.reshape(cfg.smp_batch, depth, cfg.past, KV_W)

    new_k = jnp.zeros((cfg.ctx_batch, depth, cfg.ctx_seq, KV_W), F32)
    new_v = jnp.zeros((cfg.ctx_batch, depth, cfg.ctx_seq, KV_W), F32)
    for l in range(depth):
        q, k, v, z, new_k, new_v = _inproj(cfg, l, x, mod, n1, w_in, gain, bdq, bdk, cos, sin, new_k, new_v)
        a_pair, c_pair, expert_bf = _mixer(cfg, l, q, k, v, ck, cv, z, taps, dw_b, ln_g, ln_b, expert_w)
        xm, hl, lp, cnt = _outproj(cfg, l, a_pair, c_pair, x, mod, n2, w_out, wr, br)
        plan = _plan(cfg, cnt[:, :, 0])
        ys = _moe(cfg, hl.reshape(cfg.n_tiles * LT, D_MODEL), plan, *expert_bf)
        x = _combine(cfg, l, xm, ys, lp, mod, plan, paired_out=(l == depth - 1))

    y_prompt = x[0].reshape(cfg.ctx_batch, cfg.ctx_seq, D_MODEL)
    y_sample = x[1].reshape(cfg.smp_batch, cfg.smp_seq, D_MODEL)
    cache_shape = (cfg.ctx_batch, depth, cfg.ctx_seq, N_KV_HEADS, HEAD_DIM)
    return y_prompt, y_sample, new_k.reshape(cache_shape), new_v.reshape(cache_shape)


def kernel(x_prompt, x_sample, cache_k, cache_v, c, c_ctx, w_mod, b_mod, norm1_g, norm2_g, w_in, q_norm_g,
           k_norm_g, conv_dw_w, conv_dw_b, conv_ln_g, conv_ln_b, w_out, w_router, b_router, w_gate, w_up, w_down):
    cfg = Cfg(ctx_batch=x_prompt.shape[0], ctx_seq=x_prompt.shape[1], smp_batch=x_sample.shape[0],
              smp_seq=x_sample.shape[1], past=cache_k.shape[2], depth=w_mod.shape[0])
    return _forward(cfg, x_prompt, x_sample, cache_k, cache_v, c, c_ctx, w_mod, b_mod, norm1_g, norm2_g,
                    w_in, q_norm_g, k_norm_g, conv_dw_w, conv_dw_b, conv_ln_g, conv_ln_b, w_out,
                    w_router, b_router, w_gate, w_up, w_down)
```

```python
import functools
import math
from typing import NamedTuple

import jax
import jax.numpy as jnp
from jax import lax
from jax.experimental import pallas as pl
from jax.experimental.pallas import tpu as pltpu

F32 = jnp.float32
BF16 = jnp.bfloat16

D_MODEL = 1024
HEAD_DIM = 64
N_Q_HEADS = 8
N_KV_HEADS = 2
Q_PER_KV = N_Q_HEADS // N_KV_HEADS
Q_W = N_Q_HEADS * HEAD_DIM
KV_W = N_KV_HEADS * HEAD_DIM
QK_W = Q_W + KV_W
CONV_CH = D_MODEL // 2
CONV_K = 31
CONV_PAD = CONV_K // 2
IN_COLS = Q_W + 2 * KV_W + 2 * CONV_CH
GRID_W = 64
AXIS_PAIRS = HEAD_DIM // 4
ROPE_THETA = 10000.0
N_EXPERTS = 16
N_GROUPS = 4
EXPERTS_PER_GROUP = N_EXPERTS // N_GROUPS
D_FF = 512
EPS = 1e-6

LANES = 128
BF16_ROWS = 16
TM = 512
TOP_K = 2
LT = 1280
MOE_TM = 512
MOE_PAIR = 2
COMBINE_TILES = 2
CONV_ROWS = 256
CTX_SEQS = 4
HALO = 16
CONV_INNER = CONV_ROWS + 16
PAD_ROWS = 24 + CONV_INNER + 8
SEG_SIZES = (512, 256, 128, 64, 32, 16)
SEG_SPLIT = 128
Q_SCALE = (1.0 / math.sqrt(HEAD_DIM)) * math.log2(math.e)
VMEM_LIMIT = 48 * 1024 * 1024
MIXER_VMEM_LIMIT = 56 * 1024 * 1024


class Cfg(NamedTuple):
    ctx_batch: int
    ctx_seq: int
    smp_batch: int
    smp_seq: int
    past: int
    depth: int

    @property
    def t_ctx(self):
        return self.ctx_batch * self.ctx_seq

    @property
    def t_all(self):
        return self.t_ctx + self.smp_batch * self.smp_seq

    @property
    def n_tiles(self):
        return self.t_all // TM

    @property
    def n_ctx_tiles(self):
        return self.t_ctx // TM

    @property
    def smp_tiles(self):
        return self.smp_seq // TM

    @property
    def moe_tiles(self):
        rows = TOP_K * self.t_all + self.n_tiles * N_EXPERTS * (BF16_ROWS - 1) + N_EXPERTS * (MOE_TM - 1)
        step_rows = MOE_PAIR * MOE_TM
        return -(-rows // step_rows) * MOE_PAIR


def _mod_row(cfg, i):
    return jnp.where(i < cfg.n_ctx_tiles, 0, 1 + (i - cfg.n_ctx_tiles) // cfg.smp_tiles)


def _pair_specs(cfg, width, tile_of=lambda i: i):
    nct = cfg.n_ctx_tiles
    return [pl.BlockSpec((TM, width), lambda i, *_: (jnp.minimum(tile_of(i), nct - 1), 0)),
            pl.BlockSpec((TM, width), lambda i, *_: (jnp.maximum(tile_of(i) - nct, 0), 0))]


def _pair_shapes(cfg, width, dtype):
    return [jax.ShapeDtypeStruct((cfg.t_ctx, width), dtype),
            jax.ShapeDtypeStruct((cfg.t_all - cfg.t_ctx, width), dtype)]


def _load_pair(is_ctx, c_ref, s_ref):
    return jnp.where(is_ctx, c_ref[...], s_ref[...])


def _store_pair(is_ctx, c_ref, s_ref, val):
    @pl.when(is_ctx)
    def _():
        c_ref[...] = val

    @pl.when(jnp.logical_not(is_ctx))
    def _():
        s_ref[...] = val


def _stream_specs(cfg, width, paired, tile_of=lambda i: i):
    if paired:
        return _pair_specs(cfg, width, tile_of)
    return [pl.BlockSpec((TM, width), lambda i, *_: (tile_of(i), 0))]


def _stream_shapes(cfg, width, dtype, paired):
    if paired:
        return _pair_shapes(cfg, width, dtype)
    return [jax.ShapeDtypeStruct((cfg.t_all, width), dtype)]


def _load_stream(is_ctx, refs):
    return _load_pair(is_ctx, *refs) if len(refs) == 2 else refs[0][...]


def _cparams(sem, vmem=VMEM_LIMIT):
    return pltpu.CompilerParams(dimension_semantics=sem, vmem_limit_bytes=vmem)


def _dot(a, b):
    return jnp.dot(a, b, preferred_element_type=F32)


def _dot_nt(a, b):
    return lax.dot_general(a, b, (((1,), (1,)), ((), ())), preferred_element_type=F32)


def _sigmoid(x):
    return 1.0 / (1.0 + jnp.exp(-x))


def _mod_kernel(c_ref, w_ref, b_ref, o_ref):
    c = c_ref[...]
    a = (c * _sigmoid(c)).astype(BF16)
    o_ref[0] = _dot(a, w_ref[0].astype(BF16)) + b_ref[0]


def _modulation(cvec, w_mod, b_mod):
    depth = w_mod.shape[0]
    out = pl.pallas_call(
        _mod_kernel,
        grid=(depth, 6),
        in_specs=[
            pl.BlockSpec((8, D_MODEL), lambda l, j: (0, 0)),
            pl.BlockSpec((1, D_MODEL, D_MODEL), lambda l, j: (l, 0, j)),
            pl.BlockSpec((1, 1, D_MODEL), lambda l, j: (l, 0, j)),
        ],
        out_specs=pl.BlockSpec((1, 8, D_MODEL), lambda l, j: (l, 0, j)),
        out_shape=jax.ShapeDtypeStruct((depth, 8, 6 * D_MODEL), F32),
        compiler_params=_cparams(("arbitrary", "arbitrary")),
        name="modulation",
    )(cvec, w_mod, b_mod.reshape(depth, 1, 6 * D_MODEL))
    return out.reshape(depth, 8, 6, D_MODEL)


def _inproj_kernel(*refs, n_tiles, n_ctx_tiles, ctx_seq, n_x):
    x_refs = refs[:n_x]
    (mod_ref, n1_ref, w_ref, gain_ref, bdq_ref, bdk_ref, cos_ref, sin_ref,
     kc_in, vc_in, q_ref, k_ref, v_ref, z_ref, kc_ref, vc_ref, w_bf, proj_prev) = refs[n_x:]
    del kc_in, vc_in
    step = pl.program_id(0)

    @pl.when(step == 0)
    def _():
        w_bf[...] = w_ref[0].astype(BF16)
        proj_prev[...] = jnp.zeros_like(proj_prev)

    x = _load_stream(jnp.minimum(step, n_tiles - 1) < n_ctx_tiles, x_refs)
    mod = mod_ref[0, 0]
    sh1, sc1 = mod[0:1], mod[1:2]
    ms = jnp.mean(x * x, axis=-1, keepdims=True)
    h = (x * lax.rsqrt(ms + EPS) * n1_ref[0]) * (1.0 + sc1) + sh1
    proj_new = _dot(h.astype(BF16), w_bf[...])

    qk = proj_prev[:, :QK_W]
    sq = (qk * qk).astype(BF16)
    half = Q_W // 2
    msq = jnp.concatenate([_dot(sq[:, :half], bdq_ref[...]),
                           _dot(sq[:, half:Q_W], bdq_ref[...]),
                           _dot(sq[:, Q_W:], bdk_ref[...])], axis=1)
    qkn = qk * lax.rsqrt(msq + EPS) * gain_ref[0]
    v = proj_prev[:, QK_W:QK_W + KV_W]
    v_ref[...] = v
    a = proj_prev[:, QK_W + KV_W:QK_W + KV_W + CONV_CH]
    b = proj_prev[:, QK_W + KV_W + CONV_CH:]
    z_ref[...] = a * _sigmoid(b)
    cos = cos_ref[...]
    sin = sin_ref[...]
    first = (lax.broadcasted_iota(jnp.int32, (TM, LANES), 1) % 32) < 16
    for c in range(QK_W // LANES):
        blk = qkn[:, c * LANES:(c + 1) * LANES]
        partner = jnp.where(first, pltpu.roll(blk, LANES - 16, 1), pltpu.roll(blk, 16, 1))
        rot = blk * cos + partner * sin
        if c < Q_W // LANES:
            q_ref[:, c * LANES:(c + 1) * LANES] = (rot * Q_SCALE).astype(BF16)
        else:
            k = rot
            k_ref[...] = k

    proj_prev[...] = proj_new

    @pl.when(step - 1 < n_ctx_tiles)
    def _():
        for s in range(TM // ctx_seq):
            kc_ref[s, 0] = k[s * ctx_seq:(s + 1) * ctx_seq]
            vc_ref[s, 0] = v[s * ctx_seq:(s + 1) * ctx_seq]


def _inproj(cfg, l, x, mod, n1, w_in, gain, bdq, bdk, cos, sin, kc, vc):
    t = cfg.t_all
    n_x = len(x)
    nt, nct, st = cfg.n_tiles, cfg.n_ctx_tiles, cfg.smp_tiles
    tile_a = lambda i: jnp.minimum(i, nt - 1)
    tile_b = lambda i: jnp.maximum(i - 1, 0)
    rope_map = lambda i: (jnp.where(tile_b(i) < nct, 0, 1 + (tile_b(i) - nct) % st), 0)
    seqs = TM // cfg.ctx_seq
    cache_spec = lambda: pl.BlockSpec((seqs, 1, cfg.ctx_seq, KV_W),
                                      lambda i: (jnp.minimum(tile_b(i), nct - 1), l, 0, 0))
    return pl.pallas_call(
        functools.partial(_inproj_kernel, n_tiles=nt, n_ctx_tiles=nct, ctx_seq=cfg.ctx_seq, n_x=n_x),
        grid=(nt + 1,),
        in_specs=_stream_specs(cfg, D_MODEL, n_x == 2, tile_a) + [
            pl.BlockSpec((1, 1, 6, D_MODEL), lambda i: (l, _mod_row(cfg, tile_a(i)), 0, 0)),
            pl.BlockSpec((1, 1, D_MODEL), lambda i: (l, 0, 0)),
            pl.BlockSpec((1, D_MODEL, IN_COLS), lambda i: (l, 0, 0)),
            pl.BlockSpec((1, 1, QK_W), lambda i: (l, 0, 0)),
            pl.BlockSpec((Q_W // 2, Q_W // 2), lambda i: (0, 0)),
            pl.BlockSpec((KV_W, KV_W), lambda i: (0, 0)),
            pl.BlockSpec((TM, LANES), rope_map),
            pl.BlockSpec((TM, LANES), rope_map),
            pl.BlockSpec(memory_space=pl.ANY),
            pl.BlockSpec(memory_space=pl.ANY),
        ],
        out_specs=[
            pl.BlockSpec((TM, Q_W), lambda i: (tile_b(i), 0)),
            pl.BlockSpec((TM, KV_W), lambda i: (tile_b(i), 0)),
            pl.BlockSpec((TM, KV_W), lambda i: (tile_b(i), 0)),
            pl.BlockSpec((TM, CONV_CH), lambda i: (tile_b(i), 0)),
            cache_spec(), cache_spec(),
        ],
        out_shape=[
            jax.ShapeDtypeStruct((t, Q_W), BF16),
            jax.ShapeDtypeStruct((t, KV_W), F32),
            jax.ShapeDtypeStruct((t, KV_W), F32),
            jax.ShapeDtypeStruct((t, CONV_CH), F32),
            jax.ShapeDtypeStruct(kc.shape, F32),
            jax.ShapeDtypeStruct(vc.shape, F32),
        ],
        scratch_shapes=[pltpu.VMEM((D_MODEL, IN_COLS), BF16), pltpu.VMEM((TM, IN_COLS), F32)],
        input_output_aliases={n_x + 8: 4, n_x + 9: 5},
        compiler_params=_cparams(("arbitrary",)),
        name="inproj",
    )(*x, mod, n1, w_in, gain, bdq, bdk, cos, sin, kc, vc)


def _mixer_kernel(*refs, rq, n_past, seq_blocks, n_seq):
    if n_past:
        q_ref, k_ref, v_ref, ck_ref, cv_ref = refs[:5]
        refs = refs[5:]
    else:
        q_ref, k_ref, v_ref = refs[:3]
        refs = refs[3:]
    z_ref, zp_ref, zn_ref, w_ref, b_ref, g_ref, beta_ref = refs[:7]
    refs = refs[7:]
    if n_past:
        for src, dst in zip(refs[0:3], refs[5:8]):
            dst[...] = src[...].astype(BF16)
        refs = refs[3:5] + refs[8:]
    o_ref, co_ref, krep, vrep, pad, taps = refs
    head_w = Q_PER_KV * HEAD_DIM
    j = pl.program_id(1)

    @pl.when(jnp.logical_and(pl.program_id(0) == 0, j == 0))
    def _():
        diag = (lax.broadcasted_iota(jnp.int32, (LANES, LANES), 0)
                == lax.broadcasted_iota(jnp.int32, (LANES, LANES), 1))
        for c in range(CONV_CH // LANES):
            for p in range(4):
                for q in range(4):
                    for half in range(2):
                        tap = w_ref[0, 8 * q + 2 * p + half:8 * q + 2 * p + half + 1, c * LANES:(c + 1) * LANES]
                        taps[c, p, q * LANES:(q + 1) * LANES, half * LANES:(half + 1) * LANES] = (
                            jnp.where(diag, tap, 0.0).astype(BF16))

    n_new = k_ref.shape[0] // n_seq

    @pl.when(j == 0)
    def _():
        src = lax.broadcasted_iota(jnp.int32, (KV_W, head_w), 0)
        dst = lax.broadcasted_iota(jnp.int32, (KV_W, head_w), 1) % HEAD_DIM
        for h in range(N_KV_HEADS):
            rep = jnp.where(src == dst + h * HEAD_DIM, 1.0, 0.0).astype(BF16)
            if n_past:
                krep[0, h, :n_past, :] = _dot(ck_ref[0, 0].astype(BF16), rep).astype(BF16)
                vrep[0, h, :n_past, :] = _dot(cv_ref[0, 0].astype(BF16), rep).astype(BF16)
            for s in range(n_seq):
                rows = slice(s * n_new, (s + 1) * n_new)
                krep[s, h, n_past:n_past + n_new, :] = _dot(k_ref[rows, :].astype(BF16), rep).astype(BF16)
                vrep[s, h, n_past:n_past + n_new, :] = _dot(v_ref[rows, :].astype(BF16), rep).astype(BF16)

    lane_head = lax.broadcasted_iota(jnp.int32, (rq, head_w), 1) // HEAD_DIM
    end = HALO + CONV_ROWS
    for s in range(n_seq):
        rows = slice(s * rq, (s + 1) * rq)
        if seq_blocks > 1:
            pad[0:HALO, :] = jnp.where(j > 0, zp_ref[...], 0.0)
            pad[end:end + HALO, :] = jnp.where(j < seq_blocks - 1, zn_ref[...], 0.0)
        else:
            pad[0:HALO, :] = jnp.zeros((HALO, CONV_CH), F32)
            pad[end:end + HALO, :] = jnp.zeros((HALO, CONV_CH), F32)
        pad[end + HALO:, :] = jnp.zeros((PAD_ROWS - end - HALO, CONV_CH), F32)
        pad[HALO:end, :] = z_ref[rows, :]

        for h in range(N_KV_HEADS):
            qh = q_ref[rows, h * head_w:(h + 1) * head_w]
            kh = krep[s, h]
            vh = vrep[s, h]
            acc = jnp.zeros((rq, head_w), F32)
            for g in range(Q_PER_KV):
                mask = lane_head == g
                qm = jnp.where(mask, qh, jnp.zeros_like(qh))
                sc = _dot_nt(qm, kh)
                m = jnp.max(sc, axis=-1, keepdims=True)
                p = jnp.exp2(sc - m)
                denom = jnp.sum(p, axis=-1, keepdims=True)
                o = _dot(p.astype(BF16), vh)
                acc = jnp.where(mask, o * (1.0 / denom), acc)
            o_ref[rows, h * head_w:(h + 1) * head_w] = acc.astype(BF16)
        conv_cols = [_conv_cols(pad, taps, c) for c in range(CONV_CH // LANES)]
        co_ref[rows, :] = _conv_finish(conv_cols, b_ref, g_ref, beta_ref)


def _mixer(cfg, l, q, k, v, cache_k, cache_v, z, taps, dw_b, ln_g, ln_b, expert_w):
    head_w = Q_PER_KV * HEAD_DIM
    rq = CONV_ROWS
    hb = CONV_ROWS // HALO
    last = cfg.t_all // HALO - 1
    vec = lambda: pl.BlockSpec((1, 1, CONV_CH), lambda b, j: (l, 0, 0))

    def conv_specs(row_block, n_seq=1):
        return [
            pl.BlockSpec((n_seq * CONV_ROWS, CONV_CH), lambda b, j: (row_block(b, j), 0)),
            pl.BlockSpec((HALO, CONV_CH), lambda b, j: (jnp.maximum(row_block(b, j) * hb - 1, 0), 0)),
            pl.BlockSpec((HALO, CONV_CH), lambda b, j: (jnp.minimum(row_block(b, j) * hb + hb, last), 0)),
            pl.BlockSpec((1, 32, CONV_CH), lambda b, j: (l, 0, 0)),
            vec(), vec(), vec(),
        ]

    def scratch(sk, n_seq=1):
        return [pltpu.VMEM((n_seq, N_KV_HEADS, sk, head_w), BF16), pltpu.VMEM((n_seq, N_KV_HEADS, sk, head_w), BF16),
                pltpu.VMEM((PAD_ROWS, CONV_CH), F32),
                pltpu.VMEM((CONV_CH // LANES, 4, 4 * LANES, 2 * LANES), BF16)]

    sc = cfg.ctx_seq
    rows = CTX_SEQS * sc
    a_ctx, c_ctx = pl.pallas_call(
        functools.partial(_mixer_kernel, rq=rq, n_past=0, seq_blocks=1, n_seq=CTX_SEQS),
        grid=(cfg.ctx_batch // CTX_SEQS, 1),
        in_specs=[
            pl.BlockSpec((rows, Q_W), lambda b, j: (b, 0)),
            pl.BlockSpec((rows, KV_W), lambda b, j: (b, 0)),
            pl.BlockSpec((rows, KV_W), lambda b, j: (b, 0)),
        ] + conv_specs(lambda b, j: b, CTX_SEQS),
        out_specs=[pl.BlockSpec((rows, Q_W), lambda b, j: (b, 0)),
                   pl.BlockSpec((rows, CONV_CH), lambda b, j: (b, 0))],
        out_shape=[jax.ShapeDtypeStruct((cfg.t_ctx, Q_W), BF16),
                   jax.ShapeDtypeStruct((cfg.t_ctx, CONV_CH), BF16)],
        scratch_shapes=scratch(sc, CTX_SEQS),
        compiler_params=_cparams(("arbitrary", "arbitrary")),
        name="mixer_ctx",
    )(q, k, v, z, z, z, taps, dw_b, ln_g, ln_b)
    ss = cfg.smp_seq
    sk = cfg.past + ss
    kv_base = cfg.t_ctx // ss
    q_base = cfg.t_ctx // rq
    nqb = ss // rq
    t_smp = cfg.smp_batch * ss
    steps = cfg.smp_batch * nqb
    w_rows = [w.shape[0] // (cfg.depth * steps) for w in expert_w]
    w_specs = lambda off: [pl.BlockSpec((r, w.shape[1]), lambda b, j: (off + b * nqb + j, 0))
                           for r, w in zip(w_rows, expert_w)]
    a_smp, c_smp, *expert_bf = pl.pallas_call(
        functools.partial(_mixer_kernel, rq=rq, n_past=cfg.past, seq_blocks=nqb, n_seq=1),
        grid=(cfg.smp_batch, nqb),
        in_specs=[
            pl.BlockSpec((rq, Q_W), lambda b, j: (q_base + b * nqb + j, 0)),
            pl.BlockSpec((ss, KV_W), lambda b, j: (kv_base + b, 0)),
            pl.BlockSpec((ss, KV_W), lambda b, j: (kv_base + b, 0)),
            pl.BlockSpec((1, 1, cfg.past, KV_W), lambda b, j: (b, l, 0, 0)),
            pl.BlockSpec((1, 1, cfg.past, KV_W), lambda b, j: (b, l, 0, 0)),
        ] + conv_specs(lambda b, j: q_base + b * nqb + j) + w_specs(l * steps),
        out_specs=[pl.BlockSpec((rq, Q_W), lambda b, j: (b * nqb + j, 0)),
                   pl.BlockSpec((rq, CONV_CH), lambda b, j: (b * nqb + j, 0))] + w_specs(0),
        out_shape=[jax.ShapeDtypeStruct((t_smp, Q_W), BF16),
                   jax.ShapeDtypeStruct((t_smp, CONV_CH), BF16)]
        + [jax.ShapeDtypeStruct((r * steps, w.shape[1]), BF16) for r, w in zip(w_rows, expert_w)],
        scratch_shapes=scratch(sk),
        compiler_params=_cparams(("arbitrary", "arbitrary"), MIXER_VMEM_LIMIT),
        name="mixer_smp",
    )(q, k, v, cache_k, cache_v, z, z, z, taps, dw_b, ln_g, ln_b, *expert_w)
    return (a_ctx, a_smp), (c_ctx, c_smp), expert_bf


def _conv_cols(pad, taps, c):
    cols = slice(c * LANES, (c + 1) * LANES)
    lhs = jnp.concatenate([pad[8 * q:8 * q + CONV_INNER, cols].astype(BF16) for q in range(4)], axis=1)
    acc = None
    for p in range(4):
        pair = _dot(lhs, taps[c, p])
        for half in range(2):
            o = 2 * p + half
            shifted = pair[o:o + CONV_ROWS, half * LANES:(half + 1) * LANES]
            acc = shifted if acc is None else acc + shifted
    return acc


def _conv_tap_rows(conv_dw_w):
    return jnp.pad(conv_dw_w, ((0, 0), (HALO - CONV_PAD, 32 - CONV_K - (HALO - CONV_PAD)), (0, 0)))


def _conv_finish(blocks, b_ref, g_ref, beta_ref):
    y = jnp.concatenate(blocks, axis=1) + b_ref[0]
    mu = jnp.mean(y, axis=-1, keepdims=True)
    yc = y - mu
    var = jnp.mean(yc * yc, axis=-1, keepdims=True)
    yn = yc * lax.rsqrt(var + EPS) * g_ref[0] + beta_ref[0]
    return (yn * _sigmoid(yn)).astype(BF16)


def _split_bf16(x):
    hi = x.astype(BF16)
    lo = (x - hi.astype(F32)).astype(BF16)
    return hi, lo


def _outproj_kernel(ac_ref, as_ref, cc_ref, cs_ref, *refs, n_tiles, n_ctx_tiles, n_x):
    x_refs = refs[:n_x]
    (mod_ref, n2_ref, wo_ref, wr_ref, br_ref, xm_ref, hl_ref, lp_ref, cnt_ref,
     wo_bf, h_prev, lt_prev, before_bf, eye_bf, rows_i) = refs[n_x:]
    step = pl.program_id(0)

    @pl.when(step == 0)
    def _():
        wo_bf[...] = wo_ref[0].astype(BF16)
        h_prev[...] = jnp.zeros_like(h_prev)
        lt_prev[...] = jnp.zeros_like(lt_prev)
        r = lax.broadcasted_iota(jnp.int32, (TM, TM), 0)
        c = lax.broadcasted_iota(jnp.int32, (TM, TM), 1)
        before_bf[...] = jnp.where(r < c, 1.0, 0.0).astype(BF16)
        eye_bf[...] = jnp.where(r == c, 1.0, 0.0).astype(BF16)
        rows_i[...] = lax.broadcasted_iota(jnp.int32, (LT, TM), 0)

    is_ctx = jnp.minimum(step, n_tiles - 1) < n_ctx_tiles
    ac = jnp.concatenate([_load_pair(is_ctx, ac_ref, as_ref), _load_pair(is_ctx, cc_ref, cs_ref)], axis=1)
    y = _dot(ac, wo_bf[...])
    mod = mod_ref[0, 0]
    g1, sh2, sc2 = mod[2:3], mod[3:4], mod[4:5]
    xm = _load_stream(is_ctx, x_refs) + g1 * y
    xm_ref[...] = xm
    ms = jnp.mean(xm * xm, axis=-1, keepdims=True)
    h2 = (xm * lax.rsqrt(ms + EPS) * n2_ref[0]) * (1.0 + sc2) + sh2
    h_new, h_lo = _split_bf16(h2)
    w_hi, w_lo = _split_bf16(wr_ref[...])
    logits = _dot(h_new, w_hi) + (_dot(h_new, w_lo) + _dot(h_lo, w_hi))
    lt_new = logits.T[0:N_EXPERTS, :]

    h_hi = h_prev[...]
    s_all = _sigmoid(lt_prev[...])
    sb_all = s_all + br_ref[...]

    rows8 = 2 * EXPERTS_PER_GROUP
    row = lax.broadcasted_iota(jnp.int32, (rows8, TM), 0)
    member = row % EXPERTS_PER_GROUP
    second = row >= EXPERTS_PER_GROUP

    def partner(x, k):
        return jnp.where(member + k < EXPERTS_PER_GROUP,
                         pltpu.roll(x, rows8 - k, 0), pltpu.roll(x, EXPERTS_PER_GROUP - k, 0))

    def other_group(x):
        return pltpu.roll(x, EXPERTS_PER_GROUP, 0)

    def group_sum(x):
        return x + partner(x, 1) + (partner(x, 2) + partner(x, 3))

    def ahead(other, mine, other_first):
        return jnp.where((other >= mine) if other_first else (other > mine), 1.0, 0.0)

    sel, gscore = [], []
    for hv in range(2):
        sb8 = sb_all[hv * rows8:(hv + 1) * rows8]
        rank = jnp.zeros((rows8, TM), F32)
        for k in range(1, EXPERTS_PER_GROUP):
            pk = partner(sb8, k)
            rank = rank + jnp.where(member + k >= EXPERTS_PER_GROUP, ahead(pk, sb8, True), ahead(pk, sb8, False))
        sel.append(rank < 1.5)
        gscore.append(group_sum(jnp.where(sel[hv], sb8, 0.0)))
    gate8, used8 = [], []
    for hv in range(2):
        mine, far = gscore[hv], gscore[1 - hv]
        near = other_group(mine)
        rank = jnp.where(second, ahead(near, mine, True), ahead(near, mine, False))
        rank = rank + ahead(far, mine, hv == 1) + ahead(other_group(far), mine, hv == 1)
        chosen = rank < 0.5
        used8.append(jnp.where(chosen, jnp.where(sel[hv], 1.0, 0.0), 0.0))
        gate8.append(used8[hv] * s_all[hv * rows8:(hv + 1) * rows8])
    total = group_sum(gate8[0]) + group_sum(gate8[1])
    total = total + other_group(total)
    inv_total = 1.0 / total
    gate16 = jnp.concatenate([gate8[0] * inv_total, gate8[1] * inv_total], axis=0)
    used16 = jnp.concatenate(used8, axis=0)
    used = used16 > 0.5

    prior = _dot(used16.astype(BF16), before_bf[...])
    n_e = jnp.sum(used16, axis=-1, keepdims=True)
    m_e = jnp.floor((n_e + (BF16_ROWS - 1)) * (1.0 / BF16_ROWS)) * BF16_ROWS
    lower = (lax.broadcasted_iota(jnp.int32, (N_EXPERTS, N_EXPERTS), 0)
             > lax.broadcasted_iota(jnp.int32, (N_EXPERTS, N_EXPERTS), 1))
    start = _dot(jnp.where(lower, 1.0, 0.0).astype(BF16),
                 jnp.broadcast_to(m_e, (N_EXPERTS, LANES)).astype(BF16))[:, 0:1]
    pos16 = start + prior
    pos_a = jnp.min(jnp.where(used, pos16, float(LT)), axis=0, keepdims=True)
    pos_b = jnp.max(jnp.where(used, pos16, -1.0), axis=0, keepdims=True)
    gate_a = jnp.sum(jnp.where(used, jnp.where(pos16 == pos_a, gate16, 0.0), 0.0), axis=0, keepdims=True)
    gate_b = jnp.sum(jnp.where(used, jnp.where(pos16 == pos_b, gate16, 0.0), 0.0), axis=0, keepdims=True)
    rows = rows_i[...]
    perm = jnp.where(rows == pos_a.astype(jnp.int32), 1.0,
                     jnp.where(rows == pos_b.astype(jnp.int32), 1.0, 0.0)).astype(BF16)
    hl_ref[0] = _dot(perm, h_hi).astype(BF16)

    def split32(p):
        hi = jnp.floor(p * (1.0 / 32.0))
        return [hi, p - 32.0 * hi]

    def split3(g):
        hi = g.astype(BF16).astype(F32)
        mid = (g - hi).astype(BF16).astype(F32)
        lo = ((g - hi) - mid).astype(BF16).astype(F32)
        return [hi, mid, lo]

    parts = split32(pos_a) + split32(pos_b) + split3(gate_a) + split3(gate_b)
    lp_parts = jnp.concatenate(parts + [jnp.zeros((LANES - len(parts), TM), F32)], axis=0).astype(BF16)
    lp_ref[...] = _dot_nt(eye_bf[...], lp_parts)

    cnt_ref[0] = jnp.broadcast_to(n_e, (N_EXPERTS, LANES)).astype(jnp.int32)

    h_prev[...] = h_new
    lt_prev[...] = lt_new


def _outproj(cfg, l, a_pair, c_pair, x, mod, n2, w_out, w_router, b_router):
    t, nt, nct = cfg.t_all, cfg.n_tiles, cfg.n_ctx_tiles
    n_x = len(x)
    tile_a = lambda i: jnp.minimum(i, nt - 1)
    tile_b = lambda i: jnp.maximum(i - 1, 0)
    return pl.pallas_call(
        functools.partial(_outproj_kernel, n_tiles=nt, n_ctx_tiles=nct, n_x=n_x),
        grid=(nt + 1,),
        in_specs=_pair_specs(cfg, Q_W, tile_a) + _pair_specs(cfg, CONV_CH, tile_a)
        + _stream_specs(cfg, D_MODEL, n_x == 2, tile_a) + [
            pl.BlockSpec((1, 1, 6, D_MODEL), lambda i: (l, _mod_row(cfg, tile_a(i)), 0, 0)),
            pl.BlockSpec((1, 1, D_MODEL), lambda i: (l, 0, 0)),
            pl.BlockSpec((1, D_MODEL, D_MODEL), lambda i: (l, 0, 0)),
            pl.BlockSpec((D_MODEL, LANES), lambda i: (0, 0)),
            pl.BlockSpec((N_EXPERTS, 1), lambda i: (0, 0)),
        ],
        out_specs=_stream_specs(cfg, D_MODEL, False, tile_a) + [
            pl.BlockSpec((1, LT, D_MODEL), lambda i: (tile_b(i), 0, 0)),
            pl.BlockSpec((TM, LANES), lambda i: (tile_b(i), 0)),
            pl.BlockSpec((1, N_EXPERTS, LANES), lambda i: (tile_b(i), 0, 0)),
        ],
        out_shape=_stream_shapes(cfg, D_MODEL, F32, False) + [
            jax.ShapeDtypeStruct((nt, LT, D_MODEL), BF16),
            jax.ShapeDtypeStruct((t, LANES), F32),
            jax.ShapeDtypeStruct((nt, N_EXPERTS, LANES), jnp.int32),
        ],
        scratch_shapes=[pltpu.VMEM((D_MODEL, D_MODEL), BF16), pltpu.VMEM((TM, D_MODEL), BF16),
                        pltpu.VMEM((N_EXPERTS, TM), F32), pltpu.VMEM((TM, TM), BF16), pltpu.VMEM((TM, TM), BF16),
                        pltpu.VMEM((LT, TM), jnp.int32)],
        compiler_params=_cparams(("arbitrary",)),
        name="outproj",
    )(*a_pair, *c_pair, *x, mod, n2, w_out, w_router, b_router)


def _start_pieces(n, src_row, dst_row, src_ref, dst_ref, sem):
    def piece(size):
        off = n & ~(2 * size - 1)

        @pl.when((n & size) != 0)
        def _():
            pltpu.make_async_copy(
                src_ref.at[pl.ds(pl.multiple_of(src_row + off, BF16_ROWS), size)],
                dst_ref.at[pl.ds(pl.multiple_of(dst_row + off, BF16_ROWS), size)],
                sem).start()

    @pl.when(n >= SEG_SPLIT)
    def _():
        for size in SEG_SIZES:
            if size >= SEG_SPLIT:
                piece(size)

    for size in SEG_SIZES:
        if size < SEG_SPLIT:
            piece(size)


def _wait_rows(total, sizes, src_ref, dst_ref, sem):
    for size in sizes:
        @pl.when((total & size) != 0)
        def _():
            pltpu.make_async_copy(src_ref.at[pl.ds(0, size)], dst_ref.at[pl.ds(0, size)], sem).wait()


def _moe_kernel(tg_ref, nv_ref, ssrc_ref, sdst_ref, slen_ref, tlo_ref, thi_ref, hl_ref, *refs):
    del tg_ref
    w_refs = [refs[3 * k:3 * k + 3] for k in range(MOE_PAIR)]
    y_ref, xbuf, sem = refs[3 * MOE_PAIR:]
    i = pl.program_id(0)
    nv = nv_ref[0]

    def overlap(s, tile):
        base = tile * MOE_TM
        d = sdst_ref[s]
        lo = jnp.maximum(d, base)
        hi = jnp.minimum(d + slen_ref[s], base + MOE_TM)
        return jnp.maximum(hi - lo, 0), lo

    def start_step(step, slot):
        for k in range(MOE_PAIR):
            tile = step * MOE_PAIR + k

            @pl.when(tile < nv)
            def _():
                def body(s, carry):
                    n, lo = overlap(s, tile)
                    _start_pieces(n, ssrc_ref[s] + (lo - sdst_ref[s]), lo - tile * MOE_TM,
                                  hl_ref, xbuf.at[slot, k], sem.at[slot])
                    return carry

                lax.fori_loop(tlo_ref[tile], thi_ref[tile], body, 0)

    def wait_step(step, slot):
        for k in range(MOE_PAIR):
            tile = step * MOE_PAIR + k
            total = lax.fori_loop(tlo_ref[tile], thi_ref[tile], lambda s, acc: acc + overlap(s, tile)[0], 0)
            _wait_rows(total, SEG_SIZES, hl_ref, xbuf.at[slot, k], sem.at[slot])

    @pl.when(i == 0)
    def _():
        xbuf[...] = jnp.zeros_like(xbuf)
        start_step(0, 0)

    @pl.when((i + 1) * MOE_PAIR < nv)
    def _():
        start_step(i + 1, (i + 1) % 2)

    @pl.when(i * MOE_PAIR >= nv)
    def _():
        y_ref[...] = jnp.zeros_like(y_ref)

    @pl.when(i * MOE_PAIR < nv)
    def _():
        slot = i % 2
        wait_step(i, slot)
        for k, (wg_ref, wu_ref, wd_ref) in enumerate(w_refs):
            x = xbuf[slot, k]
            hg = _dot(x, wg_ref[0])
            hu = _dot(x, wu_ref[0])
            hid = ((hg * _sigmoid(hg)) * hu).astype(BF16)
            y_ref[k * MOE_TM:(k + 1) * MOE_TM, :] = _dot(hid, wd_ref[0]).astype(BF16)


def _moe(cfg, hl, plan, w_gate, w_up, w_down):
    n_steps = cfg.moe_tiles // MOE_PAIR

    def w_map(k):
        return lambda i, tg, nv, *_: (tg[jnp.minimum(i * MOE_PAIR + k, nv[0] - 1)], 0, 0)

    w_gate = w_gate.reshape(N_EXPERTS, D_MODEL, D_FF)
    w_up = w_up.reshape(N_EXPERTS, D_MODEL, D_FF)
    w_down = w_down.reshape(N_EXPERTS, D_FF, D_MODEL)
    w_specs = []
    for k in range(MOE_PAIR):
        w_specs += [pl.BlockSpec((1, D_MODEL, D_FF), w_map(k)), pl.BlockSpec((1, D_MODEL, D_FF), w_map(k)),
                    pl.BlockSpec((1, D_FF, D_MODEL), w_map(k))]

    return pl.pallas_call(
        _moe_kernel,
        grid_spec=pltpu.PrefetchScalarGridSpec(
            num_scalar_prefetch=7,
            grid=(n_steps,),
            in_specs=[pl.BlockSpec(memory_space=pl.ANY)] + w_specs,
            out_specs=pl.BlockSpec((MOE_PAIR * MOE_TM, D_MODEL), lambda i, *_: (i, 0)),
            scratch_shapes=[pltpu.VMEM((2, MOE_PAIR, MOE_TM, D_MODEL), BF16), pltpu.SemaphoreType.DMA((2,))],
        ),
        out_shape=jax.ShapeDtypeStruct((n_steps * MOE_PAIR * MOE_TM, D_MODEL), BF16),
        compiler_params=_cparams(("arbitrary",)),
        name="moe",
    )(plan.tile_expert, plan.n_valid, plan.seg_tiled, plan.seg_sorted, plan.seg_len, plan.tile_lo, plan.tile_hi,
      hl, *([w_gate, w_up, w_down] * MOE_PAIR))


def _combine_kernel(stiled_ref, ssorted_ref, slen_ref, xm_ref, ys_ref, lp_ref, mod_ref, *refs,
                    n_tiles, n_ctx_tiles):
    o_refs, (ybuf, sem) = refs[:-2], refs[-2:]
    step = pl.program_id(0)

    def start_step(p, slot):
        for k in range(COMBINE_TILES):
            tile = p * COMBINE_TILES + k

            def body(e, carry):
                s = e * n_tiles + tile
                _start_pieces(slen_ref[s], ssorted_ref[s], stiled_ref[s] - tile * LT,
                              ys_ref, ybuf.at[slot, k], sem.at[slot])
                return carry

            lax.fori_loop(0, N_EXPERTS, body, 0)

    def wait_step(p, slot):
        for k in range(COMBINE_TILES):
            tile = p * COMBINE_TILES + k
            total = lax.fori_loop(0, N_EXPERTS, lambda e, acc: acc + slen_ref[e * n_tiles + tile], 0)
            _wait_rows(total, (2 * SEG_SIZES[0],) + SEG_SIZES, ys_ref, ybuf.at[slot, k], sem.at[slot])

    @pl.when(step == 0)
    def _():
        ybuf[...] = jnp.zeros_like(ybuf)
        start_step(0, 0)

    @pl.when((step + 1) * COMBINE_TILES < n_tiles)
    def _():
        start_step(step + 1, (step + 1) % 2)

    slot = step % 2
    wait_step(step, slot)
    g2 = mod_ref[0, 0][5:6]
    cols = lax.broadcasted_iota(jnp.int32, (TM, LT), 1)
    new_tiles = []
    for k in range(COMBINE_TILES):
        lp = lp_ref[k * TM:(k + 1) * TM, :]
        pos_a = (lp[:, 0:1] * 32.0 + lp[:, 1:2]).astype(jnp.int32)
        pos_b = (lp[:, 2:3] * 32.0 + lp[:, 3:4]).astype(jnp.int32)
        gate_a = lp[:, 4:5] + lp[:, 5:6] + lp[:, 6:7]
        gate_b = lp[:, 7:8] + lp[:, 8:9] + lp[:, 9:10]
        mix = jnp.where(cols == pos_a, gate_a, jnp.where(cols == pos_b, gate_b, 0.0)).astype(BF16)
        y = _dot(mix, ybuf[slot, k])
        new_tiles.append(xm_ref[k * TM:(k + 1) * TM, :] + g2 * y)
    if len(o_refs) == 2:
        _store_pair(step * COMBINE_TILES < n_ctx_tiles, *o_refs, jnp.concatenate(new_tiles, axis=0))
    else:
        for k, x_new in enumerate(new_tiles):
            o_refs[0][k * TM:(k + 1) * TM, :] = x_new


def _combine(cfg, l, xm, ys, lp, mod, plan, paired_out):
    nt = cfg.n_tiles
    rows = COMBINE_TILES * TM
    n_steps, ctx_steps = nt // COMBINE_TILES, cfg.n_ctx_tiles // COMBINE_TILES
    if paired_out:
        out_specs = [pl.BlockSpec((rows, D_MODEL), lambda i, *_: (jnp.minimum(i, ctx_steps - 1), 0)),
                     pl.BlockSpec((rows, D_MODEL), lambda i, *_: (jnp.maximum(i - ctx_steps, 0), 0))]
    else:
        out_specs = [pl.BlockSpec((rows, D_MODEL), lambda i, *_: (i, 0))]
    return pl.pallas_call(
        functools.partial(_combine_kernel, n_tiles=nt, n_ctx_tiles=cfg.n_ctx_tiles),
        grid_spec=pltpu.PrefetchScalarGridSpec(
            num_scalar_prefetch=3,
            grid=(n_steps,),
            in_specs=[
                pl.BlockSpec((rows, D_MODEL), lambda i, *_: (i, 0)),
                pl.BlockSpec(memory_space=pl.ANY),
                pl.BlockSpec((rows, LANES), lambda i, *_: (i, 0)),
                pl.BlockSpec((1, 1, 6, D_MODEL), lambda i, *_: (l, _mod_row(cfg, i * COMBINE_TILES), 0, 0)),
            ],
            out_specs=out_specs,
            scratch_shapes=[pltpu.VMEM((2, COMBINE_TILES, LT, D_MODEL), BF16), pltpu.SemaphoreType.DMA((2,))],
        ),
        out_shape=_stream_shapes(cfg, D_MODEL, F32, paired_out),
        compiler_params=_cparams(("arbitrary",)),
        name="combine",
    )(plan.seg_tiled, plan.seg_sorted, plan.seg_len, xm, ys, lp, mod)


class Plan(NamedTuple):
    seg_tiled: jax.Array
    seg_sorted: jax.Array
    seg_len: jax.Array
    tile_expert: jax.Array
    n_valid: jax.Array
    tile_lo: jax.Array
    tile_hi: jax.Array


def _plan(cfg, counts):
    nt = cfg.n_tiles
    m = (counts + (BF16_ROWS - 1)) // BF16_ROWS * BF16_ROWS
    local_off = jnp.cumsum(m, axis=1) - m
    within_expert = jnp.cumsum(m, axis=0) - m
    expert_rows = jnp.sum(m, axis=0)
    expert_tiles = (expert_rows + (MOE_TM - 1)) // MOE_TM
    tiles_end = jnp.cumsum(expert_tiles)
    expert_start = (tiles_end - expert_tiles) * MOE_TM
    seg_sorted = (expert_start[None, :] + within_expert).T.reshape(-1).astype(jnp.int32)
    seg_tiled = (jnp.arange(nt, dtype=jnp.int32)[:, None] * LT + local_off).T.reshape(-1).astype(jnp.int32)
    seg_len = m.T.reshape(-1).astype(jnp.int32)
    n_valid = tiles_end[-1:].astype(jnp.int32)
    tile_ids = jnp.arange(cfg.moe_tiles, dtype=jnp.int32)
    tile_expert = jnp.minimum(jnp.sum(tile_ids[:, None] >= tiles_end[None, :], axis=1),
                              N_EXPERTS - 1).astype(jnp.int32)
    tile_base = tile_ids * MOE_TM
    seg_end = seg_sorted + seg_len
    tile_lo = jnp.sum(seg_end[None, :] <= tile_base[:, None], axis=1).astype(jnp.int32)
    tile_hi = jnp.sum(seg_sorted[None, :] < tile_base[:, None] + MOE_TM, axis=1).astype(jnp.int32)
    return Plan(seg_tiled, seg_sorted, seg_len, tile_expert, n_valid, tile_lo, tile_hi)


def _rope_tables(n):
    pos = jnp.arange(n)
    rc = jnp.stack([(pos // GRID_W).astype(F32), (pos % GRID_W).astype(F32)], axis=1)
    freqs = ROPE_THETA ** (-jnp.arange(AXIS_PAIRS, dtype=F32) / AXIS_PAIRS)
    ang = rc[:, :, None] * freqs[None, None, :]
    cos = jnp.cos(ang)[:, :, None, :]
    sin = jnp.sin(ang)[:, :, None, :] * jnp.array([-1.0, 1.0], F32)[None, None, :, None]
    cos = jnp.broadcast_to(cos, (n, 2, 2, AXIS_PAIRS)).reshape(n, HEAD_DIM)
    sin = jnp.broadcast_to(sin, (n, 2, 2, AXIS_PAIRS)).reshape(n, HEAD_DIM)
    reps = LANES // HEAD_DIM
    cos = jnp.concatenate([jnp.ones((TM, HEAD_DIM), F32), cos], axis=0)
    sin = jnp.concatenate([jnp.zeros((TM, HEAD_DIM), F32), sin], axis=0)
    return jnp.tile(cos, (1, reps)), jnp.tile(sin, (1, reps))


def _block_diag_mean(width):
    r = jnp.arange(width)
    return jnp.where((r[:, None] // HEAD_DIM) == (r[None, :] // HEAD_DIM), 1.0 / HEAD_DIM, 0.0).astype(BF16)


def _forward(cfg, x_prompt, x_sample, cache_k, cache_v, c, c_ctx, w_mod, b_mod, norm1_g, norm2_g,
             w_in, q_norm_g, k_norm_g, conv_dw_w, conv_dw_b, conv_ln_g, conv_ln_b, w_out,
             w_router, b_router, w_gate, w_up, w_down):
    depth = cfg.depth
    assert cfg.ctx_seq == CONV_ROWS and cfg.t_ctx % TM == 0 and cfg.smp_seq % TM == 0
    assert cfg.t_ctx % cfg.smp_seq == 0 and cfg.smp_batch + 1 <= 8 and cfg.ctx_batch % CTX_SEQS == 0
    assert cfg.n_ctx_tiles % COMBINE_TILES == 0 and cfg.smp_tiles % COMBINE_TILES == 0

    x = [x_prompt.reshape(cfg.t_ctx, D_MODEL), x_sample.reshape(-1, D_MODEL)]
    cvec = jnp.zeros((8, D_MODEL), F32).at[0].set(c_ctx).at[1:1 + cfg.smp_batch].set(c)
    mod = _modulation(cvec, w_mod, b_mod)

    n1 = norm1_g.reshape(depth, 1, D_MODEL)
    n2 = norm2_g.reshape(depth, 1, D_MODEL)
    gain = jnp.concatenate([jnp.tile(q_norm_g, (1, N_Q_HEADS)), jnp.tile(k_norm_g, (1, N_KV_HEADS))],
                           axis=1).reshape(depth, 1, QK_W)
    expert_w = (w_gate.reshape(-1, D_FF), w_up.reshape(-1, D_FF), w_down.reshape(-1, D_MODEL))
    taps = _conv_tap_rows(conv_dw_w)
    dw_b = conv_dw_b.reshape(depth, 1, CONV_CH)
    ln_g = conv_ln_g.reshape(depth, 1, CONV_CH)
    ln_b = conv_ln_b.reshape(depth, 1, CONV_CH)
    wr = jnp.pad(w_router, ((0, 0), (0, LANES - N_EXPERTS)))
    br = b_router.reshape(N_EXPERTS, 1)
    cos, sin = _rope_tables(cfg.smp_seq)
    bdq = _block_diag_mean(Q_W // 2)
    bdk = _block_diag_mean(KV_W)
    ck = cache_k.reshape(cfg.smp_batch, depth, cfg.past, KV_W)
    cv = cache_v.reshape(cfg.smp_batch, depth, cfg.past, KV_W)

    new_k = jnp.zeros((cfg.ctx_batch, depth, cfg.ctx_seq, KV_W), F32)
    new_v = jnp.zeros((cfg.ctx_batch, depth, cfg.ctx_seq, KV_W), F32)
    for l in range(depth):
        q, k, v, z, new_k, new_v = _inproj(cfg, l, x, mod, n1, w_in, gain, bdq, bdk, cos, sin, new_k, new_v)
        a_pair, c_pair, expert_bf = _mixer(cfg, l, q, k, v, ck, cv, z, taps, dw_b, ln_g, ln_b, expert_w)
        xm, hl, lp, cnt = _outproj(cfg, l, a_pair, c_pair, x, mod, n2, w_out, wr, br)
        plan = _plan(cfg, cnt[:, :, 0])
        ys = _moe(cfg, hl.reshape(cfg.n_tiles * LT, D_MODEL), plan, *expert_bf)
        x = _combine(cfg, l, xm, ys, lp, mod, plan, paired_out=(l == depth - 1))

    y_prompt = x[0].reshape(cfg.ctx_batch, cfg.ctx_seq, D_MODEL)
    y_sample = x[1].reshape(cfg.smp_batch, cfg.smp_seq, D_MODEL)
    cache_shape = (cfg.ctx_batch, depth, cfg.ctx_seq, N_KV_HEADS, HEAD_DIM)
    return y_prompt, y_sample, new_k.reshape(cache_shape), new_v.reshape(cache_shape)


def kernel(x_prompt, x_sample, cache_k, cache_v, c, c_ctx, w_mod, b_mod, norm1_g, norm2_g, w_in, q_norm_g,
           k_norm_g, conv_dw_w, conv_dw_b, conv_ln_g, conv_ln_b, w_out, w_router, b_router, w_gate, w_up, w_down):
    cfg = Cfg(ctx_batch=x_prompt.shape[0], ctx_seq=x_prompt.shape[1], smp_batch=x_sample.shape[0],
              smp_seq=x_sample.shape[1], past=cache_k.shape[2], depth=w_mod.shape[0])
    return _forward(cfg, x_prompt, x_sample, cache_k, cache_v, c, c_ctx, w_mod, b_mod, norm1_g, norm2_g,
                    w_in, q_norm_g, k_norm_g, conv_dw_w, conv_dw_b, conv_ln_g, conv_ln_b, w_out,
                    w_router, b_router, w_gate, w_up, w_down)
```

```python
import functools
import math
from typing import NamedTuple

import jax
import jax.numpy as jnp
from jax import lax
from jax.experimental import pallas as pl
from jax.experimental.pallas import tpu as pltpu

F32 = jnp.float32
BF16 = jnp.bfloat16

D_MODEL = 1024
HEAD_DIM = 64
N_Q_HEADS = 8
N_KV_HEADS = 2
Q_PER_KV = N_Q_HEADS // N_KV_HEADS
Q_W = N_Q_HEADS * HEAD_DIM
KV_W = N_KV_HEADS * HEAD_DIM
QK_W = Q_W + KV_W
CONV_CH = D_MODEL // 2
CONV_K = 31
CONV_PAD = CONV_K // 2
IN_COLS = Q_W + 2 * KV_W + 2 * CONV_CH
GRID_W = 64
AXIS_PAIRS = HEAD_DIM // 4
ROPE_THETA = 10000.0
N_EXPERTS = 16
N_GROUPS = 4
EXPERTS_PER_GROUP = N_EXPERTS // N_GROUPS
D_FF = 512
EPS = 1e-6

LANES = 128
BF16_ROWS = 16
TM = 512
TOP_K = 2
LT = 1280
MOE_TM = 512
MOE_PAIR = 2
COMBINE_TILES = 2
CONV_ROWS = 256
CTX_SEQS = 4
HALO = 16
CONV_INNER = CONV_ROWS + 16
PAD_ROWS = 24 + CONV_INNER + 8
SEG_SIZES = (512, 256, 128, 64, 32, 16)
SEG_SPLIT = 128
Q_SCALE = (1.0 / math.sqrt(HEAD_DIM)) * math.log2(math.e)
VMEM_LIMIT = 48 * 1024 * 1024
MIXER_VMEM_LIMIT = 56 * 1024 * 1024


class Cfg(NamedTuple):
    ctx_batch: int
    ctx_seq: int
    smp_batch: int
    smp_seq: int
    past: int
    depth: int

    @property
    def t_ctx(self):
        return self.ctx_batch * self.ctx_seq

    @property
    def t_all(self):
        return self.t_ctx + self.smp_batch * self.smp_seq

    @property
    def n_tiles(self):
        return self.t_all // TM

    @property
    def n_ctx_tiles(self):
        return self.t_ctx // TM

    @property
    def smp_tiles(self):
        return self.smp_seq // TM

    @property
    def moe_tiles(self):
        rows = TOP_K * self.t_all + self.n_tiles * N_EXPERTS * (BF16_ROWS - 1) + N_EXPERTS * (MOE_TM - 1)
        step_rows = MOE_PAIR * MOE_TM
        return -(-rows // step_rows) * MOE_PAIR


def _mod_row(cfg, i):
    return jnp.where(i < cfg.n_ctx_tiles, 0, 1 + (i - cfg.n_ctx_tiles) // cfg.smp_tiles)


def _pair_specs(cfg, width, tile_of=lambda i: i):
    nct = cfg.n_ctx_tiles
    return [pl.BlockSpec((TM, width), lambda i, *_: (jnp.minimum(tile_of(i), nct - 1), 0)),
            pl.BlockSpec((TM, width), lambda i, *_: (jnp.maximum(tile_of(i) - nct, 0), 0))]


def _pair_shapes(cfg, width, dtype):
    return [jax.ShapeDtypeStruct((cfg.t_ctx, width), dtype),
            jax.ShapeDtypeStruct((cfg.t_all - cfg.t_ctx, width), dtype)]


def _load_pair(is_ctx, c_ref, s_ref):
    return jnp.where(is_ctx, c_ref[...], s_ref[...])


def _store_pair(is_ctx, c_ref, s_ref, val):
    @pl.when(is_ctx)
    def _():
        c_ref[...] = val

    @pl.when(jnp.logical_not(is_ctx))
    def _():
        s_ref[...] = val


def _stream_specs(cfg, width, paired, tile_of=lambda i: i):
    if paired:
        return _pair_specs(cfg, width, tile_of)
    return [pl.BlockSpec((TM, width), lambda i, *_: (tile_of(i), 0))]


def _stream_shapes(cfg, width, dtype, paired):
    if paired:
        return _pair_shapes(cfg, width, dtype)
    return [jax.ShapeDtypeStruct((cfg.t_all, width), dtype)]


def _load_stream(is_ctx, refs):
    return _load_pair(is_ctx, *refs) if len(refs) == 2 else refs[0][...]


def _cparams(sem, vmem=VMEM_LIMIT):
    return pltpu.CompilerParams(dimension_semantics=sem, vmem_limit_bytes=vmem)


def _dot(a, b):
    return jnp.dot(a, b, preferred_element_type=F32)


def _dot_nt(a, b):
    return lax.dot_general(a, b, (((1,), (1,)), ((), ())), preferred_element_type=F32)


def _sigmoid(x):
    return 1.0 / (1.0 + jnp.exp(-x))


def _mod_kernel(c_ref, w_ref, b_ref, o_ref):
    c = c_ref[...]
    a = (c * _sigmoid(c)).astype(BF16)
    o_ref[0] = _dot(a, w_ref[0].astype(BF16)) + b_ref[0]


def _modulation(cvec, w_mod, b_mod):
    depth = w_mod.shape[0]
    out = pl.pallas_call(
        _mod_kernel,
        grid=(depth, 6),
        in_specs=[
            pl.BlockSpec((8, D_MODEL), lambda l, j: (0, 0)),
            pl.BlockSpec((1, D_MODEL, D_MODEL), lambda l, j: (l, 0, j)),
            pl.BlockSpec((1, 1, D_MODEL), lambda l, j: (l, 0, j)),
        ],
        out_specs=pl.BlockSpec((1, 8, D_MODEL), lambda l, j: (l, 0, j)),
        out_shape=jax.ShapeDtypeStruct((depth, 8, 6 * D_MODEL), F32),
        compiler_params=_cparams(("arbitrary", "arbitrary")),
        name="modulation",
    )(cvec, w_mod, b_mod.reshape(depth, 1, 6 * D_MODEL))
    return out.reshape(depth, 8, 6, D_MODEL)


def _inproj_kernel(*refs, n_tiles, n_ctx_tiles, ctx_seq, n_x):
    x_refs = refs[:n_x]
    (mod_ref, n1_ref, w_ref, gain_ref, bdq_ref, bdk_ref, cos_ref, sin_ref,
     kc_in, vc_in, q_ref, k_ref, v_ref, z_ref, kc_ref, vc_ref, w_bf, proj_prev) = refs[n_x:]
    del kc_in, vc_in
    step = pl.program_id(0)

    @pl.when(step == 0)
    def _():
        w_bf[...] = w_ref[0].astype(BF16)
        proj_prev[...] = jnp.zeros_like(proj_prev)

    x = _load_stream(jnp.minimum(step, n_tiles - 1) < n_ctx_tiles, x_refs)
    mod = mod_ref[0, 0]
    sh1, sc1 = mod[0:1], mod[1:2]
    ms = jnp.mean(x * x, axis=-1, keepdims=True)
    h = (x * lax.rsqrt(ms + EPS) * n1_ref[0]) * (1.0 + sc1) + sh1
    proj_new = _dot(h.astype(BF16), w_bf[...])

    qk = proj_prev[:, :QK_W]
    sq = (qk * qk).astype(BF16)
    half = Q_W // 2
    msq = jnp.concatenate([_dot(sq[:, :half], bdq_ref[...]),
                           _dot(sq[:, half:Q_W], bdq_ref[...]),
                           _dot(sq[:, Q_W:], bdk_ref[...])], axis=1)
    qkn = qk * lax.rsqrt(msq + EPS) * gain_ref[0]
    v = proj_prev[:, QK_W:QK_W + KV_W]
    v_ref[...] = v
    a = proj_prev[:, QK_W + KV_W:QK_W + KV_W + CONV_CH]
    b = proj_prev[:, QK_W + KV_W + CONV_CH:]
    z_ref[...] = a * _sigmoid(b)
    cos = cos_ref[...]
    sin = sin_ref[...]
    first = (lax.broadcasted_iota(jnp.int32, (TM, LANES), 1) % 32) < 16
    for c in range(QK_W // LANES):
        blk = qkn[:, c * LANES:(c + 1) * LANES]
        partner = jnp.where(first, pltpu.roll(blk, LANES - 16, 1), pltpu.roll(blk, 16, 1))
        rot = blk * cos + partner * sin
        if c < Q_W // LANES:
            q_ref[:, c * LANES:(c + 1) * LANES] = (rot * Q_SCALE).astype(BF16)
        else:
            k = rot
            k_ref[...] = k

    proj_prev[...] = proj_new

    @pl.when(step - 1 < n_ctx_tiles)
    def _():
        for s in range(TM // ctx_seq):
            kc_ref[s, 0] = k[s * ctx_seq:(s + 1) * ctx_seq]
            vc_ref[s, 0] = v[s * ctx_seq:(s + 1) * ctx_seq]


def _inproj(cfg, l, x, mod, n1, w_in, gain, bdq, bdk, cos, sin, kc, vc):
    t = cfg.t_all
    n_x = len(x)
    nt, nct, st = cfg.n_tiles, cfg.n_ctx_tiles, cfg.smp_tiles
    tile_a = lambda i: jnp.minimum(i, nt - 1)
    tile_b = lambda i: jnp.maximum(i - 1, 0)
    rope_map = lambda i: (jnp.where(tile_b(i) < nct, 0, 1 + (tile_b(i) - nct) % st), 0)
    seqs = TM // cfg.ctx_seq
    cache_spec = lambda: pl.BlockSpec((seqs, 1, cfg.ctx_seq, KV_W),
                                      lambda i: (jnp.minimum(tile_b(i), nct - 1), l, 0, 0))
    return pl.pallas_call(
        functools.partial(_inproj_kernel, n_tiles=nt, n_ctx_tiles=nct, ctx_seq=cfg.ctx_seq, n_x=n_x),
        grid=(nt + 1,),
        in_specs=_stream_specs(cfg, D_MODEL, n_x == 2, tile_a) + [
            pl.BlockSpec((1, 1, 6, D_MODEL), lambda i: (l, _mod_row(cfg, tile_a(i)), 0, 0)),
            pl.BlockSpec((1, 1, D_MODEL), lambda i: (l, 0, 0)),
            pl.BlockSpec((1, D_MODEL, IN_COLS), lambda i: (l, 0, 0)),
            pl.BlockSpec((1, 1, QK_W), lambda i: (l, 0, 0)),
            pl.BlockSpec((Q_W // 2, Q_W // 2), lambda i: (0, 0)),
            pl.BlockSpec((KV_W, KV_W), lambda i: (0, 0)),
            pl.BlockSpec((TM, LANES), rope_map),
            pl.BlockSpec((TM, LANES), rope_map),
            pl.BlockSpec(memory_space=pl.ANY),
            pl.BlockSpec(memory_space=pl.ANY),
        ],
        out_specs=[
            pl.BlockSpec((TM, Q_W), lambda i: (tile_b(i), 0)),
            pl.BlockSpec((TM, KV_W), lambda i: (tile_b(i), 0)),
            pl.BlockSpec((TM, KV_W), lambda i: (tile_b(i), 0)),
            pl.BlockSpec((TM, CONV_CH), lambda i: (tile_b(i), 0)),
            cache_spec(), cache_spec(),
        ],
        out_shape=[
            jax.ShapeDtypeStruct((t, Q_W), BF16),
            jax.ShapeDtypeStruct((t, KV_W), F32),
            jax.ShapeDtypeStruct((t, KV_W), F32),
            jax.ShapeDtypeStruct((t, CONV_CH), F32),
            jax.ShapeDtypeStruct(kc.shape, F32),
            jax.ShapeDtypeStruct(vc.shape, F32),
        ],
        scratch_shapes=[pltpu.VMEM((D_MODEL, IN_COLS), BF16), pltpu.VMEM((TM, IN_COLS), F32)],
        input_output_aliases={n_x + 8: 4, n_x + 9: 5},
        compiler_params=_cparams(("arbitrary",)),
        name="inproj",
    )(*x, mod, n1, w_in, gain, bdq, bdk, cos, sin, kc, vc)


def _mixer_kernel(*refs, rq, n_past, seq_blocks, n_seq):
    if n_past:
        q_ref, k_ref, v_ref, ck_ref, cv_ref = refs[:5]
        refs = refs[5:]
    else:
        q_ref, k_ref, v_ref = refs[:3]
        refs = refs[3:]
    z_ref, zp_ref, zn_ref, w_ref, b_ref, g_ref, beta_ref = refs[:7]
    refs = refs[7:]
    if n_past:
        for src, dst in zip(refs[0:3], refs[5:8]):
            dst[...] = src[...].astype(BF16)
        refs = refs[3:5] + refs[8:]
    o_ref, co_ref, krep, vrep, pad, taps = refs
    head_w = Q_PER_KV * HEAD_DIM
    j = pl.program_id(1)

    @pl.when(jnp.logical_and(pl.program_id(0) == 0, j == 0))
    def _():
        diag = (lax.broadcasted_iota(jnp.int32, (LANES, LANES), 0)
                == lax.broadcasted_iota(jnp.int32, (LANES, LANES), 1))
        for c in range(CONV_CH // LANES):
            for p in range(4):
                for q in range(4):
                    for half in range(2):
                        tap = w_ref[0, 8 * q + 2 * p + half:8 * q + 2 * p + half + 1, c * LANES:(c + 1) * LANES]
                        taps[c, p, q * LANES:(q + 1) * LANES, half * LANES:(half + 1) * LANES] = (
                            jnp.where(diag, tap, 0.0).astype(BF16))

    n_new = k_ref.shape[0] // n_seq

    @pl.when(j == 0)
    def _():
        src = lax.broadcasted_iota(jnp.int32, (KV_W, head_w), 0)
        dst = lax.broadcasted_iota(jnp.int32, (KV_W, head_w), 1) % HEAD_DIM
        for h in range(N_KV_HEADS):
            rep = jnp.where(src == dst + h * HEAD_DIM, 1.0, 0.0).astype(BF16)
            if n_past:
                krep[0, h, :n_past, :] = _dot(ck_ref[0, 0].astype(BF16), rep).astype(BF16)
                vrep[0, h, :n_past, :] = _dot(cv_ref[0, 0].astype(BF16), rep).astype(BF16)
            for s in range(n_seq):
                rows = slice(s * n_new, (s + 1) * n_new)
                krep[s, h, n_past:n_past + n_new, :] = _dot(k_ref[rows, :].astype(BF16), rep).astype(BF16)
                vrep[s, h, n_past:n_past + n_new, :] = _dot(v_ref[rows, :].astype(BF16), rep).astype(BF16)

    lane_head = lax.broadcasted_iota(jnp.int32, (rq, head_w), 1) // HEAD_DIM
    end = HALO + CONV_ROWS
    for s in range(n_seq):
        rows = slice(s * rq, (s + 1) * rq)
        if seq_blocks > 1:
            pad[0:HALO, :] = jnp.where(j > 0, zp_ref[...], 0.0)
            pad[end:end + HALO, :] = jnp.where(j < seq_blocks - 1, zn_ref[...], 0.0)
        else:
            pad[0:HALO, :] = jnp.zeros((HALO, CONV_CH), F32)
            pad[end:end + HALO, :] = jnp.zeros((HALO, CONV_CH), F32)
        pad[end + HALO:, :] = jnp.zeros((PAD_ROWS - end - HALO, CONV_CH), F32)
        pad[HALO:end, :] = z_ref[rows, :]

        for h in range(N_KV_HEADS):
            qh = q_ref[rows, h * head_w:(h + 1) * head_w]
            kh = krep[s, h]
            vh = vrep[s, h]
            acc = jnp.zeros((rq, head_w), F32)
            for g in range(Q_PER_KV):
                mask = lane_head == g
                qm = jnp.where(mask, qh, jnp.zeros_like(qh))
                sc = _dot_nt(qm, kh)
                m = jnp.max(sc, axis=-1, keepdims=True)
                p = jnp.exp2(sc - m)
                denom = jnp.sum(p, axis=-1, keepdims=True)
                o = _dot(p.astype(BF16), vh)
                acc = jnp.where(mask, o * (1.0 / denom), acc)
            o_ref[rows, h * head_w:(h + 1) * head_w] = acc.astype(BF16)
        conv_cols = [_conv_cols(pad, taps, c) for c in range(CONV_CH // LANES)]
        co_ref[rows, :] = _conv_finish(conv_cols, b_ref, g_ref, beta_ref)


def _mixer(cfg, l, q, k, v, cache_k, cache_v, z, taps, dw_b, ln_g, ln_b, expert_w):
    head_w = Q_PER_KV * HEAD_DIM
    rq = CONV_ROWS
    hb = CONV_ROWS // HALO
    last = cfg.t_all // HALO - 1
    vec = lambda: pl.BlockSpec((1, 1, CONV_CH), lambda b, j: (l, 0, 0))

    def conv_specs(row_block, n_seq=1):
        return [
            pl.BlockSpec((n_seq * CONV_ROWS, CONV_CH), lambda b, j: (row_block(b, j), 0)),
            pl.BlockSpec((HALO, CONV_CH), lambda b, j: (jnp.maximum(row_block(b, j) * hb - 1, 0), 0)),
            pl.BlockSpec((HALO, CONV_CH), lambda b, j: (jnp.minimum(row_block(b, j) * hb + hb, last), 0)),
            pl.BlockSpec((1, 32, CONV_CH), lambda b, j: (l, 0, 0)),
            vec(), vec(), vec(),
        ]

    def scratch(sk, n_seq=1):
        return [pltpu.VMEM((n_seq, N_KV_HEADS, sk, head_w), BF16), pltpu.VMEM((n_seq, N_KV_HEADS, sk, head_w), BF16),
                pltpu.VMEM((PAD_ROWS, CONV_CH), F32),
                pltpu.VMEM((CONV_CH // LANES, 4, 4 * LANES, 2 * LANES), BF16)]

    sc = cfg.ctx_seq
    rows = CTX_SEQS * sc
    a_ctx, c_ctx = pl.pallas_call(
        functools.partial(_mixer_kernel, rq=rq, n_past=0, seq_blocks=1, n_seq=CTX_SEQS),
        grid=(cfg.ctx_batch // CTX_SEQS, 1),
        in_specs=[
            pl.BlockSpec((rows, Q_W), lambda b, j: (b, 0)),
            pl.BlockSpec((rows, KV_W), lambda b, j: (b, 0)),
            pl.BlockSpec((rows, KV_W), lambda b, j: (b, 0)),
        ] + conv_specs(lambda b, j: b, CTX_SEQS),
        out_specs=[pl.BlockSpec((rows, Q_W), lambda b, j: (b, 0)),
                   pl.BlockSpec((rows, CONV_CH), lambda b, j: (b, 0))],
        out_shape=[jax.ShapeDtypeStruct((cfg.t_ctx, Q_W), BF16),
                   jax.ShapeDtypeStruct((cfg.t_ctx, CONV_CH), BF16)],
        scratch_shapes=scratch(sc, CTX_SEQS),
        compiler_params=_cparams(("arbitrary", "arbitrary")),
        name="mixer_ctx",
    )(q, k, v, z, z, z, taps, dw_b, ln_g, ln_b)
    ss = cfg.smp_seq
    sk = cfg.past + ss
    kv_base = cfg.t_ctx // ss
    q_base = cfg.t_ctx // rq
    nqb = ss // rq
    t_smp = cfg.smp_batch * ss
    steps = cfg.smp_batch * nqb
    w_rows = [w.shape[0] // (cfg.depth * steps) for w in expert_w]
    w_specs = lambda off: [pl.BlockSpec((r, w.shape[1]), lambda b, j: (off + b * nqb + j, 0))
                           for r, w in zip(w_rows, expert_w)]
    a_smp, c_smp, *expert_bf = pl.pallas_call(
        functools.partial(_mixer_kernel, rq=rq, n_past=cfg.past, seq_blocks=nqb, n_seq=1),
        grid=(cfg.smp_batch, nqb),
        in_specs=[
            pl.BlockSpec((rq, Q_W), lambda b, j: (q_base + b * nqb + j, 0)),
            pl.BlockSpec((ss, KV_W), lambda b, j: (kv_base + b, 0)),
            pl.BlockSpec((ss, KV_W), lambda b, j: (kv_base + b, 0)),
            pl.BlockSpec((1, 1, cfg.past, KV_W), lambda b, j: (b, l, 0, 0)),
            pl.BlockSpec((1, 1, cfg.past, KV_W), lambda b, j: (b, l, 0, 0)),
        ] + conv_specs(lambda b, j: q_base + b * nqb + j) + w_specs(l * steps),
        out_specs=[pl.BlockSpec((rq, Q_W), lambda b, j: (b * nqb + j, 0)),
                   pl.BlockSpec((rq, CONV_CH), lambda b, j: (b * nqb + j, 0))] + w_specs(0),
        out_shape=[jax.ShapeDtypeStruct((t_smp, Q_W), BF16),
                   jax.ShapeDtypeStruct((t_smp, CONV_CH), BF16)]
        + [jax.ShapeDtypeStruct((r * steps, w.shape[1]), BF16) for r, w in zip(w_rows, expert_w)],
        scratch_shapes=scratch(sk),
        compiler_params=_cparams(("arbitrary", "arbitrary"), MIXER_VMEM_LIMIT),
        name="mixer_smp",
    )(q, k, v, cache_k, cache_v, z, z, z, taps, dw_b, ln_g, ln_b, *expert_w)
    return (a_ctx, a_smp), (c_ctx, c_smp), expert_bf


def _conv_cols(pad, taps, c):
    cols = slice(c * LANES, (c + 1) * LANES)
    lhs = jnp.concatenate([pad[8 * q:8 * q + CONV_INNER, cols].astype(BF16) for q in range(4)], axis=1)
    acc = None
    for p in range(4):
        pair = _dot(lhs, taps[c, p])
        for half in range(2):
            o = 2 * p + half
            shifted = pair[o:o + CONV_ROWS, half * LANES:(half + 1) * LANES]
            acc = shifted if acc is None else acc + shifted
    return acc


def _conv_tap_rows(conv_dw_w):
    return jnp.pad(conv_dw_w, ((0, 0), (HALO - CONV_PAD, 32 - CONV_K - (HALO - CONV_PAD)), (0, 0)))


def _conv_finish(blocks, b_ref, g_ref, beta_ref):
    y = jnp.concatenate(blocks, axis=1) + b_ref[0]
    mu = jnp.mean(y, axis=-1, keepdims=True)
    yc = y - mu
    var = jnp.mean(yc * yc, axis=-1, keepdims=True)
    yn = yc * lax.rsqrt(var + EPS) * g_ref[0] + beta_ref[0]
    return (yn * _sigmoid(yn)).astype(BF16)


def _split_bf16(x):
    hi = x.astype(BF16)
    lo = (x - hi.astype(F32)).astype(BF16)
    return hi, lo


def _outproj_kernel(ac_ref, as_ref, cc_ref, cs_ref, *refs, n_tiles, n_ctx_tiles, n_x):
    x_refs = refs[:n_x]
    (mod_ref, n2_ref, wo_ref, wr_ref, br_ref, xm_ref, hl_ref, lp_ref, cnt_ref,
     wo_bf, h_prev, lt_prev, before_bf, eye_bf, rows_i) = refs[n_x:]
    step = pl.program_id(0)

    @pl.when(step == 0)
    def _():
        wo_bf[...] = wo_ref[0].astype(BF16)
        h_prev[...] = jnp.zeros_like(h_prev)
        lt_prev[...] = jnp.zeros_like(lt_prev)
        r = lax.broadcasted_iota(jnp.int32, (TM, TM), 0)
        c = lax.broadcasted_iota(jnp.int32, (TM, TM), 1)
        before_bf[...] = jnp.where(r < c, 1.0, 0.0).astype(BF16)
        eye_bf[...] = jnp.where(r == c, 1.0, 0.0).astype(BF16)
        rows_i[...] = lax.broadcasted_iota(jnp.int32, (LT, TM), 0)

    is_ctx = jnp.minimum(step, n_tiles - 1) < n_ctx_tiles
    ac = jnp.concatenate([_load_pair(is_ctx, ac_ref, as_ref), _load_pair(is_ctx, cc_ref, cs_ref)], axis=1)
    y = _dot(ac, wo_bf[...])
    mod = mod_ref[0, 0]
    g1, sh2, sc2 = mod[2:3], mod[3:4], mod[4:5]
    xm = _load_stream(is_ctx, x_refs) + g1 * y
    xm_ref[...] = xm
    ms = jnp.mean(xm * xm, axis=-1, keepdims=True)
    h2 = (xm * lax.rsqrt(ms + EPS) * n2_ref[0]) * (1.0 + sc2) + sh2
    h_new, h_lo = _split_bf16(h2)
    w_hi, w_lo = _split_bf16(wr_ref[...])
    logits = _dot(h_new, w_hi) + (_dot(h_new, w_lo) + _dot(h_lo, w_hi))
    lt_new = logits.T[0:N_EXPERTS, :]

    h_hi = h_prev[...]
    s_all = _sigmoid(lt_prev[...])
    sb_all = s_all + br_ref[...]

    rows8 = 2 * EXPERTS_PER_GROUP
    row = lax.broadcasted_iota(jnp.int32, (rows8, TM), 0)
    member = row % EXPERTS_PER_GROUP
    second = row >= EXPERTS_PER_GROUP

    def partner(x, k):
        return jnp.where(member + k < EXPERTS_PER_GROUP,
                         pltpu.roll(x, rows8 - k, 0), pltpu.roll(x, EXPERTS_PER_GROUP - k, 0))

    def other_group(x):
        return pltpu.roll(x, EXPERTS_PER_GROUP, 0)

    def group_sum(x):
        return x + partner(x, 1) + (partner(x, 2) + partner(x, 3))

    def ahead(other, mine, other_first):
        return jnp.where((other >= mine) if other_first else (other > mine), 1.0, 0.0)

    sel, gscore = [], []
    for hv in range(2):
        sb8 = sb_all[hv * rows8:(hv + 1) * rows8]
        rank = jnp.zeros((rows8, TM), F32)
        for k in range(1, EXPERTS_PER_GROUP):
            pk = partner(sb8, k)
            rank = rank + jnp.where(member + k >= EXPERTS_PER_GROUP, ahead(pk, sb8, True), ahead(pk, sb8, False))
        sel.append(rank < 1.5)
        gscore.append(group_sum(jnp.where(sel[hv], sb8, 0.0)))
    gate8, used8 = [], []
    for hv in range(2):
        mine, far = gscore[hv], gscore[1 - hv]
        near = other_group(mine)
        rank = jnp.where(second, ahead(near, mine, True), ahead(near, mine, False))
        rank = rank + ahead(far, mine, hv == 1) + ahead(other_group(far), mine, hv == 1)
        chosen = rank < 0.5
        used8.append(jnp.where(chosen, jnp.where(sel[hv], 1.0, 0.0), 0.0))
        gate8.append(used8[hv] * s_all[hv * rows8:(hv + 1) * rows8])
    total = group_sum(gate8[0]) + group_sum(gate8[1])
    total = total + other_group(total)
    inv_total = 1.0 / total
    gate16 = jnp.concatenate([gate8[0] * inv_total, gate8[1] * inv_total], axis=0)
    used16 = jnp.concatenate(used8, axis=0)
    used = used16 > 0.5

    prior = _dot(used16.astype(BF16), before_bf[...])
    n_e = jnp.sum(used16, axis=-1, keepdims=True)
    m_e = jnp.floor((n_e + (BF16_ROWS - 1)) * (1.0 / BF16_ROWS)) * BF16_ROWS
    lower = (lax.broadcasted_iota(jnp.int32, (N_EXPERTS, N_EXPERTS), 0)
             > lax.broadcasted_iota(jnp.int32, (N_EXPERTS, N_EXPERTS), 1))
    start = _dot(jnp.where(lower, 1.0, 0.0).astype(BF16),
                 jnp.broadcast_to(m_e, (N_EXPERTS, LANES)).astype(BF16))[:, 0:1]
    pos16 = start + prior
    pos_a = jnp.min(jnp.where(used, pos16, float(LT)), axis=0, keepdims=True)
    pos_b = jnp.max(jnp.where(used, pos16, -1.0), axis=0, keepdims=True)
    gate_a = jnp.sum(jnp.where(used, jnp.where(pos16 == pos_a, gate16, 0.0), 0.0), axis=0, keepdims=True)
    gate_b = jnp.sum(jnp.where(used, jnp.where(pos16 == pos_b, gate16, 0.0), 0.0), axis=0, keepdims=True)
    rows = rows_i[...]
    perm = jnp.where(rows == pos_a.astype(jnp.int32), 1.0,
                     jnp.where(rows == pos_b.astype(jnp.int32), 1.0, 0.0)).astype(BF16)
    hl_ref[0] = _dot(perm, h_hi).astype(BF16)

    def split32(p):
        hi = jnp.floor(p * (1.0 / 32.0))
        return [hi, p - 32.0 * hi]

    def split3(g):
        hi = g.astype(BF16).astype(F32)
        mid = (g - hi).astype(BF16).astype(F32)
        lo = ((g - hi) - mid).astype(BF16).astype(F32)
        return [hi, mid, lo]

    parts = split32(pos_a) + split32(pos_b) + split3(gate_a) + split3(gate_b)
    lp_parts = jnp.concatenate(parts + [jnp.zeros((LANES - len(parts), TM), F32)], axis=0).astype(BF16)
    lp_ref[...] = _dot_nt(eye_bf[...], lp_parts)

    cnt_ref[0] = jnp.broadcast_to(n_e, (N_EXPERTS, LANES)).astype(jnp.int32)

    h_prev[...] = h_new
    lt_prev[...] = lt_new


def _outproj(cfg, l, a_pair, c_pair, x, mod, n2, w_out, w_router, b_router):
    t, nt, nct = cfg.t_all, cfg.n_tiles, cfg.n_ctx_tiles
    n_x = len(x)
    tile_a = lambda i: jnp.minimum(i, nt - 1)
    tile_b = lambda i: jnp.maximum(i - 1, 0)
    return pl.pallas_call(
        functools.partial(_outproj_kernel, n_tiles=nt, n_ctx_tiles=nct, n_x=n_x),
        grid=(nt + 1,),
        in_specs=_pair_specs(cfg, Q_W, tile_a) + _pair_specs(cfg, CONV_CH, tile_a)
        + _stream_specs(cfg, D_MODEL, n_x == 2, tile_a) + [
            pl.BlockSpec((1, 1, 6, D_MODEL), lambda i: (l, _mod_row(cfg, tile_a(i)), 0, 0)),
            pl.BlockSpec((1, 1, D_MODEL), lambda i: (l, 0, 0)),
            pl.BlockSpec((1, D_MODEL, D_MODEL), lambda i: (l, 0, 0)),
            pl.BlockSpec((D_MODEL, LANES), lambda i: (0, 0)),
            pl.BlockSpec((N_EXPERTS, 1), lambda i: (0, 0)),
        ],
        out_specs=_stream_specs(cfg, D_MODEL, False, tile_a) + [
            pl.BlockSpec((1, LT, D_MODEL), lambda i: (tile_b(i), 0, 0)),
            pl.BlockSpec((TM, LANES), lambda i: (tile_b(i), 0)),
            pl.BlockSpec((1, N_EXPERTS, LANES), lambda i: (tile_b(i), 0, 0)),
        ],
        out_shape=_stream_shapes(cfg, D_MODEL, F32, False) + [
            jax.ShapeDtypeStruct((nt, LT, D_MODEL), BF16),
            jax.ShapeDtypeStruct((t, LANES), F32),
            jax.ShapeDtypeStruct((nt, N_EXPERTS, LANES), jnp.int32),
        ],
        scratch_shapes=[pltpu.VMEM((D_MODEL, D_MODEL), BF16), pltpu.VMEM((TM, D_MODEL), BF16),
                        pltpu.VMEM((N_EXPERTS, TM), F32), pltpu.VMEM((TM, TM), BF16), pltpu.VMEM((TM, TM), BF16),
                        pltpu.VMEM((LT, TM), jnp.int32)],
        compiler_params=_cparams(("arbitrary",)),
        name="outproj",
    )(*a_pair, *c_pair, *x, mod, n2, w_out, w_router, b_router)


def _start_pieces(n, src_row, dst_row, src_ref, dst_ref, sem):
    def piece(size):
        off = n & ~(2 * size - 1)

        @pl.when((n & size) != 0)
        def _():
            pltpu.make_async_copy(
                src_ref.at[pl.ds(pl.multiple_of(src_row + off, BF16_ROWS), size)],
                dst_ref.at[pl.ds(pl.multiple_of(dst_row + off, BF16_ROWS), size)],
                sem).start(priority=SEG_SIZES.index(size) % 2)

    @pl.when(n >= SEG_SPLIT)
    def _():
        for size in SEG_SIZES:
            if size >= SEG_SPLIT:
                piece(size)

    for size in SEG_SIZES:
        if size < SEG_SPLIT:
            piece(size)


def _wait_rows(total, sizes, src_ref, dst_ref, sem):
    for size in sizes:
        @pl.when((total & size) != 0)
        def _():
            pltpu.make_async_copy(src_ref.at[pl.ds(0, size)], dst_ref.at[pl.ds(0, size)], sem).wait()


def _moe_kernel(tg_ref, nv_ref, ssrc_ref, sdst_ref, slen_ref, tlo_ref, thi_ref, hl_ref, *refs):
    del tg_ref
    w_refs = [refs[3 * k:3 * k + 3] for k in range(MOE_PAIR)]
    y_ref, xbuf, sem = refs[3 * MOE_PAIR:]
    i = pl.program_id(0)
    nv = nv_ref[0]

    def overlap(s, tile):
        base = tile * MOE_TM
        d = sdst_ref[s]
        lo = jnp.maximum(d, base)
        hi = jnp.minimum(d + slen_ref[s], base + MOE_TM)
        return jnp.maximum(hi - lo, 0), lo

    def start_step(step, slot):
        for k in range(MOE_PAIR):
            tile = step * MOE_PAIR + k

            @pl.when(tile < nv)
            def _():
                def body(s, carry):
                    n, lo = overlap(s, tile)
                    _start_pieces(n, ssrc_ref[s] + (lo - sdst_ref[s]), lo - tile * MOE_TM,
                                  hl_ref, xbuf.at[slot, k], sem.at[slot])
                    return carry

                lax.fori_loop(tlo_ref[tile], thi_ref[tile], body, 0)

    def wait_step(step, slot):
        for k in range(MOE_PAIR):
            tile = step * MOE_PAIR + k
            total = lax.fori_loop(tlo_ref[tile], thi_ref[tile], lambda s, acc: acc + overlap(s, tile)[0], 0)
            _wait_rows(total, SEG_SIZES, hl_ref, xbuf.at[slot, k], sem.at[slot])

    @pl.when(i == 0)
    def _():
        xbuf[...] = jnp.zeros_like(xbuf)
        start_step(0, 0)

    @pl.when((i + 1) * MOE_PAIR < nv)
    def _():
        start_step(i + 1, (i + 1) % 2)

    @pl.when(i * MOE_PAIR >= nv)
    def _():
        y_ref[...] = jnp.zeros_like(y_ref)

    @pl.when(i * MOE_PAIR < nv)
    def _():
        slot = i % 2
        wait_step(i, slot)
        for k, (wg_ref, wu_ref, wd_ref) in enumerate(w_refs):
            x = xbuf[slot, k]
            hg = _dot(x, wg_ref[0])
            hu = _dot(x, wu_ref[0])
            hid = ((hg * _sigmoid(hg)) * hu).astype(BF16)
            y_ref[k * MOE_TM:(k + 1) * MOE_TM, :] = _dot(hid, wd_ref[0]).astype(BF16)


def _moe(cfg, hl, plan, w_gate, w_up, w_down):
    n_steps = cfg.moe_tiles // MOE_PAIR

    def w_map(k):
        return lambda i, tg, nv, *_: (tg[jnp.minimum(i * MOE_PAIR + k, nv[0] - 1)], 0, 0)

    w_gate = w_gate.reshape(N_EXPERTS, D_MODEL, D_FF)
    w_up = w_up.reshape(N_EXPERTS, D_MODEL, D_FF)
    w_down = w_down.reshape(N_EXPERTS, D_FF, D_MODEL)
    w_specs = []
    for k in range(MOE_PAIR):
        w_specs += [pl.BlockSpec((1, D_MODEL, D_FF), w_map(k)), pl.BlockSpec((1, D_MODEL, D_FF), w_map(k)),
                    pl.BlockSpec((1, D_FF, D_MODEL), w_map(k))]

    return pl.pallas_call(
        _moe_kernel,
        grid_spec=pltpu.PrefetchScalarGridSpec(
            num_scalar_prefetch=7,
            grid=(n_steps,),
            in_specs=[pl.BlockSpec(memory_space=pl.ANY)] + w_specs,
            out_specs=pl.BlockSpec((MOE_PAIR * MOE_TM, D_MODEL), lambda i, *_: (i, 0)),
            scratch_shapes=[pltpu.VMEM((2, MOE_PAIR, MOE_TM, D_MODEL), BF16), pltpu.SemaphoreType.DMA((2,))],
        ),
        out_shape=jax.ShapeDtypeStruct((n_steps * MOE_PAIR * MOE_TM, D_MODEL), BF16),
        compiler_params=_cparams(("arbitrary",)),
        name="moe",
    )(plan.tile_expert, plan.n_valid, plan.seg_tiled, plan.seg_sorted, plan.seg_len, plan.tile_lo, plan.tile_hi,
      hl, *([w_gate, w_up, w_down] * MOE_PAIR))


def _combine_kernel(stiled_ref, ssorted_ref, slen_ref, xm_ref, ys_ref, lp_ref, mod_ref, *refs,
                    n_tiles, n_ctx_tiles):
    o_refs, (ybuf, sem) = refs[:-2], refs[-2:]
    step = pl.program_id(0)

    def start_step(p, slot):
        for k in range(COMBINE_TILES):
            tile = p * COMBINE_TILES + k

            def body(e, carry):
                s = e * n_tiles + tile
                _start_pieces(slen_ref[s], ssorted_ref[s], stiled_ref[s] - tile * LT,
                              ys_ref, ybuf.at[slot, k], sem.at[slot])
                return carry

            lax.fori_loop(0, N_EXPERTS, body, 0)

    def wait_step(p, slot):
        for k in range(COMBINE_TILES):
            tile = p * COMBINE_TILES + k
            total = lax.fori_loop(0, N_EXPERTS, lambda e, acc: acc + slen_ref[e * n_tiles + tile], 0)
            _wait_rows(total, (2 * SEG_SIZES[0],) + SEG_SIZES, ys_ref, ybuf.at[slot, k], sem.at[slot])

    @pl.when(step == 0)
    def _():
        ybuf[...] = jnp.zeros_like(ybuf)
        start_step(0, 0)

    @pl.when((step + 1) * COMBINE_TILES < n_tiles)
    def _():
        start_step(step + 1, (step + 1) % 2)

    slot = step % 2
    wait_step(step, slot)
    g2 = mod_ref[0, 0][5:6]
    cols = lax.broadcasted_iota(jnp.int32, (TM, LT), 1)
    new_tiles = []
    for k in range(COMBINE_TILES):
        lp = lp_ref[k * TM:(k + 1) * TM, :]
        pos_a = (lp[:, 0:1] * 32.0 + lp[:, 1:2]).astype(jnp.int32)
        pos_b = (lp[:, 2:3] * 32.0 + lp[:, 3:4]).astype(jnp.int32)
        gate_a = lp[:, 4:5] + lp[:, 5:6] + lp[:, 6:7]
        gate_b = lp[:, 7:8] + lp[:, 8:9] + lp[:, 9:10]
        mix = jnp.where(cols == pos_a, gate_a, jnp.where(cols == pos_b, gate_b, 0.0)).astype(BF16)
        y = _dot(mix, ybuf[slot, k])
        new_tiles.append(xm_ref[k * TM:(k + 1) * TM, :] + g2 * y)
    if len(o_refs) == 2:
        _store_pair(step * COMBINE_TILES < n_ctx_tiles, *o_refs, jnp.concatenate(new_tiles, axis=0))
    else:
        for k, x_new in enumerate(new_tiles):
            o_refs[0][k * TM:(k + 1) * TM, :] = x_new


def _combine(cfg, l, xm, ys, lp, mod, plan, paired_out):
    nt = cfg.n_tiles
    rows = COMBINE_TILES * TM
    n_steps, ctx_steps = nt // COMBINE_TILES, cfg.n_ctx_tiles // COMBINE_TILES
    if paired_out:
        out_specs = [pl.BlockSpec((rows, D_MODEL), lambda i, *_: (jnp.minimum(i, ctx_steps - 1), 0)),
                     pl.BlockSpec((rows, D_MODEL), lambda i, *_: (jnp.maximum(i - ctx_steps, 0), 0))]
    else:
        out_specs = [pl.BlockSpec((rows, D_MODEL), lambda i, *_: (i, 0))]
    return pl.pallas_call(
        functools.partial(_combine_kernel, n_tiles=nt, n_ctx_tiles=cfg.n_ctx_tiles),
        grid_spec=pltpu.PrefetchScalarGridSpec(
            num_scalar_prefetch=3,
            grid=(n_steps,),
            in_specs=[
                pl.BlockSpec((rows, D_MODEL), lambda i, *_: (i, 0)),
                pl.BlockSpec(memory_space=pl.ANY),
                pl.BlockSpec((rows, LANES), lambda i, *_: (i, 0)),
                pl.BlockSpec((1, 1, 6, D_MODEL), lambda i, *_: (l, _mod_row(cfg, i * COMBINE_TILES), 0, 0)),
            ],
            out_specs=out_specs,
            scratch_shapes=[pltpu.VMEM((2, COMBINE_TILES, LT, D_MODEL), BF16), pltpu.SemaphoreType.DMA((2,))],
        ),
        out_shape=_stream_shapes(cfg, D_MODEL, F32, paired_out),
        compiler_params=_cparams(("arbitrary",)),
        name="combine",
    )(plan.seg_tiled, plan.seg_sorted, plan.seg_len, xm, ys, lp, mod)


class Plan(NamedTuple):
    seg_tiled: jax.Array
    seg_sorted: jax.Array
    seg_len: jax.Array
    tile_expert: jax.Array
    n_valid: jax.Array
    tile_lo: jax.Array
    tile_hi: jax.Array


def _plan(cfg, counts):
    nt = cfg.n_tiles
    m = (counts + (BF16_ROWS - 1)) // BF16_ROWS * BF16_ROWS
    local_off = jnp.cumsum(m, axis=1) - m
    within_expert = jnp.cumsum(m, axis=0) - m
    expert_rows = jnp.sum(m, axis=0)
    expert_tiles = (expert_rows + (MOE_TM - 1)) // MOE_TM
    tiles_end = jnp.cumsum(expert_tiles)
    expert_start = (tiles_end - expert_tiles) * MOE_TM
    seg_sorted = (expert_start[None, :] + within_expert).T.reshape(-1).astype(jnp.int32)
    seg_tiled = (jnp.arange(nt, dtype=jnp.int32)[:, None] * LT + local_off).T.reshape(-1).astype(jnp.int32)
    seg_len = m.T.reshape(-1).astype(jnp.int32)
    n_valid = tiles_end[-1:].astype(jnp.int32)
    tile_ids = jnp.arange(cfg.moe_tiles, dtype=jnp.int32)
    tile_expert = jnp.minimum(jnp.sum(tile_ids[:, None] >= tiles_end[None, :], axis=1),
                              N_EXPERTS - 1).astype(jnp.int32)
    tile_base = tile_ids * MOE_TM
    seg_end = seg_sorted + seg_len
    tile_lo = jnp.sum(seg_end[None, :] <= tile_base[:, None], axis=1).astype(jnp.int32)
    tile_hi = jnp.sum(seg_sorted[None, :] < tile_base[:, None] + MOE_TM, axis=1).astype(jnp.int32)
    return Plan(seg_tiled, seg_sorted, seg_len, tile_expert, n_valid, tile_lo, tile_hi)


def _rope_tables(n):
    pos = jnp.arange(n)
    rc = jnp.stack([(pos // GRID_W).astype(F32), (pos % GRID_W).astype(F32)], axis=1)
    freqs = ROPE_THETA ** (-jnp.arange(AXIS_PAIRS, dtype=F32) / AXIS_PAIRS)
    ang = rc[:, :, None] * freqs[None, None, :]
    cos = jnp.cos(ang)[:, :, None, :]
    sin = jnp.sin(ang)[:, :, None, :] * jnp.array([-1.0, 1.0], F32)[None, None, :, None]
    cos = jnp.broadcast_to(cos, (n, 2, 2, AXIS_PAIRS)).reshape(n, HEAD_DIM)
    sin = jnp.broadcast_to(sin, (n, 2, 2, AXIS_PAIRS)).reshape(n, HEAD_DIM)
    reps = LANES // HEAD_DIM
    cos = jnp.concatenate([jnp.ones((TM, HEAD_DIM), F32), cos], axis=0)
    sin = jnp.concatenate([jnp.zeros((TM, HEAD_DIM), F32), sin], axis=0)
    return jnp.tile(cos, (1, reps)), jnp.tile(sin, (1, reps))


def _block_diag_mean(width):
    r = jnp.arange(width)
    return jnp.where((r[:, None] // HEAD_DIM) == (r[None, :] // HEAD_DIM), 1.0 / HEAD_DIM, 0.0).astype(BF16)


def _forward(cfg, x_prompt, x_sample, cache_k, cache_v, c, c_ctx, w_mod, b_mod, norm1_g, norm2_g,
             w_in, q_norm_g, k_norm_g, conv_dw_w, conv_dw_b, conv_ln_g, conv_ln_b, w_out,
             w_router, b_router, w_gate, w_up, w_down):
    depth = cfg.depth
    assert cfg.ctx_seq == CONV_ROWS and cfg.t_ctx % TM == 0 and cfg.smp_seq % TM == 0
    assert cfg.t_ctx % cfg.smp_seq == 0 and cfg.smp_batch + 1 <= 8 and cfg.ctx_batch % CTX_SEQS == 0
    assert cfg.n_ctx_tiles % COMBINE_TILES == 0 and cfg.smp_tiles % COMBINE_TILES == 0

    x = [x_prompt.reshape(cfg.t_ctx, D_MODEL), x_sample.reshape(-1, D_MODEL)]
    cvec = jnp.zeros((8, D_MODEL), F32).at[0].set(c_ctx).at[1:1 + cfg.smp_batch].set(c)
    mod = _modulation(cvec, w_mod, b_mod)

    n1 = norm1_g.reshape(depth, 1, D_MODEL)
    n2 = norm2_g.reshape(depth, 1, D_MODEL)
    gain = jnp.concatenate([jnp.tile(q_norm_g, (1, N_Q_HEADS)), jnp.tile(k_norm_g, (1, N_KV_HEADS))],
                           axis=1).reshape(depth, 1, QK_W)
    expert_w = (w_gate.reshape(-1, D_FF), w_up.reshape(-1, D_FF), w_down.reshape(-1, D_MODEL))
    taps = _conv_tap_rows(conv_dw_w)
    dw_b = conv_dw_b.reshape(depth, 1, CONV_CH)
    ln_g = conv_ln_g.reshape(depth, 1, CONV_CH)
    ln_b = conv_ln_b.reshape(depth, 1, CONV_CH)
    wr = jnp.pad(w_router, ((0, 0), (0, LANES - N_EXPERTS)))
    br = b_router.reshape(N_EXPERTS, 1)
    cos, sin = _rope_tables(cfg.smp_seq)
    bdq = _block_diag_mean(Q_W // 2)
    bdk = _block_diag_mean(KV_W)
    ck = cache_k.reshape(cfg.smp_batch, depth, cfg.past, KV_W)
    cv = cache_v.reshape(cfg.smp_batch, depth, cfg.past, KV_W)

    new_k = jnp.zeros((cfg.ctx_batch, depth, cfg.ctx_seq, KV_W), F32)
    new_v = jnp.zeros((cfg.ctx_batch, depth, cfg.ctx_seq, KV_W), F32)
    for l in range(depth):
        q, k, v, z, new_k, new_v = _inproj(cfg, l, x, mod, n1, w_in, gain, bdq, bdk, cos, sin, new_k, new_v)
        a_pair, c_pair, expert_bf = _mixer(cfg, l, q, k, v, ck, cv, z, taps, dw_b, ln_g, ln_b, expert_w)
        xm, hl, lp, cnt = _outproj(cfg, l, a_pair, c_pair, x, mod, n2, w_out, wr, br)
        plan = _plan(cfg, cnt[:, :, 0])
        ys = _moe(cfg, hl.reshape(cfg.n_tiles * LT, D_MODEL), plan, *expert_bf)
        x = _combine(cfg, l, xm, ys, lp, mod, plan, paired_out=(l == depth - 1))

    y_prompt = x[0].reshape(cfg.ctx_batch, cfg.ctx_seq, D_MODEL)
    y_sample = x[1].reshape(cfg.smp_batch, cfg.smp_seq, D_MODEL)
    cache_shape = (cfg.ctx_batch, depth, cfg.ctx_seq, N_KV_HEADS, HEAD_DIM)
    return y_prompt, y_sample, new_k.reshape(cache_shape), new_v.reshape(cache_shape)


def kernel(x_prompt, x_sample, cache_k, cache_v, c, c_ctx, w_mod, b_mod, norm1_g, norm2_g, w_in, q_norm_g,
           k_norm_g, conv_dw_w, conv_dw_b, conv_ln_g, conv_ln_b, w_out, w_router, b_router, w_gate, w_up, w_down):
    cfg = Cfg(ctx_batch=x_prompt.shape[0], ctx_seq=x_prompt.shape[1], smp_batch=x_sample.shape[0],
              smp_seq=x_sample.shape[1], past=cache_k.shape[2], depth=w_mod.shape[0])
    return _forward(cfg, x_prompt, x_sample, cache_k, cache_v, c, c_ctx, w_mod, b_mod, norm1_g, norm2_g,
                    w_in, q_norm_g, k_norm_g, conv_dw_w, conv_dw_b, conv_ln_g, conv_ln_b, w_out,
                    w_router, b_router, w_gate, w_up, w_down)
```
